```python
import math, functools
import jax, jax.numpy as jnp
from jax import lax
import numpy as np

D_MODEL = 1024
BATCH = 4
SEQ = 4096
DEPTH = 2
DEC_BATCH = 32
DEC_SEQ = 8
PAST_LEN = 8192
PAGE_SIZE = 128

MIX_WIDTH = D_MODEL // 2
LRU_WIDTH = MIX_WIDTH
LRU_BLOCKS = 8
LRU_BLOCK = LRU_WIDTH // LRU_BLOCKS
LRU_C = 8.0
CONV_W = 4
DSA_HEADS = 8
DSA_KV_HEADS = 2
DSA_HD = MIX_WIDTH // DSA_HEADS
IDX_HEADS = 4
IDX_HD = 64
TOPK_MAX = 256
GDN_HEADS = 4
GDN_HD = MIX_WIDTH // GDN_HEADS
GDN_CHUNK = 64
DIFF_HEADS = 4
DIFF_HD = MIX_WIDTH // (2 * DIFF_HEADS)
MEM_LEN = 256
XA_HEADS = 4
XA_HD = 128
D_FF = 2816
N_EXPERTS = 8
TOP_K = 2
EXPERT_FF = 1024
ROPE_THETA = 10000.0
EPS = 1e-6
Q_BLOCK = 128
N_BRANCH = 4
N_DENSE = (DEPTH + 1) // 2
N_MOE = DEPTH // 2
GDN_QKV = 3 * GDN_HEADS * GDN_HD
IN_SPLITS = (
    ('lru_x', LRU_WIDTH), ('lru_gate', LRU_WIDTH),
    ('dsa_q', DSA_HEADS * DSA_HD), ('dsa_k', DSA_KV_HEADS * DSA_HD), ('dsa_v', DSA_KV_HEADS * DSA_HD),
    ('idx_q', IDX_HEADS * IDX_HD), ('idx_k', IDX_HD), ('idx_w', IDX_HEADS),
    ('gdn_qkv', GDN_QKV), ('gdn_z', GDN_HEADS * GDN_HD), ('gdn_b', GDN_HEADS), ('gdn_a', GDN_HEADS),
    ('diff_q', 2 * DIFF_HEADS * DIFF_HD), ('diff_k', 2 * DIFF_HEADS * DIFF_HD), ('diff_v', 2 * DIFF_HEADS * DIFF_HD),
    ('merge_gate', N_BRANCH * D_MODEL),
)
IN_COLS = sum(n for _, n in IN_SPLITS)

kernel_name = 'hybrid_lru_dsa_gdn_diffattn_decoder_step'

F32 = jnp.float32


def rms_norm(x, g, eps=EPS):
    xf = x.astype(F32)
    y = xf * lax.rsqrt(jnp.mean(xf * xf, axis=-1, keepdims=True) + eps)
    return (y * g.astype(F32)).astype(x.dtype)


def l2_normalize(x):
    xf = x.astype(F32)
    return xf * lax.rsqrt(jnp.sum(xf * xf, axis=-1, keepdims=True) + EPS)


def split_cols(h):
    out, off = {}, 0
    for name, n in IN_SPLITS:
        out[name] = h[..., off:off + n]
        off += n
    return out


def rope(x, pos):
    half = x.shape[-1] // 2
    inv = ROPE_THETA ** (-jnp.arange(half, dtype=F32) / half)
    ang = pos.astype(F32)[:, None] * inv[None, :]
    cos, sin = jnp.cos(ang)[:, None, :], jnp.sin(ang)[:, None, :]
    xf = x.astype(F32)
    x1, x2 = xf[..., :half], xf[..., half:]
    return jnp.concatenate([x1 * cos - x2 * sin, x2 * cos + x1 * sin], axis=-1).astype(x.dtype)


def causal_conv(x, buf, w):
    T = x.shape[1]
    xp = jnp.concatenate([buf.astype(x.dtype), x], axis=1)
    y = xp[:, CONV_W - 1:CONV_W - 1 + T] * w[CONV_W - 1]
    for j in range(CONV_W - 1):
        y = y + xp[:, j:j + T] * w[j]
    return y, xp[:, T:]


def rg_lru(x, w_a, b_a, w_x, b_x, lam, h0, reset):
    B, T, W = x.shape
    xb = x.reshape(B, T, LRU_BLOCKS, LRU_BLOCK)
    r = jax.nn.sigmoid((jnp.einsum('btni,nij->btnj', xb, w_a).reshape(B, T, W) + b_a).astype(F32))
    i = jax.nn.sigmoid((jnp.einsum('btni,nij->btnj', xb, w_x).reshape(B, T, W) + b_x).astype(F32))
    log_a = -LRU_C * r * jax.nn.softplus(-lam.astype(F32))
    rs = reset[None, :, None]
    a = jnp.where(rs, 0.0, jnp.exp(log_a))
    mult = jnp.where(rs, 1.0, jnp.sqrt(-jnp.expm1(2.0 * log_a)))
    u = mult * i * x.astype(F32)
    u = u.at[:, 0].add(a[:, 0] * h0.astype(F32))

    def combine(e, l_):
        return (e[0] * l_[0], l_[0] * e[1] + l_[1])

    _, h = lax.associative_scan(combine, (a, u), axis=1)
    return h, h[:, -1]


def gated_delta_chunked(q, k, v, g, beta, s0):
    B, T, H, Dk = q.shape
    Dv = v.shape[-1]
    C = min(GDN_CHUNK, T)
    pad = (-T) % C
    NC = (T + pad) // C

    def prep(a):
        a = jnp.pad(a.astype(F32), [(0, 0), (0, pad)] + [(0, 0)] * (a.ndim - 2))
        return jnp.moveaxis(a.reshape(B, NC, C, *a.shape[2:]), 3, 1)

    q, k, v, g, beta = prep(q) * Dk ** -0.5, prep(k), prep(v), prep(g), prep(beta)
    gc = jnp.cumsum(g, axis=-1)
    tril = jnp.tril(jnp.ones((C, C), bool))
    strict = jnp.tril(jnp.ones((C, C), bool), -1)
    diff = gc[..., :, None] - gc[..., None, :]
    decay = jnp.where(tril, jnp.exp(jnp.where(tril, diff, 0.0)), 0.0)
    k_beta = k * beta[..., None]
    v_beta = v * beta[..., None]
    lower = jnp.where(strict, jnp.einsum('bhnid,bhnjd->bhnij', k_beta, k) * decay, 0.0)
    a_mat = lower + jnp.eye(C, dtype=F32)
    u = lax.linalg.triangular_solve(a_mat, v_beta, left_side=True, lower=True, unit_diagonal=True)
    w = lax.linalg.triangular_solve(a_mat, k_beta * jnp.exp(gc)[..., None], left_side=True, lower=True, unit_diagonal=True)
    intra = jnp.where(tril, jnp.einsum('bhnid,bhnjd->bhnij', q, k) * decay, 0.0)

    def step(S, xs):
        q_i, k_i, u_i, w_i, gc_i, a_i = xs
        v_new = u_i - jnp.einsum('bhcd,bhde->bhce', w_i, S)
        o = jnp.einsum('bhcd,bhde->bhce', q_i * jnp.exp(gc_i)[..., None], S) + jnp.einsum('bhij,bhje->bhie', a_i, v_new)
        g_last = gc_i[..., -1]
        S = S * jnp.exp(g_last)[..., None, None] + jnp.einsum(
            'bhcd,bhce->bhde', k_i * jnp.exp(g_last[..., None] - gc_i)[..., None], v_new)
        return S, o

    xs = tuple(jnp.moveaxis(a, 2, 0) for a in (q, k, u, w, gc, intra))
    S, o = lax.scan(step, s0.astype(F32), xs)
    o = jnp.moveaxis(jnp.moveaxis(o, 0, 2), 1, 3).reshape(B, NC * C, H, Dv)[:, :T]
    return o, S


def map_query_blocks(fn, pos, *qs):
    B, S = qs[0].shape[:2]
    nb = S // Q_BLOCK
    to_blocks = lambda a: jnp.moveaxis(a.reshape(B, nb, Q_BLOCK, *a.shape[2:]), 1, 0)
    out = lax.map(lambda args: fn(*args), (pos.reshape(nb, Q_BLOCK),) + tuple(to_blocks(a) for a in qs))
    return jnp.moveaxis(out, 0, 1).reshape(B, S, *out.shape[3:])


def indexer_select(iq, iw, ik, q_pos, k_pos, n_sel):
    s = jnp.einsum('bthd,bsd->bths', iq, ik).astype(F32) * IDX_HD ** -0.5
    score = jnp.einsum('bth,bths->bts', iw.astype(F32) * IDX_HEADS ** -0.5, jax.nn.relu(s))
    allowed = k_pos[None, :] <= q_pos[:, None]
    score = jnp.where(allowed[None], score, -jnp.inf)
    _, idx = lax.top_k(score, n_sel)
    valid = idx <= q_pos[None, :, None]
    return idx, valid


def sparse_attend(q, k_sel, v_sel, valid):
    B, T, H, D = q.shape
    qg = q.reshape(B, T, DSA_KV_HEADS, H // DSA_KV_HEADS, D)
    s = jnp.einsum('btgqd,btkgd->btgqk', qg, k_sel).astype(F32) * D ** -0.5
    s = jnp.where(valid[:, :, None, None, :], s, -jnp.inf)
    p = jax.nn.softmax(s, axis=-1)
    o = jnp.einsum('btgqk,btkgd->btgqd', p.astype(v_sel.dtype), v_sel)
    return o.reshape(B, T, H * D)


def dsa_prompt_attend(q, iq, iw, k, v, ik, pos):
    S = k.shape[1]
    n_sel = min(TOPK_MAX, S // 4)
    take = jax.vmap(lambda rows, idx: rows[idx])

    def block(pb, qb, iqb, iwb):
        idx, valid = indexer_select(iqb, iwb, ik, pb, pos, n_sel)
        return sparse_attend(qb, take(k, idx), take(v, idx), valid)

    return map_query_blocks(block, pos, q, iq, iw)


def dsa_sample_attend(q, iq, iw, k, v, ik, pos, layer, pool_k, pool_v, pool_ik, page_table):
    DB, T = q.shape[:2]
    past = page_table.shape[1] * PAGE_SIZE
    n_keys = past + T
    ik_past = pool_ik[layer, page_table].reshape(DB, past, IDX_HD).astype(ik.dtype)
    ik_all = jnp.concatenate([ik_past, ik], axis=1)
    idx, valid = indexer_select(iq, iw, ik_all, pos, jnp.arange(n_keys), min(TOPK_MAX, n_keys // 4))
    in_past = idx < past
    pidx = jnp.minimum(idx, past - 1)
    phys = jax.vmap(lambda pt, lp: pt[lp])(page_table, pidx // PAGE_SIZE)
    off = pidx % PAGE_SIZE
    nidx = jnp.clip(idx - past, 0, T - 1)
    take = jax.vmap(lambda rows, i: rows[i])

    def gather(pool, new):
        return jnp.where(in_past[..., None, None], pool[layer, phys, off].astype(new.dtype), take(new, nidx))

    return sparse_attend(q, gather(pool_k, k), gather(pool_v, v), valid)


def diff_attend(q, k, v, q_pos, k_pos, lam, sub_g, lam_init):
    B, T, H, _, D = q.shape
    s = jnp.einsum('bthmd,bshmd->bhmts', q, k).astype(F32) * D ** -0.5
    s = jnp.where((k_pos[None, :] <= q_pos[:, None]), s, -jnp.inf)
    p = jax.nn.softmax(s, axis=-1)
    a = p[:, :, 0] - lam * p[:, :, 1]
    o = jnp.einsum('bhts,bshe->bthe', a.astype(v.dtype), v)
    o = rms_norm(o, sub_g) * (1.0 - lam_init)
    return o.reshape(B, T, H * 2 * D)


def diff_prompt_attend(q, k, v, lam, sub_g, lam_init, pos):
    return map_query_blocks(lambda pb, qb: diff_attend(qb, k, v, pb, pos, lam, sub_g, lam_init), pos, q)


def diff_sample_attend(q, k, v, lam, sub_g, lam_init, pos, layer, pool_k, pool_v, page_table):
    DB, T = q.shape[:2]
    past = page_table.shape[1] * PAGE_SIZE
    k_past = pool_k[layer, page_table].reshape(DB, past, DIFF_HEADS, 2, DIFF_HD).astype(k.dtype)
    v_past = pool_v[layer, page_table].reshape(DB, past, DIFF_HEADS, 2 * DIFF_HD).astype(v.dtype)
    k_all = jnp.concatenate([k_past, k], axis=1)
    v_all = jnp.concatenate([v_past, v], axis=1)
    return diff_attend(q, k_all, v_all, pos, jnp.arange(past + T), lam, sub_g, lam_init)


def cross_attend(xn, mk, mv, wq, wo):
    B, T, _ = xn.shape
    q = (xn @ wq).reshape(B, T, XA_HEADS, XA_HD)
    s = jnp.einsum('bthd,bmhd->bhtm', q, mk).astype(F32) * XA_HD ** -0.5
    p = jax.nn.softmax(s, axis=-1)
    o = jnp.einsum('bhtm,bmhd->bthd', p.astype(mv.dtype), mv).reshape(B, T, XA_HEADS * XA_HD)
    return o @ wo


def swiglu(x, wg, wu, wd):
    return (jax.nn.silu(x @ wg) * (x @ wu)) @ wd


def moe_ffn(x, w_router, b_router, w_gate, w_up, w_down):
    logits = (x @ w_router).astype(F32) + b_router.astype(F32)
    top_v, top_i = lax.top_k(logits, TOP_K)
    probs = jax.nn.softmax(top_v, axis=-1)
    gates = jnp.sum(jax.nn.one_hot(top_i, N_EXPERTS, dtype=F32) * probs[..., None], axis=-2)
    y = jnp.zeros_like(x)
    for e in range(N_EXPERTS):
        y = y + gates[..., e:e + 1].astype(x.dtype) * swiglu(x, w_gate[e], w_up[e], w_down[e])
    return y


def mixer(xn, pos, layer, P, lru_buf, lru_h0, gdn_buf, gdn_s0, attend_dsa, attend_diff):
    B, T, _ = xn.shape
    parts = split_cols(xn @ P['w_in'])
    xa, lru_buf_new = causal_conv(parts['lru_x'], lru_buf, P['lru_conv_w'])
    xa = xa + P['lru_conv_b']
    h, h_last = rg_lru(xa, P['lru_wa'], P['lru_ba'], P['lru_wx'], P['lru_bx'], P['lru_lambda'], lru_h0, pos == 0)
    o_a = h * jax.nn.gelu(parts['lru_gate'].astype(F32))
    q = rope(parts['dsa_q'].reshape(B, T, DSA_HEADS, DSA_HD), pos)
    k = rope(parts['dsa_k'].reshape(B, T, DSA_KV_HEADS, DSA_HD), pos)
    v = parts['dsa_v'].reshape(B, T, DSA_KV_HEADS, DSA_HD)
    iq = rope(parts['idx_q'].reshape(B, T, IDX_HEADS, IDX_HD), pos)
    ik = rope(parts['idx_k'].reshape(B, T, 1, IDX_HD), pos)[:, :, 0]
    o_b = attend_dsa(q, iq, parts['idx_w'], k, v, ik)
    qkv, gdn_buf_new = causal_conv(parts['gdn_qkv'], gdn_buf, P['gdn_conv_w'])
    gq, gk, gv = jnp.split(jax.nn.silu(qkv), 3, axis=-1)
    hs = (B, T, GDN_HEADS, GDN_HD)
    beta = jax.nn.sigmoid(parts['gdn_b'].astype(F32))
    g = -jnp.exp(P['gdn_a_log'].astype(F32)) * jax.nn.softplus(parts['gdn_a'].astype(F32) + P['gdn_dt_bias'].astype(F32))
    o_c, s_new = gated_delta_chunked(l2_normalize(gq.reshape(hs)), l2_normalize(gk.reshape(hs)), gv.reshape(hs), g, beta, gdn_s0)
    o_c = (rms_norm(o_c, P['gdn_norm_g']) * jax.nn.silu(parts['gdn_z'].reshape(hs).astype(F32))).reshape(B, T, MIX_WIDTH)
    lam_init = 0.8 - 0.6 * math.exp(-0.3 * layer)
    dq = rope(parts['diff_q'].reshape(B, T, 2 * DIFF_HEADS, DIFF_HD), pos).reshape(B, T, DIFF_HEADS, 2, DIFF_HD)
    dk = rope(parts['diff_k'].reshape(B, T, 2 * DIFF_HEADS, DIFF_HD), pos).reshape(B, T, DIFF_HEADS, 2, DIFF_HD)
    dv = parts['diff_v'].reshape(B, T, DIFF_HEADS, 2 * DIFF_HD)
    lq1, lk1, lq2, lk2 = P['diff_lambda'].astype(F32)
    lam = jnp.exp(jnp.sum(lq1 * lk1)) - jnp.exp(jnp.sum(lq2 * lk2)) + lam_init
    o_d = attend_diff(dq, dk, dv, lam, P['diff_subln_g'], lam_init)
    branches = jnp.stack([o.astype(xn.dtype) for o in (o_a, o_b, o_c, o_d)], axis=2)
    proj = jnp.einsum('btnc,ncd->btnd', branches, P['w_branch'])
    gates = jax.nn.sigmoid(parts['merge_gate'].reshape(B, T, N_BRANCH, D_MODEL))
    out = jnp.sum(gates * proj, axis=2) @ P['w_out']
    new = dict(dsa_k=k, dsa_v=v, idx_k=ik, diff_k=dk.reshape(B, T, DIFF_HEADS, 2 * DIFF_HD), diff_v=dv,
               lru_h=h_last, lru_conv=lru_buf_new, gdn_s=s_new, gdn_conv=gdn_buf_new)
    return out, new


def decoder_layer(x, pos, layer, P, mem_k, mem_v, lru_buf, lru_h0, gdn_buf, gdn_s0, attend_dsa, attend_diff):
    mix, new = mixer(rms_norm(x, P['norm_mix_g']), pos, layer, P, lru_buf, lru_h0, gdn_buf, gdn_s0, attend_dsa, attend_diff)
    x = x + mix
    x = x + cross_attend(rms_norm(x, P['norm_xa_g']), mem_k, mem_v, P['xa_wq'], P['xa_wo'])
    xn = rms_norm(x, P['norm_ffn_g'])
    f = swiglu(xn, *P['ffn']) if 'ffn' in P else moe_ffn(xn, *P['moe'])
    return x + f, new


def stack_state(states, name):
    return jnp.stack([s[name] for s in states], axis=0)


def setup_inputs(seed: int = 0) -> dict:
    key = jax.random.key(seed)
    ks = iter(jax.random.split(key, 64))

    def nrm(shape, scale=1.0):
        return scale * jax.random.normal(next(ks), shape, F32)

    def gain(shape):
        return 1.0 + 0.01 * jax.random.normal(next(ks), shape, F32)

    n_pages = PAST_LEN // PAGE_SIZE
    n_used = DEC_BATCH * n_pages
    n_pool = n_used + max(1, n_used // 4)
    u = jax.random.uniform(next(ks), (DEPTH, LRU_WIDTH), F32, 0.9, 0.999)
    sig = u ** (1.0 / LRU_C)
    lru_lambda = jnp.log(sig) - jnp.log1p(-sig)
    gdn_a_log = jnp.log(jax.random.uniform(next(ks), (DEPTH, GDN_HEADS), F32, 1.0, 16.0))
    dt = jnp.exp(jax.random.uniform(next(ks), (DEPTH, GDN_HEADS), F32, math.log(1e-3), math.log(1e-1)))
    gdn_dt_bias = dt + jnp.log(-jnp.expm1(-dt))
    page_table = jax.random.permutation(next(ks), n_pool)[:n_used].reshape(DEC_BATCH, n_pages).astype(jnp.int32)
    return {
        'x_prompt': nrm((BATCH, SEQ, D_MODEL)),
        'x_sample': nrm((DEC_BATCH, DEC_SEQ, D_MODEL)),
        'mem_prompt': nrm((BATCH, MEM_LEN, D_MODEL)),
        'cache_dsa_k': nrm((DEPTH, n_pool, PAGE_SIZE, DSA_KV_HEADS, DSA_HD)),
        'cache_dsa_v': nrm((DEPTH, n_pool, PAGE_SIZE, DSA_KV_HEADS, DSA_HD)),
        'cache_idx_k': nrm((DEPTH, n_pool, PAGE_SIZE, IDX_HD)),
        'cache_diff_k': nrm((DEPTH, n_pool, PAGE_SIZE, DIFF_HEADS, 2 * DIFF_HD)),
        'cache_diff_v': nrm((DEPTH, n_pool, PAGE_SIZE, DIFF_HEADS, 2 * DIFF_HD)),
        'cache_mem_k': nrm((DEPTH, DEC_BATCH, MEM_LEN, XA_HEADS, XA_HD)),
        'cache_mem_v': nrm((DEPTH, DEC_BATCH, MEM_LEN, XA_HEADS, XA_HD)),
        'state_lru_h': nrm((DEPTH, DEC_BATCH, LRU_WIDTH), 0.5),
        'state_lru_conv': nrm((DEPTH, DEC_BATCH, CONV_W - 1, LRU_WIDTH)),
        'state_gdn_s': nrm((DEPTH, DEC_BATCH, GDN_HEADS, GDN_HD, GDN_HD), 0.1),
        'state_gdn_conv': nrm((DEPTH, DEC_BATCH, CONV_W - 1, GDN_QKV)),
        'page_table': page_table,
        'norm_mix_g': gain((DEPTH, D_MODEL)),
        'w_in': nrm((DEPTH, D_MODEL, IN_COLS), D_MODEL ** -0.5),
        'lru_conv_w': nrm((DEPTH, CONV_W, LRU_WIDTH), 0.5),
        'lru_conv_b': nrm((DEPTH, LRU_WIDTH), 0.01),
        'lru_wa': nrm((DEPTH, LRU_BLOCKS, LRU_BLOCK, LRU_BLOCK), LRU_BLOCK ** -0.5),
        'lru_ba': nrm((DEPTH, LRU_WIDTH), 0.01),
        'lru_wx': nrm((DEPTH, LRU_BLOCKS, LRU_BLOCK, LRU_BLOCK), LRU_BLOCK ** -0.5),
        'lru_bx': nrm((DEPTH, LRU_WIDTH), 0.01),
        'lru_lambda': lru_lambda,
        'gdn_conv_w': nrm((DEPTH, CONV_W, GDN_QKV), 0.5),
        'gdn_a_log': gdn_a_log,
        'gdn_dt_bias': gdn_dt_bias,
        'gdn_norm_g': gain((DEPTH, GDN_HD)),
        'diff_lambda': nrm((DEPTH, 4, DIFF_HD), 0.1),
        'diff_subln_g': gain((DEPTH, 2 * DIFF_HD)),
        'w_branch': nrm((DEPTH, N_BRANCH, MIX_WIDTH, D_MODEL), MIX_WIDTH ** -0.5),
        'w_out': nrm((DEPTH, D_MODEL, D_MODEL), D_MODEL ** -0.5),
        'norm_xa_g': gain((DEPTH, D_MODEL)),
        'norm_mem_g': gain((DEPTH, D_MODEL)),
        'xa_wq': nrm((DEPTH, D_MODEL, XA_HEADS * XA_HD), D_MODEL ** -0.5),
        'xa_wk': nrm((DEPTH, D_MODEL, XA_HEADS * XA_HD), D_MODEL ** -0.5),
        'xa_wv': nrm((DEPTH, D_MODEL, XA_HEADS * XA_HD), D_MODEL ** -0.5),
        'xa_wo': nrm((DEPTH, XA_HEADS * XA_HD, D_MODEL), (XA_HEADS * XA_HD) ** -0.5),
        'norm_ffn_g': gain((DEPTH, D_MODEL)),
        'ffn_w_gate': nrm((N_DENSE, D_MODEL, D_FF), D_MODEL ** -0.5),
        'ffn_w_up': nrm((N_DENSE, D_MODEL, D_FF), D_MODEL ** -0.5),
        'ffn_w_down': nrm((N_DENSE, D_FF, D_MODEL), D_FF ** -0.5),
        'moe_router_w': nrm((N_MOE, D_MODEL, N_EXPERTS), D_MODEL ** -0.5),
        'moe_router_b': nrm((N_MOE, N_EXPERTS), 0.01),
        'moe_w_gate': nrm((N_MOE, N_EXPERTS, D_MODEL, EXPERT_FF), D_MODEL ** -0.5),
        'moe_w_up': nrm((N_MOE, N_EXPERTS, D_MODEL, EXPERT_FF), D_MODEL ** -0.5),
        'moe_w_down': nrm((N_MOE, N_EXPERTS, EXPERT_FF, D_MODEL), EXPERT_FF ** -0.5),
        'final_norm_g': gain((D_MODEL,)),
    }


def reference(x_prompt, x_sample, mem_prompt, cache_dsa_k, cache_dsa_v, cache_idx_k, cache_diff_k, cache_diff_v,
              cache_mem_k, cache_mem_v, state_lru_h, state_lru_conv, state_gdn_s, state_gdn_conv, page_table,
              norm_mix_g, w_in, lru_conv_w, lru_conv_b, lru_wa, lru_ba, lru_wx, lru_bx, lru_lambda,
              gdn_conv_w, gdn_a_log, gdn_dt_bias, gdn_norm_g, diff_lambda, diff_subln_g, w_branch, w_out,
              norm_xa_g, norm_mem_g, xa_wq, xa_wk, xa_wv, xa_wo, norm_ffn_g, ffn_w_gate, ffn_w_up, ffn_w_down,
              moe_router_w, moe_router_b, moe_w_gate, moe_w_up, moe_w_down, final_norm_g):
    B, S, _ = x_prompt.shape
    DB, T, _ = x_sample.shape
    M = mem_prompt.shape[1]
    past = page_table.shape[1] * PAGE_SIZE
    pos_p = jnp.arange(S, dtype=jnp.int32)
    pos_s = past + jnp.arange(T, dtype=jnp.int32)
    lru_buf0 = jnp.zeros((B, CONV_W - 1, LRU_WIDTH), x_prompt.dtype)
    lru_h0 = jnp.zeros((B, LRU_WIDTH), F32)
    gdn_buf0 = jnp.zeros((B, CONV_W - 1, GDN_QKV), x_prompt.dtype)
    gdn_s0 = jnp.zeros((B, GDN_HEADS, GDN_HD, GDN_HD), F32)

    hp, hs = x_prompt, x_sample
    new_p, new_s, mem_ks, mem_vs = [], [], [], []
    for l in range(DEPTH):
        P = dict(norm_mix_g=norm_mix_g[l], w_in=w_in[l], lru_conv_w=lru_conv_w[l], lru_conv_b=lru_conv_b[l],
                 lru_wa=lru_wa[l], lru_ba=lru_ba[l], lru_wx=lru_wx[l], lru_bx=lru_bx[l], lru_lambda=lru_lambda[l],
                 gdn_conv_w=gdn_conv_w[l], gdn_a_log=gdn_a_log[l], gdn_dt_bias=gdn_dt_bias[l], gdn_norm_g=gdn_norm_g[l],
                 diff_lambda=diff_lambda[l], diff_subln_g=diff_subln_g[l], w_branch=w_branch[l], w_out=w_out[l],
                 norm_xa_g=norm_xa_g[l], xa_wq=xa_wq[l], xa_wo=xa_wo[l], norm_ffn_g=norm_ffn_g[l])
        j = l // 2
        if l % 2 == 0:
            P['ffn'] = (ffn_w_gate[j], ffn_w_up[j], ffn_w_down[j])
        else:
            P['moe'] = (moe_router_w[j], moe_router_b[j], moe_w_gate[j], moe_w_up[j], moe_w_down[j])

        mem_n = rms_norm(mem_prompt, norm_mem_g[l])
        mk = (mem_n @ xa_wk[l]).reshape(B, M, XA_HEADS, XA_HD)
        mv = (mem_n @ xa_wv[l]).reshape(B, M, XA_HEADS, XA_HD)
        hp, st_p = decoder_layer(hp, pos_p, l, P, mk, mv, lru_buf0, lru_h0, gdn_buf0, gdn_s0,
                                 functools.partial(dsa_prompt_attend, pos=pos_p),
                                 functools.partial(diff_prompt_attend, pos=pos_p))
        new_p.append(st_p)
        mem_ks.append(mk)
        mem_vs.append(mv)

        hs, st_s = decoder_layer(hs, pos_s, l, P, cache_mem_k[l], cache_mem_v[l], state_lru_conv[l], state_lru_h[l],
                                 state_gdn_conv[l], state_gdn_s[l],
                                 functools.partial(dsa_sample_attend, pos=pos_s, layer=l, pool_k=cache_dsa_k,
                                                   pool_v=cache_dsa_v, pool_ik=cache_idx_k, page_table=page_table),
                                 functools.partial(diff_sample_attend, pos=pos_s, layer=l, pool_k=cache_diff_k,
                                                   pool_v=cache_diff_v, page_table=page_table))
        new_s.append(st_s)

    y_prompt = rms_norm(hp, final_norm_g)
    y_sample = rms_norm(hs, final_norm_g)
    dsa_k_p, dsa_k_s = stack_state(new_p, 'dsa_k'), stack_state(new_s, 'dsa_k')
    dsa_v_p, dsa_v_s = stack_state(new_p, 'dsa_v'), stack_state(new_s, 'dsa_v')
    idx_k_p, idx_k_s = stack_state(new_p, 'idx_k'), stack_state(new_s, 'idx_k')
    diff_k_p, diff_k_s = stack_state(new_p, 'diff_k'), stack_state(new_s, 'diff_k')
    diff_v_p, diff_v_s = stack_state(new_p, 'diff_v'), stack_state(new_s, 'diff_v')
    lru_h_p, lru_h_s = stack_state(new_p, 'lru_h'), stack_state(new_s, 'lru_h')
    lru_conv_p, lru_conv_s = stack_state(new_p, 'lru_conv'), stack_state(new_s, 'lru_conv')
    gdn_s_p, gdn_s_s = stack_state(new_p, 'gdn_s'), stack_state(new_s, 'gdn_s')
    gdn_conv_p, gdn_conv_s = stack_state(new_p, 'gdn_conv'), stack_state(new_s, 'gdn_conv')
    mem_k_p = jnp.stack(mem_ks, axis=0)
    mem_v_p = jnp.stack(mem_vs, axis=0)
    return (y_prompt, y_sample, dsa_k_p, dsa_k_s, dsa_v_p, dsa_v_s, idx_k_p, idx_k_s, diff_k_p, diff_k_s,
            diff_v_p, diff_v_s, lru_h_p, lru_h_s, lru_conv_p, lru_conv_s, gdn_s_p, gdn_s_s, gdn_conv_p, gdn_conv_s,
            mem_k_p, mem_v_p)
```

```python
import functools
import math

import jax
import jax.numpy as jnp
from jax import lax
from jax.experimental import pallas as pl
from jax.experimental.pallas import tpu as pltpu

F32 = jnp.float32
BF16 = jnp.bfloat16
I32 = jnp.int32

D_MODEL = 1024
DEPTH = 2
PAGE_SIZE = 128
MIX_WIDTH = 512
LRU_BLOCKS = 8
LRU_BLOCK = MIX_WIDTH // LRU_BLOCKS
LRU_C = 8.0
CONV_W = 4
DSA_HEADS = 8
DSA_KV_HEADS = 2
DSA_HD = 64
IDX_HEADS = 4
IDX_HD = 64
TOPK_MAX = 256
GDN_HEADS = 4
GDN_HD = 128
GDN_CHUNK = 64
DIFF_HEADS = 4
DIFF_HD = 64
XA_HEADS = 4
XA_HD = 128
N_EXPERTS = 8
ROPE_THETA = 10000.0
EPS = 1e-6
N_BRANCH = 4
GDN_QKV = 3 * GDN_HEADS * GDN_HD

LANES = 128
SUBLANES = 8
NEG_BIG = -1e30
INT_MIN = -2 ** 31
VMEM_LIMIT = 48 * 1024 * 1024

C_GATE = 0
C_LRUX = 4096
C_LRUG = 4608
C_DSAQ = 5120
C_DIFQ = 5632
C_DIFK = 6144
C_DIFV = 6656
C_GDNZ = 7168
C_GDNQKV = 7680
C_IDXQ = 9216
C_DSAK = 9472
C_DSAV = 9600
C_MISC = 9728
H_COLS = 9856
M_IDXK = 0
M_IDXW = 64
M_GDNB = 68
M_GDNA = 72

O_LRUX, O_LRUG, O_DSAQ, O_DSAK, O_DSAV, O_IDXQ, O_IDXK, O_IDXW = 0, 512, 1024, 1536, 1664, 1792, 2048, 2112
O_GDNQKV, O_GDNZ, O_GDNB, O_GDNA, O_DIFQ, O_DIFK, O_DIFV, O_GATE = 2116, 3652, 4164, 4168, 4172, 4684, 5196, 5708


def _cparams(*sem):
    return pltpu.CompilerParams(dimension_semantics=sem, vmem_limit_bytes=VMEM_LIMIT)


def _tile(n, pref):
    return pref if n % pref == 0 else n


def _softplus(x):
    return jnp.maximum(x, 0.0) + jnp.log1p(jnp.exp(-jnp.abs(x)))


def _sigmoid(x):
    return 1.0 / (1.0 + jnp.exp(-x))


def _silu(x):
    return x * _sigmoid(x)


def _dot(a, b):
    return jnp.dot(a, b, preferred_element_type=F32)


def _dot_nt(a, b, precision=None):
    return lax.dot_general(a, b, (((1,), (1,)), ((), ())), preferred_element_type=F32, precision=precision)


def _dot_hi(a, b):
    return jnp.dot(a, b, preferred_element_type=F32, precision=lax.Precision.HIGHEST)


def _rms_rows(x, g):
    return x * lax.rsqrt(jnp.mean(x * x, axis=-1, keepdims=True) + EPS) * g


def _rmsnorm_body(x_ref, g_ref, o_ref):
    o_ref[...] = _rms_rows(x_ref[...], g_ref[...]).astype(o_ref.dtype)


def _rmsnorm(x, g, out_dtype):
    n, d = x.shape
    tm = _tile(n, 512)
    return pl.pallas_call(
        _rmsnorm_body,
        grid=(n // tm,),
        in_specs=[pl.BlockSpec((tm, d), lambda i: (i, 0)), pl.BlockSpec((1, d), lambda i: (0, 0))],
        out_specs=pl.BlockSpec((tm, d), lambda i: (i, 0)),
        out_shape=jax.ShapeDtypeStruct((n, d), out_dtype),
        compiler_params=_cparams("parallel"),
        name="rmsnorm",
    )(x, g.reshape(1, d))


def _matmul_body(a_ref, w_ref, o_ref):
    o_ref[...] = _dot(a_ref[...], w_ref[...]).astype(o_ref.dtype)


def _matmul(a, w, tn, out_dtype=F32):
    n, k = a.shape
    c = w.shape[1]
    tm = _tile(n, 512)
    return pl.pallas_call(
        _matmul_body,
        grid=(c // tn, n // tm),
        in_specs=[pl.BlockSpec((tm, k), lambda j, i: (i, 0)), pl.BlockSpec((k, tn), lambda j, i: (0, j))],
        out_specs=pl.BlockSpec((tm, tn), lambda j, i: (i, j)),
        out_shape=jax.ShapeDtypeStruct((n, c), out_dtype),
        compiler_params=_cparams("parallel", "parallel"),
        name="matmul",
    )(a, w)


def _rope_block(x, cos, sin):
    lane = lax.broadcasted_iota(I32, x.shape, 1)
    first = (lane % 64) < 32
    partner = jnp.where(first, pltpu.roll(x, LANES - 32, 1), pltpu.roll(x, 32, 1))
    return x * cos + partner * sin


def _rope_wide(x, cos, sin):
    return [_rope_block(x[:, c * LANES:(c + 1) * LANES], cos, sin) for c in range(x.shape[1] // LANES)]


def _rope_body(cos_ref, sin_ref, dq_ref, fq_ref, fk_ref, iq_ref, dk_ref, misc_ref,
               dq_o, fq_o, fk_o, iq_o, dk_o, ik_o):
    cos = cos_ref[...]
    sin = sin_ref[...]
    for src, dst in ((dq_ref, dq_o), (fq_ref, fq_o), (fk_ref, fk_o), (iq_ref, iq_o), (dk_ref, dk_o)):
        for c, blk in enumerate(_rope_wide(src[...], cos, sin)):
            dst[:, c * LANES:(c + 1) * LANES] = blk.astype(dst.dtype)
    ik_o[...] = _rope_block(misc_ref[...], cos, sin)[:, M_IDXK:M_IDXK + IDX_HD]


def _rope_call(h, cos, sin, tab_tiles):
    n = h.shape[0]
    tm = _tile(min(n, cos.shape[0]), 512)
    nt = cos.shape[0] // tm

    def col(width, off):
        return pl.BlockSpec((tm, width), lambda i: (i, off // width))

    def out(width):
        return pl.BlockSpec((tm, width), lambda i: (i, 0))

    tab = pl.BlockSpec((tm, LANES), lambda i: (i % nt, 0))
    return pl.pallas_call(
        _rope_body,
        grid=(n // tm,),
        in_specs=[tab, tab, col(512, C_DSAQ), col(512, C_DIFQ), col(512, C_DIFK), col(256, C_IDXQ),
                  col(128, C_DSAK), col(128, C_MISC)],
        out_specs=[out(512), out(512), out(512), out(256), out(128), out(IDX_HD)],
        out_shape=[jax.ShapeDtypeStruct((n, 512), BF16), jax.ShapeDtypeStruct((n, 512), BF16),
                   jax.ShapeDtypeStruct((n, 512), F32), jax.ShapeDtypeStruct((n, 256), F32),
                   jax.ShapeDtypeStruct((n, 128), F32), jax.ShapeDtypeStruct((n, IDX_HD), F32)],
        compiler_params=_cparams("parallel"),
        name="rope",
    )(cos, sin, h, h, h, h, h, h)


def _conv_tile(x_ref, buf_ref, cw_ref, xcat, t, tc):
    @pl.when(t == 0)
    def _():
        xcat[0:SUBLANES, :] = buf_ref[...]

    @pl.when(t > 0)
    def _():
        xcat[0:SUBLANES, :] = xcat[tc:tc + SUBLANES, :]

    xcat[SUBLANES:SUBLANES + tc, :] = x_ref[...]
    cw = cw_ref[...]
    y = x_ref[...] * cw[CONV_W - 1:CONV_W, :]
    for j in range(CONV_W - 1):
        off = SUBLANES - (CONV_W - 1) + j
        y = y + xcat[off:off + tc, :] * cw[j:j + 1, :]
    return y


def _lru_body(x_ref, gate_ref, cw_ref, cb_ref, wa_ref, ba_ref, wx_ref, bx_ref, lam_ref, buf_ref, h0_ref,
              o_ref, hl_ref, xcat, hc, *, tc, reset_first):
    t = pl.program_id(1)

    @pl.when(t == 0)
    def _():
        hc[...] = h0_ref[...]

    xa = _conv_tile(x_ref, buf_ref, cw_ref, xcat, t, tc) + cb_ref[...]
    xb = xa.astype(BF16)
    r = _sigmoid(_dot(xb, wa_ref[...]) + ba_ref[...])
    gi = _sigmoid(_dot(xb, wx_ref[...]) + bx_ref[...])
    log_a = -LRU_C * r * _softplus(-lam_ref[...])
    a = jnp.exp(log_a)
    mult = jnp.sqrt(-jnp.tanh(log_a) * (a * a + 1.0))
    row = lax.broadcasted_iota(I32, a.shape, 0)
    if reset_first:
        is0 = jnp.logical_and(row == 0, t == 0)
        a = jnp.where(is0, 0.0, a)
        mult = jnp.where(is0, 1.0, mult)
    u = mult * gi * xa
    d = 1
    while d < tc:
        keep = row >= d
        a_sh = jnp.where(keep, pltpu.roll(a, d, 0), 1.0)
        u_sh = jnp.where(keep, pltpu.roll(u, d, 0), 0.0)
        u = a * u_sh + u
        a = a * a_sh
        d *= 2
    h = a * hc[...] + u
    h_last = h[tc - 1:tc, :]
    hc[...] = h_last
    hl_ref[...] = h_last
    o_ref[...] = (h * jax.nn.gelu(gate_ref[...])).astype(o_ref.dtype)


def _lru_call(h3, wl, buf, h0, reset_first):
    b, t, _ = h3.shape
    tc = _tile(t, 256)
    w = MIX_WIDTH
    row = pl.BlockSpec((1, w), lambda i, j: (0, 0))
    sq = pl.BlockSpec((w, w), lambda i, j: (0, 0))
    return pl.pallas_call(
        functools.partial(_lru_body, tc=tc, reset_first=reset_first),
        grid=(b, t // tc),
        in_specs=[pl.BlockSpec((None, tc, w), lambda i, j: (i, j, C_LRUX // w)),
                  pl.BlockSpec((None, tc, w), lambda i, j: (i, j, C_LRUG // w)),
                  pl.BlockSpec((SUBLANES, w), lambda i, j: (0, 0)), row, sq, row, sq, row, row,
                  pl.BlockSpec((None, SUBLANES, w), lambda i, j: (i, 0, 0)),
                  pl.BlockSpec((None, 1, w), lambda i, j: (i, 0, 0))],
        out_specs=[pl.BlockSpec((None, tc, w), lambda i, j: (i, j, 0)),
                   pl.BlockSpec((None, 1, w), lambda i, j: (i, 0, 0))],
        out_shape=[jax.ShapeDtypeStruct((b, t, w), BF16), jax.ShapeDtypeStruct((b, 1, w), F32)],
        scratch_shapes=[pltpu.VMEM((tc + SUBLANES, w), F32), pltpu.VMEM((1, w), F32)],
        compiler_params=_cparams("parallel", "arbitrary"),
        name="rg_lru",
    )(h3, h3, wl["lru_cw"], wl["lru_cb"], wl["lru_wa"], wl["lru_ba"], wl["lru_wx"], wl["lru_bx"], wl["lru_lam"],
      buf, h0)


def _gdn_body(qkv_ref, z_ref, misc_ref, cw_ref, alog_ref, dtb_ref, ng_ref, buf_ref, s0_ref,
              o_ref, so_ref, xcat, y_scr, state, *, tc, ck):
    t = pl.program_id(1)
    hd = GDN_HD

    @pl.when(t == 0)
    def _():
        state[...] = s0_ref[...]

    y = _conv_tile(qkv_ref, buf_ref, cw_ref, xcat, t, tc)
    y_scr[...] = _silu(y)

    ri = lax.broadcasted_iota(I32, (ck, ck), 0)
    ci = lax.broadcasted_iota(I32, (ck, ck), 1)
    tril = ri >= ci
    strict = ri > ci
    ltri = jnp.where(tril, 1.0, 0.0).astype(F32)
    eye = jnp.where(ri == ci, 1.0, 0.0).astype(F32)
    ones = jnp.ones((ck, ck), F32)
    n_sq = max(int(math.log2(ck)) - 1, 0)

    def chunk(c, carry):
        r0 = pl.multiple_of(c * ck, ck)
        misc = misc_ref[pl.ds(r0, ck), :]
        beta_all = _sigmoid(misc)
        g_all = -jnp.exp(alog_ref[...]) * _softplus(misc + dtb_ref[...])
        for hh in range(GDN_HEADS):
            q = y_scr[pl.ds(r0, ck), hh * hd:(hh + 1) * hd]
            k = y_scr[pl.ds(r0, ck), (GDN_HEADS + hh) * hd:(GDN_HEADS + hh + 1) * hd]
            v = y_scr[pl.ds(r0, ck), (2 * GDN_HEADS + hh) * hd:(2 * GDN_HEADS + hh + 1) * hd]
            q = q * lax.rsqrt(jnp.sum(q * q, axis=-1, keepdims=True) + EPS) * (hd ** -0.5)
            k = k * lax.rsqrt(jnp.sum(k * k, axis=-1, keepdims=True) + EPS)
            beta = jnp.broadcast_to(beta_all[:, M_GDNB + hh:M_GDNB + hh + 1], (ck, hd))
            g128 = jnp.broadcast_to(g_all[:, M_GDNA + hh:M_GDNA + hh + 1], (ck, hd))
            gsq = g128[:, :ck]
            gc128 = _dot_hi(ltri, g128)
            gc_col = gc128[:, :ck]
            gc_row = _dot_hi(ones, jnp.where(ri <= ci, gsq, 0.0))
            decay = jnp.where(tril, jnp.exp(jnp.where(tril, gc_col - gc_row, 0.0)), 0.0)
            kb = k * beta
            vb = v * beta
            low = jnp.where(strict, _dot_nt(kb, k, lax.Precision.HIGHEST) * decay, 0.0)
            mp = -low
            inv = eye + mp
            for _ in range(n_sq):
                mp = _dot_hi(mp, mp)
                inv = inv + _dot_hi(inv, mp)
            egc = jnp.exp(gc128)
            u = _dot_hi(inv, vb)
            w = _dot_hi(inv, kb * egc)
            intra = jnp.where(tril, _dot_nt(q, k, lax.Precision.HIGHEST) * decay, 0.0)
            s = state[hh]
            v_new = u - _dot_hi(w, s)
            o = _dot_hi(q * egc, s) + _dot_hi(intra, v_new)
            g_last = gc128[ck - 1:ck, :]
            kd = k * jnp.exp(g_last - gc128)
            state[hh] = s * jnp.exp(g_last) + lax.dot_general(
                kd, v_new, (((0,), (0,)), ((), ())), preferred_element_type=F32, precision=lax.Precision.HIGHEST)
            on = _rms_rows(o, ng_ref[...])
            zz = z_ref[pl.ds(r0, ck), hh * hd:(hh + 1) * hd]
            o_ref[pl.ds(r0, ck), hh * hd:(hh + 1) * hd] = (on * _silu(zz)).astype(o_ref.dtype)
        return carry

    lax.fori_loop(0, tc // ck, chunk, 0)
    so_ref[...] = state[...]


def _gdn_call(h3, wl, buf, s0):
    b, t, _ = h3.shape
    ck = min(GDN_CHUNK, t)
    assert t % ck == 0
    tc = _tile(t, 256)
    w = GDN_QKV
    row = pl.BlockSpec((1, LANES), lambda i, j: (0, 0))
    st = pl.BlockSpec((None, GDN_HEADS, GDN_HD, GDN_HD), lambda i, j: (i, 0, 0, 0))
    return pl.pallas_call(
        functools.partial(_gdn_body, tc=tc, ck=ck),
        grid=(b, t // tc),
        in_specs=[pl.BlockSpec((None, tc, w), lambda i, j: (i, j, C_GDNQKV // w)),
                  pl.BlockSpec((None, tc, MIX_WIDTH), lambda i, j: (i, j, C_GDNZ // MIX_WIDTH)),
                  pl.BlockSpec((None, tc, LANES), lambda i, j: (i, j, C_MISC // LANES)),
                  pl.BlockSpec((SUBLANES, w), lambda i, j: (0, 0)), row, row, row,
                  pl.BlockSpec((None, SUBLANES, w), lambda i, j: (i, 0, 0)), st],
        out_specs=[pl.BlockSpec((None, tc, MIX_WIDTH), lambda i, j: (i, j, 0)), st],
        out_shape=[jax.ShapeDtypeStruct((b, t, MIX_WIDTH), BF16),
                   jax.ShapeDtypeStruct((b, GDN_HEADS, GDN_HD, GDN_HD), F32)],
        scratch_shapes=[pltpu.VMEM((tc + SUBLANES, w), F32), pltpu.VMEM((tc, w), F32),
                        pltpu.VMEM((GDN_HEADS, GDN_HD, GDN_HD), F32)],
        compiler_params=_cparams("parallel", "arbitrary"),
        name="gated_deltanet",
    )(h3, h3, h3, wl["gdn_cw"], wl["gdn_alog"], wl["gdn_dtb"], wl["gdn_ng"], buf, s0)


def _softmax_step(s, sel, v, m_ref, l_ref, acc_ref, idx):
    if sel is not None:
        s = jnp.where(sel, s, NEG_BIG)
    m_old = m_ref[idx]
    m_new = jnp.maximum(m_old, jnp.max(s, axis=-1, keepdims=True))
    alpha = jnp.exp(m_old - m_new)
    p = jnp.exp(s - m_new)
    if sel is not None:
        p = jnp.where(sel, p, 0.0)
    l_ref[idx] = alpha * l_ref[idx] + jnp.sum(p, axis=-1, keepdims=True)
    acc_ref[idx] = alpha * acc_ref[idx] + _dot(p.astype(BF16), v)
    m_ref[idx] = m_new


def _diff_lambda(lam_ref, lam_init):
    lm = lam_ref[...]
    e1 = jnp.exp(jnp.sum(lm[0:1, :] * lm[1:2, :], axis=-1, keepdims=True))
    e2 = jnp.exp(jnp.sum(lm[2:3, :] * lm[3:4, :], axis=-1, keepdims=True))
    return e1 - e2 + lam_init


def _diff_prompt_body(lam_ref, sg_ref, q_ref, k_ref, v_ref, o_ref, m_scr, l_scr, acc_scr, *, tq, lam_init):
    qi = pl.program_id(2)
    kj = pl.program_id(3)

    @pl.when(kj == 0)
    def _():
        m_scr[...] = jnp.full(m_scr.shape, NEG_BIG, F32)
        l_scr[...] = jnp.zeros(l_scr.shape, F32)
        acc_scr[...] = jnp.zeros(acc_scr.shape, F32)

    @pl.when(kj <= qi)
    def _():
        q = q_ref[...]
        k = k_ref[...].astype(BF16)
        v = v_ref[...].astype(BF16)
        row = qi * tq + lax.broadcasted_iota(I32, (tq, tq), 0)
        col = kj * tq + lax.broadcasted_iota(I32, (tq, tq), 1)
        allowed = col <= row
        for mm in range(2):
            s = _dot_nt(q[:, mm * DIFF_HD:(mm + 1) * DIFF_HD], k[:, mm * DIFF_HD:(mm + 1) * DIFF_HD]) * (DIFF_HD ** -0.5)
            _softmax_step(s, allowed, v, m_scr, l_scr, acc_scr, mm)

    @pl.when(kj == qi)
    def _():
        lam = _diff_lambda(lam_ref, lam_init)
        o = acc_scr[0] / l_scr[0] - lam * (acc_scr[1] / l_scr[1])
        o_ref[...] = (_rms_rows(o, sg_ref[...]) * (1.0 - lam_init)).astype(o_ref.dtype)


def _diff_prompt_call(fq3, fk3, h3, wl, lam_init):
    b, t, _ = fq3.shape
    tq = _tile(t, 512)
    nq = t // tq
    hv = 2 * DIFF_HD
    return pl.pallas_call(
        functools.partial(_diff_prompt_body, tq=tq, lam_init=lam_init),
        grid=(b, DIFF_HEADS, nq, nq),
        in_specs=[pl.BlockSpec((4, DIFF_HD), lambda i, h, q, k: (0, 0)),
                  pl.BlockSpec((1, hv), lambda i, h, q, k: (0, 0)),
                  pl.BlockSpec((None, tq, hv), lambda i, h, q, k: (i, q, h)),
                  pl.BlockSpec((None, tq, hv), lambda i, h, q, k: (i, jnp.minimum(k, q), h)),
                  pl.BlockSpec((None, tq, hv), lambda i, h, q, k: (i, jnp.minimum(k, q), C_DIFV // hv + h))],
        out_specs=pl.BlockSpec((None, tq, hv), lambda i, h, q, k: (i, q, h)),
        out_shape=jax.ShapeDtypeStruct((b, t, MIX_WIDTH), BF16),
        scratch_shapes=[pltpu.VMEM((2, tq, 1), F32), pltpu.VMEM((2, tq, 1), F32), pltpu.VMEM((2, tq, hv), F32)],
        compiler_params=_cparams("parallel", "parallel", "parallel", "arbitrary"),
        name="diff_attn_prompt",
    )(wl["diff_lam"], wl["diff_sg"], fq3, fk3, h3)


def _diff_sample_body(pt_ref, lam_ref, sg_ref, q_ref, kn_ref, vn_ref, *rest, tq, npg, lam_init):
    kp = rest[:npg]
    vp = rest[npg:2 * npg]
    o_ref, qblk, m_scr, l_scr, acc_scr = rest[2 * npg:]
    s_id = pl.program_id(1)
    n_steps = pl.num_programs(1)
    nrow = 2 * DIFF_HEADS * tq
    hv = 2 * DIFF_HD

    @pl.when(s_id == 0)
    def _():
        m_scr[...] = jnp.full(m_scr.shape, NEG_BIG, F32)
        l_scr[...] = jnp.zeros(l_scr.shape, F32)
        acc_scr[...] = jnp.zeros(acc_scr.shape, F32)
        qblk[...] = jnp.zeros(qblk.shape, F32)
        q = q_ref[...].astype(F32)
        for j in range(2 * DIFF_HEADS):
            qblk[j * tq:(j + 1) * tq, j * DIFF_HD:(j + 1) * DIFF_HD] = q[:, j * DIFF_HD:(j + 1) * DIFF_HD]

    qb = qblk[...].astype(BF16)

    def update(s, sel, vs):
        if sel is not None:
            s = jnp.where(sel, s, NEG_BIG)
        m_old = m_scr[...]
        m_new = jnp.maximum(m_old, jnp.max(s, axis=-1, keepdims=True))
        alpha = jnp.exp(m_old - m_new)
        p = jnp.exp(s - m_new)
        if sel is not None:
            p = jnp.where(sel, p, 0.0)
        l_scr[...] = alpha * l_scr[...] + jnp.sum(p, axis=-1, keepdims=True)
        pb = p.astype(BF16)
        pv = None
        off = 0
        for vblk in vs:
            n_i = vblk.shape[0]
            part = _dot(pb[:, off:off + n_i], vblk)
            pv = part if pv is None else pv + part
            off += n_i
        acc_scr[...] = alpha * acc_scr[...] + pv
        m_scr[...] = m_new

    @pl.when(s_id < n_steps - 1)
    def _():
        scores = [_dot_nt(qb, kp[p][...].astype(BF16)) for p in range(npg)]
        s = jnp.concatenate(scores, axis=-1) * (DIFF_HD ** -0.5)
        update(s, None, [vp[p][...].astype(BF16) for p in range(npg)])

    @pl.when(s_id == n_steps - 1)
    def _():
        s = _dot_nt(qb, kn_ref[...].astype(BF16)) * (DIFF_HD ** -0.5)
        qrow = lax.broadcasted_iota(I32, (nrow, tq), 0) % tq
        kcol = lax.broadcasted_iota(I32, (nrow, tq), 1)
        update(s, kcol <= qrow, [vn_ref[...].astype(BF16)])
        lam = _diff_lambda(lam_ref, lam_init)
        acc = acc_scr[...] / l_scr[...]
        for hh in range(DIFF_HEADS):
            r1 = (2 * hh) * tq
            r2 = (2 * hh + 1) * tq
            o = acc[r1:r1 + tq, hh * hv:(hh + 1) * hv] - lam * acc[r2:r2 + tq, hh * hv:(hh + 1) * hv]
            o_ref[:, hh * hv:(hh + 1) * hv] = (_rms_rows(o, sg_ref[...]) * (1.0 - lam_init)).astype(o_ref.dtype)


def _diff_sample_call(fq3, fk3, h3, pool_k, pool_v, page_table, layer, wl, lam_init):
    db, t, _ = fq3.shape
    n_pages = page_table.shape[1]
    npg = 8 if n_pages % 8 == 0 else 1
    n_steps = n_pages // npg + 1
    w = MIX_WIDTH
    nrow = 2 * DIFF_HEADS * t

    def page_spec(p):
        def imap(i, s, pt):
            return (layer, pt[i, jnp.minimum(s, n_steps - 2) * npg + p], 0, 0)
        return pl.BlockSpec((None, None, PAGE_SIZE, w), imap)

    grid_spec = pltpu.PrefetchScalarGridSpec(
        num_scalar_prefetch=1,
        grid=(db, n_steps),
        in_specs=[pl.BlockSpec((4, DIFF_HD), lambda i, s, pt: (0, 0)),
                  pl.BlockSpec((1, 2 * DIFF_HD), lambda i, s, pt: (0, 0)),
                  pl.BlockSpec((None, t, w), lambda i, s, pt: (i, 0, 0)),
                  pl.BlockSpec((None, t, w), lambda i, s, pt: (i, 0, 0)),
                  pl.BlockSpec((None, t, w), lambda i, s, pt: (i, 0, C_DIFV // w))]
                 + [page_spec(p) for p in range(npg)] + [page_spec(p) for p in range(npg)],
        out_specs=pl.BlockSpec((None, t, w), lambda i, s, pt: (i, 0, 0)),
        scratch_shapes=[pltpu.VMEM((nrow, w), F32), pltpu.VMEM((nrow, 1), F32), pltpu.VMEM((nrow, 1), F32),
                        pltpu.VMEM((nrow, w), F32)],
    )
    return pl.pallas_call(
        functools.partial(_diff_sample_body, tq=t, npg=npg, lam_init=lam_init),
        grid_spec=grid_spec,
        out_shape=jax.ShapeDtypeStruct((db, t, w), BF16),
        compiler_params=_cparams("parallel", "arbitrary"),
        name="diff_attn_sample",
    )(page_table, wl["diff_lam"], wl["diff_sg"], fq3, fk3, h3, *([pool_k] * npg), *([pool_v] * npg))


def _sortable(score):
    bits = pltpu.bitcast(jnp.where(score == 0.0, 0.0, score), I32)
    return bits ^ ((bits >> 31) & 0x7FFFFFFF)


def _search_threshold(count_ge, rows, n_sel):
    def bit_body(i, t_u):
        cand_u = t_u | (jnp.int32(1) << (31 - i))
        cnt = count_ge(cand_u ^ INT_MIN)
        return jnp.where(cnt >= n_sel, cand_u, t_u)

    t_u = lax.fori_loop(0, 32, bit_body, jnp.zeros((rows, 1), I32))
    return t_u ^ INT_MIN


def _lane_fold(x):
    acc = x[:, 0:LANES]
    for c in range(1, x.shape[1] // LANES):
        acc = acc + x[:, c * LANES:(c + 1) * LANES]
    return acc


def _dsa_prompt_body(iq_ref, misc_ref, ik_ref, q_ref, k_ref, v_ref, o_ref, keys_scr, m_scr, l_scr, acc_scr,
                     *, tq, ck, n_sel):
    qi = pl.program_id(1)
    nck = (qi * tq + tq + ck - 1) // ck
    iq = iq_ref[...]
    wts = misc_ref[...][:, M_IDXW:M_IDXW + IDX_HEADS] * (IDX_HD ** -0.5 * IDX_HEADS ** -0.5)
    row_pos = qi * tq + lax.broadcasted_iota(I32, (tq, ck), 0)
    lane_pos = lax.broadcasted_iota(I32, (tq, ck), 1)

    def score_chunk(c, carry):
        k0 = pl.multiple_of(c * ck, ck)
        ik = ik_ref[pl.ds(k0, ck), :]
        sc = jnp.zeros((tq, ck), F32)
        for hh in range(IDX_HEADS):
            s = _dot_nt(iq[:, hh * IDX_HD:(hh + 1) * IDX_HD], ik, lax.Precision.HIGHEST)
            sc = sc + wts[:, hh:hh + 1] * jnp.maximum(s, 0.0)
        keys_scr[c] = jnp.where(c * ck + lane_pos <= row_pos, _sortable(sc), INT_MIN)
        return carry

    lax.fori_loop(0, nck, score_chunk, 0)

    def count(pred):
        def body(c, acc):
            return acc + _lane_fold(jnp.where(pred(keys_scr[c]), 1, 0).astype(I32))
        acc = lax.fori_loop(0, nck, body, jnp.zeros((tq, LANES), I32))
        return jnp.sum(acc, axis=-1, keepdims=True)

    thr = _search_threshold(lambda cand: count(lambda kk: kk >= cand), tq, n_sel)
    need = (n_sel - count(lambda kk: kk > thr)).astype(F32)

    m_scr[...] = jnp.full(m_scr.shape, NEG_BIG, F32)
    l_scr[...] = jnp.zeros(l_scr.shape, F32)
    acc_scr[...] = jnp.zeros(acc_scr.shape, F32)
    ui = lax.broadcasted_iota(I32, (ck, ck), 0)
    uj = lax.broadcasted_iota(I32, (ck, ck), 1)
    before = jnp.where(ui < uj, 1.0, 0.0).astype(BF16)
    hpg = DSA_HEADS // DSA_KV_HEADS

    def attend_chunk(c, eq_seen):
        k0 = pl.multiple_of(c * ck, ck)
        kk = keys_scr[c]
        allowed = c * ck + lane_pos <= row_pos
        eq = jnp.logical_and(kk == thr, allowed)
        eqf = jnp.where(eq, 1.0, 0.0)
        rank = _dot(eqf.astype(BF16), before) + eq_seen
        sel = jnp.logical_or(kk > thr, jnp.logical_and(eq, rank < need))
        for g in range(DSA_KV_HEADS):
            kg = k_ref[pl.ds(k0, ck), g * DSA_HD:(g + 1) * DSA_HD].astype(BF16)
            vg = v_ref[pl.ds(k0, ck), g * DSA_HD:(g + 1) * DSA_HD].astype(BF16)
            for hh in range(hpg):
                hd = g * hpg + hh
                s = _dot_nt(q_ref[:, hd * DSA_HD:(hd + 1) * DSA_HD], kg) * (DSA_HD ** -0.5)
                _softmax_step(s, sel, vg, m_scr, l_scr, acc_scr, hd)
        return eq_seen + jnp.sum(eqf, axis=-1, keepdims=True)

    lax.fori_loop(0, nck, attend_chunk, jnp.zeros((tq, 1), F32))
    for hd in range(DSA_HEADS):
        o_ref[:, hd * DSA_HD:(hd + 1) * DSA_HD] = (acc_scr[hd] / l_scr[hd]).astype(o_ref.dtype)


def _dsa_prompt_call(iq3, ik3, dq3, dk3, h3):
    b, t, _ = iq3.shape
    n_sel = min(TOPK_MAX, t // 4)
    tq = _tile(t, 128)
    ck = _tile(t, 512)
    full = lambda width, cb: pl.BlockSpec((None, t, width), lambda i, q: (i, 0, cb))
    return pl.pallas_call(
        functools.partial(_dsa_prompt_body, tq=tq, ck=ck, n_sel=n_sel),
        grid=(b, t // tq),
        in_specs=[pl.BlockSpec((None, tq, IDX_HEADS * IDX_HD), lambda i, q: (i, q, 0)),
                  pl.BlockSpec((None, tq, LANES), lambda i, q: (i, q, C_MISC // LANES)),
                  full(IDX_HD, 0),
                  pl.BlockSpec((None, tq, MIX_WIDTH), lambda i, q: (i, q, 0)),
                  full(LANES, 0), full(LANES, C_DSAV // LANES)],
        out_specs=pl.BlockSpec((None, tq, MIX_WIDTH), lambda i, q: (i, q, 0)),
        out_shape=jax.ShapeDtypeStruct((b, t, MIX_WIDTH), BF16),
        scratch_shapes=[pltpu.VMEM((t // ck, tq, ck), I32), pltpu.VMEM((DSA_HEADS, tq, 1), F32),
                        pltpu.VMEM((DSA_HEADS, tq, 1), F32), pltpu.VMEM((DSA_HEADS, tq, DSA_HD), F32)],
        compiler_params=_cparams("parallel", "arbitrary"),
        name="dsa_prompt",
    )(iq3, h3, ik3, dq3, dk3, h3)


def _dsa_select_body(pt_ref, iq_ref, misc_ref, ikn_ref, *rest, tq, npg, n_sel, n_pages):
    pages = rest[:npg]
    sel_ref, keys_scr, iqs = rest[npg:]
    s_id = pl.program_id(1)
    n_steps = pl.num_programs(1)
    wts = misc_ref[...][:, M_IDXW:M_IDXW + IDX_HEADS] * (IDX_HD ** -0.5 * IDX_HEADS ** -0.5)

    @pl.when(s_id == 0)
    def _():
        iq = iq_ref[...]
        for hh in range(IDX_HEADS):
            iqs[hh * tq:(hh + 1) * tq, :] = iq[:, hh * IDX_HD:(hh + 1) * IDX_HD]

    def scores(ik):
        s = jnp.maximum(_dot_nt(iqs[...], ik, lax.Precision.HIGHEST), 0.0)
        sc = jnp.zeros((tq, ik.shape[0]), F32)
        for hh in range(IDX_HEADS):
            sc = sc + wts[:, hh:hh + 1] * s[hh * tq:(hh + 1) * tq, :]
        return sc

    @pl.when(s_id < n_steps - 1)
    def _():
        for p in range(npg):
            keys_scr[s_id * npg + p] = _sortable(scores(pages[p][...]))

    @pl.when(s_id == n_steps - 1)
    def _():
        sc = scores(ikn_ref[...])
        qrow = lax.broadcasted_iota(I32, (tq, tq), 0)
        kcol = lax.broadcasted_iota(I32, (tq, tq), 1)
        new_keys = jnp.where(kcol <= qrow, _sortable(sc), INT_MIN)
        last = jnp.concatenate([new_keys, jnp.full((tq, LANES - tq), INT_MIN, I32)], axis=-1)
        keys_scr[n_pages] = last

        def count(pred):
            def body(c, acc):
                return acc + jnp.where(pred(keys_scr[c]), 1, 0).astype(I32)
            acc = lax.fori_loop(0, n_pages + 1, body, jnp.zeros((tq, LANES), I32))
            return jnp.sum(acc, axis=-1, keepdims=True)

        thr = _search_threshold(lambda cand: count(lambda kk: kk >= cand), tq, n_sel)
        need = (n_sel - count(lambda kk: kk > thr)).astype(F32)
        ui = lax.broadcasted_iota(I32, (LANES, LANES), 0)
        uj = lax.broadcasted_iota(I32, (LANES, LANES), 1)
        before = jnp.where(ui < uj, 1.0, 0.0).astype(BF16)
        lane = lax.broadcasted_iota(I32, (tq, LANES), 1)
        qr = lax.broadcasted_iota(I32, (tq, LANES), 0)

        def emit(c, eq_seen):
            kk = keys_scr[c]
            allowed = jnp.logical_or(c < n_pages, lane <= qr)
            eq = jnp.logical_and(kk == thr, allowed)
            eqf = jnp.where(eq, 1.0, 0.0)
            rank = _dot(eqf.astype(BF16), before) + eq_seen
            sel = jnp.logical_or(kk > thr, jnp.logical_and(eq, rank < need))
            sel_ref[c] = jnp.where(sel, 1.0, 0.0)
            return eq_seen + jnp.sum(eqf, axis=-1, keepdims=True)

        lax.fori_loop(0, n_pages + 1, emit, jnp.zeros((tq, 1), F32))


def _dsa_attend_body(pt_ref, q_ref, kn_ref, vn_ref, selp_ref, seln_ref, *rest, tq, npg):
    kp = rest[:npg]
    vp = rest[npg:2 * npg]
    o_ref, qblk, m_scr, l_scr, acc_scr = rest[2 * npg:]
    s_id = pl.program_id(1)
    n_steps = pl.num_programs(1)
    hpg = DSA_HEADS // DSA_KV_HEADS
    nrow = DSA_HEADS * tq

    @pl.when(s_id == 0)
    def _():
        m_scr[...] = jnp.full(m_scr.shape, NEG_BIG, F32)
        l_scr[...] = jnp.zeros(l_scr.shape, F32)
        acc_scr[...] = jnp.zeros(acc_scr.shape, F32)
        qblk[...] = jnp.zeros(qblk.shape, F32)
        q = q_ref[...].astype(F32)
        for hd in range(DSA_HEADS):
            g = hd // hpg
            qblk[hd * tq:(hd + 1) * tq, g * DSA_HD:(g + 1) * DSA_HD] = q[:, hd * DSA_HD:(hd + 1) * DSA_HD]

    qb = qblk[...].astype(BF16)

    def update(s, sel, v):
        s = jnp.where(sel, s, NEG_BIG)
        m_old = m_scr[...]
        m_new = jnp.maximum(m_old, jnp.max(s, axis=-1, keepdims=True))
        alpha = jnp.exp(m_old - m_new)
        p = jnp.where(sel, jnp.exp(s - m_new), 0.0)
        l_scr[...] = alpha * l_scr[...] + jnp.sum(p, axis=-1, keepdims=True)
        acc_scr[...] = alpha * acc_scr[...] + _dot(p.astype(BF16), v)
        m_scr[...] = m_new

    @pl.when(s_id < n_steps - 1)
    def _():
        for p in range(npg):
            s = _dot_nt(qb, kp[p][...].astype(BF16)) * (DSA_HD ** -0.5)
            sel = jnp.concatenate([selp_ref[p]] * DSA_HEADS, axis=0) > 0.5
            update(s, sel, vp[p][...].astype(BF16))

    @pl.when(s_id == n_steps - 1)
    def _():
        s = _dot_nt(qb, kn_ref[...].astype(BF16)) * (DSA_HD ** -0.5)
        sel = jnp.concatenate([seln_ref[...][:, :tq]] * DSA_HEADS, axis=0) > 0.5
        update(s, sel, vn_ref[...].astype(BF16))
        acc = acc_scr[...] / l_scr[...]
        for hd in range(DSA_HEADS):
            g = hd // hpg
            o_ref[:, hd * DSA_HD:(hd + 1) * DSA_HD] = acc[hd * tq:(hd + 1) * tq,
                                                          g * DSA_HD:(g + 1) * DSA_HD].astype(o_ref.dtype)


def _dsa_sample_call(iq3, ik3, dq3, dk3, h3, pool_k, pool_v, pool_ik, page_table, layer):
    db, t, _ = iq3.shape
    n_pages = page_table.shape[1]
    n_keys = n_pages * PAGE_SIZE + t
    n_sel = min(TOPK_MAX, n_keys // 4)
    npg = 8 if n_pages % 8 == 0 else 1
    n_steps = n_pages // npg + 1

    def page_spec(p, width):
        def imap(i, s, pt):
            return (layer, pt[i, jnp.minimum(s, n_steps - 2) * npg + p], 0, 0)
        return pl.BlockSpec((None, None, PAGE_SIZE, width), imap)

    sel = pl.pallas_call(
        functools.partial(_dsa_select_body, tq=t, npg=npg, n_sel=n_sel, n_pages=n_pages),
        grid_spec=pltpu.PrefetchScalarGridSpec(
            num_scalar_prefetch=1,
            grid=(db, n_steps),
            in_specs=[pl.BlockSpec((None, t, IDX_HEADS * IDX_HD), lambda i, s, pt: (i, 0, 0)),
                      pl.BlockSpec((None, t, LANES), lambda i, s, pt: (i, 0, C_MISC // LANES)),
                      pl.BlockSpec((None, t, IDX_HD), lambda i, s, pt: (i, 0, 0))]
                     + [page_spec(p, IDX_HD) for p in range(npg)],
            out_specs=pl.BlockSpec((None, n_pages + 1, t, LANES), lambda i, s, pt: (i, 0, 0, 0)),
            scratch_shapes=[pltpu.VMEM((n_pages + 1, t, LANES), I32), pltpu.VMEM((IDX_HEADS * t, IDX_HD), F32)],
        ),
        out_shape=jax.ShapeDtypeStruct((db, n_pages + 1, t, LANES), F32),
        compiler_params=_cparams("parallel", "arbitrary"),
        name="dsa_sample_select",
    )(page_table, iq3, h3, ik3, *([pool_ik] * npg))

    nrow = DSA_HEADS * t
    return pl.pallas_call(
        functools.partial(_dsa_attend_body, tq=t, npg=npg),
        grid_spec=pltpu.PrefetchScalarGridSpec(
            num_scalar_prefetch=1,
            grid=(db, n_steps),
            in_specs=[pl.BlockSpec((None, t, MIX_WIDTH), lambda i, s, pt: (i, 0, 0)),
                      pl.BlockSpec((None, t, LANES), lambda i, s, pt: (i, 0, 0)),
                      pl.BlockSpec((None, t, LANES), lambda i, s, pt: (i, 0, C_DSAV // LANES)),
                      pl.BlockSpec((None, npg, t, LANES), lambda i, s, pt: (i, jnp.minimum(s, n_steps - 2), 0, 0)),
                      pl.BlockSpec((None, None, t, LANES), lambda i, s, pt: (i, n_pages, 0, 0))]
                     + [page_spec(p, LANES) for p in range(npg)] + [page_spec(p, LANES) for p in range(npg)],
            out_specs=pl.BlockSpec((None, t, MIX_WIDTH), lambda i, s, pt: (i, 0, 0)),
            scratch_shapes=[pltpu.VMEM((nrow, LANES), F32), pltpu.VMEM((nrow, 1), F32), pltpu.VMEM((nrow, 1), F32),
                            pltpu.VMEM((nrow, LANES), F32)],
        ),
        out_shape=jax.ShapeDtypeStruct((db, t, MIX_WIDTH), BF16),
        compiler_params=_cparams("parallel", "arbitrary"),
        name="dsa_sample_attend",
    )(page_table, dq3, dk3, h3, sel, sel, *([pool_k] * npg), *([pool_v] * npg))


def _merge_body(x_ref, gate_ref, oa_ref, ob_ref, oc_ref, od_ref, wb_ref, wo_ref, o_ref):
    acc = None
    for bi, br in enumerate((oa_ref, ob_ref, oc_ref, od_ref)):
        proj = _dot(br[...], wb_ref[bi])
        term = _sigmoid(gate_ref[:, bi * D_MODEL:(bi + 1) * D_MODEL]) * proj
        acc = term if acc is None else acc + term
    o_ref[...] = x_ref[...] + _dot(acc.astype(BF16), wo_ref[...])


def _merge_call(x, h, branches, wl):
    n, d = x.shape
    tm = _tile(n, 256)
    br = pl.BlockSpec((tm, MIX_WIDTH), lambda i: (i, 0))
    return pl.pallas_call(
        _merge_body,
        grid=(n // tm,),
        in_specs=[pl.BlockSpec((tm, d), lambda i: (i, 0)),
                  pl.BlockSpec((tm, N_BRANCH * d), lambda i: (i, C_GATE // (N_BRANCH * d))),
                  br, br, br, br,
                  pl.BlockSpec((N_BRANCH, MIX_WIDTH, d), lambda i: (0, 0, 0)),
                  pl.BlockSpec((d, d), lambda i: (0, 0))],
        out_specs=pl.BlockSpec((tm, d), lambda i: (i, 0)),
        out_shape=jax.ShapeDtypeStruct((n, d), F32),
        compiler_params=_cparams("parallel"),
        name="merge",
    )(x, h, *branches, wl["w_branch"], wl["w_out"])


def _cross_body(x_ref, g_ref, wq_ref, mk_ref, mv_ref, wo_ref, o_ref):
    x = x_ref[...]
    xn = _rms_rows(x, g_ref[...]).astype(BF16)
    q = _dot(xn, wq_ref[...]).astype(BF16)
    outs = []
    for hh in range(XA_HEADS):
        sl = slice(hh * XA_HD, (hh + 1) * XA_HD)
        s = _dot_nt(q[:, sl], mk_ref[:, sl].astype(BF16)) * (XA_HD ** -0.5)
        p = jnp.exp(s - jnp.max(s, axis=-1, keepdims=True))
        p = p / jnp.sum(p, axis=-1, keepdims=True)
        outs.append(_dot(p.astype(BF16), mv_ref[:, sl].astype(BF16)))
    o = jnp.concatenate(outs, axis=-1).astype(BF16)
    o_ref[...] = x + _dot(o, wo_ref[...])


def _cross_call(x3, wl, mk4, mv4, layer):
    b, t, d = x3.shape
    m = mk4.shape[2]
    w = XA_HEADS * XA_HD
    tm = _tile(t, 512)
    mem = pl.BlockSpec((None, None, m, w), lambda i, j: (layer, i, 0, 0))
    return pl.pallas_call(
        _cross_body,
        grid=(b, t // tm),
        in_specs=[pl.BlockSpec((None, tm, d), lambda i, j: (i, j, 0)),
                  pl.BlockSpec((1, d), lambda i, j: (0, 0)),
                  pl.BlockSpec((d, w), lambda i, j: (0, 0)), mem, mem,
                  pl.BlockSpec((w, d), lambda i, j: (0, 0))],
        out_specs=pl.BlockSpec((None, tm, d), lambda i, j: (i, j, 0)),
        out_shape=jax.ShapeDtypeStruct((b, t, d), F32),
        compiler_params=_cparams("parallel", "parallel"),
        name="cross_attn",
    )(x3, wl["norm_xa_g"], wl["xa_wq"], mk4, mv4, wl["xa_wo"])


def _ffn_body(x_ref, g_ref, wg_ref, wu_ref, wd_ref, o_ref, xn_scr, acc_scr):
    f = pl.program_id(1)

    @pl.when(f == 0)
    def _():
        xn_scr[...] = _rms_rows(x_ref[...], g_ref[...]).astype(BF16)
        acc_scr[...] = x_ref[...]

    xn = xn_scr[...]
    hm = _silu(_dot(xn, wg_ref[...])) * _dot(xn, wu_ref[...])
    acc_scr[...] += _dot(hm.astype(BF16), wd_ref[...])

    @pl.when(f == pl.num_programs(1) - 1)
    def _():
        o_ref[...] = acc_scr[...]


def _ffn_call(x, wl):
    n, d = x.shape
    ff = wl["ffn_wg"].shape[1]
    tm = _tile(n, 512)
    tf = 256
    return pl.pallas_call(
        _ffn_body,
        grid=(n // tm, ff // tf),
        in_specs=[pl.BlockSpec((tm, d), lambda i, f: (i, 0)), pl.BlockSpec((1, d), lambda i, f: (0, 0)),
                  pl.BlockSpec((d, tf), lambda i, f: (0, f)), pl.BlockSpec((d, tf), lambda i, f: (0, f)),
                  pl.BlockSpec((tf, d), lambda i, f: (f, 0))],
        out_specs=pl.BlockSpec((tm, d), lambda i, f: (i, 0)),
        out_shape=jax.ShapeDtypeStruct((n, d), F32),
        scratch_shapes=[pltpu.VMEM((tm, d), BF16), pltpu.VMEM((tm, d), F32)],
        compiler_params=_cparams("parallel", "arbitrary"),
        name="ffn_swiglu",
    )(x, wl["norm_ffn_g"], wl["ffn_wg"], wl["ffn_wu"], wl["ffn_wd"])


def _moe_body(x_ref, g_ref, wr_ref, br_ref, wg_ref, wu_ref, wd_ref, o_ref, xn_scr, gate_scr, acc_scr):
    e = pl.program_id(1)
    f = pl.program_id(2)
    first = jnp.logical_and(e == 0, f == 0)

    @pl.when(first)
    def _():
        xn = _rms_rows(x_ref[...], g_ref[...])
        xn_scr[...] = xn.astype(BF16)
        acc_scr[...] = x_ref[...]
        logits = _dot_hi(xn, wr_ref[...]) + br_ref[...]
        lane = lax.broadcasted_iota(I32, logits.shape, 1)
        logits = jnp.where(lane < N_EXPERTS, logits, -jnp.inf)
        top1 = jnp.max(logits, axis=-1, keepdims=True)
        idx1 = jnp.min(jnp.where(logits == top1, lane, LANES), axis=-1, keepdims=True)
        rest = jnp.where(lane == idx1, -jnp.inf, logits)
        top2 = jnp.max(rest, axis=-1, keepdims=True)
        idx2 = jnp.min(jnp.where(rest == top2, lane, LANES), axis=-1, keepdims=True)
        e2 = jnp.exp(top2 - top1)
        p1 = 1.0 / (1.0 + e2)
        p2 = e2 / (1.0 + e2)
        gate_scr[...] = jnp.where(lane == idx1, p1, 0.0) + jnp.where(lane == idx2, p2, 0.0)

    gates = gate_scr[...]
    lane = lax.broadcasted_iota(I32, gates.shape, 1)
    ge = jnp.sum(jnp.where(lane == e, gates, 0.0), axis=-1, keepdims=True)
    xn = xn_scr[...]
    hm = _silu(_dot(xn, wg_ref[...])) * _dot(xn, wu_ref[...])
    acc_scr[...] += ge * _dot(hm.astype(BF16), wd_ref[...])

    @pl.when(jnp.logical_and(e == pl.num_programs(1) - 1, f == pl.num_programs(2) - 1))
    def _():
        o_ref[...] = acc_scr[...]


def _moe_call(x, wl):
    n, d = x.shape
    ne, _, fe = wl["moe_wg"].shape
    tm = _tile(n, 512)
    tf = _tile(fe, 512)
    return pl.pallas_call(
        _moe_body,
        grid=(n // tm, ne, fe // tf),
        in_specs=[pl.BlockSpec((tm, d), lambda i, e, f: (i, 0)), pl.BlockSpec((1, d), lambda i, e, f: (0, 0)),
                  pl.BlockSpec((d, LANES), lambda i, e, f: (0, 0)), pl.BlockSpec((1, LANES), lambda i, e, f: (0, 0)),
                  pl.BlockSpec((None, d, tf), lambda i, e, f: (e, 0, f)),
                  pl.BlockSpec((None, d, tf), lambda i, e, f: (e, 0, f)),
                  pl.BlockSpec((None, tf, d), lambda i, e, f: (e, f, 0))],
        out_specs=pl.BlockSpec((tm, d), lambda i, e, f: (i, 0)),
        out_shape=jax.ShapeDtypeStruct((n, d), F32),
        scratch_shapes=[pltpu.VMEM((tm, d), BF16), pltpu.VMEM((tm, LANES), F32), pltpu.VMEM((tm, d), F32)],
        compiler_params=_cparams("parallel", "arbitrary", "arbitrary"),
        name="moe_swiglu",
    )(x, wl["norm_ffn_g"], wl["moe_wr"], wl["moe_br"], wl["moe_wg"], wl["moe_wu"], wl["moe_wd"])


def _pad_rows(a, rows):
    return jnp.pad(a, ((0, rows - a.shape[0]), (0, 0)))


def _lane_row(vals, lane0):
    return jnp.zeros((1, LANES), F32).at[0, lane0:lane0 + vals.shape[0]].set(vals.astype(F32))


def _block_diag(w):
    nb, n, _ = w.shape
    eye = jnp.eye(nb, dtype=w.dtype)
    return jnp.einsum("aij,ab->aibj", w, eye).reshape(nb * n, nb * n)


def _layer_weights(l, p):
    w_in = p["w_in"][l]
    d = w_in.shape[0]
    misc = jnp.concatenate([w_in[:, O_IDXK:O_IDXK + IDX_HD + IDX_HEADS], w_in[:, O_GDNB:O_GDNB + 2 * GDN_HEADS],
                            jnp.zeros((d, LANES - IDX_HD - IDX_HEADS - 2 * GDN_HEADS), w_in.dtype)], axis=1)
    w_in_r = jnp.concatenate([
        w_in[:, O_GATE:], w_in[:, O_LRUX:O_DSAK], w_in[:, O_DIFQ:O_GATE], w_in[:, O_GDNZ:O_GDNB],
        w_in[:, O_GDNQKV:O_GDNZ], w_in[:, O_IDXQ:O_IDXK], w_in[:, O_DSAK:O_IDXQ], misc], axis=1)
    assert w_in_r.shape[1] == H_COLS
    wl = dict(
        norm_mix_g=p["norm_mix_g"][l], w_in=w_in_r.astype(BF16),
        lru_cw=_pad_rows(p["lru_conv_w"][l], SUBLANES), lru_cb=p["lru_conv_b"][l].reshape(1, -1),
        lru_wa=_block_diag(p["lru_wa"][l]).astype(BF16), lru_ba=p["lru_ba"][l].reshape(1, -1),
        lru_wx=_block_diag(p["lru_wx"][l]).astype(BF16), lru_bx=p["lru_bx"][l].reshape(1, -1),
        lru_lam=p["lru_lambda"][l].reshape(1, -1),
        gdn_cw=_pad_rows(p["gdn_conv_w"][l], SUBLANES),
        gdn_alog=_lane_row(p["gdn_a_log"][l], M_GDNA), gdn_dtb=_lane_row(p["gdn_dt_bias"][l], M_GDNA),
        gdn_ng=p["gdn_norm_g"][l].reshape(1, -1),
        diff_lam=p["diff_lambda"][l], diff_sg=p["diff_subln_g"][l].reshape(1, -1),
        w_branch=p["w_branch"][l].astype(BF16), w_out=p["w_out"][l].astype(BF16),
        norm_xa_g=p["norm_xa_g"][l].reshape(1, -1), norm_mem_g=p["norm_mem_g"][l],
        xa_wq=p["xa_wq"][l].astype(BF16), xa_wo=p["xa_wo"][l].astype(BF16),
        xa_wkv=jnp.concatenate([p["xa_wk"][l], p["xa_wv"][l]], axis=1).astype(BF16),
        norm_ffn_g=p["norm_ffn_g"][l].reshape(1, -1),
    )
    j = l // 2
    if l % 2 == 0:
        wl.update(ffn_wg=p["ffn_w_gate"][j].astype(BF16), ffn_wu=p["ffn_w_up"][j].astype(BF16),
                  ffn_wd=p["ffn_w_down"][j].astype(BF16))
    else:
        wl.update(moe_wr=jnp.pad(p["moe_router_w"][j], ((0, 0), (0, LANES - N_EXPERTS))),
                  moe_br=jnp.pad(p["moe_router_b"][j], (0, LANES - N_EXPERTS)).reshape(1, LANES),
                  moe_wg=p["moe_w_gate"][j].astype(BF16), moe_wu=p["moe_w_up"][j].astype(BF16),
                  moe_wd=p["moe_w_down"][j].astype(BF16))
    return wl


def _rope_tables(pos):
    half = DSA_HD // 2
    inv = ROPE_THETA ** (-jnp.arange(half, dtype=F32) / half)
    ang = pos.astype(F32)[:, None] * inv[None, :]
    cos, sin = jnp.cos(ang), jnp.sin(ang)
    return jnp.tile(cos, (1, 4)), jnp.tile(jnp.concatenate([-sin, sin], axis=1), (1, 2))


def _front_pad(buf):
    return jnp.pad(buf, ((0, 0), (SUBLANES - (CONV_W - 1), 0), (0, 0)))


def _group_layer(x3, l, wl, tabs, states, mem_kv, paged):
    b, t, d = x3.shape
    n = b * t
    x = x3.reshape(n, d)
    lam_init = 0.8 - 0.6 * math.exp(-0.3 * l)
    xn = _rmsnorm(x, wl["norm_mix_g"], BF16)
    h = _matmul(xn, wl["w_in"], tn=1408)
    h3 = h.reshape(b, t, H_COLS)
    dq, fq, fk, iq, dk, ik = _rope_call(h, tabs[0], tabs[1], None)
    to3 = lambda a: a.reshape(b, t, a.shape[-1])
    dq3, fq3, fk3, iq3, dk3, ik3 = map(to3, (dq, fq, fk, iq, dk, ik))

    o_a, lru_h = _lru_call(h3, wl, _front_pad(states["lru_conv"]), states["lru_h"].reshape(b, 1, -1),
                           reset_first=paged is None)
    o_c, gdn_s = _gdn_call(h3, wl, _front_pad(states["gdn_conv"]), states["gdn_s"])
    if paged is None:
        o_b = _dsa_prompt_call(iq3, ik3, dq3, dk3, h3)
        o_d = _diff_prompt_call(fq3, fk3, h3, wl, lam_init)
    else:
        o_b = _dsa_sample_call(iq3, ik3, dq3, dk3, h3, paged["dsa_k"], paged["dsa_v"], paged["idx_k"],
                               paged["page_table"], l)
        o_d = _diff_sample_call(fq3, fk3, h3, paged["diff_k"], paged["diff_v"], paged["page_table"], l, wl, lam_init)

    flat = lambda a: a.reshape(n, a.shape[-1])
    x1 = _merge_call(x, h, [flat(o_a), flat(o_b), flat(o_c), flat(o_d)], wl)
    x2 = _cross_call(x1.reshape(b, t, d), wl, mem_kv[0], mem_kv[1], mem_kv[2])
    x2 = x2.reshape(n, d)
    x3_new = (_ffn_call(x2, wl) if "ffn_wg" in wl else _moe_call(x2, wl)).reshape(b, t, d)

    new = dict(
        dsa_k=dk3.reshape(b, t, DSA_KV_HEADS, DSA_HD),
        dsa_v=h3[:, :, C_DSAV:C_DSAV + DSA_KV_HEADS * DSA_HD].reshape(b, t, DSA_KV_HEADS, DSA_HD),
        idx_k=ik3,
        diff_k=fk3.reshape(b, t, DIFF_HEADS, 2 * DIFF_HD),
        diff_v=h3[:, :, C_DIFV:C_DIFV + MIX_WIDTH].reshape(b, t, DIFF_HEADS, 2 * DIFF_HD),
        lru_h=lru_h.reshape(b, -1),
        lru_conv=jnp.concatenate([states["lru_conv"], h3[:, :, C_LRUX:C_LRUX + MIX_WIDTH]], axis=1)[:, -(CONV_W - 1):]
        if t < CONV_W - 1 else h3[:, t - (CONV_W - 1):, C_LRUX:C_LRUX + MIX_WIDTH],
        gdn_s=gdn_s,
        gdn_conv=jnp.concatenate([states["gdn_conv"], h3[:, :, C_GDNQKV:C_GDNQKV + GDN_QKV]], axis=1)[:, -(CONV_W - 1):]
        if t < CONV_W - 1 else h3[:, t - (CONV_W - 1):, C_GDNQKV:C_GDNQKV + GDN_QKV],
    )
    return x3_new, new


def kernel(x_prompt, x_sample, mem_prompt, cache_dsa_k, cache_dsa_v, cache_idx_k, cache_diff_k, cache_diff_v,
           cache_mem_k, cache_mem_v, state_lru_h, state_lru_conv, state_gdn_s, state_gdn_conv, page_table,
           norm_mix_g, w_in, lru_conv_w, lru_conv_b, lru_wa, lru_ba, lru_wx, lru_bx, lru_lambda,
           gdn_conv_w, gdn_a_log, gdn_dt_bias, gdn_norm_g, diff_lambda, diff_subln_g, w_branch, w_out,
           norm_xa_g, norm_mem_g, xa_wq, xa_wk, xa_wv, xa_wo, norm_ffn_g, ffn_w_gate, ffn_w_up, ffn_w_down,
           moe_router_w, moe_router_b, moe_w_gate, moe_w_up, moe_w_down, final_norm_g):
    params = dict(
        norm_mix_g=norm_mix_g, w_in=w_in, lru_conv_w=lru_conv_w, lru_conv_b=lru_conv_b, lru_wa=lru_wa, lru_ba=lru_ba,
        lru_wx=lru_wx, lru_bx=lru_bx, lru_lambda=lru_lambda, gdn_conv_w=gdn_conv_w, gdn_a_log=gdn_a_log,
        gdn_dt_bias=gdn_dt_bias, gdn_norm_g=gdn_norm_g, diff_lambda=diff_lambda, diff_subln_g=diff_subln_g,
        w_branch=w_branch, w_out=w_out, norm_xa_g=norm_xa_g, norm_mem_g=norm_mem_g, xa_wq=xa_wq, xa_wk=xa_wk,
        xa_wv=xa_wv, xa_wo=xa_wo, norm_ffn_g=norm_ffn_g, ffn_w_gate=ffn_w_gate, ffn_w_up=ffn_w_up,
        ffn_w_down=ffn_w_down, moe_router_w=moe_router_w, moe_router_b=moe_router_b, moe_w_gate=moe_w_gate,
        moe_w_up=moe_w_up, moe_w_down=moe_w_down)
    b, s, d = x_prompt.shape
    db, t, _ = x_sample.shape
    m = mem_prompt.shape[1]
    n_pool = cache_dsa_k.shape[1]
    past = page_table.shape[1] * PAGE_SIZE

    tabs_p = _rope_tables(jnp.arange(s))
    cs, ss = _rope_tables(past + jnp.arange(t))
    tabs_s = (jnp.tile(cs, (db, 1)), jnp.tile(ss, (db, 1)))
    paged = dict(
        dsa_k=cache_dsa_k.reshape(DEPTH, n_pool, PAGE_SIZE, DSA_KV_HEADS * DSA_HD),
        dsa_v=cache_dsa_v.reshape(DEPTH, n_pool, PAGE_SIZE, DSA_KV_HEADS * DSA_HD),
        idx_k=cache_idx_k, page_table=page_table,
        diff_k=cache_diff_k.reshape(DEPTH, n_pool, PAGE_SIZE, MIX_WIDTH),
        diff_v=cache_diff_v.reshape(DEPTH, n_pool, PAGE_SIZE, MIX_WIDTH))
    mem_k_s = cache_mem_k.reshape(DEPTH, db, m, XA_HEADS * XA_HD)
    mem_v_s = cache_mem_v.reshape(DEPTH, db, m, XA_HEADS * XA_HD)
    zero_states = dict(lru_conv=jnp.zeros((b, CONV_W - 1, MIX_WIDTH), F32), lru_h=jnp.zeros((b, MIX_WIDTH), F32),
                       gdn_conv=jnp.zeros((b, CONV_W - 1, GDN_QKV), F32),
                       gdn_s=jnp.zeros((b, GDN_HEADS, GDN_HD, GDN_HD), F32))

    hp, hs = x_prompt, x_sample
    new_p, new_s, mem_ks, mem_vs = [], [], [], []
    for l in range(DEPTH):
        wl = _layer_weights(l, params)
        mem_n = _rmsnorm(mem_prompt.reshape(b * m, d), wl["norm_mem_g"], BF16)
        kv = _matmul(mem_n, wl["xa_wkv"], tn=512)
        w = XA_HEADS * XA_HD
        mk = kv[:, :w].reshape(1, b, m, w)
        mv = kv[:, w:].reshape(1, b, m, w)
        hp, st_p = _group_layer(hp, l, wl, tabs_p, zero_states, (mk, mv, 0), None)
        new_p.append(st_p)
        mem_ks.append(mk.reshape(b, m, XA_HEADS, XA_HD))
        mem_vs.append(mv.reshape(b, m, XA_HEADS, XA_HD))
        states_s = dict(lru_conv=state_lru_conv[l], lru_h=state_lru_h[l], gdn_conv=state_gdn_conv[l],
                        gdn_s=state_gdn_s[l])
        hs, st_s = _group_layer(hs, l, wl, tabs_s, states_s, (mem_k_s, mem_v_s, l), paged)
        new_s.append(st_s)

    y_prompt = _rmsnorm(hp.reshape(b * s, d), final_norm_g, F32).reshape(b, s, d)
    y_sample = _rmsnorm(hs.reshape(db * t, d), final_norm_g, F32).reshape(db, t, d)
    stack = lambda states, name: jnp.stack([st[name] for st in states], axis=0)
    outs = [y_prompt, y_sample]
    for name in ("dsa_k", "dsa_v", "idx_k", "diff_k", "diff_v", "lru_h", "lru_conv", "gdn_s", "gdn_conv"):
        outs += [stack(new_p, name), stack(new_s, name)]
    outs += [jnp.stack(mem_ks, axis=0), jnp.stack(mem_vs, axis=0)]
    return tuple(outs)
```

```python
import functools
import math

import jax
import jax.numpy as jnp
from jax import lax
from jax.experimental import pallas as pl
from jax.experimental.pallas import tpu as pltpu

F32 = jnp.float32
BF16 = jnp.bfloat16
I32 = jnp.int32

D_MODEL = 1024
DEPTH = 2
PAGE_SIZE = 128
MIX_WIDTH = 512
LRU_BLOCKS = 8
LRU_BLOCK = MIX_WIDTH // LRU_BLOCKS
LRU_C = 8.0
CONV_W = 4
DSA_HEADS = 8
DSA_KV_HEADS = 2
DSA_HD = 64
IDX_HEADS = 4
IDX_HD = 64
TOPK_MAX = 256
GDN_HEADS = 4
GDN_HD = 128
GDN_CHUNK = 128
DIFF_HEADS = 4
DIFF_HD = 64
XA_HEADS = 4
XA_HD = 128
N_EXPERTS = 8
ROPE_THETA = 10000.0
EPS = 1e-6
N_BRANCH = 4
GDN_QKV = 3 * GDN_HEADS * GDN_HD

LANES = 128
SUBLANES = 8
NEG_BIG = -1e30
INT_MIN = -2 ** 31
VMEM_LIMIT = 48 * 1024 * 1024

C_GATE = 0
C_LRUX = 4096
C_LRUG = 4608
C_DSAQ = 5120
C_DIFQ = 5632
C_DIFK = 6144
C_DIFV = 6656
C_GDNZ = 7168
C_GDNQKV = 7680
C_IDXQ = 9216
C_DSAK = 9472
C_DSAV = 9600
C_MISC = 9728
H_COLS = 9856
M_IDXK = 0
M_IDXW = 64
M_GDNB = 68
M_GDNA = 72

O_LRUX, O_LRUG, O_DSAQ, O_DSAK, O_DSAV, O_IDXQ, O_IDXK, O_IDXW = 0, 512, 1024, 1536, 1664, 1792, 2048, 2112
O_GDNQKV, O_GDNZ, O_GDNB, O_GDNA, O_DIFQ, O_DIFK, O_DIFV, O_GATE = 2116, 3652, 4164, 4168, 4172, 4684, 5196, 5708


def _cparams(*sem):
    return pltpu.CompilerParams(dimension_semantics=sem, vmem_limit_bytes=VMEM_LIMIT)


def _tile(n, pref):
    return pref if n % pref == 0 else n


def _softplus(x):
    return jnp.maximum(x, 0.0) + jnp.log1p(jnp.exp(-jnp.abs(x)))


def _sigmoid(x):
    return 1.0 / (1.0 + jnp.exp(-x))


def _silu(x):
    return x * _sigmoid(x)


def _dot(a, b):
    return jnp.dot(a, b, preferred_element_type=F32)


def _dot_nt(a, b):
    return lax.dot_general(a, b, (((1,), (1,)), ((), ())), preferred_element_type=F32)


def _dot_hi(a, b):
    return jnp.dot(a, b, preferred_element_type=F32, precision=lax.Precision.HIGHEST)


def _split_hi_lo(x):
    hi = x.astype(BF16)
    return hi, (x - hi.astype(F32)).astype(BF16)


_NN = (((1,), (0,)), ((), ()))
_NT = (((1,), (1,)), ((), ()))
_TN = (((0,), (0,)), ((), ()))


def _dot3(a, b, dims=_NN):
    a_hi, a_lo = _split_hi_lo(a)
    b_hi, b_lo = _split_hi_lo(b)
    dg = functools.partial(lax.dot_general, dimension_numbers=dims, preferred_element_type=F32)
    ca, cb = dims[0][0][0], dims[0][1][0]
    if a.shape[ca] % LANES == 0:
        return dg(jnp.concatenate([a_hi, a_hi, a_lo], axis=ca), jnp.concatenate([b_hi, b_lo, b_hi], axis=cb))
    return dg(a_hi, b_hi) + dg(a_hi, b_lo) + dg(a_lo, b_hi)


def _dot_exact_lhs(a, b):
    ab = a.astype(BF16)
    b1 = b.astype(BF16)
    r1 = b - b1.astype(F32)
    b2 = r1.astype(BF16)
    b3 = (r1 - b2.astype(F32)).astype(BF16)
    if a.shape[1] % LANES == 0:
        return _dot(jnp.concatenate([ab, ab, ab], axis=1), jnp.concatenate([b1, b2, b3], axis=0))
    return _dot(ab, b1) + _dot(ab, b2) + _dot(ab, b3)


def _rms_rows(x, g):
    return x * lax.rsqrt(jnp.mean(x * x, axis=-1, keepdims=True) + EPS) * g


def _rmsnorm_body(x_ref, g_ref, o_ref):
    o_ref[...] = _rms_rows(x_ref[...], g_ref[...]).astype(o_ref.dtype)


def _rmsnorm(x, g, out_dtype):
    n, d = x.shape
    tm = _tile(n, 512)
    return pl.pallas_call(
        _rmsnorm_body,
        grid=(n // tm,),
        in_specs=[pl.BlockSpec((tm, d), lambda i: (i, 0)), pl.BlockSpec((1, d), lambda i: (0, 0))],
        out_specs=pl.BlockSpec((tm, d), lambda i: (i, 0)),
        out_shape=jax.ShapeDtypeStruct((n, d), out_dtype),
        compiler_params=_cparams("parallel"),
        name="rmsnorm",
    )(x, g.reshape(1, d))


def _matmul_body(a_ref, w_ref, o_ref):
    o_ref[...] = _dot(a_ref[...], w_ref[...]).astype(o_ref.dtype)


def _matmul(a, w, tn, out_dtype=F32):
    n, k = a.shape
    c = w.shape[1]
    tm = _tile(n, 512)
    return pl.pallas_call(
        _matmul_body,
        grid=(c // tn, n // tm),
        in_specs=[pl.BlockSpec((tm, k), lambda j, i: (i, 0)), pl.BlockSpec((k, tn), lambda j, i: (0, j))],
        out_specs=pl.BlockSpec((tm, tn), lambda j, i: (i, j)),
        out_shape=jax.ShapeDtypeStruct((n, c), out_dtype),
        compiler_params=_cparams("parallel", "parallel"),
        name="matmul",
    )(a, w)


def _rope_block(x, cos, sin):
    lane = lax.broadcasted_iota(I32, x.shape, 1)
    first = (lane % 64) < 32
    partner = jnp.where(first, pltpu.roll(x, LANES - 32, 1), pltpu.roll(x, 32, 1))
    return x * cos + partner * sin


def _rope_wide(x, cos, sin):
    return [_rope_block(x[:, c * LANES:(c + 1) * LANES], cos, sin) for c in range(x.shape[1] // LANES)]


def _rope_body(cos_ref, sin_ref, dq_ref, fq_ref, fk_ref, fv_ref, iq_ref, dk_ref, dv_ref, misc_ref,
               dq_o, fq_o, fk_o, fkb_o, fvb_o, iq3_o, dk_o, dkb_o, dvb_o, ik_o, ik3_o):
    cos = cos_ref[...]
    sin = sin_ref[...]
    lane = lax.broadcasted_iota(I32, cos.shape, 1)
    low = lane < IDX_HD
    for c, blk in enumerate(_rope_wide(dq_ref[...], cos, sin)):
        dq_o[:, c * LANES:(c + 1) * LANES] = (blk * DSA_HD ** -0.5).astype(BF16)
    for c, blk in enumerate(_rope_wide(fq_ref[...], cos, sin)):
        fq_o[:, c * LANES:(c + 1) * LANES] = (blk * DIFF_HD ** -0.5).astype(BF16)
    for c, blk in enumerate(_rope_wide(fk_ref[...], cos, sin)):
        fk_o[:, c * LANES:(c + 1) * LANES] = blk
        fkb_o[:, c * LANES:(c + 1) * LANES] = blk.astype(BF16)
    fvb_o[...] = fv_ref[...].astype(BF16)
    dk = _rope_block(dk_ref[...], cos, sin)
    dk_o[...] = dk
    dkb_o[...] = dk.astype(BF16)
    dvb_o[...] = dv_ref[...].astype(BF16)
    for c, blk in enumerate(_rope_wide(iq_ref[...], cos, sin)):
        hi = blk.astype(BF16).astype(F32)
        lo = blk - hi
        hi_sw = pltpu.roll(hi, IDX_HD, 1)
        lo_sw = pltpu.roll(lo, IDX_HD, 1)
        base = 2 * c * 2 * LANES
        iq3_o[:, base:base + LANES] = jnp.where(low, hi, hi_sw).astype(BF16)
        iq3_o[:, base + LANES:base + 2 * LANES] = jnp.where(low, lo, 0.0).astype(BF16)
        iq3_o[:, base + 2 * LANES:base + 3 * LANES] = jnp.where(low, hi_sw, hi).astype(BF16)
        iq3_o[:, base + 3 * LANES:base + 4 * LANES] = jnp.where(low, lo_sw, 0.0).astype(BF16)
    ik = _rope_block(misc_ref[...], cos, sin)
    ik_o[...] = ik[:, M_IDXK:M_IDXK + IDX_HD]
    hi = ik.astype(BF16).astype(F32)
    lo = ik - hi
    ik3_o[:, 0:LANES] = jnp.where(low, hi, pltpu.roll(lo, IDX_HD, 1)).astype(BF16)
    ik3_o[:, LANES:2 * LANES] = jnp.where(low, hi, 0.0).astype(BF16)


def _rope_call(h, cos, sin):
    n = h.shape[0]
    tm = _tile(min(n, cos.shape[0]), 512)
    nt = cos.shape[0] // tm

    def col(width, off):
        return pl.BlockSpec((tm, width), lambda i: (i, off // width))

    def out(width):
        return pl.BlockSpec((tm, width), lambda i: (i, 0))

    sds = jax.ShapeDtypeStruct
    tab = pl.BlockSpec((tm, LANES), lambda i: (i % nt, 0))
    return pl.pallas_call(
        _rope_body,
        grid=(n // tm,),
        in_specs=[tab, tab, col(512, C_DSAQ), col(512, C_DIFQ), col(512, C_DIFK), col(512, C_DIFV),
                  col(256, C_IDXQ), col(128, C_DSAK), col(128, C_DSAV), col(128, C_MISC)],
        out_specs=[out(512), out(512), out(512), out(512), out(512), out(1024), out(128), out(128), out(128),
                   out(IDX_HD), out(256)],
        out_shape=[sds((n, 512), BF16), sds((n, 512), BF16), sds((n, 512), F32), sds((n, 512), BF16),
                   sds((n, 512), BF16), sds((n, 1024), BF16), sds((n, 128), F32), sds((n, 128), BF16),
                   sds((n, 128), BF16), sds((n, IDX_HD), F32), sds((n, 256), BF16)],
        compiler_params=_cparams("parallel"),
        name="rope",
    )(cos, sin, h, h, h, h, h, h, h, h)


def _conv_tile(x_ref, buf_ref, cw_ref, xcat, t, tc):
    @pl.when(t == 0)
    def _():
        xcat[0:SUBLANES, :] = buf_ref[...]

    @pl.when(t > 0)
    def _():
        xcat[0:SUBLANES, :] = xcat[tc:tc + SUBLANES, :]

    xcat[SUBLANES:SUBLANES + tc, :] = x_ref[...]
    cw = cw_ref[...]
    y = x_ref[...] * cw[CONV_W - 1:CONV_W, :]
    for j in range(CONV_W - 1):
        off = SUBLANES - (CONV_W - 1) + j
        y = y + xcat[off:off + tc, :] * cw[j:j + 1, :]
    return y


def _lru_body(x_ref, gate_ref, cw_ref, cb_ref, wa_ref, ba_ref, wx_ref, bx_ref, lam_ref, buf_ref, h0_ref,
              o_ref, hl_ref, xcat, hc, *, tc, reset_first):
    t = pl.program_id(1)

    @pl.when(t == 0)
    def _():
        hc[...] = h0_ref[...]

    xa = _conv_tile(x_ref, buf_ref, cw_ref, xcat, t, tc) + cb_ref[...]
    xb = xa.astype(BF16)
    r = _sigmoid(_dot(xb, wa_ref[...]) + ba_ref[...])
    gi = _sigmoid(_dot(xb, wx_ref[...]) + bx_ref[...])
    log_a = -LRU_C * r * _softplus(-lam_ref[...])
    a = jnp.exp(log_a)
    mult = jnp.sqrt(-jnp.tanh(log_a) * (a * a + 1.0))
    row = lax.broadcasted_iota(I32, a.shape, 0)
    if reset_first:
        is0 = jnp.logical_and(row == 0, t == 0)
        a = jnp.where(is0, 0.0, a)
        mult = jnp.where(is0, 1.0, mult)
    u = mult * gi * xa
    d = 1
    while d < tc:
        keep = row >= d
        a_sh = jnp.where(keep, pltpu.roll(a, d, 0), 1.0)
        u_sh = jnp.where(keep, pltpu.roll(u, d, 0), 0.0)
        u = a * u_sh + u
        a = a * a_sh
        d *= 2
    h = a * hc[...] + u
    h_last = h[tc - 1:tc, :]
    hc[...] = h_last
    hl_ref[...] = h_last
    o_ref[...] = (h * jax.nn.gelu(gate_ref[...])).astype(o_ref.dtype)


def _lru_call(h3, wl, buf, h0, reset_first):
    b, t, _ = h3.shape
    tc = _tile(t, 256)
    w = MIX_WIDTH
    row = pl.BlockSpec((1, w), lambda i, j: (0, 0))
    sq = pl.BlockSpec((w, w), lambda i, j: (0, 0))
    return pl.pallas_call(
        functools.partial(_lru_body, tc=tc, reset_first=reset_first),
        grid=(b, t // tc),
        in_specs=[pl.BlockSpec((None, tc, w), lambda i, j: (i, j, C_LRUX // w)),
                  pl.BlockSpec((None, tc, w), lambda i, j: (i, j, C_LRUG // w)),
                  pl.BlockSpec((SUBLANES, w), lambda i, j: (0, 0)), row, sq, row, sq, row, row,
                  pl.BlockSpec((None, SUBLANES, w), lambda i, j: (i, 0, 0)),
                  pl.BlockSpec((None, 1, w), lambda i, j: (i, 0, 0))],
        out_specs=[pl.BlockSpec((None, tc, w), lambda i, j: (i, j, 0)),
                   pl.BlockSpec((None, 1, w), lambda i, j: (i, 0, 0))],
        out_shape=[jax.ShapeDtypeStruct((b, t, w), BF16), jax.ShapeDtypeStruct((b, 1, w), F32)],
        scratch_shapes=[pltpu.VMEM((tc + SUBLANES, w), F32), pltpu.VMEM((1, w), F32)],
        compiler_params=_cparams("parallel", "arbitrary"),
        name="rg_lru",
    )(h3, h3, wl["lru_cw"], wl["lru_cb"], wl["lru_wa"], wl["lru_ba"], wl["lru_wx"], wl["lru_bx"], wl["lru_lam"],
      buf, h0)


def _gdn_body(qkv_ref, z_ref, misc_ref, cw_ref, alog_ref, dtb_ref, ng_ref, buf_ref, s0_ref,
              o_ref, so_ref, xcat, y_scr, state, *, tc, ck):
    t = pl.program_id(1)
    hd = GDN_HD

    @pl.when(t == 0)
    def _():
        state[...] = s0_ref[...]

    y = _conv_tile(qkv_ref, buf_ref, cw_ref, xcat, t, tc)
    y_scr[...] = _silu(y)

    ri = lax.broadcasted_iota(I32, (ck, ck), 0)
    ci = lax.broadcasted_iota(I32, (ck, ck), 1)
    tril = ri >= ci
    strict = ri > ci
    ltri = jnp.where(tril, 1.0, 0.0).astype(F32)
    ones = jnp.ones((ck, ck), F32)
    n_sq = max(int(math.log2(ck)) - 1, 0)

    def chunk(c, carry):
        r0 = pl.multiple_of(c * ck, ck)
        misc = misc_ref[pl.ds(r0, ck), :]
        beta_all = _sigmoid(misc)
        g_all = -jnp.exp(alog_ref[...]) * _softplus(misc + dtb_ref[...])
        outs, states = [], []
        for hh in range(GDN_HEADS):
            q = y_scr[pl.ds(r0, ck), hh * hd:(hh + 1) * hd]
            k = y_scr[pl.ds(r0, ck), (GDN_HEADS + hh) * hd:(GDN_HEADS + hh + 1) * hd]
            v = y_scr[pl.ds(r0, ck), (2 * GDN_HEADS + hh) * hd:(2 * GDN_HEADS + hh + 1) * hd]
            q = q * lax.rsqrt(jnp.sum(q * q, axis=-1, keepdims=True) + EPS) * (hd ** -0.5)
            k = k * lax.rsqrt(jnp.sum(k * k, axis=-1, keepdims=True) + EPS)
            beta = jnp.broadcast_to(beta_all[:, M_GDNB + hh:M_GDNB + hh + 1], (ck, hd))
            g128 = jnp.broadcast_to(g_all[:, M_GDNA + hh:M_GDNA + hh + 1], (ck, hd))
            gsq = g128[:, :ck]
            gc128 = _dot_exact_lhs(ltri, g128)
            gc_col = gc128[:, :ck]
            gc_row = _dot_exact_lhs(ones, jnp.where(ri <= ci, gsq, 0.0))
            decay = jnp.where(tril, jnp.exp(jnp.where(tril, gc_col - gc_row, 0.0)), 0.0)
            kb = k * beta
            egc = jnp.exp(gc128)
            low = jnp.where(strict, _dot3(kb, k, _NT) * decay, 0.0)
            mp = -low
            x = jnp.concatenate([v * beta, kb * egc], axis=-1)
            x = x + _dot3(mp, x)
            for _ in range(n_sq):
                mp = _dot3(mp, mp)
                x = x + _dot3(mp, x)
            u = x[:, :hd]
            w = x[:, hd:]
            intra = jnp.where(tril, _dot3(q, k, _NT) * decay, 0.0)
            s = state[hh]
            v_new = u - _dot3(w, s)
            o = _dot3(q * egc, s) + _dot3(intra, v_new)
            g_last = gc128[ck - 1:ck, :]
            kd = k * jnp.exp(g_last - gc128)
            states.append(s * jnp.exp(g_last) + _dot3(kd, v_new, _TN))
            zz = z_ref[pl.ds(r0, ck), hh * hd:(hh + 1) * hd]
            outs.append((_rms_rows(o, ng_ref[...]) * _silu(zz)).astype(o_ref.dtype))
        for hh in range(GDN_HEADS):
            state[hh] = states[hh]
            o_ref[pl.ds(r0, ck), hh * hd:(hh + 1) * hd] = outs[hh]
        return carry

    lax.fori_loop(0, tc // ck, chunk, 0)
    so_ref[...] = state[...]


def _gdn_call(h3, wl, buf, s0):
    b, t, _ = h3.shape
    ck = min(GDN_CHUNK, t)
    assert t % ck == 0
    tc = _tile(t, 256)
    w = GDN_QKV
    row = pl.BlockSpec((1, LANES), lambda i, j: (0, 0))
    st = pl.BlockSpec((None, GDN_HEADS, GDN_HD, GDN_HD), lambda i, j: (i, 0, 0, 0))
    return pl.pallas_call(
        functools.partial(_gdn_body, tc=tc, ck=ck),
        grid=(b, t // tc),
        in_specs=[pl.BlockSpec((None, tc, w), lambda i, j: (i, j, C_GDNQKV // w)),
                  pl.BlockSpec((None, tc, MIX_WIDTH), lambda i, j: (i, j, C_GDNZ // MIX_WIDTH)),
                  pl.BlockSpec((None, tc, LANES), lambda i, j: (i, j, C_MISC // LANES)),
                  pl.BlockSpec((SUBLANES, w), lambda i, j: (0, 0)), row, row, row,
                  pl.BlockSpec((None, SUBLANES, w), lambda i, j: (i, 0, 0)), st],
        out_specs=[pl.BlockSpec((None, tc, MIX_WIDTH), lambda i, j: (i, j, 0)), st],
        out_shape=[jax.ShapeDtypeStruct((b, t, MIX_WIDTH), BF16),
                   jax.ShapeDtypeStruct((b, GDN_HEADS, GDN_HD, GDN_HD), F32)],
        scratch_shapes=[pltpu.VMEM((tc + SUBLANES, w), F32), pltpu.VMEM((tc, w), F32),
                        pltpu.VMEM((GDN_HEADS, GDN_HD, GDN_HD), F32)],
        compiler_params=_cparams("parallel", "arbitrary"),
        name="gated_deltanet",
    )(h3, h3, h3, wl["gdn_cw"], wl["gdn_alog"], wl["gdn_dtb"], wl["gdn_ng"], buf, s0)


def _online_update(s, vs, m_ref, l_ref, acc_ref, idx):
    m_old = m_ref[idx]
    m_new = jnp.maximum(m_old, jnp.max(s, axis=-1, keepdims=True))
    alpha = jnp.exp(m_old - m_new)
    p = jnp.exp(s - m_new)
    l_ref[idx] = alpha * l_ref[idx] + jnp.sum(p, axis=-1, keepdims=True)
    pb = p.astype(BF16)
    pv, off = None, 0
    for vblk in vs:
        part = _dot(pb[:, off:off + vblk.shape[0]], vblk)
        pv = part if pv is None else pv + part
        off += vblk.shape[0]
    acc_ref[idx] = alpha * acc_ref[idx] + pv
    m_ref[idx] = m_new


def _reset_softmax(m_ref, l_ref, acc_ref):
    m_ref[...] = jnp.full(m_ref.shape, NEG_BIG, F32)
    l_ref[...] = jnp.zeros(l_ref.shape, F32)
    acc_ref[...] = jnp.zeros(acc_ref.shape, F32)


def _diff_lambda(lam_ref, lam_init):
    lm = lam_ref[...]
    e1 = jnp.exp(jnp.sum(lm[0:1, :] * lm[1:2, :], axis=-1, keepdims=True))
    e2 = jnp.exp(jnp.sum(lm[2:3, :] * lm[3:4, :], axis=-1, keepdims=True))
    return e1 - e2 + lam_init


def _diff_prompt_body(qi_ref, kj_ref, lam_ref, sg_ref, q_ref, k_ref, v_ref, o_ref, m_scr, l_scr, acc_scr,
                      *, tq, lam_init):
    step = pl.program_id(2)
    qi = qi_ref[step]
    kj = kj_ref[step]

    @pl.when(kj == 0)
    def _():
        _reset_softmax(m_scr, l_scr, acc_scr)

    def tile(masked):
        v = v_ref[...]
        for mm in range(2):
            s = _dot_nt(q_ref[:, mm * DIFF_HD:(mm + 1) * DIFF_HD], k_ref[:, mm * DIFF_HD:(mm + 1) * DIFF_HD])
            if masked:
                row = lax.broadcasted_iota(I32, (tq, tq), 0)
                col = lax.broadcasted_iota(I32, (tq, tq), 1)
                s = jnp.where(col <= row, s, NEG_BIG)
            _online_update(s, [v], m_scr, l_scr, acc_scr, mm)

    @pl.when(kj < qi)
    def _():
        tile(False)

    @pl.when(kj == qi)
    def _():
        tile(True)
        lam = _diff_lambda(lam_ref, lam_init)
        o = acc_scr[0] / l_scr[0] - lam * (acc_scr[1] / l_scr[1])
        o_ref[...] = (_rms_rows(o, sg_ref[...]) * (1.0 - lam_init)).astype(o_ref.dtype)


def _diff_prompt_call(fq3, fkb3, fvb3, wl, lam_init):
    b, t, _ = fq3.shape
    tq = _tile(t, 512)
    nq = t // tq
    hv = 2 * DIFF_HD
    pairs = [(q, k) for q in range(nq) for k in range(q + 1)]
    qi_of = jnp.asarray([p[0] for p in pairs], I32)
    kj_of = jnp.asarray([p[1] for p in pairs], I32)
    grid_spec = pltpu.PrefetchScalarGridSpec(
        num_scalar_prefetch=2,
        grid=(b, DIFF_HEADS, len(pairs)),
        in_specs=[pl.BlockSpec((4, DIFF_HD), lambda i, h, s, qi, kj: (0, 0)),
                  pl.BlockSpec((1, hv), lambda i, h, s, qi, kj: (0, 0)),
                  pl.BlockSpec((None, tq, hv), lambda i, h, s, qi, kj: (i, qi[s], h)),
                  pl.BlockSpec((None, tq, hv), lambda i, h, s, qi, kj: (i, kj[s], h)),
                  pl.BlockSpec((None, tq, hv), lambda i, h, s, qi, kj: (i, kj[s], h))],
        out_specs=pl.BlockSpec((None, tq, hv), lambda i, h, s, qi, kj: (i, qi[s], h)),
        scratch_shapes=[pltpu.VMEM((2, tq, 1), F32), pltpu.VMEM((2, tq, 1), F32), pltpu.VMEM((2, tq, hv), F32)],
    )
    return pl.pallas_call(
        functools.partial(_diff_prompt_body, tq=tq, lam_init=lam_init),
        grid_spec=grid_spec,
        out_shape=jax.ShapeDtypeStruct((b, t, MIX_WIDTH), BF16),
        compiler_params=_cparams("parallel", "parallel", "arbitrary"),
        name="diff_attn_prompt",
    )(qi_of, kj_of, wl["diff_lam"], wl["diff_sg"], fq3, fkb3, fvb3)


def _diff_sample_body(pt_ref, lam_ref, sg_ref, q_ref, kn_ref, vn_ref, *rest, tq, npg, lam_init):
    kp = rest[:npg]
    vp = rest[npg:2 * npg]
    o_ref, qrows, m_scr, l_scr, acc_scr = rest[2 * npg:]
    s_id = pl.program_id(1)
    n_steps = pl.num_programs(1)
    nrow = 2 * DIFF_HEADS * tq
    hv = 2 * DIFF_HD
    prow = PAGE_SIZE * DIFF_HEADS

    @pl.when(s_id == 0)
    def _():
        _reset_softmax(m_scr, l_scr, acc_scr)
        qrows[...] = jnp.zeros(qrows.shape, F32)
        q = q_ref[...].astype(F32)
        for j in range(2 * DIFF_HEADS):
            mm = j % 2
            qrows[j * tq:(j + 1) * tq, mm * DIFF_HD:(mm + 1) * DIFF_HD] = q[:, j * DIFF_HD:(j + 1) * DIFF_HD]

    qb = qrows[...].astype(BF16)
    row_head = lax.broadcasted_iota(I32, (nrow, prow), 0) // (2 * tq)

    @pl.when(s_id < n_steps - 1)
    def _():
        key_head = lax.broadcasted_iota(I32, (nrow, prow), 1) % DIFF_HEADS
        own = key_head == row_head
        scores = [jnp.where(own, _dot_nt(qb, kp[p][...].astype(BF16)), NEG_BIG) for p in range(npg)]
        _online_update(jnp.concatenate(scores, axis=-1), [vp[p][...].astype(BF16) for p in range(npg)],
                       m_scr, l_scr, acc_scr, Ellipsis)

    @pl.when(s_id == n_steps - 1)
    def _():
        kn = jnp.concatenate([kn_ref[:, hh * hv:(hh + 1) * hv] for hh in range(DIFF_HEADS)], axis=0).astype(BF16)
        vn = jnp.concatenate([vn_ref[:, hh * hv:(hh + 1) * hv] for hh in range(DIFF_HEADS)], axis=0).astype(BF16)
        nk = DIFF_HEADS * tq
        rr = lax.broadcasted_iota(I32, (nrow, nk), 0)
        cc = lax.broadcasted_iota(I32, (nrow, nk), 1)
        ok = jnp.logical_and(cc // tq == rr // (2 * tq), cc % tq <= rr % tq)
        s = jnp.where(ok, _dot_nt(qb, kn), NEG_BIG)
        _online_update(s, [vn], m_scr, l_scr, acc_scr, Ellipsis)
        lam = _diff_lambda(lam_ref, lam_init)
        acc = acc_scr[...] / l_scr[...]
        for hh in range(DIFF_HEADS):
            r1 = (2 * hh) * tq
            r2 = (2 * hh + 1) * tq
            o = acc[r1:r1 + tq, :] - lam * acc[r2:r2 + tq, :]
            o_ref[:, hh * hv:(hh + 1) * hv] = (_rms_rows(o, sg_ref[...]) * (1.0 - lam_init)).astype(o_ref.dtype)


def _diff_sample_call(fq3, fk3, h3, pool_k, pool_v, page_table, layer, wl, lam_init):
    db, t, _ = fq3.shape
    n_pages = page_table.shape[1]
    npg = 8 if n_pages % 8 == 0 else 1
    n_steps = n_pages // npg + 1
    w = MIX_WIDTH
    hv = 2 * DIFF_HD
    nrow = 2 * DIFF_HEADS * t
    prow = PAGE_SIZE * DIFF_HEADS

    def page_spec(p):
        def imap(i, s, pt):
            return (layer, pt[i, jnp.minimum(s, n_steps - 2) * npg + p], 0, 0)
        return pl.BlockSpec((None, None, prow, hv), imap)

    grid_spec = pltpu.PrefetchScalarGridSpec(
        num_scalar_prefetch=1,
        grid=(db, n_steps),
        in_specs=[pl.BlockSpec((4, DIFF_HD), lambda i, s, pt: (0, 0)),
                  pl.BlockSpec((1, hv), lambda i, s, pt: (0, 0)),
                  pl.BlockSpec((None, t, w), lambda i, s, pt: (i, 0, 0)),
                  pl.BlockSpec((None, t, w), lambda i, s, pt: (i, 0, 0)),
                  pl.BlockSpec((None, t, w), lambda i, s, pt: (i, 0, C_DIFV // w))]
                 + [page_spec(p) for p in range(npg)] + [page_spec(p) for p in range(npg)],
        out_specs=pl.BlockSpec((None, t, w), lambda i, s, pt: (i, 0, 0)),
        scratch_shapes=[pltpu.VMEM((nrow, hv), F32), pltpu.VMEM((nrow, 1), F32), pltpu.VMEM((nrow, 1), F32),
                        pltpu.VMEM((nrow, hv), F32)],
    )
    return pl.pallas_call(
        functools.partial(_diff_sample_body, tq=t, npg=npg, lam_init=lam_init),
        grid_spec=grid_spec,
        out_shape=jax.ShapeDtypeStruct((db, t, w), BF16),
        compiler_params=_cparams("parallel", "arbitrary"),
        name="diff_attn_sample",
    )(page_table, wl["diff_lam"], wl["diff_sg"], fq3, fk3, h3, *([pool_k] * npg), *([pool_v] * npg))


def _sortable(score):
    bits = pltpu.bitcast(jnp.where(score == 0.0, 0.0, score), I32)
    return bits ^ ((bits >> 31) & 0x7FFFFFFF)


def _search_threshold(count_ge, rows, n_sel):
    def bit_body(i, t_u):
        cand_u = t_u | (jnp.int32(1) << (31 - i))
        cnt = count_ge(cand_u ^ INT_MIN)
        return jnp.where(cnt >= n_sel, cand_u, t_u)

    t_u = lax.fori_loop(0, 32, bit_body, jnp.zeros((rows, 1), I32))
    return t_u ^ INT_MIN


def _lane_fold(x):
    acc = x[:, 0:LANES]
    for c in range(1, x.shape[1] // LANES):
        acc = acc + x[:, c * LANES:(c + 1) * LANES]
    return acc


def _strictly_before(n):
    ui = lax.broadcasted_iota(I32, (n, n), 0)
    uj = lax.broadcasted_iota(I32, (n, n), 1)
    return jnp.where(ui < uj, 1.0, 0.0).astype(BF16)


def _dsa_prompt_body(iq3_ref, misc_ref, ik3_ref, q_ref, k_ref, v_ref, o_ref, iqs, qs, keys_scr, bias_scr,
                     m_scr, l_scr, acc_scr, *, tq, ck, n_sel):
    qi = pl.program_id(1)
    nck = (qi * tq + tq + ck - 1) // ck
    kw = iqs.shape[1]
    for hh in range(IDX_HEADS):
        iqs[hh * tq:(hh + 1) * tq, :] = iq3_ref[:, hh * kw:(hh + 1) * kw]
    wts = misc_ref[...][:, M_IDXW:M_IDXW + IDX_HEADS] * (IDX_HD ** -0.5 * IDX_HEADS ** -0.5)
    row_pos = qi * tq + lax.broadcasted_iota(I32, (tq, ck), 0)
    lane_pos = lax.broadcasted_iota(I32, (tq, ck), 1)

    def score_chunk(c, carry):
        k0 = pl.multiple_of(c * ck, ck)
        s = jnp.maximum(_dot_nt(iqs[...], ik3_ref[pl.ds(k0, ck), :]), 0.0)
        sc = wts[:, 0:1] * s[0:tq, :]
        for hh in range(1, IDX_HEADS):
            sc = sc + wts[:, hh:hh + 1] * s[hh * tq:(hh + 1) * tq, :]
        keys_scr[c] = jnp.where(c * ck + lane_pos <= row_pos, _sortable(sc), INT_MIN)
        return carry

    lax.fori_loop(0, nck, score_chunk, 0)

    def count(pred):
        def body(c, acc):
            return acc + _lane_fold(jnp.where(pred(keys_scr[c]), 1, 0).astype(I32))
        acc = lax.fori_loop(0, nck, body, jnp.zeros((tq, LANES), I32))
        return jnp.sum(acc, axis=-1, keepdims=True)

    thr = _search_threshold(lambda cand: count(lambda kk: kk >= cand), tq, n_sel)
    n_ge = count(lambda kk: kk >= thr)
    plain = jnp.max(jnp.where(jnp.logical_and(n_ge == n_sel, thr != INT_MIN), 0, 1)) == 0

    @pl.when(plain)
    def _():
        def body(c, carry):
            bias_scr[c] = jnp.where(keys_scr[c] >= thr, 0.0, NEG_BIG)
            return carry
        lax.fori_loop(0, nck, body, 0)

    @pl.when(jnp.logical_not(plain))
    def _():
        need = (n_sel - count(lambda kk: kk > thr)).astype(F32)
        before = _strictly_before(ck)

        def body(c, eq_seen):
            kk = keys_scr[c]
            eq = jnp.logical_and(kk == thr, c * ck + lane_pos <= row_pos)
            eqf = jnp.where(eq, 1.0, 0.0)
            rank = _dot(eqf.astype(BF16), before) + eq_seen
            sel = jnp.logical_or(kk > thr, jnp.logical_and(eq, rank < need))
            bias_scr[c] = jnp.where(sel, 0.0, NEG_BIG)
            return eq_seen + jnp.sum(eqf, axis=-1, keepdims=True)

        lax.fori_loop(0, nck, body, jnp.zeros((tq, 1), F32))

    _reset_softmax(m_scr, l_scr, acc_scr)
    hpg = DSA_HEADS // DSA_KV_HEADS
    for hd in range(DSA_HEADS):
        qs[hd * tq:(hd + 1) * tq, :] = q_ref[:, hd * DSA_HD:(hd + 1) * DSA_HD]

    def attend_chunk(c, carry):
        k0 = pl.multiple_of(c * ck, ck)
        bias = bias_scr[c]
        for g in range(DSA_KV_HEADS):
            rows = pl.ds(g * hpg * tq, hpg * tq)
            kg = k_ref[pl.ds(k0, ck), g * DSA_HD:(g + 1) * DSA_HD]
            vg = v_ref[pl.ds(k0, ck), g * DSA_HD:(g + 1) * DSA_HD]
            s = _dot_nt(qs[rows, :], kg).reshape(hpg, tq, ck) + bias[None]
            _online_update(s.reshape(hpg * tq, ck), [vg], m_scr, l_scr, acc_scr, rows)
        return carry

    lax.fori_loop(0, nck, attend_chunk, 0)
    acc = acc_scr[...] / l_scr[...]
    for hd in range(DSA_HEADS):
        o_ref[:, hd * DSA_HD:(hd + 1) * DSA_HD] = acc[hd * tq:(hd + 1) * tq, :].astype(o_ref.dtype)


def _dsa_prompt_call(iq33, ik33, dq3, dkb3, dvb3, h3):
    b, t, _ = dq3.shape
    n_sel = min(TOPK_MAX, t // 4)
    tq = _tile(t, 128)
    ck = _tile(t, 512)
    kw = iq33.shape[-1] // IDX_HEADS
    full = lambda width: pl.BlockSpec((None, t, width), lambda i, q: (i, 0, 0))
    return pl.pallas_call(
        functools.partial(_dsa_prompt_body, tq=tq, ck=ck, n_sel=n_sel),
        grid=(b, t // tq),
        in_specs=[pl.BlockSpec((None, tq, IDX_HEADS * kw), lambda i, q: (i, q, 0)),
                  pl.BlockSpec((None, tq, LANES), lambda i, q: (i, q, C_MISC // LANES)),
                  full(kw),
                  pl.BlockSpec((None, tq, MIX_WIDTH), lambda i, q: (i, q, 0)),
                  full(LANES), full(LANES)],
        out_specs=pl.BlockSpec((None, tq, MIX_WIDTH), lambda i, q: (i, q, 0)),
        out_shape=jax.ShapeDtypeStruct((b, t, MIX_WIDTH), BF16),
        scratch_shapes=[pltpu.VMEM((IDX_HEADS * tq, kw), BF16), pltpu.VMEM((DSA_HEADS * tq, DSA_HD), BF16),
                        pltpu.VMEM((t // ck, tq, ck), I32), pltpu.VMEM((t // ck, tq, ck), F32),
                        pltpu.VMEM((DSA_HEADS * tq, 1), F32), pltpu.VMEM((DSA_HEADS * tq, 1), F32),
                        pltpu.VMEM((DSA_HEADS * tq, DSA_HD), F32)],
        compiler_params=_cparams("parallel", "arbitrary"),
        name="dsa_prompt",
    )(iq33, h3, ik33, dq3, dkb3, dvb3)


def _dsa_select_body(pt_ref, iq3_ref, misc_ref, ikn_ref, *rest, tq, npg, n_sel, n_pages):
    pages = rest[:npg]
    sel_ref, keys_scr, iqs = rest[npg:]
    s_id = pl.program_id(1)
    n_steps = pl.num_programs(1)
    kw = iqs.shape[1]
    wts = misc_ref[...][:, M_IDXW:M_IDXW + IDX_HEADS] * (IDX_HD ** -0.5 * IDX_HEADS ** -0.5)

    @pl.when(s_id == 0)
    def _():
        for hh in range(IDX_HEADS):
            iqs[hh * tq:(hh + 1) * tq, :] = iq3_ref[:, hh * kw:(hh + 1) * kw]

    def scores(ik, feature_major):
        hi, lo = _split_hi_lo(ik)
        a_hi = iqs[:, 0:IDX_HD]
        a_lo = iqs[:, 2 * IDX_HD:3 * IDX_HD]
        mm = _dot if feature_major else _dot_nt
        s = jnp.maximum(mm(a_hi, hi) + mm(a_hi, lo) + mm(a_lo, hi), 0.0)
        sc = wts[:, 0:1] * s[0:tq, :]
        for hh in range(1, IDX_HEADS):
            sc = sc + wts[:, hh:hh + 1] * s[hh * tq:(hh + 1) * tq, :]
        return sc

    @pl.when(s_id < n_steps - 1)
    def _():
        for p in range(npg):
            keys_scr[s_id * npg + p] = _sortable(scores(pages[p][...], True))

    @pl.when(s_id == n_steps - 1)
    def _():
        sc = scores(ikn_ref[...], False)
        qrow = lax.broadcasted_iota(I32, (tq, tq), 0)
        kcol = lax.broadcasted_iota(I32, (tq, tq), 1)
        new_keys = jnp.where(kcol <= qrow, _sortable(sc), INT_MIN)
        keys_scr[n_pages] = jnp.concatenate([new_keys, jnp.full((tq, LANES - tq), INT_MIN, I32)], axis=-1)

        def count(pred):
            acc = jnp.sum(jnp.where(pred(keys_scr[...]), 1, 0).astype(I32), axis=0)
            return jnp.sum(acc, axis=-1, keepdims=True)

        thr = _search_threshold(lambda cand: count(lambda kk: kk >= cand[None]), tq, n_sel)
        n_ge = count(lambda kk: kk >= thr[None])
        plain = jnp.max(jnp.where(jnp.logical_and(n_ge == n_sel, thr != INT_MIN), 0, 1)) == 0

        @pl.when(plain)
        def _():
            sel_ref[...] = jnp.where(keys_scr[...] >= thr[None], 1.0, 0.0)

        @pl.when(jnp.logical_not(plain))
        def _():
            need = (n_sel - count(lambda kk: kk > thr[None])).astype(F32)
            before = _strictly_before(LANES)
            lane = lax.broadcasted_iota(I32, (tq, LANES), 1)
            qr = lax.broadcasted_iota(I32, (tq, LANES), 0)

            def emit(c, eq_seen):
                kk = keys_scr[c]
                allowed = jnp.logical_or(c < n_pages, lane <= qr)
                eq = jnp.logical_and(kk == thr, allowed)
                eqf = jnp.where(eq, 1.0, 0.0)
                rank = _dot(eqf.astype(BF16), before) + eq_seen
                sel = jnp.logical_or(kk > thr, jnp.logical_and(eq, rank < need))
                sel_ref[c] = jnp.where(sel, 1.0, 0.0)
                return eq_seen + jnp.sum(eqf, axis=-1, keepdims=True)

            lax.fori_loop(0, n_pages + 1, emit, jnp.zeros((tq, 1), F32))


def _dsa_attend_body(pt_ref, q_ref, kn_ref, vn_ref, selp_ref, seln_ref, *rest, tq, npg):
    kp = rest[:npg]
    vp = rest[npg:2 * npg]
    o_ref, qrows, m_scr, l_scr, acc_scr = rest[2 * npg:]
    s_id = pl.program_id(1)
    n_steps = pl.num_programs(1)
    hpg = DSA_HEADS // DSA_KV_HEADS
    grows = hpg * tq

    @pl.when(s_id == 0)
    def _():
        _reset_softmax(m_scr, l_scr, acc_scr)
        for hd in range(DSA_HEADS):
            qrows[hd * tq:(hd + 1) * tq, :] = q_ref[:, hd * DSA_HD:(hd + 1) * DSA_HD].astype(F32)

    qb = qrows[...].astype(BF16)
    qg = [qb[g * grows:(g + 1) * grows, :] for g in range(DSA_KV_HEADS)]

    def update(s, pv_fn):
        m_old = m_scr[...]
        m_new = jnp.maximum(m_old, jnp.max(s, axis=-1, keepdims=True))
        alpha = jnp.exp(m_old - m_new)
        p = jnp.exp(s - m_new)
        l_scr[...] = alpha * l_scr[...] + jnp.sum(p, axis=-1, keepdims=True)
        pb = p.astype(BF16)
        pv = jnp.concatenate([pv_fn(pb[g * grows:(g + 1) * grows, :], g) for g in range(DSA_KV_HEADS)], axis=0)
        acc_scr[...] = alpha * acc_scr[...] + pv
        m_scr[...] = m_new

    @pl.when(s_id < n_steps - 1)
    def _():
        parts = []
        for p in range(npg):
            s = jnp.concatenate([_dot(qg[g], kp[p][g].astype(BF16)) for g in range(DSA_KV_HEADS)], axis=0)
            sel = jnp.concatenate([selp_ref[p]] * DSA_HEADS, axis=0) > 0.5
            parts.append(jnp.where(sel, s, NEG_BIG))

        def pv_fn(pb, g):
            out = None
            for p in range(npg):
                part = _dot_nt(pb[:, p * PAGE_SIZE:(p + 1) * PAGE_SIZE], vp[p][g].astype(BF16))
                out = part if out is None else out + part
            return out

        update(jnp.concatenate(parts, axis=-1), pv_fn)

    @pl.when(s_id == n_steps - 1)
    def _():
        kn = kn_ref[...].astype(BF16)
        vn = vn_ref[...].astype(BF16)
        s = jnp.concatenate([_dot_nt(qg[g], kn[:, g * DSA_HD:(g + 1) * DSA_HD]) for g in range(DSA_KV_HEADS)], axis=0)
        sel = jnp.concatenate([seln_ref[...][:, 0:tq]] * DSA_HEADS, axis=0) > 0.5
        update(jnp.where(sel, s, NEG_BIG), lambda pb, g: _dot(pb, vn[:, g * DSA_HD:(g + 1) * DSA_HD]))
        acc = acc_scr[...] / l_scr[...]
        for hd in range(DSA_HEADS):
            o_ref[:, hd * DSA_HD:(hd + 1) * DSA_HD] = acc[hd * tq:(hd + 1) * tq, :].astype(o_ref.dtype)


def _dsa_sample_call(iq33, ik3, dq3, dk3, h3, pool_k, pool_v, pool_ik, page_table, layer):
    db, t, _ = dq3.shape
    n_pages = page_table.shape[1]
    n_keys = n_pages * PAGE_SIZE + t
    n_sel = min(TOPK_MAX, n_keys // 4)
    npg = 8 if n_pages % 8 == 0 else 1
    n_steps = n_pages // npg + 1
    kw = iq33.shape[-1] // IDX_HEADS
    sw = LANES

    def page_spec(*tail):
        def make(p):
            def imap(i, s, pt):
                return (layer, pt[i, jnp.minimum(s, n_steps - 2) * npg + p]) + (0,) * len(tail)
            return pl.BlockSpec((None, None) + tail, imap)
        return [make(p) for p in range(npg)]

    sel = pl.pallas_call(
        functools.partial(_dsa_select_body, tq=t, npg=npg, n_sel=n_sel, n_pages=n_pages),
        grid_spec=pltpu.PrefetchScalarGridSpec(
            num_scalar_prefetch=1,
            grid=(db, n_steps),
            in_specs=[pl.BlockSpec((None, t, IDX_HEADS * kw), lambda i, s, pt: (i, 0, 0)),
                      pl.BlockSpec((None, t, LANES), lambda i, s, pt: (i, 0, C_MISC // LANES)),
                      pl.BlockSpec((None, t, IDX_HD), lambda i, s, pt: (i, 0, 0))]
                     + page_spec(IDX_HD, PAGE_SIZE),
            out_specs=pl.BlockSpec((None, n_pages + 1, t, sw), lambda i, s, pt: (i, 0, 0, 0)),
            scratch_shapes=[pltpu.VMEM((n_pages + 1, t, LANES), I32), pltpu.VMEM((IDX_HEADS * t, kw), BF16)],
        ),
        out_shape=jax.ShapeDtypeStruct((db, n_pages + 1, t, sw), F32),
        compiler_params=_cparams("parallel", "arbitrary"),
        name="dsa_sample_select",
    )(page_table, iq33, h3, ik3, *([pool_ik] * npg))

    nrow = DSA_HEADS * t
    return pl.pallas_call(
        functools.partial(_dsa_attend_body, tq=t, npg=npg),
        grid_spec=pltpu.PrefetchScalarGridSpec(
            num_scalar_prefetch=1,
            grid=(db, n_steps),
            in_specs=[pl.BlockSpec((None, t, MIX_WIDTH), lambda i, s, pt: (i, 0, 0)),
                      pl.BlockSpec((None, t, LANES), lambda i, s, pt: (i, 0, 0)),
                      pl.BlockSpec((None, t, LANES), lambda i, s, pt: (i, 0, C_DSAV // LANES)),
                      pl.BlockSpec((None, npg, t, sw), lambda i, s, pt: (i, jnp.minimum(s, n_steps - 2), 0, 0)),
                      pl.BlockSpec((None, None, t, sw), lambda i, s, pt: (i, n_pages, 0, 0))]
                     + page_spec(DSA_KV_HEADS, DSA_HD, PAGE_SIZE) + page_spec(DSA_KV_HEADS, DSA_HD, PAGE_SIZE),
            out_specs=pl.BlockSpec((None, t, MIX_WIDTH), lambda i, s, pt: (i, 0, 0)),
            scratch_shapes=[pltpu.VMEM((nrow, DSA_HD), F32), pltpu.VMEM((nrow, 1), F32), pltpu.VMEM((nrow, 1), F32),
                            pltpu.VMEM((nrow, DSA_HD), F32)],
        ),
        out_shape=jax.ShapeDtypeStruct((db, t, MIX_WIDTH), BF16),
        compiler_params=_cparams("parallel", "arbitrary"),
        name="dsa_sample_attend",
    )(page_table, dq3, dk3, h3, sel, sel, *([pool_k] * npg), *([pool_v] * npg))


def _merge_body(x_ref, gate_ref, oa_ref, ob_ref, oc_ref, od_ref, wb_ref, wo_ref, o_ref):
    acc = None
    for bi, br in enumerate((oa_ref, ob_ref, oc_ref, od_ref)):
        proj = _dot(br[...], wb_ref[bi])
        term = _sigmoid(gate_ref[:, bi * D_MODEL:(bi + 1) * D_MODEL]) * proj
        acc = term if acc is None else acc + term
    o_ref[...] = x_ref[...] + _dot(acc.astype(BF16), wo_ref[...])


def _merge_call(x, h, branches, wl):
    n, d = x.shape
    tm = _tile(n, 256)
    br = pl.BlockSpec((tm, MIX_WIDTH), lambda i: (i, 0))
    return pl.pallas_call(
        _merge_body,
        grid=(n // tm,),
        in_specs=[pl.BlockSpec((tm, d), lambda i: (i, 0)),
                  pl.BlockSpec((tm, N_BRANCH * d), lambda i: (i, C_GATE // (N_BRANCH * d))),
                  br, br, br, br,
                  pl.BlockSpec((N_BRANCH, MIX_WIDTH, d), lambda i: (0, 0, 0)),
                  pl.BlockSpec((d, d), lambda i: (0, 0))],
        out_specs=pl.BlockSpec((tm, d), lambda i: (i, 0)),
        out_shape=jax.ShapeDtypeStruct((n, d), F32),
        compiler_params=_cparams("parallel"),
        name="merge",
    )(x, h, *branches, wl["w_branch"], wl["w_out"])


def _cross_body(x_ref, g_ref, wq_ref, mk_ref, mv_ref, wo_ref, o_ref, *, merged):
    x = x_ref[...]
    tm = x.shape[0]
    xn = _rms_rows(x, g_ref[...]).astype(BF16)
    q = _dot(xn, wq_ref[...]).astype(BF16)
    if merged:
        qrows = jnp.concatenate([q[:, hh * XA_HD:(hh + 1) * XA_HD] for hh in range(XA_HEADS)], axis=0)
        s = _dot_nt(qrows, mk_ref[...].astype(BF16)) * (XA_HD ** -0.5)
        rr = lax.broadcasted_iota(I32, s.shape, 0) // tm
        cc = lax.broadcasted_iota(I32, s.shape, 1) % XA_HEADS
        s = jnp.where(rr == cc, s, NEG_BIG)
        p = jnp.exp(s - jnp.max(s, axis=-1, keepdims=True))
        p = p / jnp.sum(p, axis=-1, keepdims=True)
        orow = _dot(p.astype(BF16), mv_ref[...].astype(BF16))
        outs = [orow[hh * tm:(hh + 1) * tm, :] for hh in range(XA_HEADS)]
    else:
        outs = []
        for hh in range(XA_HEADS):
            sl = slice(hh * XA_HD, (hh + 1) * XA_HD)
            s = _dot_nt(q[:, sl], mk_ref[:, sl].astype(BF16)) * (XA_HD ** -0.5)
            p = jnp.exp(s - jnp.max(s, axis=-1, keepdims=True))
            p = p / jnp.sum(p, axis=-1, keepdims=True)
            outs.append(_dot(p.astype(BF16), mv_ref[:, sl].astype(BF16)))
    o = jnp.concatenate(outs, axis=-1).astype(BF16)
    o_ref[...] = x + _dot(o, wo_ref[...])


def _cross_call(x3, wl, mk4, mv4, layer):
    b, t, d = x3.shape
    w = XA_HEADS * XA_HD
    tm = _tile(t, 512)
    merged = mk4.shape[-1] == XA_HD
    mem = pl.BlockSpec((None, None) + mk4.shape[2:], lambda i, j: (layer, i, 0, 0))
    return pl.pallas_call(
        functools.partial(_cross_body, merged=merged),
        grid=(b, t // tm),
        in_specs=[pl.BlockSpec((None, tm, d), lambda i, j: (i, j, 0)),
                  pl.BlockSpec((1, d), lambda i, j: (0, 0)),
                  pl.BlockSpec((d, w), lambda i, j: (0, 0)), mem, mem,
                  pl.BlockSpec((w, d), lambda i, j: (0, 0))],
        out_specs=pl.BlockSpec((None, tm, d), lambda i, j: (i, j, 0)),
        out_shape=jax.ShapeDtypeStruct((b, t, d), F32),
        compiler_params=_cparams("parallel", "parallel"),
        name="cross_attn",
    )(x3, wl["norm_xa_g"], wl["xa_wq"], mk4, mv4, wl["xa_wo"])


def _ffn_body(x_ref, g_ref, wg_ref, wu_ref, wd_ref, o_ref, xn_scr, acc_scr):
    f = pl.program_id(1)

    @pl.when(f == 0)
    def _():
        xn_scr[...] = _rms_rows(x_ref[...], g_ref[...]).astype(BF16)
        acc_scr[...] = x_ref[...]

    xn = xn_scr[...]
    hm = _silu(_dot(xn, wg_ref[...])) * _dot(xn, wu_ref[...])
    acc_scr[...] += _dot(hm.astype(BF16), wd_ref[...])

    @pl.when(f == pl.num_programs(1) - 1)
    def _():
        o_ref[...] = acc_scr[...]


def _ffn_call(x, wl):
    n, d = x.shape
    ff = wl["ffn_wg"].shape[1]
    tm = _tile(n, 512)
    tf = 256
    return pl.pallas_call(
        _ffn_body,
        grid=(n // tm, ff // tf),
        in_specs=[pl.BlockSpec((tm, d), lambda i, f: (i, 0)), pl.BlockSpec((1, d), lambda i, f: (0, 0)),
                  pl.BlockSpec((d, tf), lambda i, f: (0, f)), pl.BlockSpec((d, tf), lambda i, f: (0, f)),
                  pl.BlockSpec((tf, d), lambda i, f: (f, 0))],
        out_specs=pl.BlockSpec((tm, d), lambda i, f: (i, 0)),
        out_shape=jax.ShapeDtypeStruct((n, d), F32),
        scratch_shapes=[pltpu.VMEM((tm, d), BF16), pltpu.VMEM((tm, d), F32)],
        compiler_params=_cparams("parallel", "arbitrary"),
        name="ffn_swiglu",
    )(x, wl["norm_ffn_g"], wl["ffn_wg"], wl["ffn_wu"], wl["ffn_wd"])


def _moe_body(x_ref, g_ref, wr_ref, br_ref, wg_ref, wu_ref, wd_ref, o_ref, xn_scr, gate_scr, acc_scr):
    e = pl.program_id(1)
    f = pl.program_id(2)
    first = jnp.logical_and(e == 0, f == 0)

    @pl.when(first)
    def _():
        xn = _rms_rows(x_ref[...], g_ref[...])
        xn_scr[...] = xn.astype(BF16)
        acc_scr[...] = x_ref[...]
        logits = _dot_hi(xn, wr_ref[...]) + br_ref[...]
        lane = lax.broadcasted_iota(I32, logits.shape, 1)
        logits = jnp.where(lane < N_EXPERTS, logits, -jnp.inf)
        top1 = jnp.max(logits, axis=-1, keepdims=True)
        idx1 = jnp.min(jnp.where(logits == top1, lane, LANES), axis=-1, keepdims=True)
        rest = jnp.where(lane == idx1, -jnp.inf, logits)
        top2 = jnp.max(rest, axis=-1, keepdims=True)
        idx2 = jnp.min(jnp.where(rest == top2, lane, LANES), axis=-1, keepdims=True)
        e2 = jnp.exp(top2 - top1)
        p1 = 1.0 / (1.0 + e2)
        p2 = e2 / (1.0 + e2)
        gate_scr[...] = jnp.where(lane == idx1, p1, 0.0) + jnp.where(lane == idx2, p2, 0.0)

    gates = gate_scr[...]
    lane = lax.broadcasted_iota(I32, gates.shape, 1)
    ge = jnp.sum(jnp.where(lane == e, gates, 0.0), axis=-1, keepdims=True)
    xn = xn_scr[...]
    hm = _silu(_dot(xn, wg_ref[...])) * _dot(xn, wu_ref[...])
    acc_scr[...] += ge * _dot(hm.astype(BF16), wd_ref[...])

    @pl.when(jnp.logical_and(e == pl.num_programs(1) - 1, f == pl.num_programs(2) - 1))
    def _():
        o_ref[...] = acc_scr[...]


def _moe_call(x, wl):
    n, d = x.shape
    ne, _, fe = wl["moe_wg"].shape
    tm = _tile(n, 512)
    tf = _tile(fe, 512)
    return pl.pallas_call(
        _moe_body,
        grid=(n // tm, ne, fe // tf),
        in_specs=[pl.BlockSpec((tm, d), lambda i, e, f: (i, 0)), pl.BlockSpec((1, d), lambda i, e, f: (0, 0)),
                  pl.BlockSpec((d, LANES), lambda i, e, f: (0, 0)), pl.BlockSpec((1, LANES), lambda i, e, f: (0, 0)),
                  pl.BlockSpec((None, d, tf), lambda i, e, f: (e, 0, f)),
                  pl.BlockSpec((None, d, tf), lambda i, e, f: (e, 0, f)),
                  pl.BlockSpec((None, tf, d), lambda i, e, f: (e, f, 0))],
        out_specs=pl.BlockSpec((tm, d), lambda i, e, f: (i, 0)),
        out_shape=jax.ShapeDtypeStruct((n, d), F32),
        scratch_shapes=[pltpu.VMEM((tm, d), BF16), pltpu.VMEM((tm, LANES), F32), pltpu.VMEM((tm, d), F32)],
        compiler_params=_cparams("parallel", "arbitrary", "arbitrary"),
        name="moe_swiglu",
    )(x, wl["norm_ffn_g"], wl["moe_wr"], wl["moe_br"], wl["moe_wg"], wl["moe_wu"], wl["moe_wd"])


def _pad_rows(a, rows):
    return jnp.pad(a, ((0, rows - a.shape[0]), (0, 0)))


def _lane_row(vals, lane0):
    return jnp.zeros((1, LANES), F32).at[0, lane0:lane0 + vals.shape[0]].set(vals.astype(F32))


def _block_diag(w):
    nb, n, _ = w.shape
    eye = jnp.eye(nb, dtype=w.dtype)
    return jnp.einsum("aij,ab->aibj", w, eye).reshape(nb * n, nb * n)


def _layer_weights(l, p):
    w_in = p["w_in"][l]
    d = w_in.shape[0]
    misc = jnp.concatenate([w_in[:, O_IDXK:O_IDXK + IDX_HD + IDX_HEADS], w_in[:, O_GDNB:O_GDNB + 2 * GDN_HEADS],
                            jnp.zeros((d, LANES - IDX_HD - IDX_HEADS - 2 * GDN_HEADS), w_in.dtype)], axis=1)
    w_in_r = jnp.concatenate([
        w_in[:, O_GATE:], w_in[:, O_LRUX:O_DSAK], w_in[:, O_DIFQ:O_GATE], w_in[:, O_GDNZ:O_GDNB],
        w_in[:, O_GDNQKV:O_GDNZ], w_in[:, O_IDXQ:O_IDXK], w_in[:, O_DSAK:O_IDXQ], misc], axis=1)
    assert w_in_r.shape[1] == H_COLS
    wl = dict(
        norm_mix_g=p["norm_mix_g"][l], w_in=w_in_r.astype(BF16),
        lru_cw=_pad_rows(p["lru_conv_w"][l], SUBLANES), lru_cb=p["lru_conv_b"][l].reshape(1, -1),
        lru_wa=_block_diag(p["lru_wa"][l]).astype(BF16), lru_ba=p["lru_ba"][l].reshape(1, -1),
        lru_wx=_block_diag(p["lru_wx"][l]).astype(BF16), lru_bx=p["lru_bx"][l].reshape(1, -1),
        lru_lam=p["lru_lambda"][l].reshape(1, -1),
        gdn_cw=_pad_rows(p["gdn_conv_w"][l], SUBLANES),
        gdn_alog=_lane_row(p["gdn_a_log"][l], M_GDNA), gdn_dtb=_lane_row(p["gdn_dt_bias"][l], M_GDNA),
        gdn_ng=p["gdn_norm_g"][l].reshape(1, -1),
        diff_lam=p["diff_lambda"][l], diff_sg=p["diff_subln_g"][l].reshape(1, -1),
        w_branch=p["w_branch"][l].astype(BF16), w_out=p["w_out"][l].astype(BF16),
        norm_xa_g=p["norm_xa_g"][l].reshape(1, -1), norm_mem_g=p["norm_mem_g"][l],
        xa_wq=p["xa_wq"][l].astype(BF16), xa_wo=p["xa_wo"][l].astype(BF16),
        xa_wkv=jnp.concatenate([p["xa_wk"][l], p["xa_wv"][l]], axis=1).astype(BF16),
        norm_ffn_g=p["norm_ffn_g"][l].reshape(1, -1),
    )
    j = l // 2
    if l % 2 == 0:
        wl.update(ffn_wg=p["ffn_w_gate"][j].astype(BF16), ffn_wu=p["ffn_w_up"][j].astype(BF16),
                  ffn_wd=p["ffn_w_down"][j].astype(BF16))
    else:
        wl.update(moe_wr=jnp.pad(p["moe_router_w"][j], ((0, 0), (0, LANES - N_EXPERTS))),
                  moe_br=jnp.pad(p["moe_router_b"][j], (0, LANES - N_EXPERTS)).reshape(1, LANES),
                  moe_wg=p["moe_w_gate"][j].astype(BF16), moe_wu=p["moe_w_up"][j].astype(BF16),
                  moe_wd=p["moe_w_down"][j].astype(BF16))
    return wl


def _rope_tables(pos):
    half = DSA_HD // 2
    inv = ROPE_THETA ** (-jnp.arange(half, dtype=F32) / half)
    ang = pos.astype(F32)[:, None] * inv[None, :]
    cos, sin = jnp.cos(ang), jnp.sin(ang)
    return jnp.tile(cos, (1, 4)), jnp.tile(jnp.concatenate([-sin, sin], axis=1), (1, 2))


def _front_pad(buf):
    return jnp.pad(buf, ((0, 0), (SUBLANES - (CONV_W - 1), 0), (0, 0)))


def _group_layer(x3, l, wl, tabs, states, mem_kv, paged):
    b, t, d = x3.shape
    n = b * t
    x = x3.reshape(n, d)
    lam_init = 0.8 - 0.6 * math.exp(-0.3 * l)
    xn = _rmsnorm(x, wl["norm_mix_g"], BF16)
    h = _matmul(xn, wl["w_in"], tn=1408)
    h3 = h.reshape(b, t, H_COLS)
    dq, fq, fk, fkb, fvb, iq3, dk, dkb, dvb, ik, ik3 = _rope_call(h, tabs[0], tabs[1])
    to3 = lambda a: a.reshape(b, t, a.shape[-1])
    dq3, fq3, fk3, dk3, ik3d = map(to3, (dq, fq, fk, dk, ik))

    o_a, lru_h = _lru_call(h3, wl, _front_pad(states["lru_conv"]), states["lru_h"].reshape(b, 1, -1),
                           reset_first=paged is None)
    o_c, gdn_s = _gdn_call(h3, wl, _front_pad(states["gdn_conv"]), states["gdn_s"])
    if paged is None:
        o_b = _dsa_prompt_call(to3(iq3), to3(ik3), dq3, to3(dkb), to3(dvb), h3)
        o_d = _diff_prompt_call(fq3, to3(fkb), to3(fvb), wl, lam_init)
    else:
        o_b = _dsa_sample_call(to3(iq3), ik3d, dq3, dk3, h3, paged["dsa_k"], paged["dsa_v"], paged["idx_k"],
                               paged["page_table"], l)
        o_d = _diff_sample_call(fq3, fk3, h3, paged["diff_k"], paged["diff_v"], paged["page_table"], l, wl, lam_init)

    flat = lambda a: a.reshape(n, a.shape[-1])
    x1 = _merge_call(x, h, [flat(o_a), flat(o_b), flat(o_c), flat(o_d)], wl)
    x2 = _cross_call(x1.reshape(b, t, d), wl, mem_kv[0], mem_kv[1], mem_kv[2])
    x2 = x2.reshape(n, d)
    x3_new = (_ffn_call(x2, wl) if "ffn_wg" in wl else _moe_call(x2, wl)).reshape(b, t, d)

    new = dict(
        dsa_k=dk3.reshape(b, t, DSA_KV_HEADS, DSA_HD),
        dsa_v=h3[:, :, C_DSAV:C_DSAV + DSA_KV_HEADS * DSA_HD].reshape(b, t, DSA_KV_HEADS, DSA_HD),
        idx_k=ik3d,
        diff_k=fk3.reshape(b, t, DIFF_HEADS, 2 * DIFF_HD),
        diff_v=h3[:, :, C_DIFV:C_DIFV + MIX_WIDTH].reshape(b, t, DIFF_HEADS, 2 * DIFF_HD),
        lru_h=lru_h.reshape(b, -1),
        lru_conv=jnp.concatenate([states["lru_conv"], h3[:, :, C_LRUX:C_LRUX + MIX_WIDTH]], axis=1)[:, -(CONV_W - 1):]
        if t < CONV_W - 1 else h3[:, t - (CONV_W - 1):, C_LRUX:C_LRUX + MIX_WIDTH],
        gdn_s=gdn_s,
        gdn_conv=jnp.concatenate([states["gdn_conv"], h3[:, :, C_GDNQKV:C_GDNQKV + GDN_QKV]], axis=1)[:, -(CONV_W - 1):]
        if t < CONV_W - 1 else h3[:, t - (CONV_W - 1):, C_GDNQKV:C_GDNQKV + GDN_QKV],
    )
    return x3_new, new


def kernel(x_prompt, x_sample, mem_prompt, cache_dsa_k, cache_dsa_v, cache_idx_k, cache_diff_k, cache_diff_v,
           cache_mem_k, cache_mem_v, state_lru_h, state_lru_conv, state_gdn_s, state_gdn_conv, page_table,
           norm_mix_g, w_in, lru_conv_w, lru_conv_b, lru_wa, lru_ba, lru_wx, lru_bx, lru_lambda,
           gdn_conv_w, gdn_a_log, gdn_dt_bias, gdn_norm_g, diff_lambda, diff_subln_g, w_branch, w_out,
           norm_xa_g, norm_mem_g, xa_wq, xa_wk, xa_wv, xa_wo, norm_ffn_g, ffn_w_gate, ffn_w_up, ffn_w_down,
           moe_router_w, moe_router_b, moe_w_gate, moe_w_up, moe_w_down, final_norm_g):
    params = dict(
        norm_mix_g=norm_mix_g, w_in=w_in, lru_conv_w=lru_conv_w, lru_conv_b=lru_conv_b, lru_wa=lru_wa, lru_ba=lru_ba,
        lru_wx=lru_wx, lru_bx=lru_bx, lru_lambda=lru_lambda, gdn_conv_w=gdn_conv_w, gdn_a_log=gdn_a_log,
        gdn_dt_bias=gdn_dt_bias, gdn_norm_g=gdn_norm_g, diff_lambda=diff_lambda, diff_subln_g=diff_subln_g,
        w_branch=w_branch, w_out=w_out, norm_xa_g=norm_xa_g, norm_mem_g=norm_mem_g, xa_wq=xa_wq, xa_wk=xa_wk,
        xa_wv=xa_wv, xa_wo=xa_wo, norm_ffn_g=norm_ffn_g, ffn_w_gate=ffn_w_gate, ffn_w_up=ffn_w_up,
        ffn_w_down=ffn_w_down, moe_router_w=moe_router_w, moe_router_b=moe_router_b, moe_w_gate=moe_w_gate,
        moe_w_up=moe_w_up, moe_w_down=moe_w_down)
    b, s, d = x_prompt.shape
    db, t, _ = x_sample.shape
    m = mem_prompt.shape[1]
    n_pool = cache_dsa_k.shape[1]
    past = page_table.shape[1] * PAGE_SIZE

    tabs_p = _rope_tables(jnp.arange(s))
    cs, ss = _rope_tables(past + jnp.arange(t))
    tabs_s = (jnp.tile(cs, (db, 1)), jnp.tile(ss, (db, 1)))
    paged = dict(
        dsa_k=jnp.transpose(cache_dsa_k, (0, 1, 3, 4, 2)), dsa_v=jnp.transpose(cache_dsa_v, (0, 1, 3, 4, 2)),
        idx_k=jnp.transpose(cache_idx_k, (0, 1, 3, 2)), page_table=page_table,
        diff_k=cache_diff_k.reshape(DEPTH, n_pool, PAGE_SIZE * DIFF_HEADS, 2 * DIFF_HD),
        diff_v=cache_diff_v.reshape(DEPTH, n_pool, PAGE_SIZE * DIFF_HEADS, 2 * DIFF_HD))
    mem_k_s = cache_mem_k.reshape(DEPTH, db, m * XA_HEADS, XA_HD)
    mem_v_s = cache_mem_v.reshape(DEPTH, db, m * XA_HEADS, XA_HD)
    zero_states = dict(lru_conv=jnp.zeros((b, CONV_W - 1, MIX_WIDTH), F32), lru_h=jnp.zeros((b, MIX_WIDTH), F32),
                       gdn_conv=jnp.zeros((b, CONV_W - 1, GDN_QKV), F32),
                       gdn_s=jnp.zeros((b, GDN_HEADS, GDN_HD, GDN_HD), F32))

    hp, hs = x_prompt, x_sample
    new_p, new_s, mem_ks, mem_vs = [], [], [], []
    for l in range(DEPTH):
        wl = _layer_weights(l, params)
        mem_n = _rmsnorm(mem_prompt.reshape(b * m, d), wl["norm_mem_g"], BF16)
        kv = _matmul(mem_n, wl["xa_wkv"], tn=512)
        w = XA_HEADS * XA_HD
        mk = kv[:, :w].reshape(1, b, m, w)
        mv = kv[:, w:].reshape(1, b, m, w)
        hp, st_p = _group_layer(hp, l, wl, tabs_p, zero_states, (mk, mv, 0), None)
        new_p.append(st_p)
        mem_ks.append(mk.reshape(b, m, XA_HEADS, XA_HD))
        mem_vs.append(mv.reshape(b, m, XA_HEADS, XA_HD))
        states_s = dict(lru_conv=state_lru_conv[l], lru_h=state_lru_h[l], gdn_conv=state_gdn_conv[l],
                        gdn_s=state_gdn_s[l])
        hs, st_s = _group_layer(hs, l, wl, tabs_s, states_s, (mem_k_s, mem_v_s, l), paged)
        new_s.append(st_s)

    y_prompt = _rmsnorm(hp.reshape(b * s, d), final_norm_g, F32).reshape(b, s, d)
    y_sample = _rmsnorm(hs.reshape(db * t, d), final_norm_g, F32).reshape(db, t, d)
    stack = lambda states, name: jnp.stack([st[name] for st in states], axis=0)
    outs = [y_prompt, y_sample]
    for name in ("dsa_k", "dsa_v", "idx_k", "diff_k", "diff_v", "lru_h", "lru_conv", "gdn_s", "gdn_conv"):
        outs += [stack(new_p, name), stack(new_s, name)]
    outs += [jnp.stack(mem_ks, axis=0), jnp.stack(mem_vs, axis=0)]
    return tuple(outs)
```

```python
import functools
import math

import jax
import jax.numpy as jnp
from jax import lax
from jax.experimental import pallas as pl
from jax.experimental.pallas import tpu as pltpu

F32 = jnp.float32
BF16 = jnp.bfloat16
I32 = jnp.int32
I16 = jnp.int16

D_MODEL = 1024
DEPTH = 2
PAGE_SIZE = 128
MIX_WIDTH = 512
LRU_BLOCKS = 8
LRU_BLOCK = MIX_WIDTH // LRU_BLOCKS
LRU_C = 8.0
CONV_W = 4
DSA_HEADS = 8
DSA_KV_HEADS = 2
DSA_HD = 64
IDX_HEADS = 4
IDX_HD = 64
TOPK_MAX = 256
GDN_HEADS = 4
GDN_HD = 128
GDN_CHUNK = 128
DIFF_HEADS = 4
DIFF_HD = 64
XA_HEADS = 4
XA_HD = 128
N_EXPERTS = 8
ROPE_THETA = 10000.0
EPS = 1e-6
N_BRANCH = 4
GDN_QKV = 3 * GDN_HEADS * GDN_HD

LANES = 128
SUBLANES = 8
NEG_BIG = -1e30
INT_MIN = -2 ** 31
HALF_BIAS = 2 ** 15
VMEM_LIMIT = 48 * 1024 * 1024

C_GATE = 0
C_LRUX = 4096
C_LRUG = 4608
C_DSAQ = 5120
C_DIFQ = 5632
C_DIFK = 6144
C_DIFV = 6656
C_GDNZ = 7168
C_GDNQKV = 7680
C_IDXQ = 9216
C_DSAK = 9472
C_DSAV = 9600
C_MISC = 9728
H_COLS = 9856
M_IDXK = 0
M_IDXW = 64
M_GDNB = 68
M_GDNA = 72

O_LRUX, O_LRUG, O_DSAQ, O_DSAK, O_DSAV, O_IDXQ, O_IDXK, O_IDXW = 0, 512, 1024, 1536, 1664, 1792, 2048, 2112
O_GDNQKV, O_GDNZ, O_GDNB, O_GDNA, O_DIFQ, O_DIFK, O_DIFV, O_GATE = 2116, 3652, 4164, 4168, 4172, 4684, 5196, 5708


def _cparams(*sem):
    return pltpu.CompilerParams(dimension_semantics=sem, vmem_limit_bytes=VMEM_LIMIT)


def _tile(n, pref):
    return pref if n % pref == 0 else n


def _softplus(x):
    return jnp.maximum(x, 0.0) + jnp.log1p(jnp.exp(-jnp.abs(x)))


def _sigmoid(x):
    return 1.0 / (1.0 + jnp.exp(-x))


def _silu(x):
    return x * _sigmoid(x)


def _dot(a, b):
    return jnp.dot(a, b, preferred_element_type=F32)


def _dot_nt(a, b):
    return lax.dot_general(a, b, (((1,), (1,)), ((), ())), preferred_element_type=F32)


def _dot_hi(a, b):
    return jnp.dot(a, b, preferred_element_type=F32, precision=lax.Precision.HIGHEST)


def _split_hi_lo(x):
    hi = x.astype(BF16)
    return hi, (x - hi.astype(F32)).astype(BF16)


_NN = (((1,), (0,)), ((), ()))
_NT = (((1,), (1,)), ((), ()))
_TN = (((0,), (0,)), ((), ()))


def _dot3(a, b, dims=_NN):
    a_hi, a_lo = _split_hi_lo(a)
    b_hi, b_lo = _split_hi_lo(b)
    dg = functools.partial(lax.dot_general, dimension_numbers=dims, preferred_element_type=F32)
    ca, cb = dims[0][0][0], dims[0][1][0]
    if a.shape[ca] % LANES == 0:
        return dg(jnp.concatenate([a_hi, a_hi, a_lo], axis=ca), jnp.concatenate([b_hi, b_lo, b_hi], axis=cb))
    return dg(a_hi, b_hi) + dg(a_hi, b_lo) + dg(a_lo, b_hi)


def _dot_exact_lhs(a, b):
    ab = a.astype(BF16)
    b1 = b.astype(BF16)
    r1 = b - b1.astype(F32)
    b2 = r1.astype(BF16)
    b3 = (r1 - b2.astype(F32)).astype(BF16)
    if a.shape[1] % LANES == 0:
        return _dot(jnp.concatenate([ab, ab, ab], axis=1), jnp.concatenate([b1, b2, b3], axis=0))
    return _dot(ab, b1) + _dot(ab, b2) + _dot(ab, b3)


def _rms_rows(x, g):
    return x * lax.rsqrt(jnp.mean(x * x, axis=-1, keepdims=True) + EPS) * g


def _rmsnorm_body(x_ref, g_ref, o_ref):
    o_ref[...] = _rms_rows(x_ref[...], g_ref[...]).astype(o_ref.dtype)


def _rmsnorm(x, g, out_dtype):
    n, d = x.shape
    tm = _tile(n, 512)
    return pl.pallas_call(
        _rmsnorm_body,
        grid=(n // tm,),
        in_specs=[pl.BlockSpec((tm, d), lambda i: (i, 0)), pl.BlockSpec((1, d), lambda i: (0, 0))],
        out_specs=pl.BlockSpec((tm, d), lambda i: (i, 0)),
        out_shape=jax.ShapeDtypeStruct((n, d), out_dtype),
        compiler_params=_cparams("parallel"),
        name="rmsnorm",
    )(x, g.reshape(1, d))


def _matmul_body(a_ref, w_ref, o_ref, *, w_rows_are_outputs):
    mm = _dot_nt if w_rows_are_outputs else _dot
    o_ref[...] = mm(a_ref[...], w_ref[...]).astype(o_ref.dtype)


def _matmul(a, w, tn, w_rows_are_outputs=False, out_dtype=F32):
    n, k = a.shape
    c = w.shape[0] if w_rows_are_outputs else w.shape[1]
    tm = _tile(n, 512)
    w_spec = (pl.BlockSpec((tn, k), lambda j, i: (j, 0)) if w_rows_are_outputs
              else pl.BlockSpec((k, tn), lambda j, i: (0, j)))
    return pl.pallas_call(
        functools.partial(_matmul_body, w_rows_are_outputs=w_rows_are_outputs),
        grid=(c // tn, n // tm),
        in_specs=[pl.BlockSpec((tm, k), lambda j, i: (i, 0)), w_spec],
        out_specs=pl.BlockSpec((tm, tn), lambda j, i: (i, j)),
        out_shape=jax.ShapeDtypeStruct((n, c), out_dtype),
        compiler_params=_cparams("parallel", "parallel"),
        name="matmul",
    )(a, w)


def _rope_block(x, cos, sin):
    lane = lax.broadcasted_iota(I32, x.shape, 1)
    first = (lane % 64) < 32
    partner = jnp.where(first, pltpu.roll(x, LANES - 32, 1), pltpu.roll(x, 32, 1))
    return x * cos + partner * sin


def _rope_wide(x, cos, sin):
    return [_rope_block(x[:, c * LANES:(c + 1) * LANES], cos, sin) for c in range(x.shape[1] // LANES)]


def _rope_body(cos_ref, sin_ref, dq_ref, fq_ref, fk_ref, fv_ref, iq_ref, dk_ref, dv_ref, misc_ref,
               dq_o, fq_o, fk_o, fkb_o, fvb_o, iq3_o, dk_o, dkb_o, dvb_o, ik_o, ik3_o):
    cos = cos_ref[...]
    sin = sin_ref[...]
    lane = lax.broadcasted_iota(I32, cos.shape, 1)
    low = lane < IDX_HD
    for c, blk in enumerate(_rope_wide(dq_ref[...], cos, sin)):
        dq_o[:, c * LANES:(c + 1) * LANES] = (blk * DSA_HD ** -0.5).astype(BF16)
    for c, blk in enumerate(_rope_wide(fq_ref[...], cos, sin)):
        fq_o[:, c * LANES:(c + 1) * LANES] = (blk * DIFF_HD ** -0.5).astype(BF16)
    for c, blk in enumerate(_rope_wide(fk_ref[...], cos, sin)):
        fk_o[:, c * LANES:(c + 1) * LANES] = blk
        fkb_o[:, c * LANES:(c + 1) * LANES] = blk.astype(BF16)
    fvb_o[...] = fv_ref[...].astype(BF16)
    dk = _rope_block(dk_ref[...], cos, sin)
    dk_o[...] = dk
    dkb_o[...] = dk.astype(BF16)
    dv = dv_ref[...]
    dvb_o[:, 0:LANES] = jnp.where(low, dv, 1.0).astype(BF16)
    dvb_o[:, LANES:2 * LANES] = jnp.where(low, pltpu.roll(dv, DSA_HD, 1), 1.0).astype(BF16)
    for c, blk in enumerate(_rope_wide(iq_ref[...], cos, sin)):
        hi = blk.astype(BF16).astype(F32)
        lo = blk - hi
        hi_sw = pltpu.roll(hi, IDX_HD, 1)
        lo_sw = pltpu.roll(lo, IDX_HD, 1)
        base = 2 * c * 2 * LANES
        iq3_o[:, base:base + LANES] = jnp.where(low, hi, hi_sw).astype(BF16)
        iq3_o[:, base + LANES:base + 2 * LANES] = jnp.where(low, lo, 0.0).astype(BF16)
        iq3_o[:, base + 2 * LANES:base + 3 * LANES] = jnp.where(low, hi_sw, hi).astype(BF16)
        iq3_o[:, base + 3 * LANES:base + 4 * LANES] = jnp.where(low, lo_sw, 0.0).astype(BF16)
    ik = _rope_block(misc_ref[...], cos, sin)
    ik_o[...] = ik[:, M_IDXK:M_IDXK + IDX_HD]
    hi = ik.astype(BF16).astype(F32)
    lo = ik - hi
    ik3_o[:, 0:LANES] = jnp.where(low, hi, pltpu.roll(lo, IDX_HD, 1)).astype(BF16)
    ik3_o[:, LANES:2 * LANES] = jnp.where(low, hi, 0.0).astype(BF16)


def _rope_call(h, cos, sin):
    n = h.shape[0]
    tm = _tile(min(n, cos.shape[0]), 512)
    nt = cos.shape[0] // tm

    def col(width, off):
        return pl.BlockSpec((tm, width), lambda i: (i, off // width))

    def out(width):
        return pl.BlockSpec((tm, width), lambda i: (i, 0))

    sds = jax.ShapeDtypeStruct
    tab = pl.BlockSpec((tm, LANES), lambda i: (i % nt, 0))
    return pl.pallas_call(
        _rope_body,
        grid=(n // tm,),
        in_specs=[tab, tab, col(512, C_DSAQ), col(512, C_DIFQ), col(512, C_DIFK), col(512, C_DIFV),
                  col(256, C_IDXQ), col(128, C_DSAK), col(128, C_DSAV), col(128, C_MISC)],
        out_specs=[out(512), out(512), out(512), out(512), out(512), out(1024), out(128), out(128), out(256),
                   out(IDX_HD), out(256)],
        out_shape=[sds((n, 512), BF16), sds((n, 512), BF16), sds((n, 512), F32), sds((n, 512), BF16),
                   sds((n, 512), BF16), sds((n, 1024), BF16), sds((n, 128), F32), sds((n, 128), BF16),
                   sds((n, 256), BF16), sds((n, IDX_HD), F32), sds((n, 256), BF16)],
        compiler_params=_cparams("parallel"),
        name="rope",
    )(cos, sin, h, h, h, h, h, h, h, h)


def _conv_tile(x_ref, buf_ref, cw_ref, xcat, t, tc):
    @pl.when(t == 0)
    def _():
        xcat[0:SUBLANES, :] = buf_ref[...]

    @pl.when(t > 0)
    def _():
        xcat[0:SUBLANES, :] = xcat[tc:tc + SUBLANES, :]

    xcat[SUBLANES:SUBLANES + tc, :] = x_ref[...]
    cw = cw_ref[...]
    y = x_ref[...] * cw[CONV_W - 1:CONV_W, :]
    for j in range(CONV_W - 1):
        off = SUBLANES - (CONV_W - 1) + j
        y = y + xcat[off:off + tc, :] * cw[j:j + 1, :]
    return y


def _lru_body(x_ref, gate_ref, cw_ref, cb_ref, wa_ref, ba_ref, wx_ref, bx_ref, lam_ref, buf_ref, h0_ref,
              o_ref, hl_ref, xcat, hc, *, tc, reset_first):
    t = pl.program_id(1)

    @pl.when(t == 0)
    def _():
        hc[...] = h0_ref[...]

    xa = _conv_tile(x_ref, buf_ref, cw_ref, xcat, t, tc) + cb_ref[...]
    xb = xa.astype(BF16)
    r = _sigmoid(_dot(xb, wa_ref[...]) + ba_ref[...])
    gi = _sigmoid(_dot(xb, wx_ref[...]) + bx_ref[...])
    log_a = -LRU_C * r * _softplus(-lam_ref[...])
    a = jnp.exp(log_a)
    mult = jnp.sqrt(-jnp.tanh(log_a) * (a * a + 1.0))
    row = lax.broadcasted_iota(I32, a.shape, 0)
    if reset_first:
        is0 = jnp.logical_and(row == 0, t == 0)
        a = jnp.where(is0, 0.0, a)
        mult = jnp.where(is0, 1.0, mult)
    u = mult * gi * xa
    d = 1
    while d < tc:
        keep = row >= d
        a_sh = jnp.where(keep, pltpu.roll(a, d, 0), 1.0)
        u_sh = jnp.where(keep, pltpu.roll(u, d, 0), 0.0)
        u = a * u_sh + u
        a = a * a_sh
        d *= 2
    h = a * hc[...] + u
    h_last = h[tc - 1:tc, :]
    hc[...] = h_last
    hl_ref[...] = h_last
    o_ref[...] = (h * jax.nn.gelu(gate_ref[...])).astype(o_ref.dtype)


def _lru_call(h3, wl, buf, h0, reset_first):
    b, t, _ = h3.shape
    tc = _tile(t, 256)
    w = MIX_WIDTH
    row = pl.BlockSpec((1, w), lambda i, j: (0, 0))
    sq = pl.BlockSpec((w, w), lambda i, j: (0, 0))
    return pl.pallas_call(
        functools.partial(_lru_body, tc=tc, reset_first=reset_first),
        grid=(b, t // tc),
        in_specs=[pl.BlockSpec((None, tc, w), lambda i, j: (i, j, C_LRUX // w)),
                  pl.BlockSpec((None, tc, w), lambda i, j: (i, j, C_LRUG // w)),
                  pl.BlockSpec((SUBLANES, w), lambda i, j: (0, 0)), row, sq, row, sq, row, row,
                  pl.BlockSpec((None, SUBLANES, w), lambda i, j: (i, 0, 0)),
                  pl.BlockSpec((None, 1, w), lambda i, j: (i, 0, 0))],
        out_specs=[pl.BlockSpec((None, tc, w), lambda i, j: (i, j, 0)),
                   pl.BlockSpec((None, 1, w), lambda i, j: (i, 0, 0))],
        out_shape=[jax.ShapeDtypeStruct((b, t, w), BF16), jax.ShapeDtypeStruct((b, 1, w), F32)],
        scratch_shapes=[pltpu.VMEM((tc + SUBLANES, w), F32), pltpu.VMEM((1, w), F32)],
        compiler_params=_cparams("parallel", "arbitrary"),
        name="rg_lru",
    )(h3, h3, wl["lru_cw"], wl["lru_cb"], wl["lru_wa"], wl["lru_ba"], wl["lru_wx"], wl["lru_bx"], wl["lru_lam"],
      buf, h0)


def _gdn_body(qkv_ref, z_ref, misc_ref, cw_ref, alog_ref, dtb_ref, ng_ref, buf_ref, s0_ref,
              o_ref, so_ref, xcat, y_scr, state, *, tc, ck):
    t = pl.program_id(1)
    hd = GDN_HD

    @pl.when(t == 0)
    def _():
        state[...] = s0_ref[...]

    y = _conv_tile(qkv_ref, buf_ref, cw_ref, xcat, t, tc)
    y_scr[...] = _silu(y)

    ri = lax.broadcasted_iota(I32, (ck, ck), 0)
    ci = lax.broadcasted_iota(I32, (ck, ck), 1)
    tril = ri >= ci
    strict = ri > ci
    ltri = jnp.where(tril, 1.0, 0.0).astype(F32)
    ones = jnp.ones((ck, ck), F32)
    n_sq = max(int(math.log2(ck)) - 1, 0)

    def chunk(c, carry):
        r0 = pl.multiple_of(c * ck, ck)
        misc = misc_ref[pl.ds(r0, ck), :]
        beta_all = _sigmoid(misc)
        g_all = -jnp.exp(alog_ref[...]) * _softplus(misc + dtb_ref[...])
        outs, states = [], []
        for hh in range(GDN_HEADS):
            q = y_scr[pl.ds(r0, ck), hh * hd:(hh + 1) * hd]
            k = y_scr[pl.ds(r0, ck), (GDN_HEADS + hh) * hd:(GDN_HEADS + hh + 1) * hd]
            v = y_scr[pl.ds(r0, ck), (2 * GDN_HEADS + hh) * hd:(2 * GDN_HEADS + hh + 1) * hd]
            q = q * lax.rsqrt(jnp.sum(q * q, axis=-1, keepdims=True) + EPS) * (hd ** -0.5)
            k = k * lax.rsqrt(jnp.sum(k * k, axis=-1, keepdims=True) + EPS)
            beta = jnp.broadcast_to(beta_all[:, M_GDNB + hh:M_GDNB + hh + 1], (ck, hd))
            g128 = jnp.broadcast_to(g_all[:, M_GDNA + hh:M_GDNA + hh + 1], (ck, hd))
            gsq = g128[:, :ck]
            gc128 = _dot_exact_lhs(ltri, g128)
            gc_col = gc128[:, :ck]
            gc_row = _dot_exact_lhs(ones, jnp.where(ri <= ci, gsq, 0.0))
            decay = jnp.where(tril, jnp.exp(jnp.where(tril, gc_col - gc_row, 0.0)), 0.0)
            kb = k * beta
            egc = jnp.exp(gc128)
            low = jnp.where(strict, _dot3(kb, k, _NT) * decay, 0.0)
            mp = -low
            x = jnp.concatenate([v * beta, kb * egc], axis=-1)
            x = x + _dot3(mp, x)
            for _ in range(n_sq):
                mp = _dot3(mp, mp)
                x = x + _dot3(mp, x)
            u = x[:, :hd]
            w = x[:, hd:]
            intra = jnp.where(tril, _dot3(q, k, _NT) * decay, 0.0)
            s = state[hh]
            v_new = u - _dot3(w, s)
            o = _dot3(q * egc, s) + _dot3(intra, v_new)
            g_last = gc128[ck - 1:ck, :]
            kd = k * jnp.exp(g_last - gc128)
            states.append(s * jnp.exp(g_last) + _dot3(kd, v_new, _TN))
            zz = z_ref[pl.ds(r0, ck), hh * hd:(hh + 1) * hd]
            outs.append((_rms_rows(o, ng_ref[...]) * _silu(zz)).astype(o_ref.dtype))
        for hh in range(GDN_HEADS):
            state[hh] = states[hh]
            o_ref[pl.ds(r0, ck), hh * hd:(hh + 1) * hd] = outs[hh]
        return carry

    lax.fori_loop(0, tc // ck, chunk, 0)
    so_ref[...] = state[...]


def _gdn_call(h3, wl, buf, s0):
    b, t, _ = h3.shape
    ck = min(GDN_CHUNK, t)
    assert t % ck == 0
    tc = _tile(t, 256)
    w = GDN_QKV
    row = pl.BlockSpec((1, LANES), lambda i, j: (0, 0))
    st = pl.BlockSpec((None, GDN_HEADS, GDN_HD, GDN_HD), lambda i, j: (i, 0, 0, 0))
    return pl.pallas_call(
        functools.partial(_gdn_body, tc=tc, ck=ck),
        grid=(b, t // tc),
        in_specs=[pl.BlockSpec((None, tc, w), lambda i, j: (i, j, C_GDNQKV // w)),
                  pl.BlockSpec((None, tc, MIX_WIDTH), lambda i, j: (i, j, C_GDNZ // MIX_WIDTH)),
                  pl.BlockSpec((None, tc, LANES), lambda i, j: (i, j, C_MISC // LANES)),
                  pl.BlockSpec((SUBLANES, w), lambda i, j: (0, 0)), row, row, row,
                  pl.BlockSpec((None, SUBLANES, w), lambda i, j: (i, 0, 0)), st],
        out_specs=[pl.BlockSpec((None, tc, MIX_WIDTH), lambda i, j: (i, j, 0)), st],
        out_shape=[jax.ShapeDtypeStruct((b, t, MIX_WIDTH), BF16),
                   jax.ShapeDtypeStruct((b, GDN_HEADS, GDN_HD, GDN_HD), F32)],
        scratch_shapes=[pltpu.VMEM((tc + SUBLANES, w), F32), pltpu.VMEM((tc, w), F32),
                        pltpu.VMEM((GDN_HEADS, GDN_HD, GDN_HD), F32)],
        compiler_params=_cparams("parallel", "arbitrary"),
        name="gated_deltanet",
    )(h3, h3, h3, wl["gdn_cw"], wl["gdn_alog"], wl["gdn_dtb"], wl["gdn_ng"], buf, s0)


def _lane_tile(x, n):
    return x if n == LANES else jnp.concatenate([x] * (n // LANES), axis=1)


def _online_update(s, v_ext, m_ref, acc_ref, idx):
    m_old = m_ref[idx]
    m_new = jnp.maximum(m_old, jnp.max(s, axis=-1, keepdims=True))
    alpha = jnp.exp(m_old - m_new)
    p = jnp.exp(s - _lane_tile(m_new, s.shape[1]))
    acc = acc_ref[idx]
    acc_ref[idx] = _lane_tile(alpha, acc.shape[1]) * acc + _dot(p.astype(BF16), v_ext)
    m_ref[idx] = m_new


def _online_update_blocks(s, vs, m_ref, l_ref, acc_ref, idx):
    m_old = m_ref[idx]
    m_new = jnp.maximum(m_old, jnp.max(s, axis=-1, keepdims=True))
    alpha = jnp.exp(m_old - m_new)
    p = jnp.exp(s - m_new)
    l_ref[idx] = alpha * l_ref[idx] + jnp.sum(p, axis=-1, keepdims=True)
    pb = p.astype(BF16)
    pv, off = None, 0
    for vblk in vs:
        part = _dot(pb[:, off:off + vblk.shape[0]], vblk)
        pv = part if pv is None else pv + part
        off += vblk.shape[0]
    acc_ref[idx] = alpha * acc_ref[idx] + pv
    m_ref[idx] = m_new


def _reset_softmax(m_ref, l_ref, acc_ref):
    m_ref[...] = jnp.full(m_ref.shape, NEG_BIG, F32)
    l_ref[...] = jnp.zeros(l_ref.shape, F32)
    acc_ref[...] = jnp.zeros(acc_ref.shape, F32)


def _diff_lambda(lam_ref, lam_init):
    lm = lam_ref[...]
    e1 = jnp.exp(jnp.sum(lm[0:1, :] * lm[1:2, :], axis=-1, keepdims=True))
    e2 = jnp.exp(jnp.sum(lm[2:3, :] * lm[3:4, :], axis=-1, keepdims=True))
    return e1 - e2 + lam_init


def _diff_prompt_body(qi_ref, kj_ref, lam_ref, sg_ref, q_ref, k_ref, v_ref, o_ref, m_scr, acc_scr, *, tq, lam_init):
    step = pl.program_id(2)
    qi = qi_ref[step]
    kj = kj_ref[step]
    hv = 2 * DIFF_HD

    @pl.when(kj == 0)
    def _():
        m_scr[...] = jnp.full(m_scr.shape, NEG_BIG, F32)
        acc_scr[...] = jnp.zeros(acc_scr.shape, F32)

    def tile(masked):
        v = v_ref[...]
        v_ext = jnp.concatenate([v, jnp.ones_like(v)], axis=1)
        for mm in range(2):
            s = _dot_nt(q_ref[:, mm * DIFF_HD:(mm + 1) * DIFF_HD], k_ref[:, mm * DIFF_HD:(mm + 1) * DIFF_HD])
            if masked:
                row = lax.broadcasted_iota(I32, (tq, tq), 0)
                col = lax.broadcasted_iota(I32, (tq, tq), 1)
                s = jnp.where(col <= row, s, NEG_BIG)
            _online_update(s, v_ext, m_scr, acc_scr, mm)

    @pl.when(kj < qi)
    def _():
        tile(False)

    @pl.when(kj == qi)
    def _():
        tile(True)
        lam = _diff_lambda(lam_ref, lam_init)
        o = acc_scr[0, :, 0:hv] / acc_scr[0, :, hv:2 * hv] - lam * (acc_scr[1, :, 0:hv] / acc_scr[1, :, hv:2 * hv])
        o_ref[...] = (_rms_rows(o, sg_ref[...]) * (1.0 - lam_init)).astype(o_ref.dtype)


def _diff_prompt_call(fq3, fkb3, fvb3, wl, lam_init):
    b, t, _ = fq3.shape
    tq = _tile(t, 512)
    nq = t // tq
    hv = 2 * DIFF_HD
    pairs = [(q, k) for q in range(nq) for k in range(q + 1)]
    qi_of = jnp.asarray([p[0] for p in pairs], I32)
    kj_of = jnp.asarray([p[1] for p in pairs], I32)
    grid_spec = pltpu.PrefetchScalarGridSpec(
        num_scalar_prefetch=2,
        grid=(b, DIFF_HEADS, len(pairs)),
        in_specs=[pl.BlockSpec((4, DIFF_HD), lambda i, h, s, qi, kj: (0, 0)),
                  pl.BlockSpec((1, hv), lambda i, h, s, qi, kj: (0, 0)),
                  pl.BlockSpec((None, tq, hv), lambda i, h, s, qi, kj: (i, qi[s], h)),
                  pl.BlockSpec((None, tq, hv), lambda i, h, s, qi, kj: (i, kj[s], h)),
                  pl.BlockSpec((None, tq, hv), lambda i, h, s, qi, kj: (i, kj[s], h))],
        out_specs=pl.BlockSpec((None, tq, hv), lambda i, h, s, qi, kj: (i, qi[s], h)),
        scratch_shapes=[pltpu.VMEM((2, tq, LANES), F32), pltpu.VMEM((2, tq, 2 * hv), F32)],
    )
    return pl.pallas_call(
        functools.partial(_diff_prompt_body, tq=tq, lam_init=lam_init),
        grid_spec=grid_spec,
        out_shape=jax.ShapeDtypeStruct((b, t, MIX_WIDTH), BF16),
        compiler_params=_cparams("parallel", "parallel", "arbitrary"),
        name="diff_attn_prompt",
    )(qi_of, kj_of, wl["diff_lam"], wl["diff_sg"], fq3, fkb3, fvb3)


def _diff_sample_body(pt_ref, lam_ref, sg_ref, q_ref, kn_ref, vn_ref, *rest, tq, npg, lam_init):
    kp = rest[:npg]
    vp = rest[npg:2 * npg]
    o_ref, qrows, m_scr, l_scr, acc_scr = rest[2 * npg:]
    s_id = pl.program_id(1)
    n_steps = pl.num_programs(1)
    nrow = 2 * DIFF_HEADS * tq
    hv = 2 * DIFF_HD
    prow = PAGE_SIZE * DIFF_HEADS

    @pl.when(s_id == 0)
    def _():
        _reset_softmax(m_scr, l_scr, acc_scr)
        qrows[...] = jnp.zeros(qrows.shape, F32)
        q = q_ref[...].astype(F32)
        for j in range(2 * DIFF_HEADS):
            mm = j % 2
            qrows[j * tq:(j + 1) * tq, mm * DIFF_HD:(mm + 1) * DIFF_HD] = q[:, j * DIFF_HD:(j + 1) * DIFF_HD]

    qb = qrows[...].astype(BF16)
    row_head = lax.broadcasted_iota(I32, (nrow, prow), 0) // (2 * tq)

    @pl.when(s_id < n_steps - 1)
    def _():
        key_head = lax.broadcasted_iota(I32, (nrow, prow), 1) % DIFF_HEADS
        own = key_head == row_head
        scores = [jnp.where(own, _dot_nt(qb, kp[p][...].astype(BF16)), NEG_BIG) for p in range(npg)]
        _online_update_blocks(jnp.concatenate(scores, axis=-1), [vp[p][...].astype(BF16) for p in range(npg)],
                       m_scr, l_scr, acc_scr, Ellipsis)

    @pl.when(s_id == n_steps - 1)
    def _():
        kn = jnp.concatenate([kn_ref[:, hh * hv:(hh + 1) * hv] for hh in range(DIFF_HEADS)], axis=0).astype(BF16)
        vn = jnp.concatenate([vn_ref[:, hh * hv:(hh + 1) * hv] for hh in range(DIFF_HEADS)], axis=0).astype(BF16)
        nk = DIFF_HEADS * tq
        rr = lax.broadcasted_iota(I32, (nrow, nk), 0)
        cc = lax.broadcasted_iota(I32, (nrow, nk), 1)
        ok = jnp.logical_and(cc // tq == rr // (2 * tq), cc % tq <= rr % tq)
        s = jnp.where(ok, _dot_nt(qb, kn), NEG_BIG)
        _online_update_blocks(s, [vn], m_scr, l_scr, acc_scr, Ellipsis)
        lam = _diff_lambda(lam_ref, lam_init)
        acc = acc_scr[...] / l_scr[...]
        for hh in range(DIFF_HEADS):
            r1 = (2 * hh) * tq
            r2 = (2 * hh + 1) * tq
            o = acc[r1:r1 + tq, :] - lam * acc[r2:r2 + tq, :]
            o_ref[:, hh * hv:(hh + 1) * hv] = (_rms_rows(o, sg_ref[...]) * (1.0 - lam_init)).astype(o_ref.dtype)


def _diff_sample_call(fq3, fk3, h3, pool_k, pool_v, page_table, layer, wl, lam_init):
    db, t, _ = fq3.shape
    n_pages = page_table.shape[1]
    npg = 8 if n_pages % 8 == 0 else 1
    n_steps = n_pages // npg + 1
    w = MIX_WIDTH
    hv = 2 * DIFF_HD
    nrow = 2 * DIFF_HEADS * t
    prow = PAGE_SIZE * DIFF_HEADS

    def page_spec(p):
        def imap(i, s, pt):
            return (layer, pt[i, jnp.minimum(s, n_steps - 2) * npg + p], 0, 0)
        return pl.BlockSpec((None, None, prow, hv), imap)

    grid_spec = pltpu.PrefetchScalarGridSpec(
        num_scalar_prefetch=1,
        grid=(db, n_steps),
        in_specs=[pl.BlockSpec((4, DIFF_HD), lambda i, s, pt: (0, 0)),
                  pl.BlockSpec((1, hv), lambda i, s, pt: (0, 0)),
                  pl.BlockSpec((None, t, w), lambda i, s, pt: (i, 0, 0)),
                  pl.BlockSpec((None, t, w), lambda i, s, pt: (i, 0, 0)),
                  pl.BlockSpec((None, t, w), lambda i, s, pt: (i, 0, C_DIFV // w))]
                 + [page_spec(p) for p in range(npg)] + [page_spec(p) for p in range(npg)],
        out_specs=pl.BlockSpec((None, t, w), lambda i, s, pt: (i, 0, 0)),
        scratch_shapes=[pltpu.VMEM((nrow, hv), F32), pltpu.VMEM((nrow, 1), F32), pltpu.VMEM((nrow, 1), F32),
                        pltpu.VMEM((nrow, hv), F32)],
    )
    return pl.pallas_call(
        functools.partial(_diff_sample_body, tq=t, npg=npg, lam_init=lam_init),
        grid_spec=grid_spec,
        out_shape=jax.ShapeDtypeStruct((db, t, w), BF16),
        compiler_params=_cparams("parallel", "arbitrary"),
        name="diff_attn_sample",
    )(page_table, wl["diff_lam"], wl["diff_sg"], fq3, fk3, h3, *([pool_k] * npg), *([pool_v] * npg))


def _sortable(score):
    bits = pltpu.bitcast(jnp.where(score == 0.0, 0.0, score), I32)
    return bits ^ ((bits >> 31) & 0x7FFFFFFF)


def _search_threshold(count_ge, rows, n_sel):
    def bit_body(i, t_u):
        cand_u = t_u | (jnp.int32(1) << (31 - i))
        cnt = count_ge(cand_u ^ INT_MIN)
        return jnp.where(cnt >= n_sel, cand_u, t_u)

    t_u = lax.fori_loop(0, 32, bit_body, jnp.zeros((rows, 1), I32))
    return t_u ^ INT_MIN


def _lane_fold(x):
    acc = x[:, 0:LANES]
    for c in range(1, x.shape[1] // LANES):
        acc = acc + x[:, c * LANES:(c + 1) * LANES]
    return acc


def _strictly_before(n):
    ui = lax.broadcasted_iota(I32, (n, n), 0)
    uj = lax.broadcasted_iota(I32, (n, n), 1)
    return jnp.where(ui < uj, 1.0, 0.0).astype(BF16)


def _dsa_prompt_body(iq3_ref, misc_ref, ik3_ref, q_ref, k_ref, v_ref, o_ref, iqs, qs, keys_scr, half_scr, bias_scr,
                     m_scr, acc_scr, *, tq, ck, n_sel):
    qi = pl.program_id(1)
    nck = (qi * tq + tq + ck - 1) // ck
    kw = iqs.shape[1]
    for hh in range(IDX_HEADS):
        iqs[hh * tq:(hh + 1) * tq, :] = iq3_ref[:, hh * kw:(hh + 1) * kw]
    wts = misc_ref[...][:, M_IDXW:M_IDXW + IDX_HEADS] * (IDX_HD ** -0.5 * IDX_HEADS ** -0.5)
    row_pos = qi * tq + lax.broadcasted_iota(I32, (tq, ck), 0)
    lane_pos = lax.broadcasted_iota(I32, (tq, ck), 1)

    def score_chunk(c, carry):
        k0 = pl.multiple_of(c * ck, ck)
        s = jnp.maximum(_dot_nt(iqs[...], ik3_ref[pl.ds(k0, ck), :]), 0.0)
        sc = wts[:, 0:1] * s[0:tq, :]
        for hh in range(1, IDX_HEADS):
            sc = sc + wts[:, hh:hh + 1] * s[hh * tq:(hh + 1) * tq, :]
        keys_scr[c] = jnp.where(c * ck + lane_pos <= row_pos, _sortable(sc), INT_MIN)
        return carry

    lax.fori_loop(0, nck, score_chunk, 0)

    def count(pred):
        def body(c, acc):
            return acc + _lane_fold(jnp.where(pred(keys_scr[c]), 1, 0).astype(I32))
        acc = lax.fori_loop(0, nck, body, jnp.zeros((tq, LANES), I32))
        return jnp.sum(acc, axis=-1, keepdims=True)

    def count_half(half_scr, cand):
        c16 = cand.astype(I16)

        def body(c, acc):
            return acc + _lane_fold(jnp.where(half_scr[c] >= c16, jnp.int16(1), jnp.int16(0)))
        acc = lax.fori_loop(0, nck, body, jnp.zeros((tq, LANES), I16))
        return jnp.sum(acc.astype(I32), axis=-1, keepdims=True)

    def search_half(half_scr):
        def bit_body(i, t_u):
            cand_u = t_u | (jnp.int32(1) << (15 - i))
            cnt = count_half(half_scr, cand_u - HALF_BIAS)
            return jnp.where(cnt >= n_sel, cand_u, t_u)
        return lax.fori_loop(0, 16, bit_body, jnp.zeros((tq, 1), I32)) - HALF_BIAS

    def split_hi(c, carry):
        half_scr[c] = (keys_scr[c] >> 16).astype(I16)
        return carry
    lax.fori_loop(0, nck, split_hi, 0)
    t_hi = search_half(half_scr)

    def split_lo(c, carry):
        kk = keys_scr[c]
        hi = kk >> 16
        lo = (kk & 0xFFFF) - HALF_BIAS
        half_scr[c] = jnp.where(hi == t_hi, lo, jnp.where(hi > t_hi, HALF_BIAS - 1, -HALF_BIAS)).astype(I16)
        return carry
    lax.fori_loop(0, nck, split_lo, 0)
    t_lo = search_half(half_scr)
    thr = (t_hi << 16) | ((t_lo + HALF_BIAS) & 0xFFFF)
    n_ge = count(lambda kk: kk >= thr)
    plain = jnp.max(jnp.where(jnp.logical_and(n_ge == n_sel, thr != INT_MIN), 0, 1)) == 0

    @pl.when(plain)
    def _():
        def body(c, carry):
            bias_scr[c] = jnp.where(keys_scr[c] >= thr, 0.0, NEG_BIG)
            return carry
        lax.fori_loop(0, nck, body, 0)

    @pl.when(jnp.logical_not(plain))
    def _():
        need = (n_sel - count(lambda kk: kk > thr)).astype(F32)
        before = _strictly_before(ck)

        def body(c, eq_seen):
            kk = keys_scr[c]
            eq = jnp.logical_and(kk == thr, c * ck + lane_pos <= row_pos)
            eqf = jnp.where(eq, 1.0, 0.0)
            rank = _dot(eqf.astype(BF16), before) + eq_seen
            sel = jnp.logical_or(kk > thr, jnp.logical_and(eq, rank < need))
            bias_scr[c] = jnp.where(sel, 0.0, NEG_BIG)
            return eq_seen + jnp.sum(eqf, axis=-1, keepdims=True)

        lax.fori_loop(0, nck, body, jnp.zeros((tq, 1), F32))

    m_scr[...] = jnp.full(m_scr.shape, NEG_BIG, F32)
    acc_scr[...] = jnp.zeros(acc_scr.shape, F32)
    hpg = DSA_HEADS // DSA_KV_HEADS
    for hd in range(DSA_HEADS):
        qs[hd * tq:(hd + 1) * tq, :] = q_ref[:, hd * DSA_HD:(hd + 1) * DSA_HD]

    def attend_chunk(c, carry):
        k0 = pl.multiple_of(c * ck, ck)
        bias = bias_scr[c]
        for g in range(DSA_KV_HEADS):
            rows = pl.ds(g * hpg * tq, hpg * tq)
            kg = k_ref[pl.ds(k0, ck), g * DSA_HD:(g + 1) * DSA_HD]
            vg = v_ref[pl.ds(k0, ck), g * LANES:(g + 1) * LANES]
            s = _dot_nt(qs[rows, :], kg).reshape(hpg, tq, ck) + bias[None]
            _online_update(s.reshape(hpg * tq, ck), vg, m_scr, acc_scr, rows)
        return carry

    lax.fori_loop(0, nck, attend_chunk, 0)
    acc = acc_scr[...]
    out = acc / pltpu.roll(acc, DSA_HD, 1)
    for hd in range(DSA_HEADS):
        o_ref[:, hd * DSA_HD:(hd + 1) * DSA_HD] = out[hd * tq:(hd + 1) * tq, 0:DSA_HD].astype(o_ref.dtype)


def _dsa_prompt_call(iq33, ik33, dq3, dkb3, dvb3, h3):
    b, t, _ = dq3.shape
    n_sel = min(TOPK_MAX, t // 4)
    tq = _tile(t, 128)
    ck = _tile(t, 512)
    kw = iq33.shape[-1] // IDX_HEADS
    full = lambda width: pl.BlockSpec((None, t, width), lambda i, q: (i, 0, 0))
    return pl.pallas_call(
        functools.partial(_dsa_prompt_body, tq=tq, ck=ck, n_sel=n_sel),
        grid=(b, t // tq),
        in_specs=[pl.BlockSpec((None, tq, IDX_HEADS * kw), lambda i, q: (i, q, 0)),
                  pl.BlockSpec((None, tq, LANES), lambda i, q: (i, q, C_MISC // LANES)),
                  full(kw),
                  pl.BlockSpec((None, tq, MIX_WIDTH), lambda i, q: (i, q, 0)),
                  full(LANES), full(DSA_KV_HEADS * LANES)],
        out_specs=pl.BlockSpec((None, tq, MIX_WIDTH), lambda i, q: (i, q, 0)),
        out_shape=jax.ShapeDtypeStruct((b, t, MIX_WIDTH), BF16),
        scratch_shapes=[pltpu.VMEM((IDX_HEADS * tq, kw), BF16), pltpu.VMEM((DSA_HEADS * tq, DSA_HD), BF16),
                        pltpu.VMEM((t // ck, tq, ck), I32), pltpu.VMEM((t // ck, tq, ck), I16),
                        pltpu.VMEM((t // ck, tq, ck), F32),
                        pltpu.VMEM((DSA_HEADS * tq, LANES), F32), pltpu.VMEM((DSA_HEADS * tq, LANES), F32)],
        compiler_params=_cparams("parallel", "arbitrary"),
        name="dsa_prompt",
    )(iq33, h3, ik33, dq3, dkb3, dvb3)


def _dsa_select_body(pt_ref, iq3_ref, misc_ref, ikn_ref, *rest, tq, npg, n_sel, n_pages):
    pages = rest[:npg]
    sel_ref, keys_scr, iqs = rest[npg:]
    s_id = pl.program_id(1)
    n_steps = pl.num_programs(1)
    kw = iqs.shape[1]
    wts = misc_ref[...][:, M_IDXW:M_IDXW + IDX_HEADS] * (IDX_HD ** -0.5 * IDX_HEADS ** -0.5)

    @pl.when(s_id == 0)
    def _():
        for hh in range(IDX_HEADS):
            iqs[hh * tq:(hh + 1) * tq, :] = iq3_ref[:, hh * kw:(hh + 1) * kw]

    def scores(ik, feature_major):
        hi, lo = _split_hi_lo(ik)
        a_hi = iqs[:, 0:IDX_HD]
        a_lo = iqs[:, 2 * IDX_HD:3 * IDX_HD]
        mm = _dot if feature_major else _dot_nt
        s = jnp.maximum(mm(a_hi, hi) + mm(a_hi, lo) + mm(a_lo, hi), 0.0)
        sc = wts[:, 0:1] * s[0:tq, :]
        for hh in range(1, IDX_HEADS):
            sc = sc + wts[:, hh:hh + 1] * s[hh * tq:(hh + 1) * tq, :]
        return sc

    @pl.when(s_id < n_steps - 1)
    def _():
        for p in range(npg):
            keys_scr[s_id * npg + p] = _sortable(scores(pages[p][...], True))

    @pl.when(s_id == n_steps - 1)
    def _():
        sc = scores(ikn_ref[...], False)
        qrow = lax.broadcasted_iota(I32, (tq, tq), 0)
        kcol = lax.broadcasted_iota(I32, (tq, tq), 1)
        new_keys = jnp.where(kcol <= qrow, _sortable(sc), INT_MIN)
        keys_scr[n_pages] = jnp.concatenate([new_keys, jnp.full((tq, LANES - tq), INT_MIN, I32)], axis=-1)

        def count(pred):
            acc = jnp.sum(jnp.where(pred(keys_scr[...]), 1, 0).astype(I32), axis=0)
            return jnp.sum(acc, axis=-1, keepdims=True)

        thr = _search_threshold(lambda cand: count(lambda kk: kk >= cand[None]), tq, n_sel)
        n_ge = count(lambda kk: kk >= thr[None])
        plain = jnp.max(jnp.where(jnp.logical_and(n_ge == n_sel, thr != INT_MIN), 0, 1)) == 0

        @pl.when(plain)
        def _():
            sel_ref[...] = jnp.where(keys_scr[...] >= thr[None], 1.0, 0.0)

        @pl.when(jnp.logical_not(plain))
        def _():
            need = (n_sel - count(lambda kk: kk > thr[None])).astype(F32)
            before = _strictly_before(LANES)
            lane = lax.broadcasted_iota(I32, (tq, LANES), 1)
            qr = lax.broadcasted_iota(I32, (tq, LANES), 0)

            def emit(c, eq_seen):
                kk = keys_scr[c]
                allowed = jnp.logical_or(c < n_pages, lane <= qr)
                eq = jnp.logical_and(kk == thr, allowed)
                eqf = jnp.where(eq, 1.0, 0.0)
                rank = _dot(eqf.astype(BF16), before) + eq_seen
                sel = jnp.logical_or(kk > thr, jnp.logical_and(eq, rank < need))
                sel_ref[c] = jnp.where(sel, 1.0, 0.0)
                return eq_seen + jnp.sum(eqf, axis=-1, keepdims=True)

            lax.fori_loop(0, n_pages + 1, emit, jnp.zeros((tq, 1), F32))


def _dsa_attend_body(pt_ref, q_ref, kn_ref, vn_ref, selp_ref, seln_ref, *rest, tq, npg):
    kp = rest[:npg]
    vp = rest[npg:2 * npg]
    o_ref, qrows, m_scr, l_scr, acc_scr = rest[2 * npg:]
    s_id = pl.program_id(1)
    n_steps = pl.num_programs(1)
    hpg = DSA_HEADS // DSA_KV_HEADS
    grows = hpg * tq

    @pl.when(s_id == 0)
    def _():
        _reset_softmax(m_scr, l_scr, acc_scr)
        for hd in range(DSA_HEADS):
            qrows[hd * tq:(hd + 1) * tq, :] = q_ref[:, hd * DSA_HD:(hd + 1) * DSA_HD].astype(F32)

    qb = qrows[...].astype(BF16)
    qg = [qb[g * grows:(g + 1) * grows, :] for g in range(DSA_KV_HEADS)]

    def update(s, pv_fn):
        m_old = m_scr[...]
        m_new = jnp.maximum(m_old, jnp.max(s, axis=-1, keepdims=True))
        alpha = jnp.exp(m_old - m_new)
        p = jnp.exp(s - m_new)
        l_scr[...] = alpha * l_scr[...] + jnp.sum(p, axis=-1, keepdims=True)
        pb = p.astype(BF16)
        pv = jnp.concatenate([pv_fn(pb[g * grows:(g + 1) * grows, :], g) for g in range(DSA_KV_HEADS)], axis=0)
        acc_scr[...] = alpha * acc_scr[...] + pv
        m_scr[...] = m_new

    @pl.when(s_id < n_steps - 1)
    def _():
        parts = []
        for p in range(npg):
            s = jnp.concatenate([_dot(qg[g], kp[p][g].astype(BF16)) for g in range(DSA_KV_HEADS)], axis=0)
            sel = jnp.concatenate([selp_ref[p]] * DSA_HEADS, axis=0) > 0.5
            parts.append(jnp.where(sel, s, NEG_BIG))

        def pv_fn(pb, g):
            out = None
            for p in range(npg):
                part = _dot_nt(pb[:, p * PAGE_SIZE:(p + 1) * PAGE_SIZE], vp[p][g].astype(BF16))
                out = part if out is None else out + part
            return out

        update(jnp.concatenate(parts, axis=-1), pv_fn)

    @pl.when(s_id == n_steps - 1)
    def _():
        kn = kn_ref[...].astype(BF16)
        vn = vn_ref[...].astype(BF16)
        s = jnp.concatenate([_dot_nt(qg[g], kn[:, g * DSA_HD:(g + 1) * DSA_HD]) for g in range(DSA_KV_HEADS)], axis=0)
        sel = jnp.concatenate([seln_ref[...][:, 0:tq]] * DSA_HEADS, axis=0) > 0.5
        update(jnp.where(sel, s, NEG_BIG), lambda pb, g: _dot(pb, vn[:, g * DSA_HD:(g + 1) * DSA_HD]))
        acc = acc_scr[...] / l_scr[...]
        for hd in range(DSA_HEADS):
            o_ref[:, hd * DSA_HD:(hd + 1) * DSA_HD] = acc[hd * tq:(hd + 1) * tq, :].astype(o_ref.dtype)


def _dsa_sample_call(iq33, ik3, dq3, dk3, h3, pool_k, pool_v, pool_ik, page_table, layer):
    db, t, _ = dq3.shape
    n_pages = page_table.shape[1]
    n_keys = n_pages * PAGE_SIZE + t
    n_sel = min(TOPK_MAX, n_keys // 4)
    npg = 8 if n_pages % 8 == 0 else 1
    n_steps = n_pages // npg + 1
    kw = iq33.shape[-1] // IDX_HEADS
    sw = LANES

    def page_spec(*tail):
        def make(p):
            def imap(i, s, pt):
                return (layer, pt[i, jnp.minimum(s, n_steps - 2) * npg + p]) + (0,) * len(tail)
            return pl.BlockSpec((None, None) + tail, imap)
        return [make(p) for p in range(npg)]

    sel = pl.pallas_call(
        functools.partial(_dsa_select_body, tq=t, npg=npg, n_sel=n_sel, n_pages=n_pages),
        grid_spec=pltpu.PrefetchScalarGridSpec(
            num_scalar_prefetch=1,
            grid=(db, n_steps),
            in_specs=[pl.BlockSpec((None, t, IDX_HEADS * kw), lambda i, s, pt: (i, 0, 0)),
                      pl.BlockSpec((None, t, LANES), lambda i, s, pt: (i, 0, C_MISC // LANES)),
                      pl.BlockSpec((None, t, IDX_HD), lambda i, s, pt: (i, 0, 0))]
                     + page_spec(IDX_HD, PAGE_SIZE),
            out_specs=pl.BlockSpec((None, n_pages + 1, t, sw), lambda i, s, pt: (i, 0, 0, 0)),
            scratch_shapes=[pltpu.VMEM((n_pages + 1, t, LANES), I32), pltpu.VMEM((IDX_HEADS * t, kw), BF16)],
        ),
        out_shape=jax.ShapeDtypeStruct((db, n_pages + 1, t, sw), F32),
        compiler_params=_cparams("parallel", "arbitrary"),
        name="dsa_sample_select",
    )(page_table, iq33, h3, ik3, *([pool_ik] * npg))

    nrow = DSA_HEADS * t
    return pl.pallas_call(
        functools.partial(_dsa_attend_body, tq=t, npg=npg),
        grid_spec=pltpu.PrefetchScalarGridSpec(
            num_scalar_prefetch=1,
            grid=(db, n_steps),
            in_specs=[pl.BlockSpec((None, t, MIX_WIDTH), lambda i, s, pt: (i, 0, 0)),
                      pl.BlockSpec((None, t, LANES), lambda i, s, pt: (i, 0, 0)),
                      pl.BlockSpec((None, t, LANES), lambda i, s, pt: (i, 0, C_DSAV // LANES)),
                      pl.BlockSpec((None, npg, t, sw), lambda i, s, pt: (i, jnp.minimum(s, n_steps - 2), 0, 0)),
                      pl.BlockSpec((None, None, t, sw), lambda i, s, pt: (i, n_pages, 0, 0))]
                     + page_spec(DSA_KV_HEADS, DSA_HD, PAGE_SIZE) + page_spec(DSA_KV_HEADS, DSA_HD, PAGE_SIZE),
            out_specs=pl.BlockSpec((None, t, MIX_WIDTH), lambda i, s, pt: (i, 0, 0)),
            scratch_shapes=[pltpu.VMEM((nrow, DSA_HD), F32), pltpu.VMEM((nrow, 1), F32), pltpu.VMEM((nrow, 1), F32),
                            pltpu.VMEM((nrow, DSA_HD), F32)],
        ),
        out_shape=jax.ShapeDtypeStruct((db, t, MIX_WIDTH), BF16),
        compiler_params=_cparams("parallel", "arbitrary"),
        name="dsa_sample_attend",
    )(page_table, dq3, dk3, h3, sel, sel, *([pool_k] * npg), *([pool_v] * npg))


def _merge_body(x_ref, gate_ref, oa_ref, ob_ref, oc_ref, od_ref, wb_ref, wo_ref, o_ref):
    acc = None
    for bi, br in enumerate((oa_ref, ob_ref, oc_ref, od_ref)):
        proj = _dot(br[...], wb_ref[bi])
        term = _sigmoid(gate_ref[:, bi * D_MODEL:(bi + 1) * D_MODEL]) * proj
        acc = term if acc is None else acc + term
    o_ref[...] = x_ref[...] + _dot(acc.astype(BF16), wo_ref[...])


def _merge_call(x, h, branches, wl):
    n, d = x.shape
    tm = _tile(n, 256)
    br = pl.BlockSpec((tm, MIX_WIDTH), lambda i: (i, 0))
    return pl.pallas_call(
        _merge_body,
        grid=(n // tm,),
        in_specs=[pl.BlockSpec((tm, d), lambda i: (i, 0)),
                  pl.BlockSpec((tm, N_BRANCH * d), lambda i: (i, C_GATE // (N_BRANCH * d))),
                  br, br, br, br,
                  pl.BlockSpec((N_BRANCH, MIX_WIDTH, d), lambda i: (0, 0, 0)),
                  pl.BlockSpec((d, d), lambda i: (0, 0))],
        out_specs=pl.BlockSpec((tm, d), lambda i: (i, 0)),
        out_shape=jax.ShapeDtypeStruct((n, d), F32),
        compiler_params=_cparams("parallel"),
        name="merge",
    )(x, h, *branches, wl["w_branch"], wl["w_out"])


def _cross_body(x_ref, g_ref, wq_ref, mk_ref, mv_ref, wo_ref, o_ref, *, merged):
    x = x_ref[...]
    tm = x.shape[0]
    xn = _rms_rows(x, g_ref[...]).astype(BF16)
    q = _dot(xn, wq_ref[...]).astype(BF16)
    if merged:
        qrows = jnp.concatenate([q[:, hh * XA_HD:(hh + 1) * XA_HD] for hh in range(XA_HEADS)], axis=0)
        s = _dot_nt(qrows, mk_ref[...].astype(BF16)) * (XA_HD ** -0.5)
        rr = lax.broadcasted_iota(I32, s.shape, 0) // tm
        cc = lax.broadcasted_iota(I32, s.shape, 1) % XA_HEADS
        s = jnp.where(rr == cc, s, NEG_BIG)
        p = jnp.exp(s - jnp.max(s, axis=-1, keepdims=True))
        p = p / jnp.sum(p, axis=-1, keepdims=True)
        orow = _dot(p.astype(BF16), mv_ref[...].astype(BF16))
        outs = [orow[hh * tm:(hh + 1) * tm, :] for hh in range(XA_HEADS)]
    else:
        outs = []
        for hh in range(XA_HEADS):
            sl = slice(hh * XA_HD, (hh + 1) * XA_HD)
            s = _dot_nt(q[:, sl], mk_ref[:, sl].astype(BF16)) * (XA_HD ** -0.5)
            p = jnp.exp(s - jnp.max(s, axis=-1, keepdims=True))
            p = p / jnp.sum(p, axis=-1, keepdims=True)
            outs.append(_dot(p.astype(BF16), mv_ref[:, sl].astype(BF16)))
    o = jnp.concatenate(outs, axis=-1).astype(BF16)
    o_ref[...] = x + _dot(o, wo_ref[...])


def _cross_call(x3, wl, mk4, mv4, layer):
    b, t, d = x3.shape
    w = XA_HEADS * XA_HD
    tm = _tile(t, 512)
    merged = mk4.shape[-1] == XA_HD
    mem = pl.BlockSpec((None, None) + mk4.shape[2:], lambda i, j: (layer, i, 0, 0))
    return pl.pallas_call(
        functools.partial(_cross_body, merged=merged),
        grid=(b, t // tm),
        in_specs=[pl.BlockSpec((None, tm, d), lambda i, j: (i, j, 0)),
                  pl.BlockSpec((1, d), lambda i, j: (0, 0)),
                  pl.BlockSpec((d, w), lambda i, j: (0, 0)), mem, mem,
                  pl.BlockSpec((w, d), lambda i, j: (0, 0))],
        out_specs=pl.BlockSpec((None, tm, d), lambda i, j: (i, j, 0)),
        out_shape=jax.ShapeDtypeStruct((b, t, d), F32),
        compiler_params=_cparams("parallel", "parallel"),
        name="cross_attn",
    )(x3, wl["norm_xa_g"], wl["xa_wq"], mk4, mv4, wl["xa_wo"])


def _ffn_body(x_ref, g_ref, wg_ref, wu_ref, wd_ref, o_ref, xn_scr, acc_scr):
    f = pl.program_id(1)

    @pl.when(f == 0)
    def _():
        xn_scr[...] = _rms_rows(x_ref[...], g_ref[...]).astype(BF16)
        acc_scr[...] = x_ref[...]

    xn = xn_scr[...]
    hm = _silu(_dot(xn, wg_ref[...])) * _dot(xn, wu_ref[...])
    acc_scr[...] += _dot(hm.astype(BF16), wd_ref[...])

    @pl.when(f == pl.num_programs(1) - 1)
    def _():
        o_ref[...] = acc_scr[...]


def _ffn_call(x, wl):
    n, d = x.shape
    ff = wl["ffn_wg"].shape[1]
    tm = _tile(n, 512)
    tf = 256
    return pl.pallas_call(
        _ffn_body,
        grid=(n // tm, ff // tf),
        in_specs=[pl.BlockSpec((tm, d), lambda i, f: (i, 0)), pl.BlockSpec((1, d), lambda i, f: (0, 0)),
                  pl.BlockSpec((d, tf), lambda i, f: (0, f)), pl.BlockSpec((d, tf), lambda i, f: (0, f)),
                  pl.BlockSpec((tf, d), lambda i, f: (f, 0))],
        out_specs=pl.BlockSpec((tm, d), lambda i, f: (i, 0)),
        out_shape=jax.ShapeDtypeStruct((n, d), F32),
        scratch_shapes=[pltpu.VMEM((tm, d), BF16), pltpu.VMEM((tm, d), F32)],
        compiler_params=_cparams("parallel", "arbitrary"),
        name="ffn_swiglu",
    )(x, wl["norm_ffn_g"], wl["ffn_wg"], wl["ffn_wu"], wl["ffn_wd"])


def _moe_body(x_ref, g_ref, wr_ref, br_ref, wg_ref, wu_ref, wd_ref, o_ref, xn_scr, gate_scr, acc_scr):
    e = pl.program_id(1)
    f = pl.program_id(2)
    first = jnp.logical_and(e == 0, f == 0)

    @pl.when(first)
    def _():
        xn = _rms_rows(x_ref[...], g_ref[...])
        xn_scr[...] = xn.astype(BF16)
        acc_scr[...] = x_ref[...]
        logits = _dot_hi(xn, wr_ref[...]) + br_ref[...]
        lane = lax.broadcasted_iota(I32, logits.shape, 1)
        logits = jnp.where(lane < N_EXPERTS, logits, -jnp.inf)
        top1 = jnp.max(logits, axis=-1, keepdims=True)
        idx1 = jnp.min(jnp.where(logits == top1, lane, LANES), axis=-1, keepdims=True)
        rest = jnp.where(lane == idx1, -jnp.inf, logits)
        top2 = jnp.max(rest, axis=-1, keepdims=True)
        idx2 = jnp.min(jnp.where(rest == top2, lane, LANES), axis=-1, keepdims=True)
        e2 = jnp.exp(top2 - top1)
        p1 = 1.0 / (1.0 + e2)
        p2 = e2 / (1.0 + e2)
        gate_scr[...] = jnp.where(lane == idx1, p1, 0.0) + jnp.where(lane == idx2, p2, 0.0)

    gates = gate_scr[...]
    lane = lax.broadcasted_iota(I32, gates.shape, 1)
    ge = jnp.sum(jnp.where(lane == e, gates, 0.0), axis=-1, keepdims=True)
    xn = xn_scr[...]
    hm = _silu(_dot(xn, wg_ref[...])) * _dot(xn, wu_ref[...])
    acc_scr[...] += ge * _dot(hm.astype(BF16), wd_ref[...])

    @pl.when(jnp.logical_and(e == pl.num_programs(1) - 1, f == pl.num_programs(2) - 1))
    def _():
        o_ref[...] = acc_scr[...]


def _moe_call(x, wl):
    n, d = x.shape
    ne, _, fe = wl["moe_wg"].shape
    tm = _tile(n, 512)
    tf = _tile(fe, 512)
    return pl.pallas_call(
        _moe_body,
        grid=(n // tm, ne, fe // tf),
        in_specs=[pl.BlockSpec((tm, d), lambda i, e, f: (i, 0)), pl.BlockSpec((1, d), lambda i, e, f: (0, 0)),
                  pl.BlockSpec((d, LANES), lambda i, e, f: (0, 0)), pl.BlockSpec((1, LANES), lambda i, e, f: (0, 0)),
                  pl.BlockSpec((None, d, tf), lambda i, e, f: (e, 0, f)),
                  pl.BlockSpec((None, d, tf), lambda i, e, f: (e, 0, f)),
                  pl.BlockSpec((None, tf, d), lambda i, e, f: (e, f, 0))],
        out_specs=pl.BlockSpec((tm, d), lambda i, e, f: (i, 0)),
        out_shape=jax.ShapeDtypeStruct((n, d), F32),
        scratch_shapes=[pltpu.VMEM((tm, d), BF16), pltpu.VMEM((tm, LANES), F32), pltpu.VMEM((tm, d), F32)],
        compiler_params=_cparams("parallel", "arbitrary", "arbitrary"),
        name="moe_swiglu",
    )(x, wl["norm_ffn_g"], wl["moe_wr"], wl["moe_br"], wl["moe_wg"], wl["moe_wu"], wl["moe_wd"])


def _pad_rows(a, rows):
    return jnp.pad(a, ((0, rows - a.shape[0]), (0, 0)))


def _lane_row(vals, lane0):
    return jnp.zeros((1, LANES), F32).at[0, lane0:lane0 + vals.shape[0]].set(vals.astype(F32))


def _block_diag(w):
    nb, n, _ = w.shape
    eye = jnp.eye(nb, dtype=w.dtype)
    return jnp.einsum("aij,ab->aibj", w, eye).reshape(nb * n, nb * n)


def _layer_weights(l, p):
    w_t = jnp.transpose(p["w_in"], (2, 0, 1))[:, l, :]
    d = w_t.shape[1]
    misc = jnp.concatenate([w_t[O_IDXK:O_IDXK + IDX_HD + IDX_HEADS], w_t[O_GDNB:O_GDNB + 2 * GDN_HEADS],
                            jnp.zeros((LANES - IDX_HD - IDX_HEADS - 2 * GDN_HEADS, d), w_t.dtype)], axis=0)
    w_in_r = jnp.concatenate([
        w_t[O_GATE:], w_t[O_LRUX:O_DSAK], w_t[O_DIFQ:O_GATE], w_t[O_GDNZ:O_GDNB],
        w_t[O_GDNQKV:O_GDNZ], w_t[O_IDXQ:O_IDXK], w_t[O_DSAK:O_IDXQ], misc], axis=0)
    assert w_in_r.shape[0] == H_COLS
    wl = dict(
        norm_mix_g=p["norm_mix_g"][l], w_in=w_in_r.astype(BF16),
        lru_cw=_pad_rows(p["lru_conv_w"][l], SUBLANES), lru_cb=p["lru_conv_b"][l].reshape(1, -1),
        lru_wa=_block_diag(p["lru_wa"][l]).astype(BF16), lru_ba=p["lru_ba"][l].reshape(1, -1),
        lru_wx=_block_diag(p["lru_wx"][l]).astype(BF16), lru_bx=p["lru_bx"][l].reshape(1, -1),
        lru_lam=p["lru_lambda"][l].reshape(1, -1),
        gdn_cw=_pad_rows(p["gdn_conv_w"][l], SUBLANES),
        gdn_alog=_lane_row(p["gdn_a_log"][l], M_GDNA), gdn_dtb=_lane_row(p["gdn_dt_bias"][l], M_GDNA),
        gdn_ng=p["gdn_norm_g"][l].reshape(1, -1),
        diff_lam=p["diff_lambda"][l], diff_sg=p["diff_subln_g"][l].reshape(1, -1),
        w_branch=p["w_branch"][l].astype(BF16), w_out=p["w_out"][l].astype(BF16),
        norm_xa_g=p["norm_xa_g"][l].reshape(1, -1), norm_mem_g=p["norm_mem_g"][l],
        xa_wq=p["xa_wq"][l].astype(BF16), xa_wo=p["xa_wo"][l].astype(BF16),
        xa_wkv=jnp.concatenate([p["xa_wk"][l], p["xa_wv"][l]], axis=1).astype(BF16),
        norm_ffn_g=p["norm_ffn_g"][l].reshape(1, -1),
    )
    j = l // 2
    if l % 2 == 0:
        wl.update(ffn_wg=p["ffn_w_gate"][j].astype(BF16), ffn_wu=p["ffn_w_up"][j].astype(BF16),
                  ffn_wd=p["ffn_w_down"][j].astype(BF16))
    else:
        wl.update(moe_wr=jnp.pad(p["moe_router_w"][j], ((0, 0), (0, LANES - N_EXPERTS))),
                  moe_br=jnp.pad(p["moe_router_b"][j], (0, LANES - N_EXPERTS)).reshape(1, LANES),
                  moe_wg=p["moe_w_gate"][j].astype(BF16), moe_wu=p["moe_w_up"][j].astype(BF16),
                  moe_wd=p["moe_w_down"][j].astype(BF16))
    return wl


def _rope_tables(pos):
    half = DSA_HD // 2
    inv = ROPE_THETA ** (-jnp.arange(half, dtype=F32) / half)
    ang = pos.astype(F32)[:, None] * inv[None, :]
    cos, sin = jnp.cos(ang), jnp.sin(ang)
    return jnp.tile(cos, (1, 4)), jnp.tile(jnp.concatenate([-sin, sin], axis=1), (1, 2))


def _front_pad(buf):
    return jnp.pad(buf, ((0, 0), (SUBLANES - (CONV_W - 1), 0), (0, 0)))


def _group_layer(x3, l, wl, tabs, states, mem_kv, paged):
    b, t, d = x3.shape
    n = b * t
    x = x3.reshape(n, d)
    lam_init = 0.8 - 0.6 * math.exp(-0.3 * l)
    xn = _rmsnorm(x, wl["norm_mix_g"], BF16)
    h = _matmul(xn, wl["w_in"], tn=1408, w_rows_are_outputs=True)
    h3 = h.reshape(b, t, H_COLS)
    dq, fq, fk, fkb, fvb, iq3, dk, dkb, dvb, ik, ik3 = _rope_call(h, tabs[0], tabs[1])
    to3 = lambda a: a.reshape(b, t, a.shape[-1])
    dq3, fq3, fk3, dk3, ik3d = map(to3, (dq, fq, fk, dk, ik))

    o_a, lru_h = _lru_call(h3, wl, _front_pad(states["lru_conv"]), states["lru_h"].reshape(b, 1, -1),
                           reset_first=paged is None)
    o_c, gdn_s = _gdn_call(h3, wl, _front_pad(states["gdn_conv"]), states["gdn_s"])
    if paged is None:
        o_b = _dsa_prompt_call(to3(iq3), to3(ik3), dq3, to3(dkb), to3(dvb), h3)
        o_d = _diff_prompt_call(fq3, to3(fkb), to3(fvb), wl, lam_init)
    else:
        o_b = _dsa_sample_call(to3(iq3), ik3d, dq3, dk3, h3, paged["dsa_k"], paged["dsa_v"], paged["idx_k"],
                               paged["page_table"], l)
        o_d = _diff_sample_call(fq3, fk3, h3, paged["diff_k"], paged["diff_v"], paged["page_table"], l, wl, lam_init)

    flat = lambda a: a.reshape(n, a.shape[-1])
    x1 = _merge_call(x, h, [flat(o_a), flat(o_b), flat(o_c), flat(o_d)], wl)
    x2 = _cross_call(x1.reshape(b, t, d), wl, mem_kv[0], mem_kv[1], mem_kv[2])
    x2 = x2.reshape(n, d)
    x3_new = (_ffn_call(x2, wl) if "ffn_wg" in wl else _moe_call(x2, wl)).reshape(b, t, d)

    new = dict(
        dsa_k=dk3.reshape(b, t, DSA_KV_HEADS, DSA_HD),
        dsa_v=h3[:, :, C_DSAV:C_DSAV + DSA_KV_HEADS * DSA_HD].reshape(b, t, DSA_KV_HEADS, DSA_HD),
        idx_k=ik3d,
        diff_k=fk3.reshape(b, t, DIFF_HEADS, 2 * DIFF_HD),
        diff_v=h3[:, :, C_DIFV:C_DIFV + MIX_WIDTH].reshape(b, t, DIFF_HEADS, 2 * DIFF_HD),
        lru_h=lru_h.reshape(b, -1),
        lru_conv=jnp.concatenate([states["lru_conv"], h3[:, :, C_LRUX:C_LRUX + MIX_WIDTH]], axis=1)[:, -(CONV_W - 1):]
        if t < CONV_W - 1 else h3[:, t - (CONV_W - 1):, C_LRUX:C_LRUX + MIX_WIDTH],
        gdn_s=gdn_s,
        gdn_conv=jnp.concatenate([states["gdn_conv"], h3[:, :, C_GDNQKV:C_GDNQKV + GDN_QKV]], axis=1)[:, -(CONV_W - 1):]
        if t < CONV_W - 1 else h3[:, t - (CONV_W - 1):, C_GDNQKV:C_GDNQKV + GDN_QKV],
    )
    return x3_new, new


def kernel(x_prompt, x_sample, mem_prompt, cache_dsa_k, cache_dsa_v, cache_idx_k, cache_diff_k, cache_diff_v,
           cache_mem_k, cache_mem_v, state_lru_h, state_lru_conv, state_gdn_s, state_gdn_conv, page_table,
           norm_mix_g, w_in, lru_conv_w, lru_conv_b, lru_wa, lru_ba, lru_wx, lru_bx, lru_lambda,
           gdn_conv_w, gdn_a_log, gdn_dt_bias, gdn_norm_g, diff_lambda, diff_subln_g, w_branch, w_out,
           norm_xa_g, norm_mem_g, xa_wq, xa_wk, xa_wv, xa_wo, norm_ffn_g, ffn_w_gate, ffn_w_up, ffn_w_down,
           moe_router_w, moe_router_b, moe_w_gate, moe_w_up, moe_w_down, final_norm_g):
    params = dict(
        norm_mix_g=norm_mix_g, w_in=w_in, lru_conv_w=lru_conv_w, lru_conv_b=lru_conv_b, lru_wa=lru_wa, lru_ba=lru_ba,
        lru_wx=lru_wx, lru_bx=lru_bx, lru_lambda=lru_lambda, gdn_conv_w=gdn_conv_w, gdn_a_log=gdn_a_log,
        gdn_dt_bias=gdn_dt_bias, gdn_norm_g=gdn_norm_g, diff_lambda=diff_lambda, diff_subln_g=diff_subln_g,
        w_branch=w_branch, w_out=w_out, norm_xa_g=norm_xa_g, norm_mem_g=norm_mem_g, xa_wq=xa_wq, xa_wk=xa_wk,
        xa_wv=xa_wv, xa_wo=xa_wo, norm_ffn_g=norm_ffn_g, ffn_w_gate=ffn_w_gate, ffn_w_up=ffn_w_up,
        ffn_w_down=ffn_w_down, moe_router_w=moe_router_w, moe_router_b=moe_router_b, moe_w_gate=moe_w_gate,
        moe_w_up=moe_w_up, moe_w_down=moe_w_down)
    b, s, d = x_prompt.shape
    db, t, _ = x_sample.shape
    m = mem_prompt.shape[1]
    n_pool = cache_dsa_k.shape[1]
    past = page_table.shape[1] * PAGE_SIZE

    tabs_p = _rope_tables(jnp.arange(s))
    cs, ss = _rope_tables(past + jnp.arange(t))
    tabs_s = (jnp.tile(cs, (db, 1)), jnp.tile(ss, (db, 1)))
    paged = dict(
        dsa_k=jnp.transpose(cache_dsa_k, (0, 1, 3, 4, 2)), dsa_v=jnp.transpose(cache_dsa_v, (0, 1, 3, 4, 2)),
        idx_k=jnp.transpose(cache_idx_k, (0, 1, 3, 2)), page_table=page_table,
        diff_k=cache_diff_k.reshape(DEPTH, n_pool, PAGE_SIZE * DIFF_HEADS, 2 * DIFF_HD),
        diff_v=cache_diff_v.reshape(DEPTH, n_pool, PAGE_SIZE * DIFF_HEADS, 2 * DIFF_HD))
    mem_k_s = cache_mem_k.reshape(DEPTH, db, m * XA_HEADS, XA_HD)
    mem_v_s = cache_mem_v.reshape(DEPTH, db, m * XA_HEADS, XA_HD)
    zero_states = dict(lru_conv=jnp.zeros((b, CONV_W - 1, MIX_WIDTH), F32), lru_h=jnp.zeros((b, MIX_WIDTH), F32),
                       gdn_conv=jnp.zeros((b, CONV_W - 1, GDN_QKV), F32),
                       gdn_s=jnp.zeros((b, GDN_HEADS, GDN_HD, GDN_HD), F32))

    hp, hs = x_prompt, x_sample
    new_p, new_s, mem_ks, mem_vs = [], [], [], []
    for l in range(DEPTH):
        wl = _layer_weights(l, params)
        mem_n = _rmsnorm(mem_prompt.reshape(b * m, d), wl["norm_mem_g"], BF16)
        kv = _matmul(mem_n, wl["xa_wkv"], tn=512)
        w = XA_HEADS * XA_HD
        mk = kv[:, :w].reshape(1, b, m, w)
        mv = kv[:, w:].reshape(1, b, m, w)
        hp, st_p = _group_layer(hp, l, wl, tabs_p, zero_states, (mk, mv, 0), None)
        new_p.append(st_p)
        mem_ks.append(mk.reshape(b, m, XA_HEADS, XA_HD))
        mem_vs.append(mv.reshape(b, m, XA_HEADS, XA_HD))
        states_s = dict(lru_conv=state_lru_conv[l], lru_h=state_lru_h[l], gdn_conv=state_gdn_conv[l],
                        gdn_s=state_gdn_s[l])
        hs, st_s = _group_layer(hs, l, wl, tabs_s, states_s, (mem_k_s, mem_v_s, l), paged)
        new_s.append(st_s)

    y_prompt = _rmsnorm(hp.reshape(b * s, d), final_norm_g, F32).reshape(b, s, d)
    y_sample = _rmsnorm(hs.reshape(db * t, d), final_norm_g, F32).reshape(db, t, d)
    stack = lambda states, name: jnp.stack([st[name] for st in states], axis=0)
    outs = [y_prompt, y_sample]
    for name in ("dsa_k", "dsa_v", "idx_k", "diff_k", "diff_v", "lru_h", "lru_conv", "gdn_s", "gdn_conv"):
        outs += [stack(new_p, name), stack(new_s, name)]
    outs += [jnp.stack(mem_ks, axis=0), jnp.stack(mem_vs, axis=0)]
    return tuple(outs)
```

```python
import functools
import math

import jax
import jax.numpy as jnp
from jax import lax
from jax.experimental import pallas as pl
from jax.experimental.pallas import tpu as pltpu

F32 = jnp.float32
BF16 = jnp.bfloat16
I32 = jnp.int32
I16 = jnp.int16

D_MODEL = 1024
DEPTH = 2
PAGE_SIZE = 128
MIX_WIDTH = 512
LRU_BLOCKS = 8
LRU_BLOCK = MIX_WIDTH // LRU_BLOCKS
LRU_C = 8.0
CONV_W = 4
DSA_HEADS = 8
DSA_KV_HEADS = 2
DSA_HD = 64
IDX_HEADS = 4
IDX_HD = 64
TOPK_MAX = 256
GDN_HEADS = 4
GDN_HD = 128
GDN_CHUNK = 128
DIFF_HEADS = 4
DIFF_HD = 64
XA_HEADS = 4
XA_HD = 128
N_EXPERTS = 8
ROPE_THETA = 10000.0
EPS = 1e-6
N_BRANCH = 4
GDN_QKV = 3 * GDN_HEADS * GDN_HD

LANES = 128
SUBLANES = 8
NEG_BIG = -1e30
INT_MIN = -2 ** 31
HALF_BIAS = 2 ** 15
VMEM_LIMIT = 48 * 1024 * 1024

C_GATE = 0
C_LRUX = 4096
C_LRUG = 4608
C_DSAQ = 5120
C_DIFQ = 5632
C_DIFK = 6144
C_DIFV = 6656
C_GDNZ = 7168
C_GDNQKV = 7680
C_IDXQ = 9216
C_DSAK = 9472
C_DSAV = 9600
C_MISC = 9728
H_COLS = 9856
M_IDXK = 0
M_IDXW = 64
M_GDNB = 68
M_GDNA = 72

O_LRUX, O_LRUG, O_DSAQ, O_DSAK, O_DSAV, O_IDXQ, O_IDXK, O_IDXW = 0, 512, 1024, 1536, 1664, 1792, 2048, 2112
O_GDNQKV, O_GDNZ, O_GDNB, O_GDNA, O_DIFQ, O_DIFK, O_DIFV, O_GATE = 2116, 3652, 4164, 4168, 4172, 4684, 5196, 5708


def _cparams(*sem):
    return pltpu.CompilerParams(dimension_semantics=sem, vmem_limit_bytes=VMEM_LIMIT)


def _tile(n, pref):
    return pref if n % pref == 0 else n


def _softplus(x):
    return jnp.maximum(x, 0.0) + jnp.log1p(jnp.exp(-jnp.abs(x)))


def _sigmoid(x):
    return 1.0 / (1.0 + jnp.exp(-x))


def _silu(x):
    return x * _sigmoid(x)


def _dot(a, b):
    return jnp.dot(a, b, preferred_element_type=F32)


def _dot_nt(a, b):
    return lax.dot_general(a, b, (((1,), (1,)), ((), ())), preferred_element_type=F32)


def _dot_hi(a, b):
    return jnp.dot(a, b, preferred_element_type=F32, precision=lax.Precision.HIGHEST)


def _split_hi_lo(x):
    hi = x.astype(BF16)
    return hi, (x - hi.astype(F32)).astype(BF16)


_NN = (((1,), (0,)), ((), ()))
_NT = (((1,), (1,)), ((), ()))
_TN = (((0,), (0,)), ((), ()))


def _dot3(a, b, dims=_NN):
    a_hi, a_lo = _split_hi_lo(a)
    b_hi, b_lo = _split_hi_lo(b)
    dg = functools.partial(lax.dot_general, dimension_numbers=dims, preferred_element_type=F32)
    ca, cb = dims[0][0][0], dims[0][1][0]
    if a.shape[ca] % LANES == 0:
        return dg(jnp.concatenate([a_hi, a_hi, a_lo], axis=ca), jnp.concatenate([b_hi, b_lo, b_hi], axis=cb))
    return dg(a_hi, b_hi) + dg(a_hi, b_lo) + dg(a_lo, b_hi)


def _dot_exact_lhs(a, b):
    ab = a.astype(BF16)
    b1 = b.astype(BF16)
    r1 = b - b1.astype(F32)
    b2 = r1.astype(BF16)
    b3 = (r1 - b2.astype(F32)).astype(BF16)
    if a.shape[1] % LANES == 0:
        return _dot(jnp.concatenate([ab, ab, ab], axis=1), jnp.concatenate([b1, b2, b3], axis=0))
    return _dot(ab, b1) + _dot(ab, b2) + _dot(ab, b3)


def _rms_rows(x, g):
    return x * lax.rsqrt(jnp.mean(x * x, axis=-1, keepdims=True) + EPS) * g


def _rmsnorm_body(x_ref, g_ref, o_ref):
    o_ref[...] = _rms_rows(x_ref[...], g_ref[...]).astype(o_ref.dtype)


def _rmsnorm(x, g, out_dtype):
    n, d = x.shape
    tm = _tile(n, 512)
    return pl.pallas_call(
        _rmsnorm_body,
        grid=(n // tm,),
        in_specs=[pl.BlockSpec((tm, d), lambda i: (i, 0)), pl.BlockSpec((1, d), lambda i: (0, 0))],
        out_specs=pl.BlockSpec((tm, d), lambda i: (i, 0)),
        out_shape=jax.ShapeDtypeStruct((n, d), out_dtype),
        compiler_params=_cparams("parallel"),
        name="rmsnorm",
    )(x, g.reshape(1, d))


def _matmul_body(a_ref, w_ref, o_ref, *, w_rows_are_outputs):
    mm = _dot_nt if w_rows_are_outputs else _dot
    o_ref[...] = mm(a_ref[...], w_ref[...]).astype(o_ref.dtype)


def _matmul(a, w, tn, w_rows_are_outputs=False, out_dtype=F32):
    n, k = a.shape
    c = w.shape[0] if w_rows_are_outputs else w.shape[1]
    tm = _tile(n, 512)
    w_spec = (pl.BlockSpec((tn, k), lambda j, i: (j, 0)) if w_rows_are_outputs
              else pl.BlockSpec((k, tn), lambda j, i: (0, j)))
    return pl.pallas_call(
        functools.partial(_matmul_body, w_rows_are_outputs=w_rows_are_outputs),
        grid=(c // tn, n // tm),
        in_specs=[pl.BlockSpec((tm, k), lambda j, i: (i, 0)), w_spec],
        out_specs=pl.BlockSpec((tm, tn), lambda j, i: (i, j)),
        out_shape=jax.ShapeDtypeStruct((n, c), out_dtype),
        compiler_params=_cparams("parallel", "parallel"),
        name="matmul",
    )(a, w)


def _rope_block(x, cos, sin):
    lane = lax.broadcasted_iota(I32, x.shape, 1)
    first = (lane % 64) < 32
    partner = jnp.where(first, pltpu.roll(x, LANES - 32, 1), pltpu.roll(x, 32, 1))
    return x * cos + partner * sin


def _rope_wide(x, cos, sin):
    return [_rope_block(x[:, c * LANES:(c + 1) * LANES], cos, sin) for c in range(x.shape[1] // LANES)]


def _rope_body(cos_ref, sin_ref, dq_ref, fq_ref, fk_ref, fv_ref, iq_ref, dk_ref, dv_ref, misc_ref,
               dq_o, fq_o, fk_o, fkb_o, fvb_o, iq3_o, dk_o, dkb_o, dvb_o, ik_o, ik3_o):
    cos = cos_ref[...]
    sin = sin_ref[...]
    lane = lax.broadcasted_iota(I32, cos.shape, 1)
    low = lane < IDX_HD
    for c, blk in enumerate(_rope_wide(dq_ref[...], cos, sin)):
        dq_o[:, c * LANES:(c + 1) * LANES] = (blk * DSA_HD ** -0.5).astype(BF16)
    for c, blk in enumerate(_rope_wide(fq_ref[...], cos, sin)):
        fq_o[:, c * LANES:(c + 1) * LANES] = (blk * DIFF_HD ** -0.5).astype(BF16)
    for c, blk in enumerate(_rope_wide(fk_ref[...], cos, sin)):
        fk_o[:, c * LANES:(c + 1) * LANES] = blk
        fkb_o[:, c * LANES:(c + 1) * LANES] = blk.astype(BF16)
    fvb_o[...] = fv_ref[...].astype(BF16)
    dk = _rope_block(dk_ref[...], cos, sin)
    dk_o[...] = dk
    dkb_o[...] = dk.astype(BF16)
    dv = dv_ref[...]
    dvb_o[:, 0:LANES] = jnp.where(low, dv, 1.0).astype(BF16)
    dvb_o[:, LANES:2 * LANES] = jnp.where(low, pltpu.roll(dv, DSA_HD, 1), 1.0).astype(BF16)
    for c, blk in enumerate(_rope_wide(iq_ref[...], cos, sin)):
        hi = blk.astype(BF16).astype(F32)
        lo = blk - hi
        hi_sw = pltpu.roll(hi, IDX_HD, 1)
        lo_sw = pltpu.roll(lo, IDX_HD, 1)
        base = 2 * c * 2 * LANES
        iq3_o[:, base:base + LANES] = jnp.where(low, hi, hi_sw).astype(BF16)
        iq3_o[:, base + LANES:base + 2 * LANES] = jnp.where(low, lo, 0.0).astype(BF16)
        iq3_o[:, base + 2 * LANES:base + 3 * LANES] = jnp.where(low, hi_sw, hi).astype(BF16)
        iq3_o[:, base + 3 * LANES:base + 4 * LANES] = jnp.where(low, lo_sw, 0.0).astype(BF16)
    ik = _rope_block(misc_ref[...], cos, sin)
    ik_o[...] = ik[:, M_IDXK:M_IDXK + IDX_HD]
    hi = ik.astype(BF16).astype(F32)
    lo = ik - hi
    ik3_o[:, 0:LANES] = jnp.where(low, hi, pltpu.roll(lo, IDX_HD, 1)).astype(BF16)
    ik3_o[:, LANES:2 * LANES] = jnp.where(low, hi, 0.0).astype(BF16)


def _rope_call(h, cos, sin):
    n = h.shape[0]
    tm = _tile(min(n, cos.shape[0]), 512)
    nt = cos.shape[0] // tm

    def col(width, off):
        return pl.BlockSpec((tm, width), lambda i: (i, off // width))

    def out(width):
        return pl.BlockSpec((tm, width), lambda i: (i, 0))

    sds = jax.ShapeDtypeStruct
    tab = pl.BlockSpec((tm, LANES), lambda i: (i % nt, 0))
    return pl.pallas_call(
        _rope_body,
        grid=(n // tm,),
        in_specs=[tab, tab, col(512, C_DSAQ), col(512, C_DIFQ), col(512, C_DIFK), col(512, C_DIFV),
                  col(256, C_IDXQ), col(128, C_DSAK), col(128, C_DSAV), col(128, C_MISC)],
        out_specs=[out(512), out(512), out(512), out(512), out(512), out(1024), out(128), out(128), out(256),
                   out(IDX_HD), out(256)],
        out_shape=[sds((n, 512), BF16), sds((n, 512), BF16), sds((n, 512), F32), sds((n, 512), BF16),
                   sds((n, 512), BF16), sds((n, 1024), BF16), sds((n, 128), F32), sds((n, 128), BF16),
                   sds((n, 256), BF16), sds((n, IDX_HD), F32), sds((n, 256), BF16)],
        compiler_params=_cparams("parallel"),
        name="rope",
    )(cos, sin, h, h, h, h, h, h, h, h)


def _conv_tile(x_ref, buf_ref, cw_ref, xcat, t, tc):
    @pl.when(t == 0)
    def _():
        xcat[0:SUBLANES, :] = buf_ref[...]

    @pl.when(t > 0)
    def _():
        xcat[0:SUBLANES, :] = xcat[tc:tc + SUBLANES, :]

    xcat[SUBLANES:SUBLANES + tc, :] = x_ref[...]
    cw = cw_ref[...]
    y = x_ref[...] * cw[CONV_W - 1:CONV_W, :]
    for j in range(CONV_W - 1):
        off = SUBLANES - (CONV_W - 1) + j
        y = y + xcat[off:off + tc, :] * cw[j:j + 1, :]
    return y


def _lru_body(x_ref, gate_ref, cw_ref, cb_ref, wa_ref, ba_ref, wx_ref, bx_ref, lam_ref, buf_ref, h0_ref,
              o_ref, hl_ref, xcat, hc, *, tc, reset_first):
    t = pl.program_id(1)

    @pl.when(t == 0)
    def _():
        hc[...] = h0_ref[...]

    xa = _conv_tile(x_ref, buf_ref, cw_ref, xcat, t, tc) + cb_ref[...]
    xb = xa.astype(BF16)
    r = _sigmoid(_dot(xb, wa_ref[...]) + ba_ref[...])
    gi = _sigmoid(_dot(xb, wx_ref[...]) + bx_ref[...])
    log_a = -LRU_C * r * _softplus(-lam_ref[...])
    a = jnp.exp(log_a)
    mult = jnp.sqrt(-jnp.tanh(log_a) * (a * a + 1.0))
    row = lax.broadcasted_iota(I32, a.shape, 0)
    if reset_first:
        is0 = jnp.logical_and(row == 0, t == 0)
        a = jnp.where(is0, 0.0, a)
        mult = jnp.where(is0, 1.0, mult)
    u = mult * gi * xa
    d = 1
    while d < tc:
        keep = row >= d
        a_sh = jnp.where(keep, pltpu.roll(a, d, 0), 1.0)
        u_sh = jnp.where(keep, pltpu.roll(u, d, 0), 0.0)
        u = a * u_sh + u
        a = a * a_sh
        d *= 2
    h = a * hc[...] + u
    h_last = h[tc - 1:tc, :]
    hc[...] = h_last
    hl_ref[...] = h_last
    o_ref[...] = (h * jax.nn.gelu(gate_ref[...])).astype(o_ref.dtype)


def _lru_call(h3, wl, buf, h0, reset_first):
    b, t, _ = h3.shape
    tc = _tile(t, 256)
    w = MIX_WIDTH
    row = pl.BlockSpec((1, w), lambda i, j: (0, 0))
    sq = pl.BlockSpec((w, w), lambda i, j: (0, 0))
    return pl.pallas_call(
        functools.partial(_lru_body, tc=tc, reset_first=reset_first),
        grid=(b, t // tc),
        in_specs=[pl.BlockSpec((None, tc, w), lambda i, j: (i, j, C_LRUX // w)),
                  pl.BlockSpec((None, tc, w), lambda i, j: (i, j, C_LRUG // w)),
                  pl.BlockSpec((SUBLANES, w), lambda i, j: (0, 0)), row, sq, row, sq, row, row,
                  pl.BlockSpec((None, SUBLANES, w), lambda i, j: (i, 0, 0)),
                  pl.BlockSpec((None, 1, w), lambda i, j: (i, 0, 0))],
        out_specs=[pl.BlockSpec((None, tc, w), lambda i, j: (i, j, 0)),
                   pl.BlockSpec((None, 1, w), lambda i, j: (i, 0, 0))],
        out_shape=[jax.ShapeDtypeStruct((b, t, w), BF16), jax.ShapeDtypeStruct((b, 1, w), F32)],
        scratch_shapes=[pltpu.VMEM((tc + SUBLANES, w), F32), pltpu.VMEM((1, w), F32)],
        compiler_params=_cparams("parallel", "arbitrary"),
        name="rg_lru",
    )(h3, h3, wl["lru_cw"], wl["lru_cb"], wl["lru_wa"], wl["lru_ba"], wl["lru_wx"], wl["lru_bx"], wl["lru_lam"],
      buf, h0)


def _gdn_body(qkv_ref, z_ref, misc_ref, cw_ref, alog_ref, dtb_ref, ng_ref, buf_ref, s0_ref,
              o_ref, so_ref, xcat, y_scr, state, *, tc, ck):
    t = pl.program_id(1)
    hd = GDN_HD

    @pl.when(t == 0)
    def _():
        state[...] = s0_ref[...]

    y = _conv_tile(qkv_ref, buf_ref, cw_ref, xcat, t, tc)
    y_scr[...] = _silu(y)

    ri = lax.broadcasted_iota(I32, (ck, ck), 0)
    ci = lax.broadcasted_iota(I32, (ck, ck), 1)
    tril = ri >= ci
    strict = ri > ci
    ltri = jnp.where(tril, 1.0, 0.0).astype(F32)
    ones = jnp.ones((ck, ck), F32)
    n_sq = max(int(math.log2(ck)) - 1, 0)

    def chunk(c, carry):
        r0 = pl.multiple_of(c * ck, ck)
        heads = range(GDN_HEADS)
        misc = misc_ref[pl.ds(r0, ck), :]
        beta_all = _sigmoid(misc)
        g_all = -jnp.exp(alog_ref[...]) * _softplus(misc + dtb_ref[...])
        cols = lambda blk: slice(blk * hd, (blk + 1) * hd)
        q = [y_scr[pl.ds(r0, ck), cols(hh)] for hh in heads]
        k = [y_scr[pl.ds(r0, ck), cols(GDN_HEADS + hh)] for hh in heads]
        v = [y_scr[pl.ds(r0, ck), cols(2 * GDN_HEADS + hh)] for hh in heads]
        q = [x * lax.rsqrt(jnp.sum(x * x, axis=-1, keepdims=True) + EPS) * (hd ** -0.5) for x in q]
        k = [x * lax.rsqrt(jnp.sum(x * x, axis=-1, keepdims=True) + EPS) for x in k]
        beta = [jnp.broadcast_to(beta_all[:, M_GDNB + hh:M_GDNB + hh + 1], (ck, hd)) for hh in heads]
        g128 = [jnp.broadcast_to(g_all[:, M_GDNA + hh:M_GDNA + hh + 1], (ck, hd)) for hh in heads]
        gc128 = [_dot_exact_lhs(ltri, g) for g in g128]
        gc_row = [_dot_exact_lhs(ones, jnp.where(ri <= ci, g[:, :ck], 0.0)) for g in g128]
        decay = [jnp.where(tril, jnp.exp(jnp.where(tril, gc128[hh][:, :ck] - gc_row[hh], 0.0)), 0.0) for hh in heads]
        kb = [k[hh] * beta[hh] for hh in heads]
        egc = [jnp.exp(g) for g in gc128]
        low = [jnp.where(strict, _dot3(kb[hh], k[hh], _NT) * decay[hh], 0.0) for hh in heads]
        intra = [jnp.where(tril, _dot3(q[hh], k[hh], _NT) * decay[hh], 0.0) for hh in heads]
        mp = [-x for x in low]
        x = [jnp.concatenate([v[hh] * beta[hh], kb[hh] * egc[hh]], axis=-1) for hh in heads]
        x = [x[hh] + _dot3(mp[hh], x[hh]) for hh in heads]
        for _ in range(n_sq):
            mp = [_dot3(m, m) for m in mp]
            x = [x[hh] + _dot3(mp[hh], x[hh]) for hh in heads]
        s = [state[hh] for hh in heads]
        v_new = [x[hh][:, :hd] - _dot3(x[hh][:, hd:], s[hh]) for hh in heads]
        o = [_dot3(q[hh] * egc[hh], s[hh]) + _dot3(intra[hh], v_new[hh]) for hh in heads]
        g_last = [g[ck - 1:ck, :] for g in gc128]
        kd = [k[hh] * jnp.exp(g_last[hh] - gc128[hh]) for hh in heads]
        s_new = [s[hh] * jnp.exp(g_last[hh]) + _dot3(kd[hh], v_new[hh], _TN) for hh in heads]
        for hh in heads:
            zz = z_ref[pl.ds(r0, ck), cols(hh)]
            state[hh] = s_new[hh]
            o_ref[pl.ds(r0, ck), cols(hh)] = (_rms_rows(o[hh], ng_ref[...]) * _silu(zz)).astype(o_ref.dtype)
        return carry

    lax.fori_loop(0, tc // ck, chunk, 0)
    so_ref[...] = state[...]


def _gdn_call(h3, wl, buf, s0):
    b, t, _ = h3.shape
    ck = min(GDN_CHUNK, t)
    assert t % ck == 0
    tc = _tile(t, 256)
    w = GDN_QKV
    row = pl.BlockSpec((1, LANES), lambda i, j: (0, 0))
    st = pl.BlockSpec((None, GDN_HEADS, GDN_HD, GDN_HD), lambda i, j: (i, 0, 0, 0))
    return pl.pallas_call(
        functools.partial(_gdn_body, tc=tc, ck=ck),
        grid=(b, t // tc),
        in_specs=[pl.BlockSpec((None, tc, w), lambda i, j: (i, j, C_GDNQKV // w)),
                  pl.BlockSpec((None, tc, MIX_WIDTH), lambda i, j: (i, j, C_GDNZ // MIX_WIDTH)),
                  pl.BlockSpec((None, tc, LANES), lambda i, j: (i, j, C_MISC // LANES)),
                  pl.BlockSpec((SUBLANES, w), lambda i, j: (0, 0)), row, row, row,
                  pl.BlockSpec((None, SUBLANES, w), lambda i, j: (i, 0, 0)), st],
        out_specs=[pl.BlockSpec((None, tc, MIX_WIDTH), lambda i, j: (i, j, 0)), st],
        out_shape=[jax.ShapeDtypeStruct((b, t, MIX_WIDTH), BF16),
                   jax.ShapeDtypeStruct((b, GDN_HEADS, GDN_HD, GDN_HD), F32)],
        scratch_shapes=[pltpu.VMEM((tc + SUBLANES, w), F32), pltpu.VMEM((tc, w), F32),
                        pltpu.VMEM((GDN_HEADS, GDN_HD, GDN_HD), F32)],
        compiler_params=_cparams("parallel", "arbitrary"),
        name="gated_deltanet",
    )(h3, h3, h3, wl["gdn_cw"], wl["gdn_alog"], wl["gdn_dtb"], wl["gdn_ng"], buf, s0)


def _lane_tile(x, n):
    return x if n == LANES else jnp.concatenate([x] * (n // LANES), axis=1)


def _online_update(s, v_ext, m_ref, acc_ref, idx):
    m_old = m_ref[idx]
    m_new = jnp.maximum(m_old, jnp.max(s, axis=-1, keepdims=True))
    alpha = jnp.exp(m_old - m_new)
    p = jnp.exp(s - _lane_tile(m_new, s.shape[1]))
    acc = acc_ref[idx]
    acc_ref[idx] = _lane_tile(alpha, acc.shape[1]) * acc + _dot(p.astype(BF16), v_ext)
    m_ref[idx] = m_new


def _online_update_blocks(s, vs, m_ref, l_ref, acc_ref, idx):
    m_old = m_ref[idx]
    m_new = jnp.maximum(m_old, jnp.max(s, axis=-1, keepdims=True))
    alpha = jnp.exp(m_old - m_new)
    p = jnp.exp(s - m_new)
    l_ref[idx] = alpha * l_ref[idx] + jnp.sum(p, axis=-1, keepdims=True)
    pb = p.astype(BF16)
    pv, off = None, 0
    for vblk in vs:
        part = _dot(pb[:, off:off + vblk.shape[0]], vblk)
        pv = part if pv is None else pv + part
        off += vblk.shape[0]
    acc_ref[idx] = alpha * acc_ref[idx] + pv
    m_ref[idx] = m_new


def _reset_softmax(m_ref, l_ref, acc_ref):
    m_ref[...] = jnp.full(m_ref.shape, NEG_BIG, F32)
    l_ref[...] = jnp.zeros(l_ref.shape, F32)
    acc_ref[...] = jnp.zeros(acc_ref.shape, F32)


def _diff_lambda(lam_ref, lam_init):
    lm = lam_ref[...]
    e1 = jnp.exp(jnp.sum(lm[0:1, :] * lm[1:2, :], axis=-1, keepdims=True))
    e2 = jnp.exp(jnp.sum(lm[2:3, :] * lm[3:4, :], axis=-1, keepdims=True))
    return e1 - e2 + lam_init


def _diff_prompt_body(qi_ref, kj_ref, lam_ref, sg_ref, q_ref, k_ref, v_ref, o_ref, m_scr, acc_scr, *, tq, lam_init):
    step = pl.program_id(2)
    qi = qi_ref[step]
    kj = kj_ref[step]
    hv = 2 * DIFF_HD

    @pl.when(kj == 0)
    def _():
        m_scr[...] = jnp.full(m_scr.shape, NEG_BIG, F32)
        acc_scr[...] = jnp.zeros(acc_scr.shape, F32)

    def tile(masked):
        v = v_ref[...]
        v_ext = jnp.concatenate([v, jnp.ones_like(v)], axis=1)
        s = [_dot_nt(q_ref[:, mm * DIFF_HD:(mm + 1) * DIFF_HD], k_ref[:, mm * DIFF_HD:(mm + 1) * DIFF_HD])
             for mm in range(2)]
        for mm in range(2):
            sm = s[mm]
            if masked:
                row = lax.broadcasted_iota(I32, (tq, tq), 0)
                col = lax.broadcasted_iota(I32, (tq, tq), 1)
                sm = jnp.where(col <= row, sm, NEG_BIG)
            _online_update(sm, v_ext, m_scr, acc_scr, mm)

    @pl.when(kj < qi)
    def _():
        tile(False)

    @pl.when(kj == qi)
    def _():
        tile(True)
        lam = _diff_lambda(lam_ref, lam_init)
        o = acc_scr[0, :, 0:hv] / acc_scr[0, :, hv:2 * hv] - lam * (acc_scr[1, :, 0:hv] / acc_scr[1, :, hv:2 * hv])
        o_ref[...] = (_rms_rows(o, sg_ref[...]) * (1.0 - lam_init)).astype(o_ref.dtype)


def _diff_prompt_call(fq3, fkb3, fvb3, wl, lam_init):
    b, t, _ = fq3.shape
    tq = _tile(t, 512)
    nq = t // tq
    hv = 2 * DIFF_HD
    pairs = [(q, k) for q in range(nq) for k in range(q + 1)]
    qi_of = jnp.asarray([p[0] for p in pairs], I32)
    kj_of = jnp.asarray([p[1] for p in pairs], I32)
    grid_spec = pltpu.PrefetchScalarGridSpec(
        num_scalar_prefetch=2,
        grid=(b, DIFF_HEADS, len(pairs)),
        in_specs=[pl.BlockSpec((4, DIFF_HD), lambda i, h, s, qi, kj: (0, 0)),
                  pl.BlockSpec((1, hv), lambda i, h, s, qi, kj: (0, 0)),
                  pl.BlockSpec((None, tq, hv), lambda i, h, s, qi, kj: (i, qi[s], h)),
                  pl.BlockSpec((None, tq, hv), lambda i, h, s, qi, kj: (i, kj[s], h)),
                  pl.BlockSpec((None, tq, hv), lambda i, h, s, qi, kj: (i, kj[s], h))],
        out_specs=pl.BlockSpec((None, tq, hv), lambda i, h, s, qi, kj: (i, qi[s], h)),
        scratch_shapes=[pltpu.VMEM((2, tq, LANES), F32), pltpu.VMEM((2, tq, 2 * hv), F32)],
    )
    return pl.pallas_call(
        functools.partial(_diff_prompt_body, tq=tq, lam_init=lam_init),
        grid_spec=grid_spec,
        out_shape=jax.ShapeDtypeStruct((b, t, MIX_WIDTH), BF16),
        compiler_params=_cparams("parallel", "parallel", "arbitrary"),
        name="diff_attn_prompt",
    )(qi_of, kj_of, wl["diff_lam"], wl["diff_sg"], fq3, fkb3, fvb3)


def _diff_sample_body(pt_ref, lam_ref, sg_ref, q_ref, kn_ref, vn_ref, *rest, tq, npg, lam_init):
    kp = rest[:npg]
    vp = rest[npg:2 * npg]
    o_ref, qrows, m_scr, l_scr, acc_scr = rest[2 * npg:]
    s_id = pl.program_id(1)
    n_steps = pl.num_programs(1)
    nrow = 2 * DIFF_HEADS * tq
    hv = 2 * DIFF_HD
    prow = PAGE_SIZE * DIFF_HEADS

    @pl.when(s_id == 0)
    def _():
        _reset_softmax(m_scr, l_scr, acc_scr)
        qrows[...] = jnp.zeros(qrows.shape, F32)
        q = q_ref[...].astype(F32)
        for j in range(2 * DIFF_HEADS):
            mm = j % 2
            qrows[j * tq:(j + 1) * tq, mm * DIFF_HD:(mm + 1) * DIFF_HD] = q[:, j * DIFF_HD:(j + 1) * DIFF_HD]

    qb = qrows[...].astype(BF16)
    row_head = lax.broadcasted_iota(I32, (nrow, prow), 0) // (2 * tq)

    @pl.when(s_id < n_steps - 1)
    def _():
        key_head = lax.broadcasted_iota(I32, (nrow, prow), 1) % DIFF_HEADS
        own = key_head == row_head
        scores = [jnp.where(own, _dot_nt(qb, kp[p][...].astype(BF16)), NEG_BIG) for p in range(npg)]
        _online_update_blocks(jnp.concatenate(scores, axis=-1), [vp[p][...].astype(BF16) for p in range(npg)],
                       m_scr, l_scr, acc_scr, Ellipsis)

    @pl.when(s_id == n_steps - 1)
    def _():
        kn = jnp.concatenate([kn_ref[:, hh * hv:(hh + 1) * hv] for hh in range(DIFF_HEADS)], axis=0).astype(BF16)
        vn = jnp.concatenate([vn_ref[:, hh * hv:(hh + 1) * hv] for hh in range(DIFF_HEADS)], axis=0).astype(BF16)
        nk = DIFF_HEADS * tq
        rr = lax.broadcasted_iota(I32, (nrow, nk), 0)
        cc = lax.broadcasted_iota(I32, (nrow, nk), 1)
        ok = jnp.logical_and(cc // tq == rr // (2 * tq), cc % tq <= rr % tq)
        s = jnp.where(ok, _dot_nt(qb, kn), NEG_BIG)
        _online_update_blocks(s, [vn], m_scr, l_scr, acc_scr, Ellipsis)
        lam = _diff_lambda(lam_ref, lam_init)
        acc = acc_scr[...] / l_scr[...]
        for hh in range(DIFF_HEADS):
            r1 = (2 * hh) * tq
            r2 = (2 * hh + 1) * tq
            o = acc[r1:r1 + tq, :] - lam * acc[r2:r2 + tq, :]
            o_ref[:, hh * hv:(hh + 1) * hv] = (_rms_rows(o, sg_ref[...]) * (1.0 - lam_init)).astype(o_ref.dtype)


def _diff_sample_call(fq3, fk3, h3, pool_k, pool_v, page_table, layer, wl, lam_init):
    db, t, _ = fq3.shape
    n_pages = page_table.shape[1]
    npg = 8 if n_pages % 8 == 0 else 1
    n_steps = n_pages // npg + 1
    w = MIX_WIDTH
    hv = 2 * DIFF_HD
    nrow = 2 * DIFF_HEADS * t
    prow = PAGE_SIZE * DIFF_HEADS

    def page_spec(p):
        def imap(i, s, pt):
            return (layer, pt[i, jnp.minimum(s, n_steps - 2) * npg + p], 0, 0)
        return pl.BlockSpec((None, None, prow, hv), imap)

    grid_spec = pltpu.PrefetchScalarGridSpec(
        num_scalar_prefetch=1,
        grid=(db, n_steps),
        in_specs=[pl.BlockSpec((4, DIFF_HD), lambda i, s, pt: (0, 0)),
                  pl.BlockSpec((1, hv), lambda i, s, pt: (0, 0)),
                  pl.BlockSpec((None, t, w), lambda i, s, pt: (i, 0, 0)),
                  pl.BlockSpec((None, t, w), lambda i, s, pt: (i, 0, 0)),
                  pl.BlockSpec((None, t, w), lambda i, s, pt: (i, 0, C_DIFV // w))]
                 + [page_spec(p) for p in range(npg)] + [page_spec(p) for p in range(npg)],
        out_specs=pl.BlockSpec((None, t, w), lambda i, s, pt: (i, 0, 0)),
        scratch_shapes=[pltpu.VMEM((nrow, hv), F32), pltpu.VMEM((nrow, 1), F32), pltpu.VMEM((nrow, 1), F32),
                        pltpu.VMEM((nrow, hv), F32)],
    )
    return pl.pallas_call(
        functools.partial(_diff_sample_body, tq=t, npg=npg, lam_init=lam_init),
        grid_spec=grid_spec,
        out_shape=jax.ShapeDtypeStruct((db, t, w), BF16),
        compiler_params=_cparams("parallel", "arbitrary"),
        name="diff_attn_sample",
    )(page_table, wl["diff_lam"], wl["diff_sg"], fq3, fk3, h3, *([pool_k] * npg), *([pool_v] * npg))


def _sortable(score):
    bits = pltpu.bitcast(jnp.where(score == 0.0, 0.0, score), I32)
    return bits ^ ((bits >> 31) & 0x7FFFFFFF)


def _search_threshold(count_ge, rows, n_sel):
    def bit_body(i, t_u):
        cand_u = t_u | (jnp.int32(1) << (31 - i))
        cnt = count_ge(cand_u ^ INT_MIN)
        return jnp.where(cnt >= n_sel, cand_u, t_u)

    t_u = lax.fori_loop(0, 32, bit_body, jnp.zeros((rows, 1), I32))
    return t_u ^ INT_MIN


def _lane_fold(x):
    acc = x[:, 0:LANES]
    for c in range(1, x.shape[1] // LANES):
        acc = acc + x[:, c * LANES:(c + 1) * LANES]
    return acc


def _strictly_before(n):
    ui = lax.broadcasted_iota(I32, (n, n), 0)
    uj = lax.broadcasted_iota(I32, (n, n), 1)
    return jnp.where(ui < uj, 1.0, 0.0).astype(BF16)


def _dsa_prompt_body(iq3_ref, misc_ref, ik3_ref, q_ref, k_ref, v_ref, o_ref, iqs, qs, keys_scr, half_scr, bias_scr,
                     m_scr, acc_scr, *, tq, ck, n_sel):
    qi = pl.program_id(1)
    nck = (qi * tq + tq + ck - 1) // ck
    kw = iqs.shape[1]
    for hh in range(IDX_HEADS):
        iqs[hh * tq:(hh + 1) * tq, :] = iq3_ref[:, hh * kw:(hh + 1) * kw]
    wts = misc_ref[...][:, M_IDXW:M_IDXW + IDX_HEADS] * (IDX_HD ** -0.5 * IDX_HEADS ** -0.5)
    row_pos = qi * tq + lax.broadcasted_iota(I32, (tq, ck), 0)
    lane_pos = lax.broadcasted_iota(I32, (tq, ck), 1)

    def score_chunk(c, carry):
        k0 = pl.multiple_of(c * ck, ck)
        s = jnp.maximum(_dot_nt(iqs[...], ik3_ref[pl.ds(k0, ck), :]), 0.0)
        sc = wts[:, 0:1] * s[0:tq, :]
        for hh in range(1, IDX_HEADS):
            sc = sc + wts[:, hh:hh + 1] * s[hh * tq:(hh + 1) * tq, :]
        keys_scr[c] = jnp.where(c * ck + lane_pos <= row_pos, _sortable(sc), INT_MIN)
        return carry

    lax.fori_loop(0, nck, score_chunk, 0)

    def count(pred):
        def body(c, acc):
            return acc + _lane_fold(jnp.where(pred(keys_scr[c]), 1, 0).astype(I32))
        acc = lax.fori_loop(0, nck, body, jnp.zeros((tq, LANES), I32))
        return jnp.sum(acc, axis=-1, keepdims=True)

    def count_half(half_scr, cand):
        c16 = cand.astype(I16)

        def body(c, acc):
            return acc + _lane_fold(jnp.where(half_scr[c] >= c16, jnp.int16(1), jnp.int16(0)))
        acc = lax.fori_loop(0, nck, body, jnp.zeros((tq, LANES), I16))
        return jnp.sum(acc.astype(I32), axis=-1, keepdims=True)

    def search_half(half_scr):
        def bit_body(i, t_u):
            cand_u = t_u | (jnp.int32(1) << (15 - i))
            cnt = count_half(half_scr, cand_u - HALF_BIAS)
            return jnp.where(cnt >= n_sel, cand_u, t_u)
        return lax.fori_loop(0, 16, bit_body, jnp.zeros((tq, 1), I32)) - HALF_BIAS

    def split_hi(c, carry):
        half_scr[c] = (keys_scr[c] >> 16).astype(I16)
        return carry
    lax.fori_loop(0, nck, split_hi, 0)
    t_hi = search_half(half_scr)

    def split_lo(c, carry):
        kk = keys_scr[c]
        hi = kk >> 16
        lo = (kk & 0xFFFF) - HALF_BIAS
        half_scr[c] = jnp.where(hi == t_hi, lo, jnp.where(hi > t_hi, HALF_BIAS - 1, -HALF_BIAS)).astype(I16)
        return carry
    lax.fori_loop(0, nck, split_lo, 0)
    t_lo = search_half(half_scr)
    thr = (t_hi << 16) | ((t_lo + HALF_BIAS) & 0xFFFF)
    n_ge = count(lambda kk: kk >= thr)
    plain = jnp.max(jnp.where(jnp.logical_and(n_ge == n_sel, thr != INT_MIN), 0, 1)) == 0

    @pl.when(plain)
    def _():
        def body(c, carry):
            bias_scr[c] = jnp.where(keys_scr[c] >= thr, 0.0, NEG_BIG)
            return carry
        lax.fori_loop(0, nck, body, 0)

    @pl.when(jnp.logical_not(plain))
    def _():
        need = (n_sel - count(lambda kk: kk > thr)).astype(F32)
        before = _strictly_before(ck)

        def body(c, eq_seen):
            kk = keys_scr[c]
            eq = jnp.logical_and(kk == thr, c * ck + lane_pos <= row_pos)
            eqf = jnp.where(eq, 1.0, 0.0)
            rank = _dot(eqf.astype(BF16), before) + eq_seen
            sel = jnp.logical_or(kk > thr, jnp.logical_and(eq, rank < need))
            bias_scr[c] = jnp.where(sel, 0.0, NEG_BIG)
            return eq_seen + jnp.sum(eqf, axis=-1, keepdims=True)

        lax.fori_loop(0, nck, body, jnp.zeros((tq, 1), F32))

    m_scr[...] = jnp.full(m_scr.shape, NEG_BIG, F32)
    acc_scr[...] = jnp.zeros(acc_scr.shape, F32)
    hpg = DSA_HEADS // DSA_KV_HEADS
    for hd in range(DSA_HEADS):
        qs[hd * tq:(hd + 1) * tq, :] = q_ref[:, hd * DSA_HD:(hd + 1) * DSA_HD]

    def attend_chunk(c, carry):
        k0 = pl.multiple_of(c * ck, ck)
        bias = bias_scr[c]
        groups = range(DSA_KV_HEADS)
        rows = [pl.ds(g * hpg * tq, hpg * tq) for g in groups]
        s = [_dot_nt(qs[rows[g], :], k_ref[pl.ds(k0, ck), g * DSA_HD:(g + 1) * DSA_HD]) for g in groups]
        for g in groups:
            vg = v_ref[pl.ds(k0, ck), g * LANES:(g + 1) * LANES]
            sg = s[g].reshape(hpg, tq, ck) + bias[None]
            _online_update(sg.reshape(hpg * tq, ck), vg, m_scr, acc_scr, rows[g])
        return carry

    lax.fori_loop(0, nck, attend_chunk, 0)
    acc = acc_scr[...]
    out = acc / pltpu.roll(acc, DSA_HD, 1)
    for hd in range(DSA_HEADS):
        o_ref[:, hd * DSA_HD:(hd + 1) * DSA_HD] = out[hd * tq:(hd + 1) * tq, 0:DSA_HD].astype(o_ref.dtype)


def _dsa_prompt_call(iq33, ik33, dq3, dkb3, dvb3, h3):
    b, t, _ = dq3.shape
    n_sel = min(TOPK_MAX, t // 4)
    tq = _tile(t, 128)
    ck = _tile(t, 512)
    kw = iq33.shape[-1] // IDX_HEADS
    full = lambda width: pl.BlockSpec((None, t, width), lambda i, q: (i, 0, 0))
    return pl.pallas_call(
        functools.partial(_dsa_prompt_body, tq=tq, ck=ck, n_sel=n_sel),
        grid=(b, t // tq),
        in_specs=[pl.BlockSpec((None, tq, IDX_HEADS * kw), lambda i, q: (i, q, 0)),
                  pl.BlockSpec((None, tq, LANES), lambda i, q: (i, q, C_MISC // LANES)),
                  full(kw),
                  pl.BlockSpec((None, tq, MIX_WIDTH), lambda i, q: (i, q, 0)),
                  full(LANES), full(DSA_KV_HEADS * LANES)],
        out_specs=pl.BlockSpec((None, tq, MIX_WIDTH), lambda i, q: (i, q, 0)),
        out_shape=jax.ShapeDtypeStruct((b, t, MIX_WIDTH), BF16),
        scratch_shapes=[pltpu.VMEM((IDX_HEADS * tq, kw), BF16), pltpu.VMEM((DSA_HEADS * tq, DSA_HD), BF16),
                        pltpu.VMEM((t // ck, tq, ck), I32), pltpu.VMEM((t // ck, tq, ck), I16),
                        pltpu.VMEM((t // ck, tq, ck), F32),
                        pltpu.VMEM((DSA_HEADS * tq, LANES), F32), pltpu.VMEM((DSA_HEADS * tq, LANES), F32)],
        compiler_params=_cparams("parallel", "arbitrary"),
        name="dsa_prompt",
    )(iq33, h3, ik33, dq3, dkb3, dvb3)


def _dsa_select_body(pt_ref, iq3_ref, misc_ref, ikn_ref, *rest, tq, npg, n_sel, n_pages):
    pages = rest[:npg]
    sel_ref, keys_scr, iqs = rest[npg:]
    s_id = pl.program_id(1)
    n_steps = pl.num_programs(1)
    kw = iqs.shape[1]
    wts = misc_ref[...][:, M_IDXW:M_IDXW + IDX_HEADS] * (IDX_HD ** -0.5 * IDX_HEADS ** -0.5)

    @pl.when(s_id == 0)
    def _():
        for hh in range(IDX_HEADS):
            iqs[hh * tq:(hh + 1) * tq, :] = iq3_ref[:, hh * kw:(hh + 1) * kw]

    def scores(ik, feature_major):
        hi, lo = _split_hi_lo(ik)
        a_hi = iqs[:, 0:IDX_HD]
        a_lo = iqs[:, 2 * IDX_HD:3 * IDX_HD]
        mm = _dot if feature_major else _dot_nt
        s = jnp.maximum(mm(a_hi, hi) + mm(a_hi, lo) + mm(a_lo, hi), 0.0)
        sc = wts[:, 0:1] * s[0:tq, :]
        for hh in range(1, IDX_HEADS):
            sc = sc + wts[:, hh:hh + 1] * s[hh * tq:(hh + 1) * tq, :]
        return sc

    @pl.when(s_id < n_steps - 1)
    def _():
        keys = _sortable(scores(jnp.concatenate([pages[p][...] for p in range(npg)], axis=1), True))
        for p in range(npg):
            keys_scr[s_id * npg + p] = keys[:, p * PAGE_SIZE:(p + 1) * PAGE_SIZE]

    @pl.when(s_id == n_steps - 1)
    def _():
        sc = scores(ikn_ref[...], False)
        qrow = lax.broadcasted_iota(I32, (tq, tq), 0)
        kcol = lax.broadcasted_iota(I32, (tq, tq), 1)
        new_keys = jnp.where(kcol <= qrow, _sortable(sc), INT_MIN)
        keys_scr[n_pages] = jnp.concatenate([new_keys, jnp.full((tq, LANES - tq), INT_MIN, I32)], axis=-1)

        def count(pred):
            acc = jnp.sum(jnp.where(pred(keys_scr[...]), 1, 0).astype(I32), axis=0)
            return jnp.sum(acc, axis=-1, keepdims=True)

        thr = _search_threshold(lambda cand: count(lambda kk: kk >= cand[None]), tq, n_sel)
        n_ge = count(lambda kk: kk >= thr[None])
        plain = jnp.max(jnp.where(jnp.logical_and(n_ge == n_sel, thr != INT_MIN), 0, 1)) == 0

        @pl.when(plain)
        def _():
            sel_ref[...] = jnp.where(keys_scr[...] >= thr[None], 1.0, 0.0)

        @pl.when(jnp.logical_not(plain))
        def _():
            need = (n_sel - count(lambda kk: kk > thr[None])).astype(F32)
            before = _strictly_before(LANES)
            lane = lax.broadcasted_iota(I32, (tq, LANES), 1)
            qr = lax.broadcasted_iota(I32, (tq, LANES), 0)

            def emit(c, eq_seen):
                kk = keys_scr[c]
                allowed = jnp.logical_or(c < n_pages, lane <= qr)
                eq = jnp.logical_and(kk == thr, allowed)
                eqf = jnp.where(eq, 1.0, 0.0)
                rank = _dot(eqf.astype(BF16), before) + eq_seen
                sel = jnp.logical_or(kk > thr, jnp.logical_and(eq, rank < need))
                sel_ref[c] = jnp.where(sel, 1.0, 0.0)
                return eq_seen + jnp.sum(eqf, axis=-1, keepdims=True)

            lax.fori_loop(0, n_pages + 1, emit, jnp.zeros((tq, 1), F32))


def _dsa_attend_body(pt_ref, q_ref, kn_ref, vn_ref, selp_ref, seln_ref, *rest, tq, npg):
    kp = rest[:npg]
    vp = rest[npg:2 * npg]
    o_ref, qrows, m_scr, l_scr, acc_scr = rest[2 * npg:]
    s_id = pl.program_id(1)
    n_steps = pl.num_programs(1)
    hpg = DSA_HEADS // DSA_KV_HEADS
    grows = hpg * tq

    @pl.when(s_id == 0)
    def _():
        _reset_softmax(m_scr, l_scr, acc_scr)
        for hd in range(DSA_HEADS):
            qrows[hd * tq:(hd + 1) * tq, :] = q_ref[:, hd * DSA_HD:(hd + 1) * DSA_HD].astype(F32)

    qb = qrows[...].astype(BF16)
    qg = [qb[g * grows:(g + 1) * grows, :] for g in range(DSA_KV_HEADS)]

    def update(s, pv_fn):
        m_old = m_scr[...]
        m_new = jnp.maximum(m_old, jnp.max(s, axis=-1, keepdims=True))
        alpha = jnp.exp(m_old - m_new)
        p = jnp.exp(s - m_new)
        l_scr[...] = alpha * l_scr[...] + jnp.sum(p, axis=-1, keepdims=True)
        pb = p.astype(BF16)
        pv = jnp.concatenate([pv_fn(pb[g * grows:(g + 1) * grows, :], g) for g in range(DSA_KV_HEADS)], axis=0)
        acc_scr[...] = alpha * acc_scr[...] + pv
        m_scr[...] = m_new

    @pl.when(s_id < n_steps - 1)
    def _():
        kcat = [jnp.concatenate([kp[p][g] for p in range(npg)], axis=1).astype(BF16) for g in range(DSA_KV_HEADS)]
        vcat = [jnp.concatenate([vp[p][g] for p in range(npg)], axis=1).astype(BF16) for g in range(DSA_KV_HEADS)]
        s = jnp.concatenate([_dot(qg[g], kcat[g]) for g in range(DSA_KV_HEADS)], axis=0)
        sel1 = jnp.concatenate([selp_ref[p] for p in range(npg)], axis=1)
        sel = jnp.concatenate([sel1] * DSA_HEADS, axis=0) > 0.5
        update(jnp.where(sel, s, NEG_BIG), lambda pb, g: _dot_nt(pb, vcat[g]))

    @pl.when(s_id == n_steps - 1)
    def _():
        kn = kn_ref[...].astype(BF16)
        vn = vn_ref[...].astype(BF16)
        s = jnp.concatenate([_dot_nt(qg[g], kn[:, g * DSA_HD:(g + 1) * DSA_HD]) for g in range(DSA_KV_HEADS)], axis=0)
        sel = jnp.concatenate([seln_ref[...][:, 0:tq]] * DSA_HEADS, axis=0) > 0.5
        update(jnp.where(sel, s, NEG_BIG), lambda pb, g: _dot(pb, vn[:, g * DSA_HD:(g + 1) * DSA_HD]))
        acc = acc_scr[...] / l_scr[...]
        for hd in range(DSA_HEADS):
            o_ref[:, hd * DSA_HD:(hd + 1) * DSA_HD] = acc[hd * tq:(hd + 1) * tq, :].astype(o_ref.dtype)


def _dsa_sample_call(iq33, ik3, dq3, dk3, h3, pool_k, pool_v, pool_ik, page_table, layer):
    db, t, _ = dq3.shape
    n_pages = page_table.shape[1]
    n_keys = n_pages * PAGE_SIZE + t
    n_sel = min(TOPK_MAX, n_keys // 4)
    npg = 8 if n_pages % 8 == 0 else 1
    n_steps = n_pages // npg + 1
    kw = iq33.shape[-1] // IDX_HEADS
    sw = LANES

    def page_spec(*tail):
        def make(p):
            def imap(i, s, pt):
                return (layer, pt[i, jnp.minimum(s, n_steps - 2) * npg + p]) + (0,) * len(tail)
            return pl.BlockSpec((None, None) + tail, imap)
        return [make(p) for p in range(npg)]

    sel = pl.pallas_call(
        functools.partial(_dsa_select_body, tq=t, npg=npg, n_sel=n_sel, n_pages=n_pages),
        grid_spec=pltpu.PrefetchScalarGridSpec(
            num_scalar_prefetch=1,
            grid=(db, n_steps),
            in_specs=[pl.BlockSpec((None, t, IDX_HEADS * kw), lambda i, s, pt: (i, 0, 0)),
                      pl.BlockSpec((None, t, LANES), lambda i, s, pt: (i, 0, C_MISC // LANES)),
                      pl.BlockSpec((None, t, IDX_HD), lambda i, s, pt: (i, 0, 0))]
                     + page_spec(IDX_HD, PAGE_SIZE),
            out_specs=pl.BlockSpec((None, n_pages + 1, t, sw), lambda i, s, pt: (i, 0, 0, 0)),
            scratch_shapes=[pltpu.VMEM((n_pages + 1, t, LANES), I32), pltpu.VMEM((IDX_HEADS * t, kw), BF16)],
        ),
        out_shape=jax.ShapeDtypeStruct((db, n_pages + 1, t, sw), F32),
        compiler_params=_cparams("parallel", "arbitrary"),
        name="dsa_sample_select",
    )(page_table, iq33, h3, ik3, *([pool_ik] * npg))

    nrow = DSA_HEADS * t
    return pl.pallas_call(
        functools.partial(_dsa_attend_body, tq=t, npg=npg),
        grid_spec=pltpu.PrefetchScalarGridSpec(
            num_scalar_prefetch=1,
            grid=(db, n_steps),
            in_specs=[pl.BlockSpec((None, t, MIX_WIDTH), lambda i, s, pt: (i, 0, 0)),
                      pl.BlockSpec((None, t, LANES), lambda i, s, pt: (i, 0, 0)),
                      pl.BlockSpec((None, t, LANES), lambda i, s, pt: (i, 0, C_DSAV // LANES)),
                      pl.BlockSpec((None, npg, t, sw), lambda i, s, pt: (i, jnp.minimum(s, n_steps - 2), 0, 0)),
                      pl.BlockSpec((None, None, t, sw), lambda i, s, pt: (i, n_pages, 0, 0))]
                     + page_spec(DSA_KV_HEADS, DSA_HD, PAGE_SIZE) + page_spec(DSA_KV_HEADS, DSA_HD, PAGE_SIZE),
            out_specs=pl.BlockSpec((None, t, MIX_WIDTH), lambda i, s, pt: (i, 0, 0)),
            scratch_shapes=[pltpu.VMEM((nrow, DSA_HD), F32), pltpu.VMEM((nrow, 1), F32), pltpu.VMEM((nrow, 1), F32),
                            pltpu.VMEM((nrow, DSA_HD), F32)],
        ),
        out_shape=jax.ShapeDtypeStruct((db, t, MIX_WIDTH), BF16),
        compiler_params=_cparams("parallel", "arbitrary"),
        name="dsa_sample_attend",
    )(page_table, dq3, dk3, h3, sel, sel, *([pool_k] * npg), *([pool_v] * npg))


def _merge_body(x_ref, gate_ref, oa_ref, ob_ref, oc_ref, od_ref, wb_ref, wo_ref, o_ref):
    acc = None
    for bi, br in enumerate((oa_ref, ob_ref, oc_ref, od_ref)):
        proj = _dot(br[...], wb_ref[bi])
        term = _sigmoid(gate_ref[:, bi * D_MODEL:(bi + 1) * D_MODEL]) * proj
        acc = term if acc is None else acc + term
    o_ref[...] = x_ref[...] + _dot(acc.astype(BF16), wo_ref[...])


def _merge_call(x, h, branches, wl):
    n, d = x.shape
    tm = _tile(n, 256)
    br = pl.BlockSpec((tm, MIX_WIDTH), lambda i: (i, 0))
    return pl.pallas_call(
        _merge_body,
        grid=(n // tm,),
        in_specs=[pl.BlockSpec((tm, d), lambda i: (i, 0)),
                  pl.BlockSpec((tm, N_BRANCH * d), lambda i: (i, C_GATE // (N_BRANCH * d))),
                  br, br, br, br,
                  pl.BlockSpec((N_BRANCH, MIX_WIDTH, d), lambda i: (0, 0, 0)),
                  pl.BlockSpec((d, d), lambda i: (0, 0))],
        out_specs=pl.BlockSpec((tm, d), lambda i: (i, 0)),
        out_shape=jax.ShapeDtypeStruct((n, d), F32),
        compiler_params=_cparams("parallel"),
        name="merge",
    )(x, h, *branches, wl["w_branch"], wl["w_out"])


def _cross_body(x_ref, g_ref, wq_ref, mk_ref, mv_ref, wo_ref, o_ref, *, merged):
    x = x_ref[...]
    tm = x.shape[0]
    xn = _rms_rows(x, g_ref[...]).astype(BF16)
    q = _dot(xn, wq_ref[...]).astype(BF16)
    if merged:
        qrows = jnp.concatenate([q[:, hh * XA_HD:(hh + 1) * XA_HD] for hh in range(XA_HEADS)], axis=0)
        s = _dot_nt(qrows, mk_ref[...].astype(BF16)) * (XA_HD ** -0.5)
        rr = lax.broadcasted_iota(I32, s.shape, 0) // tm
        cc = lax.broadcasted_iota(I32, s.shape, 1) % XA_HEADS
        s = jnp.where(rr == cc, s, NEG_BIG)
        p = jnp.exp(s - jnp.max(s, axis=-1, keepdims=True))
        p = p / jnp.sum(p, axis=-1, keepdims=True)
        orow = _dot(p.astype(BF16), mv_ref[...].astype(BF16))
        outs = [orow[hh * tm:(hh + 1) * tm, :] for hh in range(XA_HEADS)]
    else:
        outs = []
        for hh in range(XA_HEADS):
            sl = slice(hh * XA_HD, (hh + 1) * XA_HD)
            s = _dot_nt(q[:, sl], mk_ref[:, sl].astype(BF16)) * (XA_HD ** -0.5)
            p = jnp.exp(s - jnp.max(s, axis=-1, keepdims=True))
            p = p / jnp.sum(p, axis=-1, keepdims=True)
            outs.append(_dot(p.astype(BF16), mv_ref[:, sl].astype(BF16)))
    o = jnp.concatenate(outs, axis=-1).astype(BF16)
    o_ref[...] = x + _dot(o, wo_ref[...])


def _cross_call(x3, wl, mk4, mv4, layer):
    b, t, d = x3.shape
    w = XA_HEADS * XA_HD
    tm = _tile(t, 512)
    merged = mk4.shape[-1] == XA_HD
    mem = pl.BlockSpec((None, None) + mk4.shape[2:], lambda i, j: (layer, i, 0, 0))
    return pl.pallas_call(
        functools.partial(_cross_body, merged=merged),
        grid=(b, t // tm),
        in_specs=[pl.BlockSpec((None, tm, d), lambda i, j: (i, j, 0)),
                  pl.BlockSpec((1, d), lambda i, j: (0, 0)),
                  pl.BlockSpec((d, w), lambda i, j: (0, 0)), mem, mem,
                  pl.BlockSpec((w, d), lambda i, j: (0, 0))],
        out_specs=pl.BlockSpec((None, tm, d), lambda i, j: (i, j, 0)),
        out_shape=jax.ShapeDtypeStruct((b, t, d), F32),
        compiler_params=_cparams("parallel", "parallel"),
        name="cross_attn",
    )(x3, wl["norm_xa_g"], wl["xa_wq"], mk4, mv4, wl["xa_wo"])


def _ffn_body(x_ref, g_ref, wg_ref, wu_ref, wd_ref, o_ref, xn_scr, acc_scr):
    f = pl.program_id(1)

    @pl.when(f == 0)
    def _():
        xn_scr[...] = _rms_rows(x_ref[...], g_ref[...]).astype(BF16)
        acc_scr[...] = x_ref[...]

    xn = xn_scr[...]
    hm = _silu(_dot(xn, wg_ref[...])) * _dot(xn, wu_ref[...])
    acc_scr[...] += _dot(hm.astype(BF16), wd_ref[...])

    @pl.when(f == pl.num_programs(1) - 1)
    def _():
        o_ref[...] = acc_scr[...]


def _ffn_call(x, wl):
    n, d = x.shape
    ff = wl["ffn_wg"].shape[1]
    tm = _tile(n, 512)
    tf = 256
    return pl.pallas_call(
        _ffn_body,
        grid=(n // tm, ff // tf),
        in_specs=[pl.BlockSpec((tm, d), lambda i, f: (i, 0)), pl.BlockSpec((1, d), lambda i, f: (0, 0)),
                  pl.BlockSpec((d, tf), lambda i, f: (0, f)), pl.BlockSpec((d, tf), lambda i, f: (0, f)),
                  pl.BlockSpec((tf, d), lambda i, f: (f, 0))],
        out_specs=pl.BlockSpec((tm, d), lambda i, f: (i, 0)),
        out_shape=jax.ShapeDtypeStruct((n, d), F32),
        scratch_shapes=[pltpu.VMEM((tm, d), BF16), pltpu.VMEM((tm, d), F32)],
        compiler_params=_cparams("parallel", "arbitrary"),
        name="ffn_swiglu",
    )(x, wl["norm_ffn_g"], wl["ffn_wg"], wl["ffn_wu"], wl["ffn_wd"])


def _moe_body(x_ref, g_ref, wr_ref, br_ref, wg_ref, wu_ref, wd_ref, o_ref, xn_scr, gate_scr, acc_scr):
    e = pl.program_id(1)
    f = pl.program_id(2)
    first = jnp.logical_and(e == 0, f == 0)

    @pl.when(first)
    def _():
        xn = _rms_rows(x_ref[...], g_ref[...])
        xn_scr[...] = xn.astype(BF16)
        acc_scr[...] = x_ref[...]
        logits = _dot_hi(xn, wr_ref[...]) + br_ref[...]
        lane = lax.broadcasted_iota(I32, logits.shape, 1)
        logits = jnp.where(lane < N_EXPERTS, logits, -jnp.inf)
        top1 = jnp.max(logits, axis=-1, keepdims=True)
        idx1 = jnp.min(jnp.where(logits == top1, lane, LANES), axis=-1, keepdims=True)
        rest = jnp.where(lane == idx1, -jnp.inf, logits)
        top2 = jnp.max(rest, axis=-1, keepdims=True)
        idx2 = jnp.min(jnp.where(rest == top2, lane, LANES), axis=-1, keepdims=True)
        e2 = jnp.exp(top2 - top1)
        p1 = 1.0 / (1.0 + e2)
        p2 = e2 / (1.0 + e2)
        gate_scr[...] = jnp.where(lane == idx1, p1, 0.0) + jnp.where(lane == idx2, p2, 0.0)

    gates = gate_scr[...]
    lane = lax.broadcasted_iota(I32, gates.shape, 1)
    ge = jnp.sum(jnp.where(lane == e, gates, 0.0), axis=-1, keepdims=True)
    xn = xn_scr[...]
    hm = _silu(_dot(xn, wg_ref[...])) * _dot(xn, wu_ref[...])
    acc_scr[...] += ge * _dot(hm.astype(BF16), wd_ref[...])

    @pl.when(jnp.logical_and(e == pl.num_programs(1) - 1, f == pl.num_programs(2) - 1))
    def _():
        o_ref[...] = acc_scr[...]


def _moe_call(x, wl):
    n, d = x.shape
    ne, _, fe = wl["moe_wg"].shape
    tm = _tile(n, 512)
    tf = _tile(fe, 512)
    return pl.pallas_call(
        _moe_body,
        grid=(n // tm, ne, fe // tf),
        in_specs=[pl.BlockSpec((tm, d), lambda i, e, f: (i, 0)), pl.BlockSpec((1, d), lambda i, e, f: (0, 0)),
                  pl.BlockSpec((d, LANES), lambda i, e, f: (0, 0)), pl.BlockSpec((1, LANES), lambda i, e, f: (0, 0)),
                  pl.BlockSpec((None, d, tf), lambda i, e, f: (e, 0, f)),
                  pl.BlockSpec((None, d, tf), lambda i, e, f: (e, 0, f)),
                  pl.BlockSpec((None, tf, d), lambda i, e, f: (e, f, 0))],
        out_specs=pl.BlockSpec((tm, d), lambda i, e, f: (i, 0)),
        out_shape=jax.ShapeDtypeStruct((n, d), F32),
        scratch_shapes=[pltpu.VMEM((tm, d), BF16), pltpu.VMEM((tm, LANES), F32), pltpu.VMEM((tm, d), F32)],
        compiler_params=_cparams("parallel", "arbitrary", "arbitrary"),
        name="moe_swiglu",
    )(x, wl["norm_ffn_g"], wl["moe_wr"], wl["moe_br"], wl["moe_wg"], wl["moe_wu"], wl["moe_wd"])


def _pad_rows(a, rows):
    return jnp.pad(a, ((0, rows - a.shape[0]), (0, 0)))


def _lane_row(vals, lane0):
    return jnp.zeros((1, LANES), F32).at[0, lane0:lane0 + vals.shape[0]].set(vals.astype(F32))


def _block_diag(w):
    nb, n, _ = w.shape
    eye = jnp.eye(nb, dtype=w.dtype)
    return jnp.einsum("aij,ab->aibj", w, eye).reshape(nb * n, nb * n)


def _layer_weights(l, p):
    w_t = jnp.transpose(p["w_in"], (2, 0, 1))[:, l, :]
    d = w_t.shape[1]
    misc = jnp.concatenate([w_t[O_IDXK:O_IDXK + IDX_HD + IDX_HEADS], w_t[O_GDNB:O_GDNB + 2 * GDN_HEADS],
                            jnp.zeros((LANES - IDX_HD - IDX_HEADS - 2 * GDN_HEADS, d), w_t.dtype)], axis=0)
    w_in_r = jnp.concatenate([
        w_t[O_GATE:], w_t[O_LRUX:O_DSAK], w_t[O_DIFQ:O_GATE], w_t[O_GDNZ:O_GDNB],
        w_t[O_GDNQKV:O_GDNZ], w_t[O_IDXQ:O_IDXK], w_t[O_DSAK:O_IDXQ], misc], axis=0)
    assert w_in_r.shape[0] == H_COLS
    wl = dict(
        norm_mix_g=p["norm_mix_g"][l], w_in=w_in_r.astype(BF16),
        lru_cw=_pad_rows(p["lru_conv_w"][l], SUBLANES), lru_cb=p["lru_conv_b"][l].reshape(1, -1),
        lru_wa=_block_diag(p["lru_wa"][l]).astype(BF16), lru_ba=p["lru_ba"][l].reshape(1, -1),
        lru_wx=_block_diag(p["lru_wx"][l]).astype(BF16), lru_bx=p["lru_bx"][l].reshape(1, -1),
        lru_lam=p["lru_lambda"][l].reshape(1, -1),
        gdn_cw=_pad_rows(p["gdn_conv_w"][l], SUBLANES),
        gdn_alog=_lane_row(p["gdn_a_log"][l], M_GDNA), gdn_dtb=_lane_row(p["gdn_dt_bias"][l], M_GDNA),
        gdn_ng=p["gdn_norm_g"][l].reshape(1, -1),
        diff_lam=p["diff_lambda"][l], diff_sg=p["diff_subln_g"][l].reshape(1, -1),
        w_branch=p["w_branch"][l].astype(BF16), w_out=p["w_out"][l].astype(BF16),
        norm_xa_g=p["norm_xa_g"][l].reshape(1, -1), norm_mem_g=p["norm_mem_g"][l],
        xa_wq=p["xa_wq"][l].astype(BF16), xa_wo=p["xa_wo"][l].astype(BF16),
        xa_wkv=jnp.concatenate([p["xa_wk"][l], p["xa_wv"][l]], axis=1).astype(BF16),
        norm_ffn_g=p["norm_ffn_g"][l].reshape(1, -1),
    )
    j = l // 2
    if l % 2 == 0:
        wl.update(ffn_wg=p["ffn_w_gate"][j].astype(BF16), ffn_wu=p["ffn_w_up"][j].astype(BF16),
                  ffn_wd=p["ffn_w_down"][j].astype(BF16))
    else:
        wl.update(moe_wr=jnp.pad(p["moe_router_w"][j], ((0, 0), (0, LANES - N_EXPERTS))),
                  moe_br=jnp.pad(p["moe_router_b"][j], (0, LANES - N_EXPERTS)).reshape(1, LANES),
                  moe_wg=p["moe_w_gate"][j].astype(BF16), moe_wu=p["moe_w_up"][j].astype(BF16),
                  moe_wd=p["moe_w_down"][j].astype(BF16))
    return wl


def _rope_tables(pos):
    half = DSA_HD // 2
    inv = ROPE_THETA ** (-jnp.arange(half, dtype=F32) / half)
    ang = pos.astype(F32)[:, None] * inv[None, :]
    cos, sin = jnp.cos(ang), jnp.sin(ang)
    return jnp.tile(cos, (1, 4)), jnp.tile(jnp.concatenate([-sin, sin], axis=1), (1, 2))


def _front_pad(buf):
    return jnp.pad(buf, ((0, 0), (SUBLANES - (CONV_W - 1), 0), (0, 0)))


def _group_layer(x3, l, wl, tabs, states, mem_kv, paged):
    b, t, d = x3.shape
    n = b * t
    x = x3.reshape(n, d)
    lam_init = 0.8 - 0.6 * math.exp(-0.3 * l)
    xn = _rmsnorm(x, wl["norm_mix_g"], BF16)
    h = _matmul(xn, wl["w_in"], tn=1408, w_rows_are_outputs=True)
    h3 = h.reshape(b, t, H_COLS)
    dq, fq, fk, fkb, fvb, iq3, dk, dkb, dvb, ik, ik3 = _rope_call(h, tabs[0], tabs[1])
    to3 = lambda a: a.reshape(b, t, a.shape[-1])
    dq3, fq3, fk3, dk3, ik3d = map(to3, (dq, fq, fk, dk, ik))

    o_a, lru_h = _lru_call(h3, wl, _front_pad(states["lru_conv"]), states["lru_h"].reshape(b, 1, -1),
                           reset_first=paged is None)
    o_c, gdn_s = _gdn_call(h3, wl, _front_pad(states["gdn_conv"]), states["gdn_s"])
    if paged is None:
        o_b = _dsa_prompt_call(to3(iq3), to3(ik3), dq3, to3(dkb), to3(dvb), h3)
        o_d = _diff_prompt_call(fq3, to3(fkb), to3(fvb), wl, lam_init)
    else:
        o_b = _dsa_sample_call(to3(iq3), ik3d, dq3, dk3, h3, paged["dsa_k"], paged["dsa_v"], paged["idx_k"],
                               paged["page_table"], l)
        o_d = _diff_sample_call(fq3, fk3, h3, paged["diff_k"], paged["diff_v"], paged["page_table"], l, wl, lam_init)

    flat = lambda a: a.reshape(n, a.shape[-1])
    x1 = _merge_call(x, h, [flat(o_a), flat(o_b), flat(o_c), flat(o_d)], wl)
    x2 = _cross_call(x1.reshape(b, t, d), wl, mem_kv[0], mem_kv[1], mem_kv[2])
    x2 = x2.reshape(n, d)
    x3_new = (_ffn_call(x2, wl) if "ffn_wg" in wl else _moe_call(x2, wl)).reshape(b, t, d)

    new = dict(
        dsa_k=dk3.reshape(b, t, DSA_KV_HEADS, DSA_HD),
        dsa_v=h3[:, :, C_DSAV:C_DSAV + DSA_KV_HEADS * DSA_HD].reshape(b, t, DSA_KV_HEADS, DSA_HD),
        idx_k=ik3d,
        diff_k=fk3.reshape(b, t, DIFF_HEADS, 2 * DIFF_HD),
        diff_v=h3[:, :, C_DIFV:C_DIFV + MIX_WIDTH].reshape(b, t, DIFF_HEADS, 2 * DIFF_HD),
        lru_h=lru_h.reshape(b, -1),
        lru_conv=jnp.concatenate([states["lru_conv"], h3[:, :, C_LRUX:C_LRUX + MIX_WIDTH]], axis=1)[:, -(CONV_W - 1):]
        if t < CONV_W - 1 else h3[:, t - (CONV_W - 1):, C_LRUX:C_LRUX + MIX_WIDTH],
        gdn_s=gdn_s,
        gdn_conv=jnp.concatenate([states["gdn_conv"], h3[:, :, C_GDNQKV:C_GDNQKV + GDN_QKV]], axis=1)[:, -(CONV_W - 1):]
        if t < CONV_W - 1 else h3[:, t - (CONV_W - 1):, C_GDNQKV:C_GDNQKV + GDN_QKV],
    )
    return x3_new, new


def kernel(x_prompt, x_sample, mem_prompt, cache_dsa_k, cache_dsa_v, cache_idx_k, cache_diff_k, cache_diff_v,
           cache_mem_k, cache_mem_v, state_lru_h, state_lru_conv, state_gdn_s, state_gdn_conv, page_table,
           norm_mix_g, w_in, lru_conv_w, lru_conv_b, lru_wa, lru_ba, lru_wx, lru_bx, lru_lambda,
           gdn_conv_w, gdn_a_log, gdn_dt_bias, gdn_norm_g, diff_lambda, diff_subln_g, w_branch, w_out,
           norm_xa_g, norm_mem_g, xa_wq, xa_wk, xa_wv, xa_wo, norm_ffn_g, ffn_w_gate, ffn_w_up, ffn_w_down,
           moe_router_w, moe_router_b, moe_w_gate, moe_w_up, moe_w_down, final_norm_g):
    params = dict(
        norm_mix_g=norm_mix_g, w_in=w_in, lru_conv_w=lru_conv_w, lru_conv_b=lru_conv_b, lru_wa=lru_wa, lru_ba=lru_ba,
        lru_wx=lru_wx, lru_bx=lru_bx, lru_lambda=lru_lambda, gdn_conv_w=gdn_conv_w, gdn_a_log=gdn_a_log,
        gdn_dt_bias=gdn_dt_bias, gdn_norm_g=gdn_norm_g, diff_lambda=diff_lambda, diff_subln_g=diff_subln_g,
        w_branch=w_branch, w_out=w_out, norm_xa_g=norm_xa_g, norm_mem_g=norm_mem_g, xa_wq=xa_wq, xa_wk=xa_wk,
        xa_wv=xa_wv, xa_wo=xa_wo, norm_ffn_g=norm_ffn_g, ffn_w_gate=ffn_w_gate, ffn_w_up=ffn_w_up,
        ffn_w_down=ffn_w_down, moe_router_w=moe_router_w, moe_router_b=moe_router_b, moe_w_gate=moe_w_gate,
        moe_w_up=moe_w_up, moe_w_down=moe_w_down)
    b, s, d = x_prompt.shape
    db, t, _ = x_sample.shape
    m = mem_prompt.shape[1]
    n_pool = cache_dsa_k.shape[1]
    past = page_table.shape[1] * PAGE_SIZE

    tabs_p = _rope_tables(jnp.arange(s))
    cs, ss = _rope_tables(past + jnp.arange(t))
    tabs_s = (jnp.tile(cs, (db, 1)), jnp.tile(ss, (db, 1)))
    paged = dict(
        dsa_k=jnp.transpose(cache_dsa_k, (0, 1, 3, 4, 2)), dsa_v=jnp.transpose(cache_dsa_v, (0, 1, 3, 4, 2)),
        idx_k=jnp.transpose(cache_idx_k, (0, 1, 3, 2)), page_table=page_table,
        diff_k=cache_diff_k.reshape(DEPTH, n_pool, PAGE_SIZE * DIFF_HEADS, 2 * DIFF_HD),
        diff_v=cache_diff_v.reshape(DEPTH, n_pool, PAGE_SIZE * DIFF_HEADS, 2 * DIFF_HD))
    mem_k_s = cache_mem_k.reshape(DEPTH, db, m * XA_HEADS, XA_HD)
    mem_v_s = cache_mem_v.reshape(DEPTH, db, m * XA_HEADS, XA_HD)
    zero_states = dict(lru_conv=jnp.zeros((b, CONV_W - 1, MIX_WIDTH), F32), lru_h=jnp.zeros((b, MIX_WIDTH), F32),
                       gdn_conv=jnp.zeros((b, CONV_W - 1, GDN_QKV), F32),
                       gdn_s=jnp.zeros((b, GDN_HEADS, GDN_HD, GDN_HD), F32))

    hp, hs = x_prompt, x_sample
    new_p, new_s, mem_ks, mem_vs = [], [], [], []
    for l in range(DEPTH):
        wl = _layer_weights(l, params)
        mem_n = _rmsnorm(mem_prompt.reshape(b * m, d), wl["norm_mem_g"], BF16)
        kv = _matmul(mem_n, wl["xa_wkv"], tn=512)
        w = XA_HEADS * XA_HD
        mk = kv[:, :w].reshape(1, b, m, w)
        mv = kv[:, w:].reshape(1, b, m, w)
        hp, st_p = _group_layer(hp, l, wl, tabs_p, zero_states, (mk, mv, 0), None)
        new_p.append(st_p)
        mem_ks.append(mk.reshape(b, m, XA_HEADS, XA_HD))
        mem_vs.append(mv.reshape(b, m, XA_HEADS, XA_HD))
        states_s = dict(lru_conv=state_lru_conv[l], lru_h=state_lru_h[l], gdn_conv=state_gdn_conv[l],
                        gdn_s=state_gdn_s[l])
        hs, st_s = _group_layer(hs, l, wl, tabs_s, states_s, (mem_k_s, mem_v_s, l), paged)
        new_s.append(st_s)

    y_prompt = _rmsnorm(hp.reshape(b * s, d), final_norm_g, F32).reshape(b, s, d)
    y_sample = _rmsnorm(hs.reshape(db * t, d), final_norm_g, F32).reshape(db, t, d)
    stack = lambda states, name: jnp.stack([st[name] for st in states], axis=0)
    outs = [y_prompt, y_sample]
    for name in ("dsa_k", "dsa_v", "idx_k", "diff_k", "diff_v", "lru_h", "lru_conv", "gdn_s", "gdn_conv"):
        outs += [stack(new_p, name), stack(new_s, name)]
    outs += [jnp.stack(mem_ks, axis=0), jnp.stack(mem_vs, axis=0)]
    return tuple(outs)
```

```python
import functools
import math

import jax
import jax.numpy as jnp
from jax import lax
from jax.experimental import pallas as pl
from jax.experimental.pallas import tpu as pltpu

F32 = jnp.float32
BF16 = jnp.bfloat16
I32 = jnp.int32
I16 = jnp.int16

D_MODEL = 1024
DEPTH = 2
PAGE_SIZE = 128
MIX_WIDTH = 512
LRU_BLOCKS = 8
LRU_BLOCK = MIX_WIDTH // LRU_BLOCKS
LRU_C = 8.0
CONV_W = 4
DSA_HEADS = 8
DSA_KV_HEADS = 2
DSA_HD = 64
IDX_HEADS = 4
IDX_HD = 64
TOPK_MAX = 256
GDN_HEADS = 4
GDN_HD = 128
GDN_CHUNK = 128
DIFF_HEADS = 4
DIFF_HD = 64
XA_HEADS = 4
XA_HD = 128
N_EXPERTS = 8
ROPE_THETA = 10000.0
EPS = 1e-6
N_BRANCH = 4
GDN_QKV = 3 * GDN_HEADS * GDN_HD

LANES = 128
SUBLANES = 8
NEG_BIG = -1e30
INT_MIN = -2 ** 31
HALF_BIAS = 2 ** 15
VMEM_LIMIT = 48 * 1024 * 1024

C_GATE = 0
C_LRUX = 4096
C_LRUG = 4608
C_DSAQ = 5120
C_DIFQ = 5632
C_DIFK = 6144
C_DIFV = 6656
C_GDNZ = 7168
C_GDNQKV = 7680
C_IDXQ = 9216
C_DSAK = 9472
C_DSAV = 9600
C_MISC = 9728
H_COLS = 9856
M_IDXK = 0
M_IDXW = 64
M_GDNB = 68
M_GDNA = 72

O_LRUX, O_LRUG, O_DSAQ, O_DSAK, O_DSAV, O_IDXQ, O_IDXK, O_IDXW = 0, 512, 1024, 1536, 1664, 1792, 2048, 2112
O_GDNQKV, O_GDNZ, O_GDNB, O_GDNA, O_DIFQ, O_DIFK, O_DIFV, O_GATE = 2116, 3652, 4164, 4168, 4172, 4684, 5196, 5708


def _cparams(*sem):
    return pltpu.CompilerParams(dimension_semantics=sem, vmem_limit_bytes=VMEM_LIMIT)


def _tile(n, pref):
    return pref if n % pref == 0 else n


def _pages_per_step(n_pages):
    return next(p for p in (16, 8, 1) if n_pages % p == 0)


def _softplus(x):
    return jnp.maximum(x, 0.0) + jnp.log1p(jnp.exp(-jnp.abs(x)))


def _sigmoid(x):
    return 1.0 / (1.0 + jnp.exp(-x))


def _silu(x):
    return x * _sigmoid(x)


def _dot(a, b):
    return jnp.dot(a, b, preferred_element_type=F32)


def _dot_nt(a, b):
    return lax.dot_general(a, b, (((1,), (1,)), ((), ())), preferred_element_type=F32)


def _dot_hi(a, b):
    return jnp.dot(a, b, preferred_element_type=F32, precision=lax.Precision.HIGHEST)


def _split_hi_lo(x):
    hi = x.astype(BF16)
    return hi, (x - hi.astype(F32)).astype(BF16)


_NN = (((1,), (0,)), ((), ()))
_NT = (((1,), (1,)), ((), ()))
_TN = (((0,), (0,)), ((), ()))


def _dot3(a, b, dims=_NN):
    a_hi, a_lo = _split_hi_lo(a)
    b_hi, b_lo = _split_hi_lo(b)
    dg = functools.partial(lax.dot_general, dimension_numbers=dims, preferred_element_type=F32)
    ca, cb = dims[0][0][0], dims[0][1][0]
    if a.shape[ca] % LANES == 0:
        return dg(jnp.concatenate([a_hi, a_hi, a_lo], axis=ca), jnp.concatenate([b_hi, b_lo, b_hi], axis=cb))
    return dg(a_hi, b_hi) + dg(a_hi, b_lo) + dg(a_lo, b_hi)


def _dot_exact_lhs(a, b):
    ab = a.astype(BF16)
    b1 = b.astype(BF16)
    r1 = b - b1.astype(F32)
    b2 = r1.astype(BF16)
    b3 = (r1 - b2.astype(F32)).astype(BF16)
    if a.shape[1] % LANES == 0:
        return _dot(jnp.concatenate([ab, ab, ab], axis=1), jnp.concatenate([b1, b2, b3], axis=0))
    return _dot(ab, b1) + _dot(ab, b2) + _dot(ab, b3)


def _rms_rows(x, g):
    return x * lax.rsqrt(jnp.mean(x * x, axis=-1, keepdims=True) + EPS) * g


def _rmsnorm_body(x_ref, g_ref, o_ref):
    o_ref[...] = _rms_rows(x_ref[...], g_ref[...]).astype(o_ref.dtype)


def _rmsnorm(x, g, out_dtype):
    n, d = x.shape
    tm = _tile(n, 512)
    return pl.pallas_call(
        _rmsnorm_body,
        grid=(n // tm,),
        in_specs=[pl.BlockSpec((tm, d), lambda i: (i, 0)), pl.BlockSpec((1, d), lambda i: (0, 0))],
        out_specs=pl.BlockSpec((tm, d), lambda i: (i, 0)),
        out_shape=jax.ShapeDtypeStruct((n, d), out_dtype),
        compiler_params=_cparams("parallel"),
        name="rmsnorm",
    )(x, g.reshape(1, d))


def _matmul_body(a_ref, w_ref, o_ref, *, w_rows_are_outputs):
    mm = _dot_nt if w_rows_are_outputs else _dot
    o_ref[...] = mm(a_ref[...], w_ref[...]).astype(o_ref.dtype)


def _matmul(a, w, tn, w_rows_are_outputs=False, out_dtype=F32):
    n, k = a.shape
    c = w.shape[0] if w_rows_are_outputs else w.shape[1]
    tm = _tile(n, 512)
    w_spec = (pl.BlockSpec((tn, k), lambda j, i: (j, 0)) if w_rows_are_outputs
              else pl.BlockSpec((k, tn), lambda j, i: (0, j)))
    return pl.pallas_call(
        functools.partial(_matmul_body, w_rows_are_outputs=w_rows_are_outputs),
        grid=(c // tn, n // tm),
        in_specs=[pl.BlockSpec((tm, k), lambda j, i: (i, 0)), w_spec],
        out_specs=pl.BlockSpec((tm, tn), lambda j, i: (i, j)),
        out_shape=jax.ShapeDtypeStruct((n, c), out_dtype),
        compiler_params=_cparams("parallel", "parallel"),
        name="matmul",
    )(a, w)


def _rope_block(x, cos, sin):
    lane = lax.broadcasted_iota(I32, x.shape, 1)
    first = (lane % 64) < 32
    partner = jnp.where(first, pltpu.roll(x, LANES - 32, 1), pltpu.roll(x, 32, 1))
    return x * cos + partner * sin


def _rope_wide(x, cos, sin):
    return [_rope_block(x[:, c * LANES:(c + 1) * LANES], cos, sin) for c in range(x.shape[1] // LANES)]


def _rope_body(cos_ref, sin_ref, dq_ref, fq_ref, fk_ref, fv_ref, iq_ref, dk_ref, dv_ref, misc_ref,
               dq_o, fq_o, fk_o, fkb_o, fvb_o, iq3_o, dk_o, dkb_o, dvb_o, ik_o, ik3_o):
    cos = cos_ref[...]
    sin = sin_ref[...]
    lane = lax.broadcasted_iota(I32, cos.shape, 1)
    low = lane < IDX_HD
    for c, blk in enumerate(_rope_wide(dq_ref[...], cos, sin)):
        dq_o[:, c * LANES:(c + 1) * LANES] = (blk * DSA_HD ** -0.5).astype(BF16)
    for c, blk in enumerate(_rope_wide(fq_ref[...], cos, sin)):
        fq_o[:, c * LANES:(c + 1) * LANES] = (blk * DIFF_HD ** -0.5).astype(BF16)
    for c, blk in enumerate(_rope_wide(fk_ref[...], cos, sin)):
        fk_o[:, c * LANES:(c + 1) * LANES] = blk
        fkb_o[:, c * LANES:(c + 1) * LANES] = blk.astype(BF16)
    fvb_o[...] = fv_ref[...].astype(BF16)
    dk = _rope_block(dk_ref[...], cos, sin)
    dk_o[...] = dk
    dkb_o[...] = dk.astype(BF16)
    dv = dv_ref[...]
    dvb_o[:, 0:LANES] = jnp.where(low, dv, 1.0).astype(BF16)
    dvb_o[:, LANES:2 * LANES] = jnp.where(low, pltpu.roll(dv, DSA_HD, 1), 1.0).astype(BF16)
    for c, blk in enumerate(_rope_wide(iq_ref[...], cos, sin)):
        hi = blk.astype(BF16).astype(F32)
        lo = blk - hi
        hi_sw = pltpu.roll(hi, IDX_HD, 1)
        lo_sw = pltpu.roll(lo, IDX_HD, 1)
        base = 2 * c * 2 * LANES
        iq3_o[:, base:base + LANES] = jnp.where(low, hi, hi_sw).astype(BF16)
        iq3_o[:, base + LANES:base + 2 * LANES] = jnp.where(low, lo, 0.0).astype(BF16)
        iq3_o[:, base + 2 * LANES:base + 3 * LANES] = jnp.where(low, hi_sw, hi).astype(BF16)
        iq3_o[:, base + 3 * LANES:base + 4 * LANES] = jnp.where(low, lo_sw, 0.0).astype(BF16)
    ik = _rope_block(misc_ref[...], cos, sin)
    ik_o[...] = ik[:, M_IDXK:M_IDXK + IDX_HD]
    hi = ik.astype(BF16).astype(F32)
    lo = ik - hi
    ik3_o[:, 0:LANES] = jnp.where(low, hi, pltpu.roll(lo, IDX_HD, 1)).astype(BF16)
    ik3_o[:, LANES:2 * LANES] = jnp.where(low, hi, 0.0).astype(BF16)


def _rope_call(h, cos, sin):
    n = h.shape[0]
    tm = _tile(min(n, cos.shape[0]), 512)
    nt = cos.shape[0] // tm

    def col(width, off):
        return pl.BlockSpec((tm, width), lambda i: (i, off // width))

    def out(width):
        return pl.BlockSpec((tm, width), lambda i: (i, 0))

    sds = jax.ShapeDtypeStruct
    tab = pl.BlockSpec((tm, LANES), lambda i: (i % nt, 0))
    return pl.pallas_call(
        _rope_body,
        grid=(n // tm,),
        in_specs=[tab, tab, col(512, C_DSAQ), col(512, C_DIFQ), col(512, C_DIFK), col(512, C_DIFV),
                  col(256, C_IDXQ), col(128, C_DSAK), col(128, C_DSAV), col(128, C_MISC)],
        out_specs=[out(512), out(512), out(512), out(512), out(512), out(1024), out(128), out(128), out(256),
                   out(IDX_HD), out(256)],
        out_shape=[sds((n, 512), BF16), sds((n, 512), BF16), sds((n, 512), F32), sds((n, 512), BF16),
                   sds((n, 512), BF16), sds((n, 1024), BF16), sds((n, 128), F32), sds((n, 128), BF16),
                   sds((n, 256), BF16), sds((n, IDX_HD), F32), sds((n, 256), BF16)],
        compiler_params=_cparams("parallel"),
        name="rope",
    )(cos, sin, h, h, h, h, h, h, h, h)


def _conv_tile(x_ref, buf_ref, cw_ref, xcat, t, tc):
    @pl.when(t == 0)
    def _():
        xcat[0:SUBLANES, :] = buf_ref[...]

    @pl.when(t > 0)
    def _():
        xcat[0:SUBLANES, :] = xcat[tc:tc + SUBLANES, :]

    xcat[SUBLANES:SUBLANES + tc, :] = x_ref[...]
    cw = cw_ref[...]
    y = x_ref[...] * cw[CONV_W - 1:CONV_W, :]
    for j in range(CONV_W - 1):
        off = SUBLANES - (CONV_W - 1) + j
        y = y + xcat[off:off + tc, :] * cw[j:j + 1, :]
    return y


def _lru_body(x_ref, gate_ref, cw_ref, cb_ref, wa_ref, ba_ref, wx_ref, bx_ref, lam_ref, buf_ref, h0_ref,
              o_ref, hl_ref, xcat, hc, *, tc, reset_first):
    t = pl.program_id(1)

    @pl.when(t == 0)
    def _():
        hc[...] = h0_ref[...]

    xa = _conv_tile(x_ref, buf_ref, cw_ref, xcat, t, tc) + cb_ref[...]
    xb = xa.astype(BF16)
    r = _sigmoid(_dot(xb, wa_ref[...]) + ba_ref[...])
    gi = _sigmoid(_dot(xb, wx_ref[...]) + bx_ref[...])
    log_a = -LRU_C * r * _softplus(-lam_ref[...])
    a = jnp.exp(log_a)
    mult = jnp.sqrt(-jnp.tanh(log_a) * (a * a + 1.0))
    row = lax.broadcasted_iota(I32, a.shape, 0)
    if reset_first:
        is0 = jnp.logical_and(row == 0, t == 0)
        a = jnp.where(is0, 0.0, a)
        mult = jnp.where(is0, 1.0, mult)
    u = mult * gi * xa
    d = 1
    while d < tc:
        keep = row >= d
        a_sh = jnp.where(keep, pltpu.roll(a, d, 0), 1.0)
        u_sh = jnp.where(keep, pltpu.roll(u, d, 0), 0.0)
        u = a * u_sh + u
        a = a * a_sh
        d *= 2
    h = a * hc[...] + u
    h_last = h[tc - 1:tc, :]
    hc[...] = h_last
    hl_ref[...] = h_last
    o_ref[...] = (h * jax.nn.gelu(gate_ref[...])).astype(o_ref.dtype)


def _lru_call(h3, wl, buf, h0, reset_first):
    b, t, _ = h3.shape
    tc = _tile(t, 256)
    w = MIX_WIDTH
    row = pl.BlockSpec((1, w), lambda i, j: (0, 0))
    sq = pl.BlockSpec((w, w), lambda i, j: (0, 0))
    return pl.pallas_call(
        functools.partial(_lru_body, tc=tc, reset_first=reset_first),
        grid=(b, t // tc),
        in_specs=[pl.BlockSpec((None, tc, w), lambda i, j: (i, j, C_LRUX // w)),
                  pl.BlockSpec((None, tc, w), lambda i, j: (i, j, C_LRUG // w)),
                  pl.BlockSpec((SUBLANES, w), lambda i, j: (0, 0)), row, sq, row, sq, row, row,
                  pl.BlockSpec((None, SUBLANES, w), lambda i, j: (i, 0, 0)),
                  pl.BlockSpec((None, 1, w), lambda i, j: (i, 0, 0))],
        out_specs=[pl.BlockSpec((None, tc, w), lambda i, j: (i, j, 0)),
                   pl.BlockSpec((None, 1, w), lambda i, j: (i, 0, 0))],
        out_shape=[jax.ShapeDtypeStruct((b, t, w), BF16), jax.ShapeDtypeStruct((b, 1, w), F32)],
        scratch_shapes=[pltpu.VMEM((tc + SUBLANES, w), F32), pltpu.VMEM((1, w), F32)],
        compiler_params=_cparams("parallel", "arbitrary"),
        name="rg_lru",
    )(h3, h3, wl["lru_cw"], wl["lru_cb"], wl["lru_wa"], wl["lru_ba"], wl["lru_wx"], wl["lru_bx"], wl["lru_lam"],
      buf, h0)


def _gdn_body(qkv_ref, z_ref, misc_ref, cw_ref, alog_ref, dtb_ref, ng_ref, buf_ref, s0_ref,
              o_ref, so_ref, xcat, y_scr, state, *, tc, ck):
    t = pl.program_id(1)
    hd = GDN_HD

    @pl.when(t == 0)
    def _():
        state[...] = s0_ref[...]

    y = _conv_tile(qkv_ref, buf_ref, cw_ref, xcat, t, tc)
    y_scr[...] = _silu(y)

    ri = lax.broadcasted_iota(I32, (ck, ck), 0)
    ci = lax.broadcasted_iota(I32, (ck, ck), 1)
    tril = ri >= ci
    strict = ri > ci
    ltri = jnp.where(tril, 1.0, 0.0).astype(F32)
    ones = jnp.ones((ck, ck), F32)
    n_sq = max(int(math.log2(ck)) - 1, 0)

    def chunk(c, carry):
        r0 = pl.multiple_of(c * ck, ck)
        heads = range(GDN_HEADS)
        misc = misc_ref[pl.ds(r0, ck), :]
        beta_all = _sigmoid(misc)
        g_all = -jnp.exp(alog_ref[...]) * _softplus(misc + dtb_ref[...])
        cols = lambda blk: slice(blk * hd, (blk + 1) * hd)
        q = [y_scr[pl.ds(r0, ck), cols(hh)] for hh in heads]
        k = [y_scr[pl.ds(r0, ck), cols(GDN_HEADS + hh)] for hh in heads]
        v = [y_scr[pl.ds(r0, ck), cols(2 * GDN_HEADS + hh)] for hh in heads]
        q = [x * lax.rsqrt(jnp.sum(x * x, axis=-1, keepdims=True) + EPS) * (hd ** -0.5) for x in q]
        k = [x * lax.rsqrt(jnp.sum(x * x, axis=-1, keepdims=True) + EPS) for x in k]
        beta = [jnp.broadcast_to(beta_all[:, M_GDNB + hh:M_GDNB + hh + 1], (ck, hd)) for hh in heads]
        g128 = [jnp.broadcast_to(g_all[:, M_GDNA + hh:M_GDNA + hh + 1], (ck, hd)) for hh in heads]
        gc128 = [_dot_exact_lhs(ltri, g) for g in g128]
        gc_row = [_dot_exact_lhs(ones, jnp.where(ri <= ci, g[:, :ck], 0.0)) for g in g128]
        decay = [jnp.where(tril, jnp.exp(jnp.where(tril, gc128[hh][:, :ck] - gc_row[hh], 0.0)), 0.0) for hh in heads]
        kb = [k[hh] * beta[hh] for hh in heads]
        egc = [jnp.exp(g) for g in gc128]
        low = [jnp.where(strict, _dot3(kb[hh], k[hh], _NT) * decay[hh], 0.0) for hh in heads]
        intra = [jnp.where(tril, _dot3(q[hh], k[hh], _NT) * decay[hh], 0.0) for hh in heads]
        mp = [-x for x in low]
        x = [jnp.concatenate([v[hh] * beta[hh], kb[hh] * egc[hh]], axis=-1) for hh in heads]
        x = [x[hh] + _dot3(mp[hh], x[hh]) for hh in heads]
        for _ in range(n_sq):
            mp = [_dot3(m, m) for m in mp]
            x = [x[hh] + _dot3(mp[hh], x[hh]) for hh in heads]
        s = [state[hh] for hh in heads]
        v_new = [x[hh][:, :hd] - _dot3(x[hh][:, hd:], s[hh]) for hh in heads]
        o = [_dot3(q[hh] * egc[hh], s[hh]) + _dot3(intra[hh], v_new[hh]) for hh in heads]
        g_last = [g[ck - 1:ck, :] for g in gc128]
        kd = [k[hh] * jnp.exp(g_last[hh] - gc128[hh]) for hh in heads]
        s_new = [s[hh] * jnp.exp(g_last[hh]) + _dot3(kd[hh], v_new[hh], _TN) for hh in heads]
        for hh in heads:
            zz = z_ref[pl.ds(r0, ck), cols(hh)]
            state[hh] = s_new[hh]
            o_ref[pl.ds(r0, ck), cols(hh)] = (_rms_rows(o[hh], ng_ref[...]) * _silu(zz)).astype(o_ref.dtype)
        return carry

    lax.fori_loop(0, tc // ck, chunk, 0)
    so_ref[...] = state[...]


def _gdn_call(h3, wl, buf, s0):
    b, t, _ = h3.shape
    ck = min(GDN_CHUNK, t)
    assert t % ck == 0
    tc = _tile(t, 256)
    w = GDN_QKV
    row = pl.BlockSpec((1, LANES), lambda i, j: (0, 0))
    st = pl.BlockSpec((None, GDN_HEADS, GDN_HD, GDN_HD), lambda i, j: (i, 0, 0, 0))
    return pl.pallas_call(
        functools.partial(_gdn_body, tc=tc, ck=ck),
        grid=(b, t // tc),
        in_specs=[pl.BlockSpec((None, tc, w), lambda i, j: (i, j, C_GDNQKV // w)),
                  pl.BlockSpec((None, tc, MIX_WIDTH), lambda i, j: (i, j, C_GDNZ // MIX_WIDTH)),
                  pl.BlockSpec((None, tc, LANES), lambda i, j: (i, j, C_MISC // LANES)),
                  pl.BlockSpec((SUBLANES, w), lambda i, j: (0, 0)), row, row, row,
                  pl.BlockSpec((None, SUBLANES, w), lambda i, j: (i, 0, 0)), st],
        out_specs=[pl.BlockSpec((None, tc, MIX_WIDTH), lambda i, j: (i, j, 0)), st],
        out_shape=[jax.ShapeDtypeStruct((b, t, MIX_WIDTH), BF16),
                   jax.ShapeDtypeStruct((b, GDN_HEADS, GDN_HD, GDN_HD), F32)],
        scratch_shapes=[pltpu.VMEM((tc + SUBLANES, w), F32), pltpu.VMEM((tc, w), F32),
                        pltpu.VMEM((GDN_HEADS, GDN_HD, GDN_HD), F32)],
        compiler_params=_cparams("parallel", "arbitrary"),
        name="gated_deltanet",
    )(h3, h3, h3, wl["gdn_cw"], wl["gdn_alog"], wl["gdn_dtb"], wl["gdn_ng"], buf, s0)


def _lane_tile(x, n):
    return x if n == LANES else jnp.concatenate([x] * (n // LANES), axis=1)


def _online_update(s, v_ext, m_ref, acc_ref, idx):
    m_old = m_ref[idx]
    m_new = jnp.maximum(m_old, jnp.max(s, axis=-1, keepdims=True))
    alpha = jnp.exp(m_old - m_new)
    p = jnp.exp(s - _lane_tile(m_new, s.shape[1]))
    acc = acc_ref[idx]
    acc_ref[idx] = _lane_tile(alpha, acc.shape[1]) * acc + _dot(p.astype(BF16), v_ext)
    m_ref[idx] = m_new


def _online_update_blocks(s, vs, m_ref, l_ref, acc_ref, idx):
    m_old = m_ref[idx]
    m_new = jnp.maximum(m_old, jnp.max(s, axis=-1, keepdims=True))
    alpha = jnp.exp(m_old - m_new)
    p = jnp.exp(s - m_new)
    l_ref[idx] = alpha * l_ref[idx] + jnp.sum(p, axis=-1, keepdims=True)
    pb = p.astype(BF16)
    pv, off = None, 0
    for vblk in vs:
        part = _dot(pb[:, off:off + vblk.shape[0]], vblk)
        pv = part if pv is None else pv + part
        off += vblk.shape[0]
    acc_ref[idx] = alpha * acc_ref[idx] + pv
    m_ref[idx] = m_new


def _reset_softmax(m_ref, l_ref, acc_ref):
    m_ref[...] = jnp.full(m_ref.shape, NEG_BIG, F32)
    l_ref[...] = jnp.zeros(l_ref.shape, F32)
    acc_ref[...] = jnp.zeros(acc_ref.shape, F32)


def _diff_lambda(lam_ref, lam_init):
    lm = lam_ref[...]
    e1 = jnp.exp(jnp.sum(lm[0:1, :] * lm[1:2, :], axis=-1, keepdims=True))
    e2 = jnp.exp(jnp.sum(lm[2:3, :] * lm[3:4, :], axis=-1, keepdims=True))
    return e1 - e2 + lam_init


def _diff_prompt_body(qi_ref, kj_ref, lam_ref, sg_ref, q_ref, k_ref, v_ref, o_ref, m_scr, acc_scr, *, tq, lam_init):
    step = pl.program_id(2)
    qi = qi_ref[step]
    kj = kj_ref[step]
    hv = 2 * DIFF_HD

    @pl.when(kj == 0)
    def _():
        m_scr[...] = jnp.full(m_scr.shape, NEG_BIG, F32)
        acc_scr[...] = jnp.zeros(acc_scr.shape, F32)

    def tile(masked):
        v = v_ref[...]
        v_ext = jnp.concatenate([v, jnp.ones_like(v)], axis=1)
        s = [_dot_nt(q_ref[:, mm * DIFF_HD:(mm + 1) * DIFF_HD], k_ref[:, mm * DIFF_HD:(mm + 1) * DIFF_HD])
             for mm in range(2)]
        for mm in range(2):
            sm = s[mm]
            if masked:
                row = lax.broadcasted_iota(I32, (tq, tq), 0)
                col = lax.broadcasted_iota(I32, (tq, tq), 1)
                sm = jnp.where(col <= row, sm, NEG_BIG)
            _online_update(sm, v_ext, m_scr, acc_scr, mm)

    @pl.when(kj < qi)
    def _():
        tile(False)

    @pl.when(kj == qi)
    def _():
        tile(True)
        lam = _diff_lambda(lam_ref, lam_init)
        o = acc_scr[0, :, 0:hv] / acc_scr[0, :, hv:2 * hv] - lam * (acc_scr[1, :, 0:hv] / acc_scr[1, :, hv:2 * hv])
        o_ref[...] = (_rms_rows(o, sg_ref[...]) * (1.0 - lam_init)).astype(o_ref.dtype)


def _diff_prompt_call(fq3, fkb3, fvb3, wl, lam_init):
    b, t, _ = fq3.shape
    tq = _tile(t, 512)
    nq = t // tq
    hv = 2 * DIFF_HD
    pairs = [(q, k) for q in range(nq) for k in range(q + 1)]
    qi_of = jnp.asarray([p[0] for p in pairs], I32)
    kj_of = jnp.asarray([p[1] for p in pairs], I32)
    grid_spec = pltpu.PrefetchScalarGridSpec(
        num_scalar_prefetch=2,
        grid=(b, DIFF_HEADS, len(pairs)),
        in_specs=[pl.BlockSpec((4, DIFF_HD), lambda i, h, s, qi, kj: (0, 0)),
                  pl.BlockSpec((1, hv), lambda i, h, s, qi, kj: (0, 0)),
                  pl.BlockSpec((None, tq, hv), lambda i, h, s, qi, kj: (i, qi[s], h)),
                  pl.BlockSpec((None, tq, hv), lambda i, h, s, qi, kj: (i, kj[s], h)),
                  pl.BlockSpec((None, tq, hv), lambda i, h, s, qi, kj: (i, kj[s], h))],
        out_specs=pl.BlockSpec((None, tq, hv), lambda i, h, s, qi, kj: (i, qi[s], h)),
        scratch_shapes=[pltpu.VMEM((2, tq, LANES), F32), pltpu.VMEM((2, tq, 2 * hv), F32)],
    )
    return pl.pallas_call(
        functools.partial(_diff_prompt_body, tq=tq, lam_init=lam_init),
        grid_spec=grid_spec,
        out_shape=jax.ShapeDtypeStruct((b, t, MIX_WIDTH), BF16),
        compiler_params=_cparams("parallel", "parallel", "arbitrary"),
        name="diff_attn_prompt",
    )(qi_of, kj_of, wl["diff_lam"], wl["diff_sg"], fq3, fkb3, fvb3)


def _diff_sample_body(pt_ref, lam_ref, sg_ref, q_ref, kn_ref, vn_ref, *rest, tq, npg, lam_init):
    kp = rest[:npg]
    vp = rest[npg:2 * npg]
    o_ref, qrows, m_scr, l_scr, acc_scr = rest[2 * npg:]
    s_id = pl.program_id(1)
    n_steps = pl.num_programs(1)
    nrow = 2 * DIFF_HEADS * tq
    hv = 2 * DIFF_HD
    prow = PAGE_SIZE * DIFF_HEADS

    @pl.when(s_id == 0)
    def _():
        _reset_softmax(m_scr, l_scr, acc_scr)
        qrows[...] = jnp.zeros(qrows.shape, F32)
        q = q_ref[...].astype(F32)
        for j in range(2 * DIFF_HEADS):
            mm = j % 2
            qrows[j * tq:(j + 1) * tq, mm * DIFF_HD:(mm + 1) * DIFF_HD] = q[:, j * DIFF_HD:(j + 1) * DIFF_HD]

    qb = qrows[...].astype(BF16)
    row_head = lax.broadcasted_iota(I32, (nrow, prow), 0) // (2 * tq)

    @pl.when(s_id < n_steps - 1)
    def _():
        key_head = lax.broadcasted_iota(I32, (nrow, prow), 1) % DIFF_HEADS
        own = key_head == row_head
        scores = [jnp.where(own, _dot_nt(qb, kp[p][...].astype(BF16)), NEG_BIG) for p in range(npg)]
        _online_update_blocks(jnp.concatenate(scores, axis=-1), [vp[p][...].astype(BF16) for p in range(npg)],
                       m_scr, l_scr, acc_scr, Ellipsis)

    @pl.when(s_id == n_steps - 1)
    def _():
        kn = jnp.concatenate([kn_ref[:, hh * hv:(hh + 1) * hv] for hh in range(DIFF_HEADS)], axis=0).astype(BF16)
        vn = jnp.concatenate([vn_ref[:, hh * hv:(hh + 1) * hv] for hh in range(DIFF_HEADS)], axis=0).astype(BF16)
        nk = DIFF_HEADS * tq
        rr = lax.broadcasted_iota(I32, (nrow, nk), 0)
        cc = lax.broadcasted_iota(I32, (nrow, nk), 1)
        ok = jnp.logical_and(cc // tq == rr // (2 * tq), cc % tq <= rr % tq)
        s = jnp.where(ok, _dot_nt(qb, kn), NEG_BIG)
        _online_update_blocks(s, [vn], m_scr, l_scr, acc_scr, Ellipsis)
        lam = _diff_lambda(lam_ref, lam_init)
        acc = acc_scr[...] / l_scr[...]
        for hh in range(DIFF_HEADS):
            r1 = (2 * hh) * tq
            r2 = (2 * hh + 1) * tq
            o = acc[r1:r1 + tq, :] - lam * acc[r2:r2 + tq, :]
            o_ref[:, hh * hv:(hh + 1) * hv] = (_rms_rows(o, sg_ref[...]) * (1.0 - lam_init)).astype(o_ref.dtype)


def _diff_sample_call(fq3, fk3, h3, pool_k, pool_v, page_table, layer, wl, lam_init):
    db, t, _ = fq3.shape
    n_pages = page_table.shape[1]
    npg = _pages_per_step(n_pages)
    n_steps = n_pages // npg + 1
    w = MIX_WIDTH
    hv = 2 * DIFF_HD
    nrow = 2 * DIFF_HEADS * t
    prow = PAGE_SIZE * DIFF_HEADS

    def page_spec(p):
        def imap(i, s, pt):
            return (layer, pt[i, jnp.minimum(s, n_steps - 2) * npg + p], 0, 0)
        return pl.BlockSpec((None, None, prow, hv), imap)

    grid_spec = pltpu.PrefetchScalarGridSpec(
        num_scalar_prefetch=1,
        grid=(db, n_steps),
        in_specs=[pl.BlockSpec((4, DIFF_HD), lambda i, s, pt: (0, 0)),
                  pl.BlockSpec((1, hv), lambda i, s, pt: (0, 0)),
                  pl.BlockSpec((None, t, w), lambda i, s, pt: (i, 0, 0)),
                  pl.BlockSpec((None, t, w), lambda i, s, pt: (i, 0, 0)),
                  pl.BlockSpec((None, t, w), lambda i, s, pt: (i, 0, C_DIFV // w))]
                 + [page_spec(p) for p in range(npg)] + [page_spec(p) for p in range(npg)],
        out_specs=pl.BlockSpec((None, t, w), lambda i, s, pt: (i, 0, 0)),
        scratch_shapes=[pltpu.VMEM((nrow, hv), F32), pltpu.VMEM((nrow, 1), F32), pltpu.VMEM((nrow, 1), F32),
                        pltpu.VMEM((nrow, hv), F32)],
    )
    return pl.pallas_call(
        functools.partial(_diff_sample_body, tq=t, npg=npg, lam_init=lam_init),
        grid_spec=grid_spec,
        out_shape=jax.ShapeDtypeStruct((db, t, w), BF16),
        compiler_params=_cparams("parallel", "arbitrary"),
        name="diff_attn_sample",
    )(page_table, wl["diff_lam"], wl["diff_sg"], fq3, fk3, h3, *([pool_k] * npg), *([pool_v] * npg))


def _sortable(score):
    bits = pltpu.bitcast(jnp.where(score == 0.0, 0.0, score), I32)
    return bits ^ ((bits >> 31) & 0x7FFFFFFF)


def _search_threshold(count_ge, rows, n_sel):
    def bit_body(i, t_u):
        cand_u = t_u | (jnp.int32(1) << (31 - i))
        cnt = count_ge(cand_u ^ INT_MIN)
        return jnp.where(cnt >= n_sel, cand_u, t_u)

    t_u = lax.fori_loop(0, 32, bit_body, jnp.zeros((rows, 1), I32))
    return t_u ^ INT_MIN


def _lane_fold(x):
    acc = x[:, 0:LANES]
    for c in range(1, x.shape[1] // LANES):
        acc = acc + x[:, c * LANES:(c + 1) * LANES]
    return acc


def _strictly_before(n):
    ui = lax.broadcasted_iota(I32, (n, n), 0)
    uj = lax.broadcasted_iota(I32, (n, n), 1)
    return jnp.where(ui < uj, 1.0, 0.0).astype(BF16)


def _dsa_prompt_body(iq3_ref, misc_ref, ik3_ref, q_ref, k_ref, v_ref, o_ref, iqs, qs, keys_scr, half_scr, bias_scr,
                     m_scr, acc_scr, w_scr, spread_scr, c16_scr, cnt_scr, cnt16_scr, *, tq, ck, n_sel):
    qi = pl.program_id(1)
    nck = (qi * tq + tq + ck - 1) // ck
    kw = iqs.shape[1]
    for hh in range(IDX_HEADS):
        iqs[hh * tq:(hh + 1) * tq, :] = iq3_ref[:, hh * kw:(hh + 1) * kw]
    wts = misc_ref[...][:, M_IDXW:M_IDXW + IDX_HEADS] * (IDX_HD ** -0.5 * IDX_HEADS ** -0.5)
    for hh in range(IDX_HEADS):
        w_scr[hh] = jnp.broadcast_to(wts[:, hh:hh + 1], (tq, LANES))
    row_pos = qi * tq + lax.broadcasted_iota(I32, (tq, ck), 0)
    lane_pos = lax.broadcasted_iota(I32, (tq, ck), 1)

    def score_chunk(c, carry):
        k0 = pl.multiple_of(c * ck, ck)
        s = jnp.maximum(_dot_nt(iqs[...], ik3_ref[pl.ds(k0, ck), :]), 0.0)
        sc = _lane_tile(w_scr[0], ck) * s[0:tq, :]
        for hh in range(1, IDX_HEADS):
            sc = sc + _lane_tile(w_scr[hh], ck) * s[hh * tq:(hh + 1) * tq, :]
        keys_scr[c] = jnp.where(c * ck + lane_pos <= row_pos, _sortable(sc), INT_MIN)
        return carry

    lax.fori_loop(0, nck, score_chunk, 0)

    def parked(slot):
        return _lane_tile(spread_scr[slot], ck)

    def count(pred):
        cnt_scr[...] = jnp.zeros((tq, LANES), I32)

        def body(c, carry):
            cnt_scr[...] += _lane_fold(jnp.where(pred(keys_scr[c]), 1, 0).astype(I32))
            return carry
        lax.fori_loop(0, nck, body, 0)
        return jnp.sum(cnt_scr[...], axis=-1, keepdims=True)

    ones_sq = jnp.ones((LANES, LANES), BF16)

    def search_half(half_scr):
        def bit_body(i, t_u):
            cand_u = t_u | (jnp.int32(1) << (15 - i))
            c16_scr[...] = (cand_u - HALF_BIAS).astype(I16)
            cnt16_scr[...] = jnp.zeros((tq, LANES), I16)

            def body(c, carry):
                hit = half_scr[c] >= _lane_tile(c16_scr[...], ck)
                cnt16_scr[...] += _lane_fold(jnp.where(hit, jnp.int16(1), jnp.int16(0)))
                return carry
            lax.fori_loop(0, nck, body, 0)
            tot = _dot(cnt16_scr[...].astype(F32).astype(BF16), ones_sq)
            return jnp.where(tot >= n_sel, cand_u, t_u)
        return lax.fori_loop(0, 16, bit_body, jnp.zeros((tq, LANES), I32)) - HALF_BIAS

    def split_hi(c, carry):
        half_scr[c] = (keys_scr[c] >> 16).astype(I16)
        return carry
    lax.fori_loop(0, nck, split_hi, 0)
    t_hi = search_half(half_scr)

    spread_scr[0] = t_hi

    def split_lo(c, carry):
        kk = keys_scr[c]
        hi = kk >> 16
        lo = (kk & 0xFFFF) - HALF_BIAS
        t_hi_b = parked(0)
        half_scr[c] = jnp.where(hi == t_hi_b, lo, jnp.where(hi > t_hi_b, HALF_BIAS - 1, -HALF_BIAS)).astype(I16)
        return carry
    lax.fori_loop(0, nck, split_lo, 0)
    t_lo = search_half(half_scr)
    thr_rep = (t_hi << 16) | ((t_lo + HALF_BIAS) & 0xFFFF)
    spread_scr[1] = thr_rep
    thr = thr_rep[:, 0:1]
    n_ge = count(lambda kk: kk >= parked(1))
    plain = jnp.max(jnp.where(jnp.logical_and(n_ge == n_sel, thr != INT_MIN), 0, 1)) == 0

    @pl.when(plain)
    def _():
        def body(c, carry):
            bias_scr[c] = jnp.where(keys_scr[c] >= parked(1), 0.0, NEG_BIG)
            return carry
        lax.fori_loop(0, nck, body, 0)

    @pl.when(jnp.logical_not(plain))
    def _():
        need = (n_sel - count(lambda kk: kk > parked(1))).astype(F32)
        before = _strictly_before(ck)

        def body(c, eq_seen):
            kk = keys_scr[c]
            thr_b = parked(1)
            eq = jnp.logical_and(kk == thr_b, c * ck + lane_pos <= row_pos)
            eqf = jnp.where(eq, 1.0, 0.0)
            rank = _dot(eqf.astype(BF16), before) + eq_seen
            sel = jnp.logical_or(kk > thr_b, jnp.logical_and(eq, rank < need))
            bias_scr[c] = jnp.where(sel, 0.0, NEG_BIG)
            return eq_seen + jnp.sum(eqf, axis=-1, keepdims=True)

        lax.fori_loop(0, nck, body, jnp.zeros((tq, 1), F32))

    m_scr[...] = jnp.full(m_scr.shape, NEG_BIG, F32)
    acc_scr[...] = jnp.zeros(acc_scr.shape, F32)
    hpg = DSA_HEADS // DSA_KV_HEADS
    for hd in range(DSA_HEADS):
        qs[hd * tq:(hd + 1) * tq, :] = q_ref[:, hd * DSA_HD:(hd + 1) * DSA_HD]

    def attend_chunk(c, carry):
        k0 = pl.multiple_of(c * ck, ck)
        bias = bias_scr[c]
        groups = range(DSA_KV_HEADS)
        rows = [pl.ds(g * hpg * tq, hpg * tq) for g in groups]
        s = [_dot_nt(qs[rows[g], :], k_ref[pl.ds(k0, ck), g * DSA_HD:(g + 1) * DSA_HD]) for g in groups]
        for g in groups:
            vg = v_ref[pl.ds(k0, ck), g * LANES:(g + 1) * LANES]
            sg = s[g].reshape(hpg, tq, ck) + bias[None]
            _online_update(sg.reshape(hpg * tq, ck), vg, m_scr, acc_scr, rows[g])
        return carry

    lax.fori_loop(0, nck, attend_chunk, 0)
    acc = acc_scr[...]
    out = acc / pltpu.roll(acc, DSA_HD, 1)
    for hd in range(DSA_HEADS):
        o_ref[:, hd * DSA_HD:(hd + 1) * DSA_HD] = out[hd * tq:(hd + 1) * tq, 0:DSA_HD].astype(o_ref.dtype)


def _dsa_prompt_call(iq33, ik33, dq3, dkb3, dvb3, h3):
    b, t, _ = dq3.shape
    n_sel = min(TOPK_MAX, t // 4)
    tq = _tile(t, 256)
    ck = _tile(t, 512)
    kw = iq33.shape[-1] // IDX_HEADS
    full = lambda width: pl.BlockSpec((None, t, width), lambda i, q: (i, 0, 0))
    return pl.pallas_call(
        functools.partial(_dsa_prompt_body, tq=tq, ck=ck, n_sel=n_sel),
        grid=(b, t // tq),
        in_specs=[pl.BlockSpec((None, tq, IDX_HEADS * kw), lambda i, q: (i, q, 0)),
                  pl.BlockSpec((None, tq, LANES), lambda i, q: (i, q, C_MISC // LANES)),
                  full(kw),
                  pl.BlockSpec((None, tq, MIX_WIDTH), lambda i, q: (i, q, 0)),
                  full(LANES), full(DSA_KV_HEADS * LANES)],
        out_specs=pl.BlockSpec((None, tq, MIX_WIDTH), lambda i, q: (i, q, 0)),
        out_shape=jax.ShapeDtypeStruct((b, t, MIX_WIDTH), BF16),
        scratch_shapes=[pltpu.VMEM((IDX_HEADS * tq, kw), BF16), pltpu.VMEM((DSA_HEADS * tq, DSA_HD), BF16),
                        pltpu.VMEM((t // ck, tq, ck), I32), pltpu.VMEM((t // ck, tq, ck), I16),
                        pltpu.VMEM((t // ck, tq, ck), F32),
                        pltpu.VMEM((DSA_HEADS * tq, LANES), F32), pltpu.VMEM((DSA_HEADS * tq, LANES), F32),
                        pltpu.VMEM((IDX_HEADS, tq, LANES), F32), pltpu.VMEM((2, tq, LANES), I32),
                        pltpu.VMEM((tq, LANES), I16), pltpu.VMEM((tq, LANES), I32), pltpu.VMEM((tq, LANES), I16)],
        compiler_params=_cparams("parallel", "arbitrary"),
        name="dsa_prompt",
    )(iq33, h3, ik33, dq3, dkb3, dvb3)


def _dsa_select_body(pt_ref, iq3_ref, misc_ref, ikn_ref, *rest, tq, npg, n_sel, n_pages):
    pages = rest[:npg]
    sel_ref, keys_scr, iqs = rest[npg:]
    s_id = pl.program_id(1)
    n_steps = pl.num_programs(1)
    kw = iqs.shape[1]
    wts = misc_ref[...][:, M_IDXW:M_IDXW + IDX_HEADS] * (IDX_HD ** -0.5 * IDX_HEADS ** -0.5)

    @pl.when(s_id == 0)
    def _():
        for hh in range(IDX_HEADS):
            iqs[hh * tq:(hh + 1) * tq, :] = iq3_ref[:, hh * kw:(hh + 1) * kw]

    def scores(ik, feature_major):
        hi, lo = _split_hi_lo(ik)
        a_hi = iqs[:, 0:IDX_HD]
        a_lo = iqs[:, 2 * IDX_HD:3 * IDX_HD]
        mm = _dot if feature_major else _dot_nt
        s = jnp.maximum(mm(a_hi, hi) + mm(a_hi, lo) + mm(a_lo, hi), 0.0)
        sc = wts[:, 0:1] * s[0:tq, :]
        for hh in range(1, IDX_HEADS):
            sc = sc + wts[:, hh:hh + 1] * s[hh * tq:(hh + 1) * tq, :]
        return sc

    @pl.when(s_id < n_steps - 1)
    def _():
        keys = _sortable(scores(jnp.concatenate([pages[p][...] for p in range(npg)], axis=1), True))
        for p in range(npg):
            keys_scr[s_id * npg + p] = keys[:, p * PAGE_SIZE:(p + 1) * PAGE_SIZE]

    @pl.when(s_id == n_steps - 1)
    def _():
        sc = scores(ikn_ref[...], False)
        qrow = lax.broadcasted_iota(I32, (tq, tq), 0)
        kcol = lax.broadcasted_iota(I32, (tq, tq), 1)
        new_keys = jnp.where(kcol <= qrow, _sortable(sc), INT_MIN)
        keys_scr[n_pages] = jnp.concatenate([new_keys, jnp.full((tq, LANES - tq), INT_MIN, I32)], axis=-1)

        def count(pred):
            acc = jnp.sum(jnp.where(pred(keys_scr[...]), 1, 0).astype(I32), axis=0)
            return jnp.sum(acc, axis=-1, keepdims=True)

        thr = _search_threshold(lambda cand: count(lambda kk: kk >= cand[None]), tq, n_sel)
        n_ge = count(lambda kk: kk >= thr[None])
        plain = jnp.max(jnp.where(jnp.logical_and(n_ge == n_sel, thr != INT_MIN), 0, 1)) == 0

        @pl.when(plain)
        def _():
            sel_ref[...] = jnp.where(keys_scr[...] >= thr[None], 1.0, 0.0)

        @pl.when(jnp.logical_not(plain))
        def _():
            need = (n_sel - count(lambda kk: kk > thr[None])).astype(F32)
            before = _strictly_before(LANES)
            lane = lax.broadcasted_iota(I32, (tq, LANES), 1)
            qr = lax.broadcasted_iota(I32, (tq, LANES), 0)

            def emit(c, eq_seen):
                kk = keys_scr[c]
                allowed = jnp.logical_or(c < n_pages, lane <= qr)
                eq = jnp.logical_and(kk == thr, allowed)
                eqf = jnp.where(eq, 1.0, 0.0)
                rank = _dot(eqf.astype(BF16), before) + eq_seen
                sel = jnp.logical_or(kk > thr, jnp.logical_and(eq, rank < need))
                sel_ref[c] = jnp.where(sel, 1.0, 0.0)
                return eq_seen + jnp.sum(eqf, axis=-1, keepdims=True)

            lax.fori_loop(0, n_pages + 1, emit, jnp.zeros((tq, 1), F32))


def _dsa_attend_body(pt_ref, q_ref, kn_ref, vn_ref, selp_ref, seln_ref, *rest, tq, npg):
    kp = rest[:npg]
    vp = rest[npg:2 * npg]
    o_ref, qrows, m_scr, l_scr, acc_scr = rest[2 * npg:]
    s_id = pl.program_id(1)
    n_steps = pl.num_programs(1)
    hpg = DSA_HEADS // DSA_KV_HEADS
    grows = hpg * tq

    @pl.when(s_id == 0)
    def _():
        _reset_softmax(m_scr, l_scr, acc_scr)
        for hd in range(DSA_HEADS):
            qrows[hd * tq:(hd + 1) * tq, :] = q_ref[:, hd * DSA_HD:(hd + 1) * DSA_HD].astype(F32)

    qb = qrows[...].astype(BF16)
    qg = [qb[g * grows:(g + 1) * grows, :] for g in range(DSA_KV_HEADS)]

    def update(s, pv_fn):
        m_old = m_scr[...]
        m_new = jnp.maximum(m_old, jnp.max(s, axis=-1, keepdims=True))
        alpha = jnp.exp(m_old - m_new)
        p = jnp.exp(s - m_new)
        l_scr[...] = alpha * l_scr[...] + jnp.sum(p, axis=-1, keepdims=True)
        pb = p.astype(BF16)
        pv = jnp.concatenate([pv_fn(pb[g * grows:(g + 1) * grows, :], g) for g in range(DSA_KV_HEADS)], axis=0)
        acc_scr[...] = alpha * acc_scr[...] + pv
        m_scr[...] = m_new

    @pl.when(s_id < n_steps - 1)
    def _():
        kcat = [jnp.concatenate([kp[p][g] for p in range(npg)], axis=1).astype(BF16) for g in range(DSA_KV_HEADS)]
        vcat = [jnp.concatenate([vp[p][g] for p in range(npg)], axis=1).astype(BF16) for g in range(DSA_KV_HEADS)]
        s = jnp.concatenate([_dot(qg[g], kcat[g]) for g in range(DSA_KV_HEADS)], axis=0)
        sel1 = jnp.concatenate([selp_ref[p] for p in range(npg)], axis=1)
        sel = jnp.concatenate([sel1] * DSA_HEADS, axis=0) > 0.5
        update(jnp.where(sel, s, NEG_BIG), lambda pb, g: _dot_nt(pb, vcat[g]))

    @pl.when(s_id == n_steps - 1)
    def _():
        kn = kn_ref[...].astype(BF16)
        vn = vn_ref[...].astype(BF16)
        s = jnp.concatenate([_dot_nt(qg[g], kn[:, g * DSA_HD:(g + 1) * DSA_HD]) for g in range(DSA_KV_HEADS)], axis=0)
        sel = jnp.concatenate([seln_ref[...][:, 0:tq]] * DSA_HEADS, axis=0) > 0.5
        update(jnp.where(sel, s, NEG_BIG), lambda pb, g: _dot(pb, vn[:, g * DSA_HD:(g + 1) * DSA_HD]))
        acc = acc_scr[...] / l_scr[...]
        for hd in range(DSA_HEADS):
            o_ref[:, hd * DSA_HD:(hd + 1) * DSA_HD] = acc[hd * tq:(hd + 1) * tq, :].astype(o_ref.dtype)


def _dsa_sample_call(iq33, ik3, dq3, dk3, h3, pool_k, pool_v, pool_ik, page_table, layer):
    db, t, _ = dq3.shape
    n_pages = page_table.shape[1]
    n_keys = n_pages * PAGE_SIZE + t
    n_sel = min(TOPK_MAX, n_keys // 4)
    npg = _pages_per_step(n_pages)
    n_steps = n_pages // npg + 1
    kw = iq33.shape[-1] // IDX_HEADS
    sw = LANES

    def page_spec(*tail):
        def make(p):
            def imap(i, s, pt):
                return (layer, pt[i, jnp.minimum(s, n_steps - 2) * npg + p]) + (0,) * len(tail)
            return pl.BlockSpec((None, None) + tail, imap)
        return [make(p) for p in range(npg)]

    sel = pl.pallas_call(
        functools.partial(_dsa_select_body, tq=t, npg=npg, n_sel=n_sel, n_pages=n_pages),
        grid_spec=pltpu.PrefetchScalarGridSpec(
            num_scalar_prefetch=1,
            grid=(db, n_steps),
            in_specs=[pl.BlockSpec((None, t, IDX_HEADS * kw), lambda i, s, pt: (i, 0, 0)),
                      pl.BlockSpec((None, t, LANES), lambda i, s, pt: (i, 0, C_MISC // LANES)),
                      pl.BlockSpec((None, t, IDX_HD), lambda i, s, pt: (i, 0, 0))]
                     + page_spec(IDX_HD, PAGE_SIZE),
            out_specs=pl.BlockSpec((None, n_pages + 1, t, sw), lambda i, s, pt: (i, 0, 0, 0)),
            scratch_shapes=[pltpu.VMEM((n_pages + 1, t, LANES), I32), pltpu.VMEM((IDX_HEADS * t, kw), BF16)],
        ),
        out_shape=jax.ShapeDtypeStruct((db, n_pages + 1, t, sw), F32),
        compiler_params=_cparams("parallel", "arbitrary"),
        name="dsa_sample_select",
    )(page_table, iq33, h3, ik3, *([pool_ik] * npg))

    nrow = DSA_HEADS * t
    return pl.pallas_call(
        functools.partial(_dsa_attend_body, tq=t, npg=npg),
        grid_spec=pltpu.PrefetchScalarGridSpec(
            num_scalar_prefetch=1,
            grid=(db, n_steps),
            in_specs=[pl.BlockSpec((None, t, MIX_WIDTH), lambda i, s, pt: (i, 0, 0)),
                      pl.BlockSpec((None, t, LANES), lambda i, s, pt: (i, 0, 0)),
                      pl.BlockSpec((None, t, LANES), lambda i, s, pt: (i, 0, C_DSAV // LANES)),
                      pl.BlockSpec((None, npg, t, sw), lambda i, s, pt: (i, jnp.minimum(s, n_steps - 2), 0, 0)),
                      pl.BlockSpec((None, None, t, sw), lambda i, s, pt: (i, n_pages, 0, 0))]
                     + page_spec(DSA_KV_HEADS, DSA_HD, PAGE_SIZE) + page_spec(DSA_KV_HEADS, DSA_HD, PAGE_SIZE),
            out_specs=pl.BlockSpec((None, t, MIX_WIDTH), lambda i, s, pt: (i, 0, 0)),
            scratch_shapes=[pltpu.VMEM((nrow, DSA_HD), F32), pltpu.VMEM((nrow, 1), F32), pltpu.VMEM((nrow, 1), F32),
                            pltpu.VMEM((nrow, DSA_HD), F32)],
        ),
        out_shape=jax.ShapeDtypeStruct((db, t, MIX_WIDTH), BF16),
        compiler_params=_cparams("parallel", "arbitrary"),
        name="dsa_sample_attend",
    )(page_table, dq3, dk3, h3, sel, sel, *([pool_k] * npg), *([pool_v] * npg))


def _merge_body(x_ref, gate_ref, oa_ref, ob_ref, oc_ref, od_ref, wb_ref, wo_ref, o_ref):
    acc = None
    for bi, br in enumerate((oa_ref, ob_ref, oc_ref, od_ref)):
        proj = _dot(br[...], wb_ref[bi])
        term = _sigmoid(gate_ref[:, bi * D_MODEL:(bi + 1) * D_MODEL]) * proj
        acc = term if acc is None else acc + term
    o_ref[...] = x_ref[...] + _dot(acc.astype(BF16), wo_ref[...])


def _merge_call(x, h, branches, wl):
    n, d = x.shape
    tm = _tile(n, 256)
    br = pl.BlockSpec((tm, MIX_WIDTH), lambda i: (i, 0))
    return pl.pallas_call(
        _merge_body,
        grid=(n // tm,),
        in_specs=[pl.BlockSpec((tm, d), lambda i: (i, 0)),
                  pl.BlockSpec((tm, N_BRANCH * d), lambda i: (i, C_GATE // (N_BRANCH * d))),
                  br, br, br, br,
                  pl.BlockSpec((N_BRANCH, MIX_WIDTH, d), lambda i: (0, 0, 0)),
                  pl.BlockSpec((d, d), lambda i: (0, 0))],
        out_specs=pl.BlockSpec((tm, d), lambda i: (i, 0)),
        out_shape=jax.ShapeDtypeStruct((n, d), F32),
        compiler_params=_cparams("parallel"),
        name="merge",
    )(x, h, *branches, wl["w_branch"], wl["w_out"])


def _cross_body(x_ref, g_ref, wq_ref, mk_ref, mv_ref, wo_ref, o_ref, *, merged):
    x = x_ref[...]
    tm = x.shape[0]
    xn = _rms_rows(x, g_ref[...]).astype(BF16)
    q = _dot(xn, wq_ref[...]).astype(BF16)
    if merged:
        qrows = jnp.concatenate([q[:, hh * XA_HD:(hh + 1) * XA_HD] for hh in range(XA_HEADS)], axis=0)
        s = _dot_nt(qrows, mk_ref[...].astype(BF16)) * (XA_HD ** -0.5)
        rr = lax.broadcasted_iota(I32, s.shape, 0) // tm
        cc = lax.broadcasted_iota(I32, s.shape, 1) % XA_HEADS
        s = jnp.where(rr == cc, s, NEG_BIG)
        p = jnp.exp(s - jnp.max(s, axis=-1, keepdims=True))
        p = p / jnp.sum(p, axis=-1, keepdims=True)
        orow = _dot(p.astype(BF16), mv_ref[...].astype(BF16))
        outs = [orow[hh * tm:(hh + 1) * tm, :] for hh in range(XA_HEADS)]
    else:
        outs = []
        for hh in range(XA_HEADS):
            sl = slice(hh * XA_HD, (hh + 1) * XA_HD)
            s = _dot_nt(q[:, sl], mk_ref[:, sl].astype(BF16)) * (XA_HD ** -0.5)
            p = jnp.exp(s - jnp.max(s, axis=-1, keepdims=True))
            p = p / jnp.sum(p, axis=-1, keepdims=True)
            outs.append(_dot(p.astype(BF16), mv_ref[:, sl].astype(BF16)))
    o = jnp.concatenate(outs, axis=-1).astype(BF16)
    o_ref[...] = x + _dot(o, wo_ref[...])


def _cross_call(x3, wl, mk4, mv4, layer):
    b, t, d = x3.shape
    w = XA_HEADS * XA_HD
    tm = _tile(t, 512)
    merged = mk4.shape[-1] == XA_HD
    mem = pl.BlockSpec((None, None) + mk4.shape[2:], lambda i, j: (layer, i, 0, 0))
    return pl.pallas_call(
        functools.partial(_cross_body, merged=merged),
        grid=(b, t // tm),
        in_specs=[pl.BlockSpec((None, tm, d), lambda i, j: (i, j, 0)),
                  pl.BlockSpec((1, d), lambda i, j: (0, 0)),
                  pl.BlockSpec((d, w), lambda i, j: (0, 0)), mem, mem,
                  pl.BlockSpec((w, d), lambda i, j: (0, 0))],
        out_specs=pl.BlockSpec((None, tm, d), lambda i, j: (i, j, 0)),
        out_shape=jax.ShapeDtypeStruct((b, t, d), F32),
        compiler_params=_cparams("parallel", "parallel"),
        name="cross_attn",
    )(x3, wl["norm_xa_g"], wl["xa_wq"], mk4, mv4, wl["xa_wo"])


def _ffn_body(x_ref, g_ref, wg_ref, wu_ref, wd_ref, o_ref, xn_scr, acc_scr):
    f = pl.program_id(1)

    @pl.when(f == 0)
    def _():
        xn_scr[...] = _rms_rows(x_ref[...], g_ref[...]).astype(BF16)
        acc_scr[...] = x_ref[...]

    xn = xn_scr[...]
    hm = _silu(_dot(xn, wg_ref[...])) * _dot(xn, wu_ref[...])
    acc_scr[...] += _dot(hm.astype(BF16), wd_ref[...])

    @pl.when(f == pl.num_programs(1) - 1)
    def _():
        o_ref[...] = acc_scr[...]


def _ffn_call(x, wl):
    n, d = x.shape
    ff = wl["ffn_wg"].shape[1]
    tm = _tile(n, 512)
    tf = 256
    return pl.pallas_call(
        _ffn_body,
        grid=(n // tm, ff // tf),
        in_specs=[pl.BlockSpec((tm, d), lambda i, f: (i, 0)), pl.BlockSpec((1, d), lambda i, f: (0, 0)),
                  pl.BlockSpec((d, tf), lambda i, f: (0, f)), pl.BlockSpec((d, tf), lambda i, f: (0, f)),
                  pl.BlockSpec((tf, d), lambda i, f: (f, 0))],
        out_specs=pl.BlockSpec((tm, d), lambda i, f: (i, 0)),
        out_shape=jax.ShapeDtypeStruct((n, d), F32),
        scratch_shapes=[pltpu.VMEM((tm, d), BF16), pltpu.VMEM((tm, d), F32)],
        compiler_params=_cparams("parallel", "arbitrary"),
        name="ffn_swiglu",
    )(x, wl["norm_ffn_g"], wl["ffn_wg"], wl["ffn_wu"], wl["ffn_wd"])


def _moe_body(x_ref, g_ref, wr_ref, br_ref, wg_ref, wu_ref, wd_ref, o_ref, xn_scr, gate_scr, acc_scr):
    e = pl.program_id(1)
    f = pl.program_id(2)
    first = jnp.logical_and(e == 0, f == 0)

    @pl.when(first)
    def _():
        xn = _rms_rows(x_ref[...], g_ref[...])
        xn_scr[...] = xn.astype(BF16)
        acc_scr[...] = x_ref[...]
        logits = _dot_hi(xn, wr_ref[...]) + br_ref[...]
        lane = lax.broadcasted_iota(I32, logits.shape, 1)
        logits = jnp.where(lane < N_EXPERTS, logits, -jnp.inf)
        top1 = jnp.max(logits, axis=-1, keepdims=True)
        idx1 = jnp.min(jnp.where(logits == top1, lane, LANES), axis=-1, keepdims=True)
        rest = jnp.where(lane == idx1, -jnp.inf, logits)
        top2 = jnp.max(rest, axis=-1, keepdims=True)
        idx2 = jnp.min(jnp.where(rest == top2, lane, LANES), axis=-1, keepdims=True)
        e2 = jnp.exp(top2 - top1)
        p1 = 1.0 / (1.0 + e2)
        p2 = e2 / (1.0 + e2)
        gate_scr[...] = jnp.where(lane == idx1, p1, 0.0) + jnp.where(lane == idx2, p2, 0.0)

    gates = gate_scr[...]
    lane = lax.broadcasted_iota(I32, gates.shape, 1)
    ge = jnp.sum(jnp.where(lane == e, gates, 0.0), axis=-1, keepdims=True)
    xn = xn_scr[...]
    hm = _silu(_dot(xn, wg_ref[...])) * _dot(xn, wu_ref[...])
    acc_scr[...] += ge * _dot(hm.astype(BF16), wd_ref[...])

    @pl.when(jnp.logical_and(e == pl.num_programs(1) - 1, f == pl.num_programs(2) - 1))
    def _():
        o_ref[...] = acc_scr[...]


def _moe_call(x, wl):
    n, d = x.shape
    ne, _, fe = wl["moe_wg"].shape
    tm = _tile(n, 512)
    tf = _tile(fe, 512)
    return pl.pallas_call(
        _moe_body,
        grid=(n // tm, ne, fe // tf),
        in_specs=[pl.BlockSpec((tm, d), lambda i, e, f: (i, 0)), pl.BlockSpec((1, d), lambda i, e, f: (0, 0)),
                  pl.BlockSpec((d, LANES), lambda i, e, f: (0, 0)), pl.BlockSpec((1, LANES), lambda i, e, f: (0, 0)),
                  pl.BlockSpec((None, d, tf), lambda i, e, f: (e, 0, f)),
                  pl.BlockSpec((None, d, tf), lambda i, e, f: (e, 0, f)),
                  pl.BlockSpec((None, tf, d), lambda i, e, f: (e, f, 0))],
        out_specs=pl.BlockSpec((tm, d), lambda i, e, f: (i, 0)),
        out_shape=jax.ShapeDtypeStruct((n, d), F32),
        scratch_shapes=[pltpu.VMEM((tm, d), BF16), pltpu.VMEM((tm, LANES), F32), pltpu.VMEM((tm, d), F32)],
        compiler_params=_cparams("parallel", "arbitrary", "arbitrary"),
        name="moe_swiglu",
    )(x, wl["norm_ffn_g"], wl["moe_wr"], wl["moe_br"], wl["moe_wg"], wl["moe_wu"], wl["moe_wd"])


def _pad_rows(a, rows):
    return jnp.pad(a, ((0, rows - a.shape[0]), (0, 0)))


def _lane_row(vals, lane0):
    return jnp.zeros((1, LANES), F32).at[0, lane0:lane0 + vals.shape[0]].set(vals.astype(F32))


def _block_diag(w):
    nb, n, _ = w.shape
    eye = jnp.eye(nb, dtype=w.dtype)
    return jnp.einsum("aij,ab->aibj", w, eye).reshape(nb * n, nb * n)


def _layer_weights(l, p):
    w_t = jnp.transpose(p["w_in"], (2, 0, 1))[:, l, :]
    d = w_t.shape[1]
    misc = jnp.concatenate([w_t[O_IDXK:O_IDXK + IDX_HD + IDX_HEADS], w_t[O_GDNB:O_GDNB + 2 * GDN_HEADS],
                            jnp.zeros((LANES - IDX_HD - IDX_HEADS - 2 * GDN_HEADS, d), w_t.dtype)], axis=0)
    w_in_r = jnp.concatenate([
        w_t[O_GATE:], w_t[O_LRUX:O_DSAK], w_t[O_DIFQ:O_GATE], w_t[O_GDNZ:O_GDNB],
        w_t[O_GDNQKV:O_GDNZ], w_t[O_IDXQ:O_IDXK], w_t[O_DSAK:O_IDXQ], misc], axis=0)
    assert w_in_r.shape[0] == H_COLS
    wl = dict(
        norm_mix_g=p["norm_mix_g"][l], w_in=w_in_r.astype(BF16),
        lru_cw=_pad_rows(p["lru_conv_w"][l], SUBLANES), lru_cb=p["lru_conv_b"][l].reshape(1, -1),
        lru_wa=_block_diag(p["lru_wa"][l]).astype(BF16), lru_ba=p["lru_ba"][l].reshape(1, -1),
        lru_wx=_block_diag(p["lru_wx"][l]).astype(BF16), lru_bx=p["lru_bx"][l].reshape(1, -1),
        lru_lam=p["lru_lambda"][l].reshape(1, -1),
        gdn_cw=_pad_rows(p["gdn_conv_w"][l], SUBLANES),
        gdn_alog=_lane_row(p["gdn_a_log"][l], M_GDNA), gdn_dtb=_lane_row(p["gdn_dt_bias"][l], M_GDNA),
        gdn_ng=p["gdn_norm_g"][l].reshape(1, -1),
        diff_lam=p["diff_lambda"][l], diff_sg=p["diff_subln_g"][l].reshape(1, -1),
        w_branch=p["w_branch"][l].astype(BF16), w_out=p["w_out"][l].astype(BF16),
        norm_xa_g=p["norm_xa_g"][l].reshape(1, -1), norm_mem_g=p["norm_mem_g"][l],
        xa_wq=p["xa_wq"][l].astype(BF16), xa_wo=p["xa_wo"][l].astype(BF16),
        xa_wkv=jnp.concatenate([p["xa_wk"][l], p["xa_wv"][l]], axis=1).astype(BF16),
        norm_ffn_g=p["norm_ffn_g"][l].reshape(1, -1),
    )
    j = l // 2
    if l % 2 == 0:
        wl.update(ffn_wg=p["ffn_w_gate"][j].astype(BF16), ffn_wu=p["ffn_w_up"][j].astype(BF16),
                  ffn_wd=p["ffn_w_down"][j].astype(BF16))
    else:
        wl.update(moe_wr=jnp.pad(p["moe_router_w"][j], ((0, 0), (0, LANES - N_EXPERTS))),
                  moe_br=jnp.pad(p["moe_router_b"][j], (0, LANES - N_EXPERTS)).reshape(1, LANES),
                  moe_wg=p["moe_w_gate"][j].astype(BF16), moe_wu=p["moe_w_up"][j].astype(BF16),
                  moe_wd=p["moe_w_down"][j].astype(BF16))
    return wl


def _rope_tables(pos):
    half = DSA_HD // 2
    inv = ROPE_THETA ** (-jnp.arange(half, dtype=F32) / half)
    ang = pos.astype(F32)[:, None] * inv[None, :]
    cos, sin = jnp.cos(ang), jnp.sin(ang)
    return jnp.tile(cos, (1, 4)), jnp.tile(jnp.concatenate([-sin, sin], axis=1), (1, 2))


def _front_pad(buf):
    return jnp.pad(buf, ((0, 0), (SUBLANES - (CONV_W - 1), 0), (0, 0)))


def _group_layer(x3, l, wl, tabs, states, mem_kv, paged):
    b, t, d = x3.shape
    n = b * t
    x = x3.reshape(n, d)
    lam_init = 0.8 - 0.6 * math.exp(-0.3 * l)
    xn = _rmsnorm(x, wl["norm_mix_g"], BF16)
    h = _matmul(xn, wl["w_in"], tn=1408, w_rows_are_outputs=True)
    h3 = h.reshape(b, t, H_COLS)
    dq, fq, fk, fkb, fvb, iq3, dk, dkb, dvb, ik, ik3 = _rope_call(h, tabs[0], tabs[1])
    to3 = lambda a: a.reshape(b, t, a.shape[-1])
    dq3, fq3, fk3, dk3, ik3d = map(to3, (dq, fq, fk, dk, ik))

    o_a, lru_h = _lru_call(h3, wl, _front_pad(states["lru_conv"]), states["lru_h"].reshape(b, 1, -1),
                           reset_first=paged is None)
    o_c, gdn_s = _gdn_call(h3, wl, _front_pad(states["gdn_conv"]), states["gdn_s"])
    if paged is None:
        o_b = _dsa_prompt_call(to3(iq3), to3(ik3), dq3, to3(dkb), to3(dvb), h3)
        o_d = _diff_prompt_call(fq3, to3(fkb), to3(fvb), wl, lam_init)
    else:
        o_b = _dsa_sample_call(to3(iq3), ik3d, dq3, dk3, h3, paged["dsa_k"], paged["dsa_v"], paged["idx_k"],
                               paged["page_table"], l)
        o_d = _diff_sample_call(fq3, fk3, h3, paged["diff_k"], paged["diff_v"], paged["page_table"], l, wl, lam_init)

    flat = lambda a: a.reshape(n, a.shape[-1])
    x1 = _merge_call(x, h, [flat(o_a), flat(o_b), flat(o_c), flat(o_d)], wl)
    x2 = _cross_call(x1.reshape(b, t, d), wl, mem_kv[0], mem_kv[1], mem_kv[2])
    x2 = x2.reshape(n, d)
    x3_new = (_ffn_call(x2, wl) if "ffn_wg" in wl else _moe_call(x2, wl)).reshape(b, t, d)

    new = dict(
        dsa_k=dk3.reshape(b, t, DSA_KV_HEADS, DSA_HD),
        dsa_v=h3[:, :, C_DSAV:C_DSAV + DSA_KV_HEADS * DSA_HD].reshape(b, t, DSA_KV_HEADS, DSA_HD),
        idx_k=ik3d,
        diff_k=fk3.reshape(b, t, DIFF_HEADS, 2 * DIFF_HD),
        diff_v=h3[:, :, C_DIFV:C_DIFV + MIX_WIDTH].reshape(b, t, DIFF_HEADS, 2 * DIFF_HD),
        lru_h=lru_h.reshape(b, -1),
        lru_conv=jnp.concatenate([states["lru_conv"], h3[:, :, C_LRUX:C_LRUX + MIX_WIDTH]], axis=1)[:, -(CONV_W - 1):]
        if t < CONV_W - 1 else h3[:, t - (CONV_W - 1):, C_LRUX:C_LRUX + MIX_WIDTH],
        gdn_s=gdn_s,
        gdn_conv=jnp.concatenate([states["gdn_conv"], h3[:, :, C_GDNQKV:C_GDNQKV + GDN_QKV]], axis=1)[:, -(CONV_W - 1):]
        if t < CONV_W - 1 else h3[:, t - (CONV_W - 1):, C_GDNQKV:C_GDNQKV + GDN_QKV],
    )
    return x3_new, new


def kernel(x_prompt, x_sample, mem_prompt, cache_dsa_k, cache_dsa_v, cache_idx_k, cache_diff_k, cache_diff_v,
           cache_mem_k, cache_mem_v, state_lru_h, state_lru_conv, state_gdn_s, state_gdn_conv, page_table,
           norm_mix_g, w_in, lru_conv_w, lru_conv_b, lru_wa, lru_ba, lru_wx, lru_bx, lru_lambda,
           gdn_conv_w, gdn_a_log, gdn_dt_bias, gdn_norm_g, diff_lambda, diff_subln_g, w_branch, w_out,
           norm_xa_g, norm_mem_g, xa_wq, xa_wk, xa_wv, xa_wo, norm_ffn_g, ffn_w_gate, ffn_w_up, ffn_w_down,
           moe_router_w, moe_router_b, moe_w_gate, moe_w_up, moe_w_down, final_norm_g):
    params = dict(
        norm_mix_g=norm_mix_g, w_in=w_in, lru_conv_w=lru_conv_w, lru_conv_b=lru_conv_b, lru_wa=lru_wa, lru_ba=lru_ba,
        lru_wx=lru_wx, lru_bx=lru_bx, lru_lambda=lru_lambda, gdn_conv_w=gdn_conv_w, gdn_a_log=gdn_a_log,
        gdn_dt_bias=gdn_dt_bias, gdn_norm_g=gdn_norm_g, diff_lambda=diff_lambda, diff_subln_g=diff_subln_g,
        w_branch=w_branch, w_out=w_out, norm_xa_g=norm_xa_g, norm_mem_g=norm_mem_g, xa_wq=xa_wq, xa_wk=xa_wk,
        xa_wv=xa_wv, xa_wo=xa_wo, norm_ffn_g=norm_ffn_g, ffn_w_gate=ffn_w_gate, ffn_w_up=ffn_w_up,
        ffn_w_down=ffn_w_down, moe_router_w=moe_router_w, moe_router_b=moe_router_b, moe_w_gate=moe_w_gate,
        moe_w_up=moe_w_up, moe_w_down=moe_w_down)
    b, s, d = x_prompt.shape
    db, t, _ = x_sample.shape
    m = mem_prompt.shape[1]
    n_pool = cache_dsa_k.shape[1]
    past = page_table.shape[1] * PAGE_SIZE

    tabs_p = _rope_tables(jnp.arange(s))
    cs, ss = _rope_tables(past + jnp.arange(t))
    tabs_s = (jnp.tile(cs, (db, 1)), jnp.tile(ss, (db, 1)))
    paged = dict(
        dsa_k=jnp.transpose(cache_dsa_k, (0, 1, 3, 4, 2)), dsa_v=jnp.transpose(cache_dsa_v, (0, 1, 3, 4, 2)),
        idx_k=jnp.transpose(cache_idx_k, (0, 1, 3, 2)), page_table=page_table,
        diff_k=cache_diff_k.reshape(DEPTH, n_pool, PAGE_SIZE * DIFF_HEADS, 2 * DIFF_HD),
        diff_v=cache_diff_v.reshape(DEPTH, n_pool, PAGE_SIZE * DIFF_HEADS, 2 * DIFF_HD))
    mem_k_s = cache_mem_k.reshape(DEPTH, db, m * XA_HEADS, XA_HD)
    mem_v_s = cache_mem_v.reshape(DEPTH, db, m * XA_HEADS, XA_HD)
    zero_states = dict(lru_conv=jnp.zeros((b, CONV_W - 1, MIX_WIDTH), F32), lru_h=jnp.zeros((b, MIX_WIDTH), F32),
                       gdn_conv=jnp.zeros((b, CONV_W - 1, GDN_QKV), F32),
                       gdn_s=jnp.zeros((b, GDN_HEADS, GDN_HD, GDN_HD), F32))

    hp, hs = x_prompt, x_sample
    new_p, new_s, mem_ks, mem_vs = [], [], [], []
    for l in range(DEPTH):
        wl = _layer_weights(l, params)
        mem_n = _rmsnorm(mem_prompt.reshape(b * m, d), wl["norm_mem_g"], BF16)
        kv = _matmul(mem_n, wl["xa_wkv"], tn=512)
        w = XA_HEADS * XA_HD
        mk = kv[:, :w].reshape(1, b, m, w)
        mv = kv[:, w:].reshape(1, b, m, w)
        hp, st_p = _group_layer(hp, l, wl, tabs_p, zero_states, (mk, mv, 0), None)
        new_p.append(st_p)
        mem_ks.append(mk.reshape(b, m, XA_HEADS, XA_HD))
        mem_vs.append(mv.reshape(b, m, XA_HEADS, XA_HD))
        states_s = dict(lru_conv=state_lru_conv[l], lru_h=state_lru_h[l], gdn_conv=state_gdn_conv[l],
                        gdn_s=state_gdn_s[l])
        hs, st_s = _group_layer(hs, l, wl, tabs_s, states_s, (mem_k_s, mem_v_s, l), paged)
        new_s.append(st_s)

    y_prompt = _rmsnorm(hp.reshape(b * s, d), final_norm_g, F32).reshape(b, s, d)
    y_sample = _rmsnorm(hs.reshape(db * t, d), final_norm_g, F32).reshape(db, t, d)
    stack = lambda states, name: jnp.stack([st[name] for st in states], axis=0)
    outs = [y_prompt, y_sample]
    for name in ("dsa_k", "dsa_v", "idx_k", "diff_k", "diff_v", "lru_h", "lru_conv", "gdn_s", "gdn_conv"):
        outs += [stack(new_p, name), stack(new_s, name)]
    outs += [jnp.stack(mem_ks, axis=0), jnp.stack(mem_vs, axis=0)]
    return tuple(outs)
```

```python
import functools
import math

import jax
import jax.numpy as jnp
from jax import lax
from jax.experimental import pallas as pl
from jax.experimental.pallas import tpu as pltpu

F32 = jnp.float32
BF16 = jnp.bfloat16
I32 = jnp.int32
I16 = jnp.int16

D_MODEL = 1024
DEPTH = 2
PAGE_SIZE = 128
MIX_WIDTH = 512
LRU_BLOCKS = 8
LRU_BLOCK = MIX_WIDTH // LRU_BLOCKS
LRU_C = 8.0
CONV_W = 4
DSA_HEADS = 8
DSA_KV_HEADS = 2
DSA_HD = 64
IDX_HEADS = 4
IDX_HD = 64
TOPK_MAX = 256
GDN_HEADS = 4
GDN_HD = 128
GDN_CHUNK = 128
DIFF_HEADS = 4
DIFF_HD = 64
XA_HEADS = 4
XA_HD = 128
N_EXPERTS = 8
ROPE_THETA = 10000.0
EPS = 1e-6
N_BRANCH = 4
GDN_QKV = 3 * GDN_HEADS * GDN_HD

LANES = 128
SUBLANES = 8
NEG_BIG = -1e30
INT_MIN = -2 ** 31
HALF_BIAS = 2 ** 15
VMEM_LIMIT = 48 * 1024 * 1024
VMEM_LIMIT_DSA_PROMPT = 58 * 1024 * 1024

C_GATE = 0
C_LRUX = 4096
C_LRUG = 4608
C_DSAQ = 5120
C_DIFQ = 5632
C_DIFK = 6144
C_DIFV = 6656
C_GDNZ = 7168
C_GDNQKV = 7680
C_IDXQ = 9216
C_DSAK = 9472
C_DSAV = 9600
C_MISC = 9728
H_COLS = 9856
M_IDXK = 0
M_IDXW = 64
M_GDNB = 68
M_GDNA = 72

O_LRUX, O_LRUG, O_DSAQ, O_DSAK, O_DSAV, O_IDXQ, O_IDXK, O_IDXW = 0, 512, 1024, 1536, 1664, 1792, 2048, 2112
O_GDNQKV, O_GDNZ, O_GDNB, O_GDNA, O_DIFQ, O_DIFK, O_DIFV, O_GATE = 2116, 3652, 4164, 4168, 4172, 4684, 5196, 5708


def _cparams(*sem):
    return pltpu.CompilerParams(dimension_semantics=sem, vmem_limit_bytes=VMEM_LIMIT)


def _tile(n, pref):
    return pref if n % pref == 0 else n


def _pages_per_step(n_pages):
    return next(p for p in (16, 8, 1) if n_pages % p == 0)


def _softplus(x):
    return jnp.maximum(x, 0.0) + jnp.log1p(jnp.exp(-jnp.abs(x)))


def _sigmoid(x):
    return 1.0 / (1.0 + jnp.exp(-x))


def _silu(x):
    return x * _sigmoid(x)


def _dot(a, b):
    return jnp.dot(a, b, preferred_element_type=F32)


def _dot_nt(a, b):
    return lax.dot_general(a, b, (((1,), (1,)), ((), ())), preferred_element_type=F32)


def _dot_hi(a, b):
    return jnp.dot(a, b, preferred_element_type=F32, precision=lax.Precision.HIGHEST)


def _split_hi_lo(x):
    hi = x.astype(BF16)
    return hi, (x - hi.astype(F32)).astype(BF16)


_NN = (((1,), (0,)), ((), ()))
_NT = (((1,), (1,)), ((), ()))
_TN = (((0,), (0,)), ((), ()))


def _dot3(a, b, dims=_NN):
    a_hi, a_lo = _split_hi_lo(a)
    b_hi, b_lo = _split_hi_lo(b)
    dg = functools.partial(lax.dot_general, dimension_numbers=dims, preferred_element_type=F32)
    ca, cb = dims[0][0][0], dims[0][1][0]
    if a.shape[ca] % LANES == 0:
        return dg(jnp.concatenate([a_hi, a_hi, a_lo], axis=ca), jnp.concatenate([b_hi, b_lo, b_hi], axis=cb))
    return dg(a_hi, b_hi) + dg(a_hi, b_lo) + dg(a_lo, b_hi)


def _dot_exact_lhs(a, b):
    ab = a.astype(BF16)
    b1 = b.astype(BF16)
    r1 = b - b1.astype(F32)
    b2 = r1.astype(BF16)
    b3 = (r1 - b2.astype(F32)).astype(BF16)
    if a.shape[1] % LANES == 0:
        return _dot(jnp.concatenate([ab, ab, ab], axis=1), jnp.concatenate([b1, b2, b3], axis=0))
    return _dot(ab, b1) + _dot(ab, b2) + _dot(ab, b3)


def _rms_rows(x, g):
    return x * lax.rsqrt(jnp.mean(x * x, axis=-1, keepdims=True) + EPS) * g


def _rmsnorm_body(x_ref, g_ref, o_ref):
    o_ref[...] = _rms_rows(x_ref[...], g_ref[...]).astype(o_ref.dtype)


def _rmsnorm(x, g, out_dtype):
    n, d = x.shape
    tm = _tile(n, 512)
    return pl.pallas_call(
        _rmsnorm_body,
        grid=(n // tm,),
        in_specs=[pl.BlockSpec((tm, d), lambda i: (i, 0)), pl.BlockSpec((1, d), lambda i: (0, 0))],
        out_specs=pl.BlockSpec((tm, d), lambda i: (i, 0)),
        out_shape=jax.ShapeDtypeStruct((n, d), out_dtype),
        compiler_params=_cparams("parallel"),
        name="rmsnorm",
    )(x, g.reshape(1, d))


def _matmul_body(a_ref, w_ref, o_ref, *, w_rows_are_outputs):
    mm = _dot_nt if w_rows_are_outputs else _dot
    o_ref[...] = mm(a_ref[...], w_ref[...]).astype(o_ref.dtype)


def _matmul(a, w, tn, w_rows_are_outputs=False, out_dtype=F32):
    n, k = a.shape
    c = w.shape[0] if w_rows_are_outputs else w.shape[1]
    tm = _tile(n, 512)
    w_spec = (pl.BlockSpec((tn, k), lambda j, i: (j, 0)) if w_rows_are_outputs
              else pl.BlockSpec((k, tn), lambda j, i: (0, j)))
    return pl.pallas_call(
        functools.partial(_matmul_body, w_rows_are_outputs=w_rows_are_outputs),
        grid=(c // tn, n // tm),
        in_specs=[pl.BlockSpec((tm, k), lambda j, i: (i, 0)), w_spec],
        out_specs=pl.BlockSpec((tm, tn), lambda j, i: (i, j)),
        out_shape=jax.ShapeDtypeStruct((n, c), out_dtype),
        compiler_params=_cparams("parallel", "parallel"),
        name="matmul",
    )(a, w)


def _rope_block(x, cos, sin):
    lane = lax.broadcasted_iota(I32, x.shape, 1)
    first = (lane % 64) < 32
    partner = jnp.where(first, pltpu.roll(x, LANES - 32, 1), pltpu.roll(x, 32, 1))
    return x * cos + partner * sin


def _rope_wide(x, cos, sin):
    return [_rope_block(x[:, c * LANES:(c + 1) * LANES], cos, sin) for c in range(x.shape[1] // LANES)]


def _rope_body(cos_ref, sin_ref, dq_ref, fq_ref, fk_ref, fv_ref, iq_ref, dk_ref, dv_ref, misc_ref,
               dq_o, fq_o, fk_o, fkb_o, fvb_o, iq3_o, dk_o, dkb_o, dvb_o, ik_o, ik3_o):
    cos = cos_ref[...]
    sin = sin_ref[...]
    lane = lax.broadcasted_iota(I32, cos.shape, 1)
    low = lane < IDX_HD
    for c, blk in enumerate(_rope_wide(dq_ref[...], cos, sin)):
        dq_o[:, c * LANES:(c + 1) * LANES] = (blk * DSA_HD ** -0.5).astype(BF16)
    for c, blk in enumerate(_rope_wide(fq_ref[...], cos, sin)):
        fq_o[:, c * LANES:(c + 1) * LANES] = (blk * DIFF_HD ** -0.5).astype(BF16)
    for c, blk in enumerate(_rope_wide(fk_ref[...], cos, sin)):
        fk_o[:, c * LANES:(c + 1) * LANES] = blk
        fkb_o[:, c * LANES:(c + 1) * LANES] = blk.astype(BF16)
    fvb_o[...] = fv_ref[...].astype(BF16)
    dk = _rope_block(dk_ref[...], cos, sin)
    dk_o[...] = dk
    dkb_o[...] = dk.astype(BF16)
    dv = dv_ref[...]
    dvb_o[:, 0:LANES] = jnp.where(low, dv, 1.0).astype(BF16)
    dvb_o[:, LANES:2 * LANES] = jnp.where(low, pltpu.roll(dv, DSA_HD, 1), 1.0).astype(BF16)
    for c, blk in enumerate(_rope_wide(iq_ref[...], cos, sin)):
        hi = blk.astype(BF16).astype(F32)
        lo = blk - hi
        hi_sw = pltpu.roll(hi, IDX_HD, 1)
        lo_sw = pltpu.roll(lo, IDX_HD, 1)
        base = 2 * c * 2 * LANES
        iq3_o[:, base:base + LANES] = jnp.where(low, hi, hi_sw).astype(BF16)
        iq3_o[:, base + LANES:base + 2 * LANES] = jnp.where(low, lo, 0.0).astype(BF16)
        iq3_o[:, base + 2 * LANES:base + 3 * LANES] = jnp.where(low, hi_sw, hi).astype(BF16)
        iq3_o[:, base + 3 * LANES:base + 4 * LANES] = jnp.where(low, lo_sw, 0.0).astype(BF16)
    ik = _rope_block(misc_ref[...], cos, sin)
    ik_o[...] = ik[:, M_IDXK:M_IDXK + IDX_HD]
    hi = ik.astype(BF16).astype(F32)
    lo = ik - hi
    ik3_o[:, 0:LANES] = jnp.where(low, hi, pltpu.roll(lo, IDX_HD, 1)).astype(BF16)
    ik3_o[:, LANES:2 * LANES] = jnp.where(low, hi, 0.0).astype(BF16)


def _rope_call(h, cos, sin):
    n = h.shape[0]
    tm = _tile(min(n, cos.shape[0]), 512)
    nt = cos.shape[0] // tm

    def col(width, off):
        return pl.BlockSpec((tm, width), lambda i: (i, off // width))

    def out(width):
        return pl.BlockSpec((tm, width), lambda i: (i, 0))

    sds = jax.ShapeDtypeStruct
    tab = pl.BlockSpec((tm, LANES), lambda i: (i % nt, 0))
    return pl.pallas_call(
        _rope_body,
        grid=(n // tm,),
        in_specs=[tab, tab, col(512, C_DSAQ), col(512, C_DIFQ), col(512, C_DIFK), col(512, C_DIFV),
                  col(256, C_IDXQ), col(128, C_DSAK), col(128, C_DSAV), col(128, C_MISC)],
        out_specs=[out(512), out(512), out(512), out(512), out(512), out(1024), out(128), out(128), out(256),
                   out(IDX_HD), out(256)],
        out_shape=[sds((n, 512), BF16), sds((n, 512), BF16), sds((n, 512), F32), sds((n, 512), BF16),
                   sds((n, 512), BF16), sds((n, 1024), BF16), sds((n, 128), F32), sds((n, 128), BF16),
                   sds((n, 256), BF16), sds((n, IDX_HD), F32), sds((n, 256), BF16)],
        compiler_params=_cparams("parallel"),
        name="rope",
    )(cos, sin, h, h, h, h, h, h, h, h)


def _conv_tile(x_ref, buf_ref, cw_ref, xcat, t, tc):
    @pl.when(t == 0)
    def _():
        xcat[0:SUBLANES, :] = buf_ref[...]

    @pl.when(t > 0)
    def _():
        xcat[0:SUBLANES, :] = xcat[tc:tc + SUBLANES, :]

    xcat[SUBLANES:SUBLANES + tc, :] = x_ref[...]
    cw = cw_ref[...]
    y = x_ref[...] * cw[CONV_W - 1:CONV_W, :]
    for j in range(CONV_W - 1):
        off = SUBLANES - (CONV_W - 1) + j
        y = y + xcat[off:off + tc, :] * cw[j:j + 1, :]
    return y


def _lru_body(x_ref, gate_ref, cw_ref, cb_ref, wa_ref, ba_ref, wx_ref, bx_ref, lam_ref, buf_ref, h0_ref,
              o_ref, hl_ref, xcat, hc, *, tc, reset_first):
    t = pl.program_id(1)

    @pl.when(t == 0)
    def _():
        hc[...] = h0_ref[...]

    xa = _conv_tile(x_ref, buf_ref, cw_ref, xcat, t, tc) + cb_ref[...]
    xb = xa.astype(BF16)
    r = _sigmoid(_dot(xb, wa_ref[...]) + ba_ref[...])
    gi = _sigmoid(_dot(xb, wx_ref[...]) + bx_ref[...])
    log_a = -LRU_C * r * _softplus(-lam_ref[...])
    a = jnp.exp(log_a)
    mult = jnp.sqrt(-jnp.tanh(log_a) * (a * a + 1.0))
    row = lax.broadcasted_iota(I32, a.shape, 0)
    if reset_first:
        is0 = jnp.logical_and(row == 0, t == 0)
        a = jnp.where(is0, 0.0, a)
        mult = jnp.where(is0, 1.0, mult)
    u = mult * gi * xa
    d = 1
    while d < tc:
        keep = row >= d
        a_sh = jnp.where(keep, pltpu.roll(a, d, 0), 1.0)
        u_sh = jnp.where(keep, pltpu.roll(u, d, 0), 0.0)
        u = a * u_sh + u
        a = a * a_sh
        d *= 2
    h = a * hc[...] + u
    h_last = h[tc - 1:tc, :]
    hc[...] = h_last
    hl_ref[...] = h_last
    o_ref[...] = (h * jax.nn.gelu(gate_ref[...])).astype(o_ref.dtype)


def _lru_call(h3, wl, buf, h0, reset_first):
    b, t, _ = h3.shape
    tc = _tile(t, 256)
    w = MIX_WIDTH
    row = pl.BlockSpec((1, w), lambda i, j: (0, 0))
    sq = pl.BlockSpec((w, w), lambda i, j: (0, 0))
    return pl.pallas_call(
        functools.partial(_lru_body, tc=tc, reset_first=reset_first),
        grid=(b, t // tc),
        in_specs=[pl.BlockSpec((None, tc, w), lambda i, j: (i, j, C_LRUX // w)),
                  pl.BlockSpec((None, tc, w), lambda i, j: (i, j, C_LRUG // w)),
                  pl.BlockSpec((SUBLANES, w), lambda i, j: (0, 0)), row, sq, row, sq, row, row,
                  pl.BlockSpec((None, SUBLANES, w), lambda i, j: (i, 0, 0)),
                  pl.BlockSpec((None, 1, w), lambda i, j: (i, 0, 0))],
        out_specs=[pl.BlockSpec((None, tc, w), lambda i, j: (i, j, 0)),
                   pl.BlockSpec((None, 1, w), lambda i, j: (i, 0, 0))],
        out_shape=[jax.ShapeDtypeStruct((b, t, w), BF16), jax.ShapeDtypeStruct((b, 1, w), F32)],
        scratch_shapes=[pltpu.VMEM((tc + SUBLANES, w), F32), pltpu.VMEM((1, w), F32)],
        compiler_params=_cparams("parallel", "arbitrary"),
        name="rg_lru",
    )(h3, h3, wl["lru_cw"], wl["lru_cb"], wl["lru_wa"], wl["lru_ba"], wl["lru_wx"], wl["lru_bx"], wl["lru_lam"],
      buf, h0)


def _gdn_body(qkv_ref, z_ref, misc_ref, cw_ref, alog_ref, dtb_ref, ng_ref, buf_ref, s0_ref,
              o_ref, so_ref, xcat, y_scr, state, *, tc, ck):
    t = pl.program_id(1)
    hd = GDN_HD

    @pl.when(t == 0)
    def _():
        state[...] = s0_ref[...]

    y = _conv_tile(qkv_ref, buf_ref, cw_ref, xcat, t, tc)
    y_scr[...] = _silu(y)

    ri = lax.broadcasted_iota(I32, (ck, ck), 0)
    ci = lax.broadcasted_iota(I32, (ck, ck), 1)
    tril = ri >= ci
    strict = ri > ci
    ltri = jnp.where(tril, 1.0, 0.0).astype(F32)
    ones = jnp.ones((ck, ck), F32)
    n_sq = max(int(math.log2(ck)) - 1, 0)

    def chunk(c, carry):
        r0 = pl.multiple_of(c * ck, ck)
        heads = range(GDN_HEADS)
        misc = misc_ref[pl.ds(r0, ck), :]
        beta_all = _sigmoid(misc)
        g_all = -jnp.exp(alog_ref[...]) * _softplus(misc + dtb_ref[...])
        cols = lambda blk: slice(blk * hd, (blk + 1) * hd)
        q = [y_scr[pl.ds(r0, ck), cols(hh)] for hh in heads]
        k = [y_scr[pl.ds(r0, ck), cols(GDN_HEADS + hh)] for hh in heads]
        v = [y_scr[pl.ds(r0, ck), cols(2 * GDN_HEADS + hh)] for hh in heads]
        q = [x * lax.rsqrt(jnp.sum(x * x, axis=-1, keepdims=True) + EPS) * (hd ** -0.5) for x in q]
        k = [x * lax.rsqrt(jnp.sum(x * x, axis=-1, keepdims=True) + EPS) for x in k]
        beta = [jnp.broadcast_to(beta_all[:, M_GDNB + hh:M_GDNB + hh + 1], (ck, hd)) for hh in heads]
        g128 = [jnp.broadcast_to(g_all[:, M_GDNA + hh:M_GDNA + hh + 1], (ck, hd)) for hh in heads]
        gc128 = [_dot_exact_lhs(ltri, g) for g in g128]
        gc_row = [_dot_exact_lhs(ones, jnp.where(ri <= ci, g[:, :ck], 0.0)) for g in g128]
        decay = [jnp.where(tril, jnp.exp(jnp.where(tril, gc128[hh][:, :ck] - gc_row[hh], 0.0)), 0.0) for hh in heads]
        kb = [k[hh] * beta[hh] for hh in heads]
        egc = [jnp.exp(g) for g in gc128]
        low = [jnp.where(strict, _dot3(kb[hh], k[hh], _NT) * decay[hh], 0.0) for hh in heads]
        intra = [jnp.where(tril, _dot3(q[hh], k[hh], _NT) * decay[hh], 0.0) for hh in heads]
        mp = [-x for x in low]
        x = [jnp.concatenate([v[hh] * beta[hh], kb[hh] * egc[hh]], axis=-1) for hh in heads]
        x = [x[hh] + _dot3(mp[hh], x[hh]) for hh in heads]
        for _ in range(n_sq):
            mp = [_dot3(m, m) for m in mp]
            x = [x[hh] + _dot3(mp[hh], x[hh]) for hh in heads]
        s = [state[hh] for hh in heads]
        v_new = [x[hh][:, :hd] - _dot3(x[hh][:, hd:], s[hh]) for hh in heads]
        o = [_dot3(q[hh] * egc[hh], s[hh]) + _dot3(intra[hh], v_new[hh]) for hh in heads]
        g_last = [g[ck - 1:ck, :] for g in gc128]
        kd = [k[hh] * jnp.exp(g_last[hh] - gc128[hh]) for hh in heads]
        s_new = [s[hh] * jnp.exp(g_last[hh]) + _dot3(kd[hh], v_new[hh], _TN) for hh in heads]
        for hh in heads:
            zz = z_ref[pl.ds(r0, ck), cols(hh)]
            state[hh] = s_new[hh]
            o_ref[pl.ds(r0, ck), cols(hh)] = (_rms_rows(o[hh], ng_ref[...]) * _silu(zz)).astype(o_ref.dtype)
        return carry

    lax.fori_loop(0, tc // ck, chunk, 0)
    so_ref[...] = state[...]


def _gdn_call(h3, wl, buf, s0):
    b, t, _ = h3.shape
    ck = min(GDN_CHUNK, t)
    assert t % ck == 0
    tc = _tile(t, 256)
    w = GDN_QKV
    row = pl.BlockSpec((1, LANES), lambda i, j: (0, 0))
    st = pl.BlockSpec((None, GDN_HEADS, GDN_HD, GDN_HD), lambda i, j: (i, 0, 0, 0))
    return pl.pallas_call(
        functools.partial(_gdn_body, tc=tc, ck=ck),
        grid=(b, t // tc),
        in_specs=[pl.BlockSpec((None, tc, w), lambda i, j: (i, j, C_GDNQKV // w)),
                  pl.BlockSpec((None, tc, MIX_WIDTH), lambda i, j: (i, j, C_GDNZ // MIX_WIDTH)),
                  pl.BlockSpec((None, tc, LANES), lambda i, j: (i, j, C_MISC // LANES)),
                  pl.BlockSpec((SUBLANES, w), lambda i, j: (0, 0)), row, row, row,
                  pl.BlockSpec((None, SUBLANES, w), lambda i, j: (i, 0, 0)), st],
        out_specs=[pl.BlockSpec((None, tc, MIX_WIDTH), lambda i, j: (i, j, 0)), st],
        out_shape=[jax.ShapeDtypeStruct((b, t, MIX_WIDTH), BF16),
                   jax.ShapeDtypeStruct((b, GDN_HEADS, GDN_HD, GDN_HD), F32)],
        scratch_shapes=[pltpu.VMEM((tc + SUBLANES, w), F32), pltpu.VMEM((tc, w), F32),
                        pltpu.VMEM((GDN_HEADS, GDN_HD, GDN_HD), F32)],
        compiler_params=_cparams("parallel", "arbitrary"),
        name="gated_deltanet",
    )(h3, h3, h3, wl["gdn_cw"], wl["gdn_alog"], wl["gdn_dtb"], wl["gdn_ng"], buf, s0)


def _lane_tile(x, n):
    return x if n == LANES else jnp.concatenate([x] * (n // LANES), axis=1)


def _online_update(s, v_ext, m_ref, acc_ref, idx):
    m_old = m_ref[idx]
    m_new = jnp.maximum(m_old, jnp.max(s, axis=-1, keepdims=True))
    alpha = jnp.exp(m_old - m_new)
    p = jnp.exp(s - _lane_tile(m_new, s.shape[1]))
    acc = acc_ref[idx]
    acc_ref[idx] = _lane_tile(alpha, acc.shape[1]) * acc + _dot(p.astype(BF16), v_ext)
    m_ref[idx] = m_new


def _online_update_blocks(s, vs, m_ref, l_ref, acc_ref, idx):
    m_old = m_ref[idx]
    m_new = jnp.maximum(m_old, jnp.max(s, axis=-1, keepdims=True))
    alpha = jnp.exp(m_old - m_new)
    p = jnp.exp(s - m_new)
    l_ref[idx] = alpha * l_ref[idx] + jnp.sum(p, axis=-1, keepdims=True)
    pb = p.astype(BF16)
    pv, off = None, 0
    for vblk in vs:
        part = _dot(pb[:, off:off + vblk.shape[0]], vblk)
        pv = part if pv is None else pv + part
        off += vblk.shape[0]
    acc_ref[idx] = alpha * acc_ref[idx] + pv
    m_ref[idx] = m_new


def _reset_softmax(m_ref, l_ref, acc_ref):
    m_ref[...] = jnp.full(m_ref.shape, NEG_BIG, F32)
    l_ref[...] = jnp.zeros(l_ref.shape, F32)
    acc_ref[...] = jnp.zeros(acc_ref.shape, F32)


def _diff_lambda(lam_ref, lam_init):
    lm = lam_ref[...]
    e1 = jnp.exp(jnp.sum(lm[0:1, :] * lm[1:2, :], axis=-1, keepdims=True))
    e2 = jnp.exp(jnp.sum(lm[2:3, :] * lm[3:4, :], axis=-1, keepdims=True))
    return e1 - e2 + lam_init


def _diff_prompt_body(qi_ref, kj_ref, lam_ref, sg_ref, q_ref, k_ref, v_ref, o_ref, m_scr, acc_scr, *, tq, lam_init):
    step = pl.program_id(2)
    qi = qi_ref[step]
    kj = kj_ref[step]
    hv = 2 * DIFF_HD

    @pl.when(kj == 0)
    def _():
        m_scr[...] = jnp.full(m_scr.shape, NEG_BIG, F32)
        acc_scr[...] = jnp.zeros(acc_scr.shape, F32)

    def tile(masked):
        v = v_ref[...]
        v_ext = jnp.concatenate([v, jnp.ones_like(v)], axis=1)
        s = [_dot_nt(q_ref[:, mm * DIFF_HD:(mm + 1) * DIFF_HD], k_ref[:, mm * DIFF_HD:(mm + 1) * DIFF_HD])
             for mm in range(2)]
        for mm in range(2):
            sm = s[mm]
            if masked:
                row = lax.broadcasted_iota(I32, (tq, tq), 0)
                col = lax.broadcasted_iota(I32, (tq, tq), 1)
                sm = jnp.where(col <= row, sm, NEG_BIG)
            _online_update(sm, v_ext, m_scr, acc_scr, mm)

    @pl.when(kj < qi)
    def _():
        tile(False)

    @pl.when(kj == qi)
    def _():
        tile(True)
        lam = _diff_lambda(lam_ref, lam_init)
        o = acc_scr[0, :, 0:hv] / acc_scr[0, :, hv:2 * hv] - lam * (acc_scr[1, :, 0:hv] / acc_scr[1, :, hv:2 * hv])
        o_ref[...] = (_rms_rows(o, sg_ref[...]) * (1.0 - lam_init)).astype(o_ref.dtype)


def _diff_prompt_call(fq3, fkb3, fvb3, wl, lam_init):
    b, t, _ = fq3.shape
    tq = _tile(t, 512)
    nq = t // tq
    hv = 2 * DIFF_HD
    pairs = [(q, k) for q in range(nq) for k in range(q + 1)]
    qi_of = jnp.asarray([p[0] for p in pairs], I32)
    kj_of = jnp.asarray([p[1] for p in pairs], I32)
    grid_spec = pltpu.PrefetchScalarGridSpec(
        num_scalar_prefetch=2,
        grid=(b, DIFF_HEADS, len(pairs)),
        in_specs=[pl.BlockSpec((4, DIFF_HD), lambda i, h, s, qi, kj: (0, 0)),
                  pl.BlockSpec((1, hv), lambda i, h, s, qi, kj: (0, 0)),
                  pl.BlockSpec((None, tq, hv), lambda i, h, s, qi, kj: (i, qi[s], h)),
                  pl.BlockSpec((None, tq, hv), lambda i, h, s, qi, kj: (i, kj[s], h)),
                  pl.BlockSpec((None, tq, hv), lambda i, h, s, qi, kj: (i, kj[s], h))],
        out_specs=pl.BlockSpec((None, tq, hv), lambda i, h, s, qi, kj: (i, qi[s], h)),
        scratch_shapes=[pltpu.VMEM((2, tq, LANES), F32), pltpu.VMEM((2, tq, 2 * hv), F32)],
    )
    return pl.pallas_call(
        functools.partial(_diff_prompt_body, tq=tq, lam_init=lam_init),
        grid_spec=grid_spec,
        out_shape=jax.ShapeDtypeStruct((b, t, MIX_WIDTH), BF16),
        compiler_params=_cparams("parallel", "parallel", "arbitrary"),
        name="diff_attn_prompt",
    )(qi_of, kj_of, wl["diff_lam"], wl["diff_sg"], fq3, fkb3, fvb3)


def _diff_sample_body(pt_ref, lam_ref, sg_ref, q_ref, kn_ref, vn_ref, *rest, tq, npg, lam_init):
    kp = rest[:npg]
    vp = rest[npg:2 * npg]
    o_ref, qrows, m_scr, l_scr, acc_scr = rest[2 * npg:]
    s_id = pl.program_id(1)
    n_steps = pl.num_programs(1)
    nrow = 2 * DIFF_HEADS * tq
    hv = 2 * DIFF_HD
    prow = PAGE_SIZE * DIFF_HEADS

    @pl.when(s_id == 0)
    def _():
        _reset_softmax(m_scr, l_scr, acc_scr)
        qrows[...] = jnp.zeros(qrows.shape, F32)
        q = q_ref[...].astype(F32)
        for j in range(2 * DIFF_HEADS):
            mm = j % 2
            qrows[j * tq:(j + 1) * tq, mm * DIFF_HD:(mm + 1) * DIFF_HD] = q[:, j * DIFF_HD:(j + 1) * DIFF_HD]

    qb = qrows[...].astype(BF16)
    row_head = lax.broadcasted_iota(I32, (nrow, prow), 0) // (2 * tq)

    @pl.when(s_id < n_steps - 1)
    def _():
        key_head = lax.broadcasted_iota(I32, (nrow, prow), 1) % DIFF_HEADS
        own = key_head == row_head
        scores = [jnp.where(own, _dot_nt(qb, kp[p][...].astype(BF16)), NEG_BIG) for p in range(npg)]
        _online_update_blocks(jnp.concatenate(scores, axis=-1), [vp[p][...].astype(BF16) for p in range(npg)],
                       m_scr, l_scr, acc_scr, Ellipsis)

    @pl.when(s_id == n_steps - 1)
    def _():
        kn = jnp.concatenate([kn_ref[:, hh * hv:(hh + 1) * hv] for hh in range(DIFF_HEADS)], axis=0).astype(BF16)
        vn = jnp.concatenate([vn_ref[:, hh * hv:(hh + 1) * hv] for hh in range(DIFF_HEADS)], axis=0).astype(BF16)
        nk = DIFF_HEADS * tq
        rr = lax.broadcasted_iota(I32, (nrow, nk), 0)
        cc = lax.broadcasted_iota(I32, (nrow, nk), 1)
        ok = jnp.logical_and(cc // tq == rr // (2 * tq), cc % tq <= rr % tq)
        s = jnp.where(ok, _dot_nt(qb, kn), NEG_BIG)
        _online_update_blocks(s, [vn], m_scr, l_scr, acc_scr, Ellipsis)
        lam = _diff_lambda(lam_ref, lam_init)
        acc = acc_scr[...] / l_scr[...]
        for hh in range(DIFF_HEADS):
            r1 = (2 * hh) * tq
            r2 = (2 * hh + 1) * tq
            o = acc[r1:r1 + tq, :] - lam * acc[r2:r2 + tq, :]
            o_ref[:, hh * hv:(hh + 1) * hv] = (_rms_rows(o, sg_ref[...]) * (1.0 - lam_init)).astype(o_ref.dtype)


def _diff_sample_call(fq3, fk3, h3, pool_k, pool_v, page_table, layer, wl, lam_init):
    db, t, _ = fq3.shape
    n_pages = page_table.shape[1]
    npg = _pages_per_step(n_pages)
    n_steps = n_pages // npg + 1
    w = MIX_WIDTH
    hv = 2 * DIFF_HD
    nrow = 2 * DIFF_HEADS * t
    prow = PAGE_SIZE * DIFF_HEADS

    def page_spec(p):
        def imap(i, s, pt):
            return (layer, pt[i, jnp.minimum(s, n_steps - 2) * npg + p], 0, 0)
        return pl.BlockSpec((None, None, prow, hv), imap)

    grid_spec = pltpu.PrefetchScalarGridSpec(
        num_scalar_prefetch=1,
        grid=(db, n_steps),
        in_specs=[pl.BlockSpec((4, DIFF_HD), lambda i, s, pt: (0, 0)),
                  pl.BlockSpec((1, hv), lambda i, s, pt: (0, 0)),
                  pl.BlockSpec((None, t, w), lambda i, s, pt: (i, 0, 0)),
                  pl.BlockSpec((None, t, w), lambda i, s, pt: (i, 0, 0)),
                  pl.BlockSpec((None, t, w), lambda i, s, pt: (i, 0, C_DIFV // w))]
                 + [page_spec(p) for p in range(npg)] + [page_spec(p) for p in range(npg)],
        out_specs=pl.BlockSpec((None, t, w), lambda i, s, pt: (i, 0, 0)),
        scratch_shapes=[pltpu.VMEM((nrow, hv), F32), pltpu.VMEM((nrow, 1), F32), pltpu.VMEM((nrow, 1), F32),
                        pltpu.VMEM((nrow, hv), F32)],
    )
    return pl.pallas_call(
        functools.partial(_diff_sample_body, tq=t, npg=npg, lam_init=lam_init),
        grid_spec=grid_spec,
        out_shape=jax.ShapeDtypeStruct((db, t, w), BF16),
        compiler_params=_cparams("parallel", "arbitrary"),
        name="diff_attn_sample",
    )(page_table, wl["diff_lam"], wl["diff_sg"], fq3, fk3, h3, *([pool_k] * npg), *([pool_v] * npg))


def _sortable(score):
    bits = pltpu.bitcast(jnp.where(score == 0.0, 0.0, score), I32)
    return bits ^ ((bits >> 31) & 0x7FFFFFFF)


def _search_threshold(count_ge, rows, n_sel):
    def bit_body(i, t_u):
        cand_u = t_u | (jnp.int32(1) << (31 - i))
        cnt = count_ge(cand_u ^ INT_MIN)
        return jnp.where(cnt >= n_sel, cand_u, t_u)

    t_u = lax.fori_loop(0, 32, bit_body, jnp.zeros((rows, LANES), I32))
    return t_u ^ INT_MIN


def _lane_fold(x):
    acc = x[:, 0:LANES]
    for c in range(1, x.shape[1] // LANES):
        acc = acc + x[:, c * LANES:(c + 1) * LANES]
    return acc


def _strictly_before(n):
    ui = lax.broadcasted_iota(I32, (n, n), 0)
    uj = lax.broadcasted_iota(I32, (n, n), 1)
    return jnp.where(ui < uj, 1.0, 0.0).astype(BF16)


def _dsa_prompt_body(iq3_ref, misc_ref, ik3_ref, q_ref, k_ref, v_ref, o_ref, iqs, qs, keys_scr, half_scr, bias_scr,
                     m_scr, acc_scr, w_scr, spread_scr, c16_scr, cnt_scr, cnt16_scr, *, tq, ck, n_sel):
    qi = pl.program_id(1)
    nck = (qi * tq + tq + ck - 1) // ck
    kw = iqs.shape[1]
    for hh in range(IDX_HEADS):
        iqs[hh * tq:(hh + 1) * tq, :] = iq3_ref[:, hh * kw:(hh + 1) * kw]
    wts = misc_ref[...][:, M_IDXW:M_IDXW + IDX_HEADS] * (IDX_HD ** -0.5 * IDX_HEADS ** -0.5)
    for hh in range(IDX_HEADS):
        w_scr[hh] = jnp.broadcast_to(wts[:, hh:hh + 1], (tq, LANES))
    row_pos = qi * tq + lax.broadcasted_iota(I32, (tq, ck), 0)
    lane_pos = lax.broadcasted_iota(I32, (tq, ck), 1)

    def score_chunk(c, carry):
        k0 = pl.multiple_of(c * ck, ck)
        s = jnp.maximum(_dot_nt(iqs[...], ik3_ref[pl.ds(k0, ck), :]), 0.0)
        sc = _lane_tile(w_scr[0], ck) * s[0:tq, :]
        for hh in range(1, IDX_HEADS):
            sc = sc + _lane_tile(w_scr[hh], ck) * s[hh * tq:(hh + 1) * tq, :]
        keys_scr[c] = jnp.where(c * ck + lane_pos <= row_pos, _sortable(sc), INT_MIN)
        return carry

    lax.fori_loop(0, nck, score_chunk, 0)

    def parked(slot):
        return _lane_tile(spread_scr[slot], ck)

    def count(pred):
        cnt_scr[...] = jnp.zeros((tq, LANES), I32)

        def body(c, carry):
            cnt_scr[...] += _lane_fold(jnp.where(pred(keys_scr[c]), 1, 0).astype(I32))
            return carry
        lax.fori_loop(0, nck, body, 0)
        return jnp.sum(cnt_scr[...], axis=-1, keepdims=True)

    ones_sq = jnp.ones((LANES, LANES), BF16)

    def search_half(half_scr):
        def bit_body(i, t_u):
            cand_u = t_u | (jnp.int32(1) << (15 - i))
            c16_scr[...] = (cand_u - HALF_BIAS).astype(I16)
            cnt16_scr[...] = jnp.zeros((tq, LANES), I16)

            def body(c, carry):
                hit = half_scr[c] >= _lane_tile(c16_scr[...], ck)
                cnt16_scr[...] += _lane_fold(jnp.where(hit, jnp.int16(1), jnp.int16(0)))
                return carry
            lax.fori_loop(0, nck, body, 0)
            tot = _dot(cnt16_scr[...].astype(F32).astype(BF16), ones_sq)
            return jnp.where(tot >= n_sel, cand_u, t_u)
        return lax.fori_loop(0, 16, bit_body, jnp.zeros((tq, LANES), I32)) - HALF_BIAS

    def split_hi(c, carry):
        half_scr[c] = (keys_scr[c] >> 16).astype(I16)
        return carry
    lax.fori_loop(0, nck, split_hi, 0)
    t_hi = search_half(half_scr)

    spread_scr[0] = t_hi

    def split_lo(c, carry):
        kk = keys_scr[c]
        hi = kk >> 16
        lo = (kk & 0xFFFF) - HALF_BIAS
        t_hi_b = parked(0)
        half_scr[c] = jnp.where(hi == t_hi_b, lo, jnp.where(hi > t_hi_b, HALF_BIAS - 1, -HALF_BIAS)).astype(I16)
        return carry
    lax.fori_loop(0, nck, split_lo, 0)
    t_lo = search_half(half_scr)
    thr_rep = (t_hi << 16) | ((t_lo + HALF_BIAS) & 0xFFFF)
    spread_scr[1] = thr_rep
    thr = thr_rep[:, 0:1]
    n_ge = count(lambda kk: kk >= parked(1))
    plain = jnp.max(jnp.where(jnp.logical_and(n_ge == n_sel, thr != INT_MIN), 0, 1)) == 0

    @pl.when(plain)
    def _():
        def body(c, carry):
            bias_scr[c] = jnp.where(keys_scr[c] >= parked(1), 0.0, NEG_BIG)
            return carry
        lax.fori_loop(0, nck, body, 0)

    @pl.when(jnp.logical_not(plain))
    def _():
        need = (n_sel - count(lambda kk: kk > parked(1))).astype(F32)
        before = _strictly_before(ck)

        def body(c, eq_seen):
            kk = keys_scr[c]
            thr_b = parked(1)
            eq = jnp.logical_and(kk == thr_b, c * ck + lane_pos <= row_pos)
            eqf = jnp.where(eq, 1.0, 0.0)
            rank = _dot(eqf.astype(BF16), before) + eq_seen
            sel = jnp.logical_or(kk > thr_b, jnp.logical_and(eq, rank < need))
            bias_scr[c] = jnp.where(sel, 0.0, NEG_BIG)
            return eq_seen + jnp.sum(eqf, axis=-1, keepdims=True)

        lax.fori_loop(0, nck, body, jnp.zeros((tq, 1), F32))

    m_scr[...] = jnp.full(m_scr.shape, NEG_BIG, F32)
    acc_scr[...] = jnp.zeros(acc_scr.shape, F32)
    hpg = DSA_HEADS // DSA_KV_HEADS
    for hd in range(DSA_HEADS):
        qs[hd * tq:(hd + 1) * tq, :] = q_ref[:, hd * DSA_HD:(hd + 1) * DSA_HD]

    def attend_chunk(c, carry):
        k0 = pl.multiple_of(c * ck, ck)
        bias = bias_scr[c]
        groups = range(DSA_KV_HEADS)
        rows = [pl.ds(g * hpg * tq, hpg * tq) for g in groups]
        s = [_dot_nt(qs[rows[g], :], k_ref[pl.ds(k0, ck), g * DSA_HD:(g + 1) * DSA_HD]) for g in groups]
        for g in groups:
            vg = v_ref[pl.ds(k0, ck), g * LANES:(g + 1) * LANES]
            sg = s[g].reshape(hpg, tq, ck) + bias[None]
            _online_update(sg.reshape(hpg * tq, ck), vg, m_scr, acc_scr, rows[g])
        return carry

    lax.fori_loop(0, nck, attend_chunk, 0)
    acc = acc_scr[...]
    out = acc / pltpu.roll(acc, DSA_HD, 1)
    for hd in range(DSA_HEADS):
        o_ref[:, hd * DSA_HD:(hd + 1) * DSA_HD] = out[hd * tq:(hd + 1) * tq, 0:DSA_HD].astype(o_ref.dtype)


def _dsa_prompt_call(iq33, ik33, dq3, dkb3, dvb3, h3):
    b, t, _ = dq3.shape
    n_sel = min(TOPK_MAX, t // 4)
    tq = _tile(t, 512)
    ck = _tile(t, 512)
    kw = iq33.shape[-1] // IDX_HEADS
    full = lambda width: pl.BlockSpec((None, t, width), lambda i, q: (i, 0, 0))
    return pl.pallas_call(
        functools.partial(_dsa_prompt_body, tq=tq, ck=ck, n_sel=n_sel),
        grid=(b, t // tq),
        in_specs=[pl.BlockSpec((None, tq, IDX_HEADS * kw), lambda i, q: (i, q, 0)),
                  pl.BlockSpec((None, tq, LANES), lambda i, q: (i, q, C_MISC // LANES)),
                  full(kw),
                  pl.BlockSpec((None, tq, MIX_WIDTH), lambda i, q: (i, q, 0)),
                  full(LANES), full(DSA_KV_HEADS * LANES)],
        out_specs=pl.BlockSpec((None, tq, MIX_WIDTH), lambda i, q: (i, q, 0)),
        out_shape=jax.ShapeDtypeStruct((b, t, MIX_WIDTH), BF16),
        scratch_shapes=[pltpu.VMEM((IDX_HEADS * tq, kw), BF16), pltpu.VMEM((DSA_HEADS * tq, DSA_HD), BF16),
                        pltpu.VMEM((t // ck, tq, ck), I32), pltpu.VMEM((t // ck, tq, ck), I16),
                        pltpu.VMEM((t // ck, tq, ck), F32),
                        pltpu.VMEM((DSA_HEADS * tq, LANES), F32), pltpu.VMEM((DSA_HEADS * tq, LANES), F32),
                        pltpu.VMEM((IDX_HEADS, tq, LANES), F32), pltpu.VMEM((2, tq, LANES), I32),
                        pltpu.VMEM((tq, LANES), I16), pltpu.VMEM((tq, LANES), I32), pltpu.VMEM((tq, LANES), I16)],
        compiler_params=pltpu.CompilerParams(dimension_semantics=("parallel", "arbitrary"),
                                             vmem_limit_bytes=VMEM_LIMIT_DSA_PROMPT),
        name="dsa_prompt",
    )(iq33, h3, ik33, dq3, dkb3, dvb3)


def _dsa_select_body(pt_ref, iq3_ref, misc_ref, ikn_ref, *rest, tq, npg, n_sel, n_pages):
    pages = rest[:npg]
    sel_ref, keys_scr, iqs = rest[npg:]
    s_id = pl.program_id(1)
    n_steps = pl.num_programs(1)
    kw = iqs.shape[1]
    wts = misc_ref[...][:, M_IDXW:M_IDXW + IDX_HEADS] * (IDX_HD ** -0.5 * IDX_HEADS ** -0.5)

    @pl.when(s_id == 0)
    def _():
        for hh in range(IDX_HEADS):
            iqs[hh * tq:(hh + 1) * tq, :] = iq3_ref[:, hh * kw:(hh + 1) * kw]

    def scores(ik, feature_major):
        hi, lo = _split_hi_lo(ik)
        a_hi = iqs[:, 0:IDX_HD]
        a_lo = iqs[:, 2 * IDX_HD:3 * IDX_HD]
        mm = _dot if feature_major else _dot_nt
        s = jnp.maximum(mm(a_hi, hi) + mm(a_hi, lo) + mm(a_lo, hi), 0.0)
        sc = wts[:, 0:1] * s[0:tq, :]
        for hh in range(1, IDX_HEADS):
            sc = sc + wts[:, hh:hh + 1] * s[hh * tq:(hh + 1) * tq, :]
        return sc

    @pl.when(s_id < n_steps - 1)
    def _():
        keys = _sortable(scores(jnp.concatenate([pages[p][...] for p in range(npg)], axis=1), True))
        for p in range(npg):
            keys_scr[s_id * npg + p] = keys[:, p * PAGE_SIZE:(p + 1) * PAGE_SIZE]

    @pl.when(s_id == n_steps - 1)
    def _():
        sc = scores(ikn_ref[...], False)
        qrow = lax.broadcasted_iota(I32, (tq, tq), 0)
        kcol = lax.broadcasted_iota(I32, (tq, tq), 1)
        new_keys = jnp.where(kcol <= qrow, _sortable(sc), INT_MIN)
        keys_scr[n_pages] = jnp.concatenate([new_keys, jnp.full((tq, LANES - tq), INT_MIN, I32)], axis=-1)

        ones_sq = jnp.ones((LANES, LANES), BF16)

        def count(pred):
            acc = jnp.sum(jnp.where(pred(keys_scr[...]), 1, 0).astype(I32), axis=0)
            return _dot(acc.astype(F32).astype(BF16), ones_sq)

        thr_rep = _search_threshold(lambda cand: count(lambda kk: kk >= cand[None]), tq, n_sel)
        thr = thr_rep[:, 0:1]
        n_ge = count(lambda kk: kk >= thr_rep[None])[:, 0:1]
        plain = jnp.max(jnp.where(jnp.logical_and(n_ge == n_sel, thr != INT_MIN), 0, 1)) == 0

        @pl.when(plain)
        def _():
            sel_ref[...] = jnp.where(keys_scr[...] >= thr_rep[None], 1.0, 0.0)

        @pl.when(jnp.logical_not(plain))
        def _():
            need = n_sel - count(lambda kk: kk > thr_rep[None])[:, 0:1]
            before = _strictly_before(LANES)
            lane = lax.broadcasted_iota(I32, (tq, LANES), 1)
            qr = lax.broadcasted_iota(I32, (tq, LANES), 0)

            def emit(c, eq_seen):
                kk = keys_scr[c]
                allowed = jnp.logical_or(c < n_pages, lane <= qr)
                eq = jnp.logical_and(kk == thr, allowed)
                eqf = jnp.where(eq, 1.0, 0.0)
                rank = _dot(eqf.astype(BF16), before) + eq_seen
                sel = jnp.logical_or(kk > thr, jnp.logical_and(eq, rank < need))
                sel_ref[c] = jnp.where(sel, 1.0, 0.0)
                return eq_seen + jnp.sum(eqf, axis=-1, keepdims=True)

            lax.fori_loop(0, n_pages + 1, emit, jnp.zeros((tq, 1), F32))


def _dsa_attend_body(pt_ref, q_ref, kn_ref, vn_ref, selp_ref, seln_ref, *rest, tq, npg):
    kp = rest[:npg]
    vp = rest[npg:2 * npg]
    o_ref, qrows, m_scr, l_scr, acc_scr = rest[2 * npg:]
    s_id = pl.program_id(1)
    n_steps = pl.num_programs(1)
    hpg = DSA_HEADS // DSA_KV_HEADS
    grows = hpg * tq

    @pl.when(s_id == 0)
    def _():
        _reset_softmax(m_scr, l_scr, acc_scr)
        for hd in range(DSA_HEADS):
            qrows[hd * tq:(hd + 1) * tq, :] = q_ref[:, hd * DSA_HD:(hd + 1) * DSA_HD].astype(F32)

    qb = qrows[...].astype(BF16)
    qg = [qb[g * grows:(g + 1) * grows, :] for g in range(DSA_KV_HEADS)]

    def update(s, pv_fn):
        m_old = m_scr[...]
        m_new = jnp.maximum(m_old, jnp.max(s, axis=-1, keepdims=True))
        alpha = jnp.exp(m_old - m_new)
        p = jnp.exp(s - m_new)
        l_scr[...] = alpha * l_scr[...] + jnp.sum(p, axis=-1, keepdims=True)
        pb = p.astype(BF16)
        pv = jnp.concatenate([pv_fn(pb[g * grows:(g + 1) * grows, :], g) for g in range(DSA_KV_HEADS)], axis=0)
        acc_scr[...] = alpha * acc_scr[...] + pv
        m_scr[...] = m_new

    @pl.when(s_id < n_steps - 1)
    def _():
        kcat = [jnp.concatenate([kp[p][g] for p in range(npg)], axis=1).astype(BF16) for g in range(DSA_KV_HEADS)]
        vcat = [jnp.concatenate([vp[p][g] for p in range(npg)], axis=1).astype(BF16) for g in range(DSA_KV_HEADS)]
        s = jnp.concatenate([_dot(qg[g], kcat[g]) for g in range(DSA_KV_HEADS)], axis=0)
        sel1 = jnp.concatenate([selp_ref[p] for p in range(npg)], axis=1)
        sel = jnp.concatenate([sel1] * DSA_HEADS, axis=0) > 0.5
        update(jnp.where(sel, s, NEG_BIG), lambda pb, g: _dot_nt(pb, vcat[g]))

    @pl.when(s_id == n_steps - 1)
    def _():
        kn = kn_ref[...].astype(BF16)
        vn = vn_ref[...].astype(BF16)
        s = jnp.concatenate([_dot_nt(qg[g], kn[:, g * DSA_HD:(g + 1) * DSA_HD]) for g in range(DSA_KV_HEADS)], axis=0)
        sel = jnp.concatenate([seln_ref[...][:, 0:tq]] * DSA_HEADS, axis=0) > 0.5
        update(jnp.where(sel, s, NEG_BIG), lambda pb, g: _dot(pb, vn[:, g * DSA_HD:(g + 1) * DSA_HD]))
        acc = acc_scr[...] / l_scr[...]
        for hd in range(DSA_HEADS):
            o_ref[:, hd * DSA_HD:(hd + 1) * DSA_HD] = acc[hd * tq:(hd + 1) * tq, :].astype(o_ref.dtype)


def _dsa_sample_call(iq33, ik3, dq3, dk3, h3, pool_k, pool_v, pool_ik, page_table, layer):
    db, t, _ = dq3.shape
    n_pages = page_table.shape[1]
    n_keys = n_pages * PAGE_SIZE + t
    n_sel = min(TOPK_MAX, n_keys // 4)
    npg = _pages_per_step(n_pages)
    n_steps = n_pages // npg + 1
    kw = iq33.shape[-1] // IDX_HEADS
    sw = LANES

    def page_spec(*tail):
        def make(p):
            def imap(i, s, pt):
                return (layer, pt[i, jnp.minimum(s, n_steps - 2) * npg + p]) + (0,) * len(tail)
            return pl.BlockSpec((None, None) + tail, imap)
        return [make(p) for p in range(npg)]

    sel = pl.pallas_call(
        functools.partial(_dsa_select_body, tq=t, npg=npg, n_sel=n_sel, n_pages=n_pages),
        grid_spec=pltpu.PrefetchScalarGridSpec(
            num_scalar_prefetch=1,
            grid=(db, n_steps),
            in_specs=[pl.BlockSpec((None, t, IDX_HEADS * kw), lambda i, s, pt: (i, 0, 0)),
                      pl.BlockSpec((None, t, LANES), lambda i, s, pt: (i, 0, C_MISC // LANES)),
                      pl.BlockSpec((None, t, IDX_HD), lambda i, s, pt: (i, 0, 0))]
                     + page_spec(IDX_HD, PAGE_SIZE),
            out_specs=pl.BlockSpec((None, n_pages + 1, t, sw), lambda i, s, pt: (i, 0, 0, 0)),
            scratch_shapes=[pltpu.VMEM((n_pages + 1, t, LANES), I32), pltpu.VMEM((IDX_HEADS * t, kw), BF16)],
        ),
        out_shape=jax.ShapeDtypeStruct((db, n_pages + 1, t, sw), F32),
        compiler_params=_cparams("parallel", "arbitrary"),
        name="dsa_sample_select",
    )(page_table, iq33, h3, ik3, *([pool_ik] * npg))

    nrow = DSA_HEADS * t
    return pl.pallas_call(
        functools.partial(_dsa_attend_body, tq=t, npg=npg),
        grid_spec=pltpu.PrefetchScalarGridSpec(
            num_scalar_prefetch=1,
            grid=(db, n_steps),
            in_specs=[pl.BlockSpec((None, t, MIX_WIDTH), lambda i, s, pt: (i, 0, 0)),
                      pl.BlockSpec((None, t, LANES), lambda i, s, pt: (i, 0, 0)),
                      pl.BlockSpec((None, t, LANES), lambda i, s, pt: (i, 0, C_DSAV // LANES)),
                      pl.BlockSpec((None, npg, t, sw), lambda i, s, pt: (i, jnp.minimum(s, n_steps - 2), 0, 0)),
                      pl.BlockSpec((None, None, t, sw), lambda i, s, pt: (i, n_pages, 0, 0))]
                     + page_spec(DSA_KV_HEADS, DSA_HD, PAGE_SIZE) + page_spec(DSA_KV_HEADS, DSA_HD, PAGE_SIZE),
            out_specs=pl.BlockSpec((None, t, MIX_WIDTH), lambda i, s, pt: (i, 0, 0)),
            scratch_shapes=[pltpu.VMEM((nrow, DSA_HD), F32), pltpu.VMEM((nrow, 1), F32), pltpu.VMEM((nrow, 1), F32),
                            pltpu.VMEM((nrow, DSA_HD), F32)],
        ),
        out_shape=jax.ShapeDtypeStruct((db, t, MIX_WIDTH), BF16),
        compiler_params=_cparams("parallel", "arbitrary"),
        name="dsa_sample_attend",
    )(page_table, dq3, dk3, h3, sel, sel, *([pool_k] * npg), *([pool_v] * npg))


def _merge_body(x_ref, gate_ref, oa_ref, ob_ref, oc_ref, od_ref, wb_ref, wo_ref, o_ref):
    acc = None
    for bi, br in enumerate((oa_ref, ob_ref, oc_ref, od_ref)):
        proj = _dot(br[...], wb_ref[bi])
        term = _sigmoid(gate_ref[:, bi * D_MODEL:(bi + 1) * D_MODEL]) * proj
        acc = term if acc is None else acc + term
    o_ref[...] = x_ref[...] + _dot(acc.astype(BF16), wo_ref[...])


def _merge_call(x, h, branches, wl):
    n, d = x.shape
    tm = _tile(n, 256)
    br = pl.BlockSpec((tm, MIX_WIDTH), lambda i: (i, 0))
    return pl.pallas_call(
        _merge_body,
        grid=(n // tm,),
        in_specs=[pl.BlockSpec((tm, d), lambda i: (i, 0)),
                  pl.BlockSpec((tm, N_BRANCH * d), lambda i: (i, C_GATE // (N_BRANCH * d))),
                  br, br, br, br,
                  pl.BlockSpec((N_BRANCH, MIX_WIDTH, d), lambda i: (0, 0, 0)),
                  pl.BlockSpec((d, d), lambda i: (0, 0))],
        out_specs=pl.BlockSpec((tm, d), lambda i: (i, 0)),
        out_shape=jax.ShapeDtypeStruct((n, d), F32),
        compiler_params=_cparams("parallel"),
        name="merge",
    )(x, h, *branches, wl["w_branch"], wl["w_out"])


def _cross_body(x_ref, g_ref, wq_ref, mk_ref, mv_ref, wo_ref, o_ref, *, merged):
    x = x_ref[...]
    tm = x.shape[0]
    xn = _rms_rows(x, g_ref[...]).astype(BF16)
    q = _dot(xn, wq_ref[...]).astype(BF16)
    if merged:
        qrows = jnp.concatenate([q[:, hh * XA_HD:(hh + 1) * XA_HD] for hh in range(XA_HEADS)], axis=0)
        s = _dot_nt(qrows, mk_ref[...].astype(BF16)) * (XA_HD ** -0.5)
        rr = lax.broadcasted_iota(I32, s.shape, 0) // tm
        cc = lax.broadcasted_iota(I32, s.shape, 1) % XA_HEADS
        s = jnp.where(rr == cc, s, NEG_BIG)
        p = jnp.exp(s - jnp.max(s, axis=-1, keepdims=True))
        p = p / jnp.sum(p, axis=-1, keepdims=True)
        orow = _dot(p.astype(BF16), mv_ref[...].astype(BF16))
        outs = [orow[hh * tm:(hh + 1) * tm, :] for hh in range(XA_HEADS)]
    else:
        outs = []
        for hh in range(XA_HEADS):
            sl = slice(hh * XA_HD, (hh + 1) * XA_HD)
            s = _dot_nt(q[:, sl], mk_ref[:, sl].astype(BF16)) * (XA_HD ** -0.5)
            p = jnp.exp(s - jnp.max(s, axis=-1, keepdims=True))
            p = p / jnp.sum(p, axis=-1, keepdims=True)
            outs.append(_dot(p.astype(BF16), mv_ref[:, sl].astype(BF16)))
    o = jnp.concatenate(outs, axis=-1).astype(BF16)
    o_ref[...] = x + _dot(o, wo_ref[...])


def _cross_call(x3, wl, mk4, mv4, layer):
    b, t, d = x3.shape
    w = XA_HEADS * XA_HD
    tm = _tile(t, 512)
    merged = mk4.shape[-1] == XA_HD
    mem = pl.BlockSpec((None, None) + mk4.shape[2:], lambda i, j: (layer, i, 0, 0))
    return pl.pallas_call(
        functools.partial(_cross_body, merged=merged),
        grid=(b, t // tm),
        in_specs=[pl.BlockSpec((None, tm, d), lambda i, j: (i, j, 0)),
                  pl.BlockSpec((1, d), lambda i, j: (0, 0)),
                  pl.BlockSpec((d, w), lambda i, j: (0, 0)), mem, mem,
                  pl.BlockSpec((w, d), lambda i, j: (0, 0))],
        out_specs=pl.BlockSpec((None, tm, d), lambda i, j: (i, j, 0)),
        out_shape=jax.ShapeDtypeStruct((b, t, d), F32),
        compiler_params=_cparams("parallel", "parallel"),
        name="cross_attn",
    )(x3, wl["norm_xa_g"], wl["xa_wq"], mk4, mv4, wl["xa_wo"])


def _ffn_body(x_ref, g_ref, wg_ref, wu_ref, wd_ref, o_ref, xn_scr, acc_scr):
    f = pl.program_id(1)

    @pl.when(f == 0)
    def _():
        xn_scr[...] = _rms_rows(x_ref[...], g_ref[...]).astype(BF16)
        acc_scr[...] = x_ref[...]

    xn = xn_scr[...]
    hm = _silu(_dot(xn, wg_ref[...])) * _dot(xn, wu_ref[...])
    acc_scr[...] += _dot(hm.astype(BF16), wd_ref[...])

    @pl.when(f == pl.num_programs(1) - 1)
    def _():
        o_ref[...] = acc_scr[...]


def _ffn_call(x, wl):
    n, d = x.shape
    ff = wl["ffn_wg"].shape[1]
    tm = _tile(n, 512)
    tf = _tile(ff, 1408)
    return pl.pallas_call(
        _ffn_body,
        grid=(n // tm, ff // tf),
        in_specs=[pl.BlockSpec((tm, d), lambda i, f: (i, 0)), pl.BlockSpec((1, d), lambda i, f: (0, 0)),
                  pl.BlockSpec((d, tf), lambda i, f: (0, f)), pl.BlockSpec((d, tf), lambda i, f: (0, f)),
                  pl.BlockSpec((tf, d), lambda i, f: (f, 0))],
        out_specs=pl.BlockSpec((tm, d), lambda i, f: (i, 0)),
        out_shape=jax.ShapeDtypeStruct((n, d), F32),
        scratch_shapes=[pltpu.VMEM((tm, d), BF16), pltpu.VMEM((tm, d), F32)],
        compiler_params=_cparams("parallel", "arbitrary"),
        name="ffn_swiglu",
    )(x, wl["norm_ffn_g"], wl["ffn_wg"], wl["ffn_wu"], wl["ffn_wd"])


def _moe_body(x_ref, g_ref, wr_ref, br_ref, wg_ref, wu_ref, wd_ref, o_ref, xn_scr, gate_scr, acc_scr):
    e = pl.program_id(1)
    f = pl.program_id(2)
    first = jnp.logical_and(e == 0, f == 0)

    @pl.when(first)
    def _():
        xn = _rms_rows(x_ref[...], g_ref[...])
        xn_scr[...] = xn.astype(BF16)
        acc_scr[...] = x_ref[...]
        logits = _dot_hi(xn, wr_ref[...]) + br_ref[...]
        lane = lax.broadcasted_iota(I32, logits.shape, 1)
        logits = jnp.where(lane < N_EXPERTS, logits, -jnp.inf)
        top1 = jnp.max(logits, axis=-1, keepdims=True)
        idx1 = jnp.min(jnp.where(logits == top1, lane, LANES), axis=-1, keepdims=True)
        rest = jnp.where(lane == idx1, -jnp.inf, logits)
        top2 = jnp.max(rest, axis=-1, keepdims=True)
        idx2 = jnp.min(jnp.where(rest == top2, lane, LANES), axis=-1, keepdims=True)
        e2 = jnp.exp(top2 - top1)
        p1 = 1.0 / (1.0 + e2)
        p2 = e2 / (1.0 + e2)
        gate_scr[...] = jnp.where(lane == idx1, p1, 0.0) + jnp.where(lane == idx2, p2, 0.0)

    gates = gate_scr[...]
    lane = lax.broadcasted_iota(I32, gates.shape, 1)
    ge = jnp.sum(jnp.where(lane == e, gates, 0.0), axis=-1, keepdims=True)
    xn = xn_scr[...]
    hm = _silu(_dot(xn, wg_ref[...])) * _dot(xn, wu_ref[...])
    acc_scr[...] += ge * _dot(hm.astype(BF16), wd_ref[...])

    @pl.when(jnp.logical_and(e == pl.num_programs(1) - 1, f == pl.num_programs(2) - 1))
    def _():
        o_ref[...] = acc_scr[...]


def _moe_call(x, wl):
    n, d = x.shape
    ne, _, fe = wl["moe_wg"].shape
    tm = _tile(n, 512)
    tf = _tile(fe, 1024)
    return pl.pallas_call(
        _moe_body,
        grid=(n // tm, ne, fe // tf),
        in_specs=[pl.BlockSpec((tm, d), lambda i, e, f: (i, 0)), pl.BlockSpec((1, d), lambda i, e, f: (0, 0)),
                  pl.BlockSpec((d, LANES), lambda i, e, f: (0, 0)), pl.BlockSpec((1, LANES), lambda i, e, f: (0, 0)),
                  pl.BlockSpec((None, d, tf), lambda i, e, f: (e, 0, f)),
                  pl.BlockSpec((None, d, tf), lambda i, e, f: (e, 0, f)),
                  pl.BlockSpec((None, tf, d), lambda i, e, f: (e, f, 0))],
        out_specs=pl.BlockSpec((tm, d), lambda i, e, f: (i, 0)),
        out_shape=jax.ShapeDtypeStruct((n, d), F32),
        scratch_shapes=[pltpu.VMEM((tm, d), BF16), pltpu.VMEM((tm, LANES), F32), pltpu.VMEM((tm, d), F32)],
        compiler_params=_cparams("parallel", "arbitrary", "arbitrary"),
        name="moe_swiglu",
    )(x, wl["norm_ffn_g"], wl["moe_wr"], wl["moe_br"], wl["moe_wg"], wl["moe_wu"], wl["moe_wd"])


def _pad_rows(a, rows):
    return jnp.pad(a, ((0, rows - a.shape[0]), (0, 0)))


def _lane_row(vals, lane0):
    return jnp.zeros((1, LANES), F32).at[0, lane0:lane0 + vals.shape[0]].set(vals.astype(F32))


def _block_diag(w):
    nb, n, _ = w.shape
    eye = jnp.eye(nb, dtype=w.dtype)
    return jnp.einsum("aij,ab->aibj", w, eye).reshape(nb * n, nb * n)


def _layer_weights(l, p):
    w_t = jnp.transpose(p["w_in"], (2, 0, 1))[:, l, :]
    d = w_t.shape[1]
    misc = jnp.concatenate([w_t[O_IDXK:O_IDXK + IDX_HD + IDX_HEADS], w_t[O_GDNB:O_GDNB + 2 * GDN_HEADS],
                            jnp.zeros((LANES - IDX_HD - IDX_HEADS - 2 * GDN_HEADS, d), w_t.dtype)], axis=0)
    w_in_r = jnp.concatenate([
        w_t[O_GATE:], w_t[O_LRUX:O_DSAK], w_t[O_DIFQ:O_GATE], w_t[O_GDNZ:O_GDNB],
        w_t[O_GDNQKV:O_GDNZ], w_t[O_IDXQ:O_IDXK], w_t[O_DSAK:O_IDXQ], misc], axis=0)
    assert w_in_r.shape[0] == H_COLS
    wl = dict(
        norm_mix_g=p["norm_mix_g"][l], w_in=w_in_r.astype(BF16),
        lru_cw=_pad_rows(p["lru_conv_w"][l], SUBLANES), lru_cb=p["lru_conv_b"][l].reshape(1, -1),
        lru_wa=_block_diag(p["lru_wa"][l]).astype(BF16), lru_ba=p["lru_ba"][l].reshape(1, -1),
        lru_wx=_block_diag(p["lru_wx"][l]).astype(BF16), lru_bx=p["lru_bx"][l].reshape(1, -1),
        lru_lam=p["lru_lambda"][l].reshape(1, -1),
        gdn_cw=_pad_rows(p["gdn_conv_w"][l], SUBLANES),
        gdn_alog=_lane_row(p["gdn_a_log"][l], M_GDNA), gdn_dtb=_lane_row(p["gdn_dt_bias"][l], M_GDNA),
        gdn_ng=p["gdn_norm_g"][l].reshape(1, -1),
        diff_lam=p["diff_lambda"][l], diff_sg=p["diff_subln_g"][l].reshape(1, -1),
        w_branch=p["w_branch"][l].astype(BF16), w_out=p["w_out"][l].astype(BF16),
        norm_xa_g=p["norm_xa_g"][l].reshape(1, -1), norm_mem_g=p["norm_mem_g"][l],
        xa_wq=p["xa_wq"][l].astype(BF16), xa_wo=p["xa_wo"][l].astype(BF16),
        xa_wkv=jnp.concatenate([p["xa_wk"][l], p["xa_wv"][l]], axis=1).astype(BF16),
        norm_ffn_g=p["norm_ffn_g"][l].reshape(1, -1),
    )
    j = l // 2
    if l % 2 == 0:
        wl.update(ffn_wg=p["ffn_w_gate"][j].astype(BF16), ffn_wu=p["ffn_w_up"][j].astype(BF16),
                  ffn_wd=p["ffn_w_down"][j].astype(BF16))
    else:
        wl.update(moe_wr=jnp.pad(p["moe_router_w"][j], ((0, 0), (0, LANES - N_EXPERTS))),
                  moe_br=jnp.pad(p["moe_router_b"][j], (0, LANES - N_EXPERTS)).reshape(1, LANES),
                  moe_wg=p["moe_w_gate"][j].astype(BF16), moe_wu=p["moe_w_up"][j].astype(BF16),
                  moe_wd=p["moe_w_down"][j].astype(BF16))
    return wl


def _rope_tables(pos):
    half = DSA_HD // 2
    inv = ROPE_THETA ** (-jnp.arange(half, dtype=F32) / half)
    ang = pos.astype(F32)[:, None] * inv[None, :]
    cos, sin = jnp.cos(ang), jnp.sin(ang)
    return jnp.tile(cos, (1, 4)), jnp.tile(jnp.concatenate([-sin, sin], axis=1), (1, 2))


def _front_pad(buf):
    return jnp.pad(buf, ((0, 0), (SUBLANES - (CONV_W - 1), 0), (0, 0)))


def _group_layer(x3, l, wl, tabs, states, mem_kv, paged):
    b, t, d = x3.shape
    n = b * t
    x = x3.reshape(n, d)
    lam_init = 0.8 - 0.6 * math.exp(-0.3 * l)
    xn = _rmsnorm(x, wl["norm_mix_g"], BF16)
    h = _matmul(xn, wl["w_in"], tn=1408, w_rows_are_outputs=True)
    h3 = h.reshape(b, t, H_COLS)
    dq, fq, fk, fkb, fvb, iq3, dk, dkb, dvb, ik, ik3 = _rope_call(h, tabs[0], tabs[1])
    to3 = lambda a: a.reshape(b, t, a.shape[-1])
    dq3, fq3, fk3, dk3, ik3d = map(to3, (dq, fq, fk, dk, ik))

    o_a, lru_h = _lru_call(h3, wl, _front_pad(states["lru_conv"]), states["lru_h"].reshape(b, 1, -1),
                           reset_first=paged is None)
    o_c, gdn_s = _gdn_call(h3, wl, _front_pad(states["gdn_conv"]), states["gdn_s"])
    if paged is None:
        o_b = _dsa_prompt_call(to3(iq3), to3(ik3), dq3, to3(dkb), to3(dvb), h3)
        o_d = _diff_prompt_call(fq3, to3(fkb), to3(fvb), wl, lam_init)
    else:
        o_b = _dsa_sample_call(to3(iq3), ik3d, dq3, dk3, h3, paged["dsa_k"], paged["dsa_v"], paged["idx_k"],
                               paged["page_table"], l)
        o_d = _diff_sample_call(fq3, fk3, h3, paged["diff_k"], paged["diff_v"], paged["page_table"], l, wl, lam_init)

    flat = lambda a: a.reshape(n, a.shape[-1])
    x1 = _merge_call(x, h, [flat(o_a), flat(o_b), flat(o_c), flat(o_d)], wl)
    x2 = _cross_call(x1.reshape(b, t, d), wl, mem_kv[0], mem_kv[1], mem_kv[2])
    x2 = x2.reshape(n, d)
    x3_new = (_ffn_call(x2, wl) if "ffn_wg" in wl else _moe_call(x2, wl)).reshape(b, t, d)

    new = dict(
        dsa_k=dk3.reshape(b, t, DSA_KV_HEADS, DSA_HD),
        dsa_v=h3[:, :, C_DSAV:C_DSAV + DSA_KV_HEADS * DSA_HD].reshape(b, t, DSA_KV_HEADS, DSA_HD),
        idx_k=ik3d,
        diff_k=fk3.reshape(b, t, DIFF_HEADS, 2 * DIFF_HD),
        diff_v=h3[:, :, C_DIFV:C_DIFV + MIX_WIDTH].reshape(b, t, DIFF_HEADS, 2 * DIFF_HD),
        lru_h=lru_h.reshape(b, -1),
        lru_conv=jnp.concatenate([states["lru_conv"], h3[:, :, C_LRUX:C_LRUX + MIX_WIDTH]], axis=1)[:, -(CONV_W - 1):]
        if t < CONV_W - 1 else h3[:, t - (CONV_W - 1):, C_LRUX:C_LRUX + MIX_WIDTH],
        gdn_s=gdn_s,
        gdn_conv=jnp.concatenate([states["gdn_conv"], h3[:, :, C_GDNQKV:C_GDNQKV + GDN_QKV]], axis=1)[:, -(CONV_W - 1):]
        if t < CONV_W - 1 else h3[:, t - (CONV_W - 1):, C_GDNQKV:C_GDNQKV + GDN_QKV],
    )
    return x3_new, new


def kernel(x_prompt, x_sample, mem_prompt, cache_dsa_k, cache_dsa_v, cache_idx_k, cache_diff_k, cache_diff_v,
           cache_mem_k, cache_mem_v, state_lru_h, state_lru_conv, state_gdn_s, state_gdn_conv, page_table,
           norm_mix_g, w_in, lru_conv_w, lru_conv_b, lru_wa, lru_ba, lru_wx, lru_bx, lru_lambda,
           gdn_conv_w, gdn_a_log, gdn_dt_bias, gdn_norm_g, diff_lambda, diff_subln_g, w_branch, w_out,
           norm_xa_g, norm_mem_g, xa_wq, xa_wk, xa_wv, xa_wo, norm_ffn_g, ffn_w_gate, ffn_w_up, ffn_w_down,
           moe_router_w, moe_router_b, moe_w_gate, moe_w_up, moe_w_down, final_norm_g):
    params = dict(
        norm_mix_g=norm_mix_g, w_in=w_in, lru_conv_w=lru_conv_w, lru_conv_b=lru_conv_b, lru_wa=lru_wa, lru_ba=lru_ba,
        lru_wx=lru_wx, lru_bx=lru_bx, lru_lambda=lru_lambda, gdn_conv_w=gdn_conv_w, gdn_a_log=gdn_a_log,
        gdn_dt_bias=gdn_dt_bias, gdn_norm_g=gdn_norm_g, diff_lambda=diff_lambda, diff_subln_g=diff_subln_g,
        w_branch=w_branch, w_out=w_out, norm_xa_g=norm_xa_g, norm_mem_g=norm_mem_g, xa_wq=xa_wq, xa_wk=xa_wk,
        xa_wv=xa_wv, xa_wo=xa_wo, norm_ffn_g=norm_ffn_g, ffn_w_gate=ffn_w_gate, ffn_w_up=ffn_w_up,
        ffn_w_down=ffn_w_down, moe_router_w=moe_router_w, moe_router_b=moe_router_b, moe_w_gate=moe_w_gate,
        moe_w_up=moe_w_up, moe_w_down=moe_w_down)
    b, s, d = x_prompt.shape
    db, t, _ = x_sample.shape
    m = mem_prompt.shape[1]
    n_pool = cache_dsa_k.shape[1]
    past = page_table.shape[1] * PAGE_SIZE

    tabs_p = _rope_tables(jnp.arange(s))
    cs, ss = _rope_tables(past + jnp.arange(t))
    tabs_s = (jnp.tile(cs, (db, 1)), jnp.tile(ss, (db, 1)))
    paged = dict(
        dsa_k=jnp.transpose(cache_dsa_k, (0, 1, 3, 4, 2)), dsa_v=jnp.transpose(cache_dsa_v, (0, 1, 3, 4, 2)),
        idx_k=jnp.transpose(cache_idx_k, (0, 1, 3, 2)), page_table=page_table,
        diff_k=cache_diff_k.reshape(DEPTH, n_pool, PAGE_SIZE * DIFF_HEADS, 2 * DIFF_HD),
        diff_v=cache_diff_v.reshape(DEPTH, n_pool, PAGE_SIZE * DIFF_HEADS, 2 * DIFF_HD))
    mem_k_s = cache_mem_k.reshape(DEPTH, db, m * XA_HEADS, XA_HD)
    mem_v_s = cache_mem_v.reshape(DEPTH, db, m * XA_HEADS, XA_HD)
    zero_states = dict(lru_conv=jnp.zeros((b, CONV_W - 1, MIX_WIDTH), F32), lru_h=jnp.zeros((b, MIX_WIDTH), F32),
                       gdn_conv=jnp.zeros((b, CONV_W - 1, GDN_QKV), F32),
                       gdn_s=jnp.zeros((b, GDN_HEADS, GDN_HD, GDN_HD), F32))

    hp, hs = x_prompt, x_sample
    new_p, new_s, mem_ks, mem_vs = [], [], [], []
    for l in range(DEPTH):
        wl = _layer_weights(l, params)
        mem_n = _rmsnorm(mem_prompt.reshape(b * m, d), wl["norm_mem_g"], BF16)
        kv = _matmul(mem_n, wl["xa_wkv"], tn=512)
        w = XA_HEADS * XA_HD
        mk = kv[:, :w].reshape(1, b, m, w)
        mv = kv[:, w:].reshape(1, b, m, w)
        hp, st_p = _group_layer(hp, l, wl, tabs_p, zero_states, (mk, mv, 0), None)
        new_p.append(st_p)
        mem_ks.append(mk.reshape(b, m, XA_HEADS, XA_HD))
        mem_vs.append(mv.reshape(b, m, XA_HEADS, XA_HD))
        states_s = dict(lru_conv=state_lru_conv[l], lru_h=state_lru_h[l], gdn_conv=state_gdn_conv[l],
                        gdn_s=state_gdn_s[l])
        hs, st_s = _group_layer(hs, l, wl, tabs_s, states_s, (mem_k_s, mem_v_s, l), paged)
        new_s.append(st_s)

    y_prompt = _rmsnorm(hp.reshape(b * s, d), final_norm_g, F32).reshape(b, s, d)
    y_sample = _rmsnorm(hs.reshape(db * t, d), final_norm_g, F32).reshape(db, t, d)
    stack = lambda states, name: jnp.stack([st[name] for st in states], axis=0)
    outs = [y_prompt, y_sample]
    for name in ("dsa_k", "dsa_v", "idx_k", "diff_k", "diff_v", "lru_h", "lru_conv", "gdn_s", "gdn_conv"):
        outs += [stack(new_p, name), stack(new_s, name)]
    outs += [jnp.stack(mem_ks, axis=0), jnp.stack(mem_vs, axis=0)]
    return tuple(outs)
```

```python
import functools
import math

import jax
import jax.numpy as jnp
from jax import lax
from jax.experimental import pallas as pl
from jax.experimental.pallas import tpu as pltpu

F32 = jnp.float32
BF16 = jnp.bfloat16
I32 = jnp.int32
I16 = jnp.int16

D_MODEL = 1024
DEPTH = 2
PAGE_SIZE = 128
MIX_WIDTH = 512
LRU_BLOCKS = 8
LRU_BLOCK = MIX_WIDTH // LRU_BLOCKS
LRU_C = 8.0
CONV_W = 4
DSA_HEADS = 8
DSA_KV_HEADS = 2
DSA_HD = 64
IDX_HEADS = 4
IDX_HD = 64
TOPK_MAX = 256
GDN_HEADS = 4
GDN_HD = 128
GDN_CHUNK = 128
DIFF_HEADS = 4
DIFF_HD = 64
XA_HEADS = 4
XA_HD = 128
N_EXPERTS = 8
ROPE_THETA = 10000.0
EPS = 1e-6
N_BRANCH = 4
GDN_QKV = 3 * GDN_HEADS * GDN_HD

LANES = 128
SUBLANES = 8
NEG_BIG = -1e30
INT_MIN = -2 ** 31
HALF_BIAS = 2 ** 15
VMEM_LIMIT = 48 * 1024 * 1024
VMEM_LIMIT_DSA_PROMPT = 58 * 1024 * 1024

C_GATE = 0
C_LRUX = 4096
C_LRUG = 4608
C_DSAQ = 5120
C_DIFQ = 5632
C_DIFK = 6144
C_DIFV = 6656
C_GDNZ = 7168
C_GDNQKV = 7680
C_IDXQ = 9216
C_DSAK = 9472
C_DSAV = 9600
C_MISC = 9728
H_COLS = 9856
M_IDXK = 0
M_IDXW = 64
M_GDNB = 68
M_GDNA = 72

O_LRUX, O_LRUG, O_DSAQ, O_DSAK, O_DSAV, O_IDXQ, O_IDXK, O_IDXW = 0, 512, 1024, 1536, 1664, 1792, 2048, 2112
O_GDNQKV, O_GDNZ, O_GDNB, O_GDNA, O_DIFQ, O_DIFK, O_DIFV, O_GATE = 2116, 3652, 4164, 4168, 4172, 4684, 5196, 5708


def _cparams(*sem):
    return pltpu.CompilerParams(dimension_semantics=sem, vmem_limit_bytes=VMEM_LIMIT)


def _tile(n, pref):
    return pref if n % pref == 0 else n


def _pages_per_step(n_pages):
    return next(p for p in (16, 8, 1) if n_pages % p == 0)


def _softplus(x):
    return jnp.maximum(x, 0.0) + jnp.log1p(jnp.exp(-jnp.abs(x)))


def _sigmoid(x):
    return 1.0 / (1.0 + jnp.exp(-x))


def _silu(x):
    return x * _sigmoid(x)


def _dot(a, b):
    return jnp.dot(a, b, preferred_element_type=F32)


def _dot_nt(a, b):
    return lax.dot_general(a, b, (((1,), (1,)), ((), ())), preferred_element_type=F32)


def _dot_hi(a, b):
    return jnp.dot(a, b, preferred_element_type=F32, precision=lax.Precision.HIGHEST)


def _split_hi_lo(x):
    hi = x.astype(BF16)
    return hi, (x - hi.astype(F32)).astype(BF16)


_NN = (((1,), (0,)), ((), ()))
_NT = (((1,), (1,)), ((), ()))
_TN = (((0,), (0,)), ((), ()))


def _dot3(a, b, dims=_NN):
    a_hi, a_lo = _split_hi_lo(a)
    b_hi, b_lo = _split_hi_lo(b)
    dg = functools.partial(lax.dot_general, dimension_numbers=dims, preferred_element_type=F32)
    ca, cb = dims[0][0][0], dims[0][1][0]
    if a.shape[ca] % LANES == 0:
        return dg(jnp.concatenate([a_hi, a_hi, a_lo], axis=ca), jnp.concatenate([b_hi, b_lo, b_hi], axis=cb))
    return dg(a_hi, b_hi) + dg(a_hi, b_lo) + dg(a_lo, b_hi)


def _dot_exact_lhs(a, b):
    ab = a.astype(BF16)
    b1 = b.astype(BF16)
    r1 = b - b1.astype(F32)
    b2 = r1.astype(BF16)
    b3 = (r1 - b2.astype(F32)).astype(BF16)
    if a.shape[1] % LANES == 0:
        return _dot(jnp.concatenate([ab, ab, ab], axis=1), jnp.concatenate([b1, b2, b3], axis=0))
    return _dot(ab, b1) + _dot(ab, b2) + _dot(ab, b3)


def _rms_rows(x, g):
    return x * lax.rsqrt(jnp.mean(x * x, axis=-1, keepdims=True) + EPS) * g


def _rmsnorm_body(x_ref, g_ref, o_ref):
    o_ref[...] = _rms_rows(x_ref[...], g_ref[...]).astype(o_ref.dtype)


def _rmsnorm(x, g, out_dtype):
    n, d = x.shape
    tm = _tile(n, 512)
    return pl.pallas_call(
        _rmsnorm_body,
        grid=(n // tm,),
        in_specs=[pl.BlockSpec((tm, d), lambda i: (i, 0)), pl.BlockSpec((1, d), lambda i: (0, 0))],
        out_specs=pl.BlockSpec((tm, d), lambda i: (i, 0)),
        out_shape=jax.ShapeDtypeStruct((n, d), out_dtype),
        compiler_params=_cparams("parallel"),
        name="rmsnorm",
    )(x, g.reshape(1, d))


def _matmul_body(a_ref, w_ref, o_ref, *, w_rows_are_outputs):
    mm = _dot_nt if w_rows_are_outputs else _dot
    o_ref[...] = mm(a_ref[...], w_ref[...]).astype(o_ref.dtype)


def _matmul(a, w, tn, w_rows_are_outputs=False, out_dtype=F32):
    n, k = a.shape
    c = w.shape[0] if w_rows_are_outputs else w.shape[1]
    tm = _tile(n, 512)
    w_spec = (pl.BlockSpec((tn, k), lambda j, i: (j, 0)) if w_rows_are_outputs
              else pl.BlockSpec((k, tn), lambda j, i: (0, j)))
    return pl.pallas_call(
        functools.partial(_matmul_body, w_rows_are_outputs=w_rows_are_outputs),
        grid=(c // tn, n // tm),
        in_specs=[pl.BlockSpec((tm, k), lambda j, i: (i, 0)), w_spec],
        out_specs=pl.BlockSpec((tm, tn), lambda j, i: (i, j)),
        out_shape=jax.ShapeDtypeStruct((n, c), out_dtype),
        compiler_params=_cparams("parallel", "parallel"),
        name="matmul",
    )(a, w)


def _rope_block(x, cos, sin):
    lane = lax.broadcasted_iota(I32, x.shape, 1)
    first = (lane % 64) < 32
    partner = jnp.where(first, pltpu.roll(x, LANES - 32, 1), pltpu.roll(x, 32, 1))
    return x * cos + partner * sin


def _rope_wide(x, cos, sin):
    return [_rope_block(x[:, c * LANES:(c + 1) * LANES], cos, sin) for c in range(x.shape[1] // LANES)]


def _rope_body(cos_ref, sin_ref, dq_ref, fq_ref, fk_ref, fv_ref, iq_ref, dk_ref, dv_ref, misc_ref,
               dq_o, fq_o, fk_o, fkb_o, fvb_o, iq3_o, dk_o, dkb_o, dvb_o, ik_o, ik3_o):
    cos = cos_ref[...]
    sin = sin_ref[...]
    lane = lax.broadcasted_iota(I32, cos.shape, 1)
    low = lane < IDX_HD
    for c, blk in enumerate(_rope_wide(dq_ref[...], cos, sin)):
        dq_o[:, c * LANES:(c + 1) * LANES] = (blk * DSA_HD ** -0.5).astype(BF16)
    for c, blk in enumerate(_rope_wide(fq_ref[...], cos, sin)):
        fq_o[:, c * LANES:(c + 1) * LANES] = (blk * DIFF_HD ** -0.5).astype(BF16)
    for c, blk in enumerate(_rope_wide(fk_ref[...], cos, sin)):
        fk_o[:, c * LANES:(c + 1) * LANES] = blk
        fkb_o[:, c * LANES:(c + 1) * LANES] = blk.astype(BF16)
    fvb_o[...] = fv_ref[...].astype(BF16)
    dk = _rope_block(dk_ref[...], cos, sin)
    dk_o[...] = dk
    dkb_o[...] = dk.astype(BF16)
    dv = dv_ref[...]
    dvb_o[:, 0:LANES] = jnp.where(low, dv, 1.0).astype(BF16)
    dvb_o[:, LANES:2 * LANES] = jnp.where(low, pltpu.roll(dv, DSA_HD, 1), 1.0).astype(BF16)
    for c, blk in enumerate(_rope_wide(iq_ref[...], cos, sin)):
        hi = blk.astype(BF16).astype(F32)
        lo = blk - hi
        hi_sw = pltpu.roll(hi, IDX_HD, 1)
        lo_sw = pltpu.roll(lo, IDX_HD, 1)
        base = 2 * c * 2 * LANES
        iq3_o[:, base:base + LANES] = jnp.where(low, hi, hi_sw).astype(BF16)
        iq3_o[:, base + LANES:base + 2 * LANES] = jnp.where(low, lo, 0.0).astype(BF16)
        iq3_o[:, base + 2 * LANES:base + 3 * LANES] = jnp.where(low, hi_sw, hi).astype(BF16)
        iq3_o[:, base + 3 * LANES:base + 4 * LANES] = jnp.where(low, lo_sw, 0.0).astype(BF16)
    ik = _rope_block(misc_ref[...], cos, sin)
    ik_o[...] = ik[:, M_IDXK:M_IDXK + IDX_HD]
    hi = ik.astype(BF16).astype(F32)
    lo = ik - hi
    ik3_o[:, 0:LANES] = jnp.where(low, hi, pltpu.roll(lo, IDX_HD, 1)).astype(BF16)
    ik3_o[:, LANES:2 * LANES] = jnp.where(low, hi, 0.0).astype(BF16)


def _rope_call(h, cos, sin):
    n = h.shape[0]
    tm = _tile(min(n, cos.shape[0]), 512)
    nt = cos.shape[0] // tm

    def col(width, off):
        return pl.BlockSpec((tm, width), lambda i: (i, off // width))

    def out(width):
        return pl.BlockSpec((tm, width), lambda i: (i, 0))

    sds = jax.ShapeDtypeStruct
    tab = pl.BlockSpec((tm, LANES), lambda i: (i % nt, 0))
    return pl.pallas_call(
        _rope_body,
        grid=(n // tm,),
        in_specs=[tab, tab, col(512, C_DSAQ), col(512, C_DIFQ), col(512, C_DIFK), col(512, C_DIFV),
                  col(256, C_IDXQ), col(128, C_DSAK), col(128, C_DSAV), col(128, C_MISC)],
        out_specs=[out(512), out(512), out(512), out(512), out(512), out(1024), out(128), out(128), out(256),
                   out(IDX_HD), out(256)],
        out_shape=[sds((n, 512), BF16), sds((n, 512), BF16), sds((n, 512), F32), sds((n, 512), BF16),
                   sds((n, 512), BF16), sds((n, 1024), BF16), sds((n, 128), F32), sds((n, 128), BF16),
                   sds((n, 256), BF16), sds((n, IDX_HD), F32), sds((n, 256), BF16)],
        compiler_params=_cparams("parallel"),
        name="rope",
    )(cos, sin, h, h, h, h, h, h, h, h)


def _conv_tile(x_ref, buf_ref, cw_ref, xcat, t, tc):
    @pl.when(t == 0)
    def _():
        xcat[0:SUBLANES, :] = buf_ref[...]

    @pl.when(t > 0)
    def _():
        xcat[0:SUBLANES, :] = xcat[tc:tc + SUBLANES, :]

    xcat[SUBLANES:SUBLANES + tc, :] = x_ref[...]
    cw = cw_ref[...]
    y = x_ref[...] * cw[CONV_W - 1:CONV_W, :]
    for j in range(CONV_W - 1):
        off = SUBLANES - (CONV_W - 1) + j
        y = y + xcat[off:off + tc, :] * cw[j:j + 1, :]
    return y


def _lru_body(x_ref, gate_ref, cw_ref, cb_ref, wa_ref, ba_ref, wx_ref, bx_ref, lam_ref, buf_ref, h0_ref,
              o_ref, hl_ref, xcat, hc, *, tc, reset_first):
    t = pl.program_id(1)

    @pl.when(t == 0)
    def _():
        hc[...] = h0_ref[...]

    xa = _conv_tile(x_ref, buf_ref, cw_ref, xcat, t, tc) + cb_ref[...]
    xb = xa.astype(BF16)
    r = _sigmoid(_dot(xb, wa_ref[...]) + ba_ref[...])
    gi = _sigmoid(_dot(xb, wx_ref[...]) + bx_ref[...])
    log_a = -LRU_C * r * _softplus(-lam_ref[...])
    a = jnp.exp(log_a)
    mult = jnp.sqrt(-jnp.tanh(log_a) * (a * a + 1.0))
    row = lax.broadcasted_iota(I32, a.shape, 0)
    if reset_first:
        is0 = jnp.logical_and(row == 0, t == 0)
        a = jnp.where(is0, 0.0, a)
        mult = jnp.where(is0, 1.0, mult)
    u = mult * gi * xa
    d = 1
    while d < tc:
        keep = row >= d
        a_sh = jnp.where(keep, pltpu.roll(a, d, 0), 1.0)
        u_sh = jnp.where(keep, pltpu.roll(u, d, 0), 0.0)
        u = a * u_sh + u
        a = a * a_sh
        d *= 2
    h = a * hc[...] + u
    h_last = h[tc - 1:tc, :]
    hc[...] = h_last
    hl_ref[...] = h_last
    o_ref[...] = (h * jax.nn.gelu(gate_ref[...])).astype(o_ref.dtype)


def _lru_call(h3, wl, buf, h0, reset_first):
    b, t, _ = h3.shape
    tc = _tile(t, 256)
    w = MIX_WIDTH
    row = pl.BlockSpec((1, w), lambda i, j: (0, 0))
    sq = pl.BlockSpec((w, w), lambda i, j: (0, 0))
    return pl.pallas_call(
        functools.partial(_lru_body, tc=tc, reset_first=reset_first),
        grid=(b, t // tc),
        in_specs=[pl.BlockSpec((None, tc, w), lambda i, j: (i, j, C_LRUX // w)),
                  pl.BlockSpec((None, tc, w), lambda i, j: (i, j, C_LRUG // w)),
                  pl.BlockSpec((SUBLANES, w), lambda i, j: (0, 0)), row, sq, row, sq, row, row,
                  pl.BlockSpec((None, SUBLANES, w), lambda i, j: (i, 0, 0)),
                  pl.BlockSpec((None, 1, w), lambda i, j: (i, 0, 0))],
        out_specs=[pl.BlockSpec((None, tc, w), lambda i, j: (i, j, 0)),
                   pl.BlockSpec((None, 1, w), lambda i, j: (i, 0, 0))],
        out_shape=[jax.ShapeDtypeStruct((b, t, w), BF16), jax.ShapeDtypeStruct((b, 1, w), F32)],
        scratch_shapes=[pltpu.VMEM((tc + SUBLANES, w), F32), pltpu.VMEM((1, w), F32)],
        compiler_params=_cparams("parallel", "arbitrary"),
        name="rg_lru",
    )(h3, h3, wl["lru_cw"], wl["lru_cb"], wl["lru_wa"], wl["lru_ba"], wl["lru_wx"], wl["lru_bx"], wl["lru_lam"],
      buf, h0)


def _gdn_body(qkv_ref, z_ref, misc_ref, cw_ref, alog_ref, dtb_ref, ng_ref, buf_ref, s0_ref,
              o_ref, so_ref, xcat, y_scr, state, *, tc, ck):
    t = pl.program_id(1)
    hd = GDN_HD

    @pl.when(t == 0)
    def _():
        state[...] = s0_ref[...]

    y = _conv_tile(qkv_ref, buf_ref, cw_ref, xcat, t, tc)
    y_scr[...] = _silu(y)

    ri = lax.broadcasted_iota(I32, (ck, ck), 0)
    ci = lax.broadcasted_iota(I32, (ck, ck), 1)
    tril = ri >= ci
    strict = ri > ci
    ltri = jnp.where(tril, 1.0, 0.0).astype(F32)
    ones = jnp.ones((ck, ck), F32)
    n_sq = max(int(math.log2(ck)) - 1, 0)

    def chunk(c, carry):
        r0 = pl.multiple_of(c * ck, ck)
        heads = range(GDN_HEADS)
        misc = misc_ref[pl.ds(r0, ck), :]
        beta_all = _sigmoid(misc)
        g_all = -jnp.exp(alog_ref[...]) * _softplus(misc + dtb_ref[...])
        cols = lambda blk: slice(blk * hd, (blk + 1) * hd)
        q = [y_scr[pl.ds(r0, ck), cols(hh)] for hh in heads]
        k = [y_scr[pl.ds(r0, ck), cols(GDN_HEADS + hh)] for hh in heads]
        v = [y_scr[pl.ds(r0, ck), cols(2 * GDN_HEADS + hh)] for hh in heads]
        q = [x * lax.rsqrt(jnp.sum(x * x, axis=-1, keepdims=True) + EPS) * (hd ** -0.5) for x in q]
        k = [x * lax.rsqrt(jnp.sum(x * x, axis=-1, keepdims=True) + EPS) for x in k]
        beta = [jnp.broadcast_to(beta_all[:, M_GDNB + hh:M_GDNB + hh + 1], (ck, hd)) for hh in heads]
        g128 = [jnp.broadcast_to(g_all[:, M_GDNA + hh:M_GDNA + hh + 1], (ck, hd)) for hh in heads]
        gc128 = [_dot_exact_lhs(ltri, g) for g in g128]
        gc_row = [_dot_exact_lhs(ones, jnp.where(ri <= ci, g[:, :ck], 0.0)) for g in g128]
        decay = [jnp.where(tril, jnp.exp(jnp.where(tril, gc128[hh][:, :ck] - gc_row[hh], 0.0)), 0.0) for hh in heads]
        kb = [k[hh] * beta[hh] for hh in heads]
        egc = [jnp.exp(g) for g in gc128]
        low = [jnp.where(strict, _dot3(kb[hh], k[hh], _NT) * decay[hh], 0.0) for hh in heads]
        intra = [jnp.where(tril, _dot3(q[hh], k[hh], _NT) * decay[hh], 0.0) for hh in heads]
        mp = [-x for x in low]
        x = [jnp.concatenate([v[hh] * beta[hh], kb[hh] * egc[hh]], axis=-1) for hh in heads]
        x = [x[hh] + _dot3(mp[hh], x[hh]) for hh in heads]
        for _ in range(n_sq):
            mp = [_dot3(m, m) for m in mp]
            x = [x[hh] + _dot3(mp[hh], x[hh]) for hh in heads]
        s = [state[hh] for hh in heads]
        v_new = [x[hh][:, :hd] - _dot3(x[hh][:, hd:], s[hh]) for hh in heads]
        o = [_dot3(q[hh] * egc[hh], s[hh]) + _dot3(intra[hh], v_new[hh]) for hh in heads]
        g_last = [g[ck - 1:ck, :] for g in gc128]
        kd = [k[hh] * jnp.exp(g_last[hh] - gc128[hh]) for hh in heads]
        s_new = [s[hh] * jnp.exp(g_last[hh]) + _dot3(kd[hh], v_new[hh], _TN) for hh in heads]
        for hh in heads:
            zz = z_ref[pl.ds(r0, ck), cols(hh)]
            state[hh] = s_new[hh]
            o_ref[pl.ds(r0, ck), cols(hh)] = (_rms_rows(o[hh], ng_ref[...]) * _silu(zz)).astype(o_ref.dtype)
        return carry

    lax.fori_loop(0, tc // ck, chunk, 0)
    so_ref[...] = state[...]


def _gdn_call(h3, wl, buf, s0):
    b, t, _ = h3.shape
    ck = min(GDN_CHUNK, t)
    assert t % ck == 0
    tc = _tile(t, 256)
    w = GDN_QKV
    row = pl.BlockSpec((1, LANES), lambda i, j: (0, 0))
    st = pl.BlockSpec((None, GDN_HEADS, GDN_HD, GDN_HD), lambda i, j: (i, 0, 0, 0))
    return pl.pallas_call(
        functools.partial(_gdn_body, tc=tc, ck=ck),
        grid=(b, t // tc),
        in_specs=[pl.BlockSpec((None, tc, w), lambda i, j: (i, j, C_GDNQKV // w)),
                  pl.BlockSpec((None, tc, MIX_WIDTH), lambda i, j: (i, j, C_GDNZ // MIX_WIDTH)),
                  pl.BlockSpec((None, tc, LANES), lambda i, j: (i, j, C_MISC // LANES)),
                  pl.BlockSpec((SUBLANES, w), lambda i, j: (0, 0)), row, row, row,
                  pl.BlockSpec((None, SUBLANES, w), lambda i, j: (i, 0, 0)), st],
        out_specs=[pl.BlockSpec((None, tc, MIX_WIDTH), lambda i, j: (i, j, 0)), st],
        out_shape=[jax.ShapeDtypeStruct((b, t, MIX_WIDTH), BF16),
                   jax.ShapeDtypeStruct((b, GDN_HEADS, GDN_HD, GDN_HD), F32)],
        scratch_shapes=[pltpu.VMEM((tc + SUBLANES, w), F32), pltpu.VMEM((tc, w), F32),
                        pltpu.VMEM((GDN_HEADS, GDN_HD, GDN_HD), F32)],
        compiler_params=_cparams("parallel", "arbitrary"),
        name="gated_deltanet",
    )(h3, h3, h3, wl["gdn_cw"], wl["gdn_alog"], wl["gdn_dtb"], wl["gdn_ng"], buf, s0)


def _lane_tile(x, n):
    return x if n == LANES else jnp.concatenate([x] * (n // LANES), axis=1)


def _online_update(s, v_ext, m_ref, acc_ref, idx):
    m_old = m_ref[idx]
    m_new = jnp.maximum(m_old, jnp.max(s, axis=-1, keepdims=True))
    alpha = jnp.exp(m_old - m_new)
    p = jnp.exp(s - _lane_tile(m_new, s.shape[1]))
    acc = acc_ref[idx]
    acc_ref[idx] = _lane_tile(alpha, acc.shape[1]) * acc + _dot(p.astype(BF16), v_ext)
    m_ref[idx] = m_new


def _online_update_blocks(s, vs, m_ref, l_ref, acc_ref, idx):
    m_old = m_ref[idx]
    m_new = jnp.maximum(m_old, jnp.max(s, axis=-1, keepdims=True))
    alpha = jnp.exp(m_old - m_new)
    p = jnp.exp(s - m_new)
    l_ref[idx] = alpha * l_ref[idx] + jnp.sum(p, axis=-1, keepdims=True)
    pb = p.astype(BF16)
    pv, off = None, 0
    for vblk in vs:
        part = _dot(pb[:, off:off + vblk.shape[0]], vblk)
        pv = part if pv is None else pv + part
        off += vblk.shape[0]
    acc_ref[idx] = alpha * acc_ref[idx] + pv
    m_ref[idx] = m_new


def _reset_softmax(m_ref, l_ref, acc_ref):
    m_ref[...] = jnp.full(m_ref.shape, NEG_BIG, F32)
    l_ref[...] = jnp.zeros(l_ref.shape, F32)
    acc_ref[...] = jnp.zeros(acc_ref.shape, F32)


def _diff_lambda(lam_ref, lam_init):
    lm = lam_ref[...]
    e1 = jnp.exp(jnp.sum(lm[0:1, :] * lm[1:2, :], axis=-1, keepdims=True))
    e2 = jnp.exp(jnp.sum(lm[2:3, :] * lm[3:4, :], axis=-1, keepdims=True))
    return e1 - e2 + lam_init


def _diff_prompt_body(qi_ref, kj_ref, lam_ref, sg_ref, q_ref, k_ref, v_ref, o_ref, m_scr, acc_scr, *, tq, lam_init):
    step = pl.program_id(2)
    qi = qi_ref[step]
    kj = kj_ref[step]
    hv = 2 * DIFF_HD

    @pl.when(kj == 0)
    def _():
        m_scr[...] = jnp.full(m_scr.shape, NEG_BIG, F32)
        acc_scr[...] = jnp.zeros(acc_scr.shape, F32)

    def tile(masked):
        v = v_ref[...]
        v_ext = jnp.concatenate([v, jnp.ones_like(v)], axis=1)
        s = [_dot_nt(q_ref[:, mm * DIFF_HD:(mm + 1) * DIFF_HD], k_ref[:, mm * DIFF_HD:(mm + 1) * DIFF_HD])
             for mm in range(2)]
        for mm in range(2):
            sm = s[mm]
            if masked:
                row = lax.broadcasted_iota(I32, (tq, tq), 0)
                col = lax.broadcasted_iota(I32, (tq, tq), 1)
                sm = jnp.where(col <= row, sm, NEG_BIG)
            _online_update(sm, v_ext, m_scr, acc_scr, mm)

    @pl.when(kj < qi)
    def _():
        tile(False)

    @pl.when(kj == qi)
    def _():
        tile(True)
        lam = _diff_lambda(lam_ref, lam_init)
        o = acc_scr[0, :, 0:hv] / acc_scr[0, :, hv:2 * hv] - lam * (acc_scr[1, :, 0:hv] / acc_scr[1, :, hv:2 * hv])
        o_ref[...] = (_rms_rows(o, sg_ref[...]) * (1.0 - lam_init)).astype(o_ref.dtype)


def _diff_prompt_call(fq3, fkb3, fvb3, wl, lam_init):
    b, t, _ = fq3.shape
    tq = _tile(t, 512)
    nq = t // tq
    hv = 2 * DIFF_HD
    pairs = [(q, k) for q in range(nq) for k in range(q + 1)]
    qi_of = jnp.asarray([p[0] for p in pairs], I32)
    kj_of = jnp.asarray([p[1] for p in pairs], I32)
    grid_spec = pltpu.PrefetchScalarGridSpec(
        num_scalar_prefetch=2,
        grid=(b, DIFF_HEADS, len(pairs)),
        in_specs=[pl.BlockSpec((4, DIFF_HD), lambda i, h, s, qi, kj: (0, 0)),
                  pl.BlockSpec((1, hv), lambda i, h, s, qi, kj: (0, 0)),
                  pl.BlockSpec((None, tq, hv), lambda i, h, s, qi, kj: (i, qi[s], h)),
                  pl.BlockSpec((None, tq, hv), lambda i, h, s, qi, kj: (i, kj[s], h)),
                  pl.BlockSpec((None, tq, hv), lambda i, h, s, qi, kj: (i, kj[s], h))],
        out_specs=pl.BlockSpec((None, tq, hv), lambda i, h, s, qi, kj: (i, qi[s], h)),
        scratch_shapes=[pltpu.VMEM((2, tq, LANES), F32), pltpu.VMEM((2, tq, 2 * hv), F32)],
    )
    return pl.pallas_call(
        functools.partial(_diff_prompt_body, tq=tq, lam_init=lam_init),
        grid_spec=grid_spec,
        out_shape=jax.ShapeDtypeStruct((b, t, MIX_WIDTH), BF16),
        compiler_params=_cparams("parallel", "parallel", "arbitrary"),
        name="diff_attn_prompt",
    )(qi_of, kj_of, wl["diff_lam"], wl["diff_sg"], fq3, fkb3, fvb3)


def _diff_sample_body(pt_ref, lam_ref, sg_ref, q_ref, kn_ref, vn_ref, *rest, tq, npg, lam_init):
    kp = rest[:npg]
    vp = rest[npg:2 * npg]
    o_ref, qrows, m_scr, l_scr, acc_scr = rest[2 * npg:]
    s_id = pl.program_id(1)
    n_steps = pl.num_programs(1)
    nrow = 2 * DIFF_HEADS * tq
    hv = 2 * DIFF_HD
    prow = PAGE_SIZE * DIFF_HEADS

    @pl.when(s_id == 0)
    def _():
        _reset_softmax(m_scr, l_scr, acc_scr)
        qrows[...] = jnp.zeros(qrows.shape, F32)
        q = q_ref[...].astype(F32)
        for j in range(2 * DIFF_HEADS):
            mm = j % 2
            qrows[j * tq:(j + 1) * tq, mm * DIFF_HD:(mm + 1) * DIFF_HD] = q[:, j * DIFF_HD:(j + 1) * DIFF_HD]

    qb = qrows[...].astype(BF16)
    row_head = lax.broadcasted_iota(I32, (nrow, prow), 0) // (2 * tq)

    @pl.when(s_id < n_steps - 1)
    def _():
        key_head = lax.broadcasted_iota(I32, (nrow, prow), 1) % DIFF_HEADS
        own = key_head == row_head
        scores = [jnp.where(own, _dot_nt(qb, kp[p][...].astype(BF16)), NEG_BIG) for p in range(npg)]
        _online_update_blocks(jnp.concatenate(scores, axis=-1), [vp[p][...].astype(BF16) for p in range(npg)],
                       m_scr, l_scr, acc_scr, Ellipsis)

    @pl.when(s_id == n_steps - 1)
    def _():
        kn = jnp.concatenate([kn_ref[:, hh * hv:(hh + 1) * hv] for hh in range(DIFF_HEADS)], axis=0).astype(BF16)
        vn = jnp.concatenate([vn_ref[:, hh * hv:(hh + 1) * hv] for hh in range(DIFF_HEADS)], axis=0).astype(BF16)
        nk = DIFF_HEADS * tq
        rr = lax.broadcasted_iota(I32, (nrow, nk), 0)
        cc = lax.broadcasted_iota(I32, (nrow, nk), 1)
        ok = jnp.logical_and(cc // tq == rr // (2 * tq), cc % tq <= rr % tq)
        s = jnp.where(ok, _dot_nt(qb, kn), NEG_BIG)
        _online_update_blocks(s, [vn], m_scr, l_scr, acc_scr, Ellipsis)
        lam = _diff_lambda(lam_ref, lam_init)
        acc = acc_scr[...] / l_scr[...]
        for hh in range(DIFF_HEADS):
            r1 = (2 * hh) * tq
            r2 = (2 * hh + 1) * tq
            o = acc[r1:r1 + tq, :] - lam * acc[r2:r2 + tq, :]
            o_ref[:, hh * hv:(hh + 1) * hv] = (_rms_rows(o, sg_ref[...]) * (1.0 - lam_init)).astype(o_ref.dtype)


def _diff_sample_call(fq3, fk3, h3, pool_k, pool_v, page_table, layer, wl, lam_init):
    db, t, _ = fq3.shape
    n_pages = page_table.shape[1]
    npg = _pages_per_step(n_pages)
    n_steps = n_pages // npg + 1
    w = MIX_WIDTH
    hv = 2 * DIFF_HD
    nrow = 2 * DIFF_HEADS * t
    prow = PAGE_SIZE * DIFF_HEADS

    def page_spec(p):
        def imap(i, s, pt):
            return (layer, pt[i, jnp.minimum(s, n_steps - 2) * npg + p], 0, 0)
        return pl.BlockSpec((None, None, prow, hv), imap)

    grid_spec = pltpu.PrefetchScalarGridSpec(
        num_scalar_prefetch=1,
        grid=(db, n_steps),
        in_specs=[pl.BlockSpec((4, DIFF_HD), lambda i, s, pt: (0, 0)),
                  pl.BlockSpec((1, hv), lambda i, s, pt: (0, 0)),
                  pl.BlockSpec((None, t, w), lambda i, s, pt: (i, 0, 0)),
                  pl.BlockSpec((None, t, w), lambda i, s, pt: (i, 0, 0)),
                  pl.BlockSpec((None, t, w), lambda i, s, pt: (i, 0, C_DIFV // w))]
                 + [page_spec(p) for p in range(npg)] + [page_spec(p) for p in range(npg)],
        out_specs=pl.BlockSpec((None, t, w), lambda i, s, pt: (i, 0, 0)),
        scratch_shapes=[pltpu.VMEM((nrow, hv), F32), pltpu.VMEM((nrow, 1), F32), pltpu.VMEM((nrow, 1), F32),
                        pltpu.VMEM((nrow, hv), F32)],
    )
    return pl.pallas_call(
        functools.partial(_diff_sample_body, tq=t, npg=npg, lam_init=lam_init),
        grid_spec=grid_spec,
        out_shape=jax.ShapeDtypeStruct((db, t, w), BF16),
        compiler_params=_cparams("parallel", "arbitrary"),
        name="diff_attn_sample",
    )(page_table, wl["diff_lam"], wl["diff_sg"], fq3, fk3, h3, *([pool_k] * npg), *([pool_v] * npg))


def _sortable(score):
    bits = pltpu.bitcast(jnp.where(score == 0.0, 0.0, score), I32)
    return bits ^ ((bits >> 31) & 0x7FFFFFFF)


def _search_threshold(count_ge, shape, n_sel):
    def bit_body(i, t_u):
        cand_u = t_u | (jnp.int32(1) << (31 - i))
        cnt = count_ge(cand_u ^ INT_MIN)
        return jnp.where(cnt >= n_sel, cand_u, t_u)

    t_u = lax.fori_loop(0, 32, bit_body, jnp.zeros(shape, I32))
    return t_u ^ INT_MIN


def _lane_fold(x):
    acc = x[:, 0:LANES]
    for c in range(1, x.shape[1] // LANES):
        acc = acc + x[:, c * LANES:(c + 1) * LANES]
    return acc


def _strictly_before(n):
    ui = lax.broadcasted_iota(I32, (n, n), 0)
    uj = lax.broadcasted_iota(I32, (n, n), 1)
    return jnp.where(ui < uj, 1.0, 0.0).astype(BF16)


def _dsa_prompt_body(iq3_ref, misc_ref, ik3_ref, q_ref, k_ref, v_ref, o_ref, iqs, qs, keys_scr, half_scr, bias_scr,
                     m_scr, acc_scr, w_scr, spread_scr, c16_scr, cnt_scr, cnt16_scr, *, tq, ck, n_sel):
    qi = pl.program_id(1)
    nck = (qi * tq + tq + ck - 1) // ck
    kw = iqs.shape[1]
    for hh in range(IDX_HEADS):
        iqs[hh * tq:(hh + 1) * tq, :] = iq3_ref[:, hh * kw:(hh + 1) * kw]
    wts = misc_ref[...][:, M_IDXW:M_IDXW + IDX_HEADS] * (IDX_HD ** -0.5 * IDX_HEADS ** -0.5)
    for hh in range(IDX_HEADS):
        w_scr[hh] = jnp.broadcast_to(wts[:, hh:hh + 1], (tq, LANES))
    row_pos = qi * tq + lax.broadcasted_iota(I32, (tq, ck), 0)
    lane_pos = lax.broadcasted_iota(I32, (tq, ck), 1)

    def score_chunk(c, carry):
        k0 = pl.multiple_of(c * ck, ck)
        s = jnp.maximum(_dot_nt(iqs[...], ik3_ref[pl.ds(k0, ck), :]), 0.0)
        sc = _lane_tile(w_scr[0], ck) * s[0:tq, :]
        for hh in range(1, IDX_HEADS):
            sc = sc + _lane_tile(w_scr[hh], ck) * s[hh * tq:(hh + 1) * tq, :]
        keys_scr[c] = jnp.where(c * ck + lane_pos <= row_pos, _sortable(sc), INT_MIN)
        return carry

    lax.fori_loop(0, nck, score_chunk, 0)

    def parked(slot):
        return _lane_tile(spread_scr[slot], ck)

    def count(pred):
        cnt_scr[...] = jnp.zeros((tq, LANES), I32)

        def body(c, carry):
            cnt_scr[...] += _lane_fold(jnp.where(pred(keys_scr[c]), 1, 0).astype(I32))
            return carry
        lax.fori_loop(0, nck, body, 0)
        return jnp.sum(cnt_scr[...], axis=-1, keepdims=True)

    ones_sq = jnp.ones((LANES, LANES), BF16)

    def search_half(half_scr):
        def bit_body(i, t_u):
            cand_u = t_u | (jnp.int32(1) << (15 - i))
            c16_scr[...] = (cand_u - HALF_BIAS).astype(I16)
            cnt16_scr[...] = jnp.zeros((tq, LANES), I16)

            def body(c, carry):
                hit = half_scr[c] >= _lane_tile(c16_scr[...], ck)
                cnt16_scr[...] += _lane_fold(jnp.where(hit, jnp.int16(1), jnp.int16(0)))
                return carry
            lax.fori_loop(0, nck, body, 0)
            tot = _dot(cnt16_scr[...].astype(F32).astype(BF16), ones_sq)
            return jnp.where(tot >= n_sel, cand_u, t_u)
        return lax.fori_loop(0, 16, bit_body, jnp.zeros((tq, LANES), I32)) - HALF_BIAS

    def split_hi(c, carry):
        half_scr[c] = (keys_scr[c] >> 16).astype(I16)
        return carry
    lax.fori_loop(0, nck, split_hi, 0)
    t_hi = search_half(half_scr)

    spread_scr[0] = t_hi

    def split_lo(c, carry):
        kk = keys_scr[c]
        hi = kk >> 16
        lo = (kk & 0xFFFF) - HALF_BIAS
        t_hi_b = parked(0)
        half_scr[c] = jnp.where(hi == t_hi_b, lo, jnp.where(hi > t_hi_b, HALF_BIAS - 1, -HALF_BIAS)).astype(I16)
        return carry
    lax.fori_loop(0, nck, split_lo, 0)
    t_lo = search_half(half_scr)
    thr_rep = (t_hi << 16) | ((t_lo + HALF_BIAS) & 0xFFFF)
    spread_scr[1] = thr_rep
    thr = thr_rep[:, 0:1]
    n_ge = count(lambda kk: kk >= parked(1))
    plain = jnp.max(jnp.where(jnp.logical_and(n_ge == n_sel, thr != INT_MIN), 0, 1)) == 0

    @pl.when(plain)
    def _():
        def body(c, carry):
            bias_scr[c] = jnp.where(keys_scr[c] >= parked(1), 0.0, NEG_BIG)
            return carry
        lax.fori_loop(0, nck, body, 0)

    @pl.when(jnp.logical_not(plain))
    def _():
        need = (n_sel - count(lambda kk: kk > parked(1))).astype(F32)
        before = _strictly_before(ck)

        def body(c, eq_seen):
            kk = keys_scr[c]
            thr_b = parked(1)
            eq = jnp.logical_and(kk == thr_b, c * ck + lane_pos <= row_pos)
            eqf = jnp.where(eq, 1.0, 0.0)
            rank = _dot(eqf.astype(BF16), before) + eq_seen
            sel = jnp.logical_or(kk > thr_b, jnp.logical_and(eq, rank < need))
            bias_scr[c] = jnp.where(sel, 0.0, NEG_BIG)
            return eq_seen + jnp.sum(eqf, axis=-1, keepdims=True)

        lax.fori_loop(0, nck, body, jnp.zeros((tq, 1), F32))

    m_scr[...] = jnp.full(m_scr.shape, NEG_BIG, F32)
    acc_scr[...] = jnp.zeros(acc_scr.shape, F32)
    hpg = DSA_HEADS // DSA_KV_HEADS
    for hd in range(DSA_HEADS):
        qs[hd * tq:(hd + 1) * tq, :] = q_ref[:, hd * DSA_HD:(hd + 1) * DSA_HD]

    def attend_chunk(c, carry):
        k0 = pl.multiple_of(c * ck, ck)
        bias = bias_scr[c]
        groups = range(DSA_KV_HEADS)
        rows = [pl.ds(g * hpg * tq, hpg * tq) for g in groups]
        s = [_dot_nt(qs[rows[g], :], k_ref[pl.ds(k0, ck), g * DSA_HD:(g + 1) * DSA_HD]) for g in groups]
        for g in groups:
            vg = v_ref[pl.ds(k0, ck), g * LANES:(g + 1) * LANES]
            sg = s[g].reshape(hpg, tq, ck) + bias[None]
            _online_update(sg.reshape(hpg * tq, ck), vg, m_scr, acc_scr, rows[g])
        return carry

    lax.fori_loop(0, nck, attend_chunk, 0)
    acc = acc_scr[...]
    out = acc / pltpu.roll(acc, DSA_HD, 1)
    for hd in range(DSA_HEADS):
        o_ref[:, hd * DSA_HD:(hd + 1) * DSA_HD] = out[hd * tq:(hd + 1) * tq, 0:DSA_HD].astype(o_ref.dtype)


def _dsa_prompt_call(iq33, ik33, dq3, dkb3, dvb3, h3):
    b, t, _ = dq3.shape
    n_sel = min(TOPK_MAX, t // 4)
    tq = _tile(t, 512)
    ck = _tile(t, 512)
    kw = iq33.shape[-1] // IDX_HEADS
    full = lambda width: pl.BlockSpec((None, t, width), lambda i, q: (i, 0, 0))
    return pl.pallas_call(
        functools.partial(_dsa_prompt_body, tq=tq, ck=ck, n_sel=n_sel),
        grid=(b, t // tq),
        in_specs=[pl.BlockSpec((None, tq, IDX_HEADS * kw), lambda i, q: (i, q, 0)),
                  pl.BlockSpec((None, tq, LANES), lambda i, q: (i, q, C_MISC // LANES)),
                  full(kw),
                  pl.BlockSpec((None, tq, MIX_WIDTH), lambda i, q: (i, q, 0)),
                  full(LANES), full(DSA_KV_HEADS * LANES)],
        out_specs=pl.BlockSpec((None, tq, MIX_WIDTH), lambda i, q: (i, q, 0)),
        out_shape=jax.ShapeDtypeStruct((b, t, MIX_WIDTH), BF16),
        scratch_shapes=[pltpu.VMEM((IDX_HEADS * tq, kw), BF16), pltpu.VMEM((DSA_HEADS * tq, DSA_HD), BF16),
                        pltpu.VMEM((t // ck, tq, ck), I32), pltpu.VMEM((t // ck, tq, ck), I16),
                        pltpu.VMEM((t // ck, tq, ck), F32),
                        pltpu.VMEM((DSA_HEADS * tq, LANES), F32), pltpu.VMEM((DSA_HEADS * tq, LANES), F32),
                        pltpu.VMEM((IDX_HEADS, tq, LANES), F32), pltpu.VMEM((2, tq, LANES), I32),
                        pltpu.VMEM((tq, LANES), I16), pltpu.VMEM((tq, LANES), I32), pltpu.VMEM((tq, LANES), I16)],
        compiler_params=pltpu.CompilerParams(dimension_semantics=("parallel", "arbitrary"),
                                             vmem_limit_bytes=VMEM_LIMIT_DSA_PROMPT),
        name="dsa_prompt",
    )(iq33, h3, ik33, dq3, dkb3, dvb3)


def _dsa_select_body(pt_ref, iq3_ref, misc_ref, ikn_ref, *rest, tq, npg, ns, n_sel, n_pages):
    pages = rest[:ns * npg]
    sel_ref, keys_scr, iqs = rest[ns * npg:]
    s_id = pl.program_id(1)
    n_steps = pl.num_programs(1)
    kw = iqs.shape[-1]
    seqs = range(ns)
    wts = [misc_ref[q][:, M_IDXW:M_IDXW + IDX_HEADS] * (IDX_HD ** -0.5 * IDX_HEADS ** -0.5) for q in seqs]

    @pl.when(s_id == 0)
    def _():
        for q in seqs:
            for hh in range(IDX_HEADS):
                iqs[q, hh * tq:(hh + 1) * tq, :] = iq3_ref[q, :, hh * kw:(hh + 1) * kw]

    def scores(q, ik, feature_major):
        hi, lo = _split_hi_lo(ik)
        a_hi = iqs[q, :, 0:IDX_HD]
        a_lo = iqs[q, :, 2 * IDX_HD:3 * IDX_HD]
        mm = _dot if feature_major else _dot_nt
        s = jnp.maximum(mm(a_hi, hi) + mm(a_hi, lo) + mm(a_lo, hi), 0.0)
        sc = wts[q][:, 0:1] * s[0:tq, :]
        for hh in range(1, IDX_HEADS):
            sc = sc + wts[q][:, hh:hh + 1] * s[hh * tq:(hh + 1) * tq, :]
        return sc

    @pl.when(s_id < n_steps - 1)
    def _():
        for q in seqs:
            ik = jnp.concatenate([pages[q * npg + p][...] for p in range(npg)], axis=1)
            keys = _sortable(scores(q, ik, True))
            for p in range(npg):
                keys_scr[q, s_id * npg + p] = keys[:, p * PAGE_SIZE:(p + 1) * PAGE_SIZE]

    @pl.when(s_id == n_steps - 1)
    def _():
        qrow = lax.broadcasted_iota(I32, (tq, tq), 0)
        kcol = lax.broadcasted_iota(I32, (tq, tq), 1)
        for q in seqs:
            new_keys = jnp.where(kcol <= qrow, _sortable(scores(q, ikn_ref[q], False)), INT_MIN)
            keys_scr[q, n_pages] = jnp.concatenate([new_keys, jnp.full((tq, LANES - tq), INT_MIN, I32)], axis=-1)

        def count(pred):
            acc = jnp.sum(jnp.where(pred(keys_scr[...]), 1, 0).astype(I32), axis=1)
            return jnp.sum(acc, axis=-1, keepdims=True)

        thr = _search_threshold(lambda cand: count(lambda kk: kk >= cand[:, None]), (ns, tq, 1), n_sel)
        n_ge = count(lambda kk: kk >= thr[:, None])
        plain = jnp.max(jnp.where(jnp.logical_and(n_ge == n_sel, thr != INT_MIN), 0, 1)) == 0

        @pl.when(plain)
        def _():
            sel_ref[...] = jnp.where(keys_scr[...] >= thr[:, None], 1.0, 0.0)

        @pl.when(jnp.logical_not(plain))
        def _():
            need = (n_sel - count(lambda kk: kk > thr[:, None])).astype(F32)
            before = _strictly_before(LANES)
            lane = lax.broadcasted_iota(I32, (tq, LANES), 1)
            qr = lax.broadcasted_iota(I32, (tq, LANES), 0)
            for q in seqs:
                def emit(c, eq_seen, q=q):
                    kk = keys_scr[q, c]
                    allowed = jnp.logical_or(c < n_pages, lane <= qr)
                    eq = jnp.logical_and(kk == thr[q], allowed)
                    eqf = jnp.where(eq, 1.0, 0.0)
                    rank = _dot(eqf.astype(BF16), before) + eq_seen
                    sel = jnp.logical_or(kk > thr[q], jnp.logical_and(eq, rank < need[q]))
                    sel_ref[q, c] = jnp.where(sel, 1.0, 0.0)
                    return eq_seen + jnp.sum(eqf, axis=-1, keepdims=True)

                lax.fori_loop(0, n_pages + 1, emit, jnp.zeros((tq, 1), F32))


def _dsa_attend_body(pt_ref, q_ref, kn_ref, vn_ref, selp_ref, seln_ref, *rest, tq, npg):
    kp = rest[:npg]
    vp = rest[npg:2 * npg]
    o_ref, qrows, m_scr, l_scr, acc_scr = rest[2 * npg:]
    s_id = pl.program_id(1)
    n_steps = pl.num_programs(1)
    hpg = DSA_HEADS // DSA_KV_HEADS
    grows = hpg * tq

    @pl.when(s_id == 0)
    def _():
        _reset_softmax(m_scr, l_scr, acc_scr)
        for hd in range(DSA_HEADS):
            qrows[hd * tq:(hd + 1) * tq, :] = q_ref[:, hd * DSA_HD:(hd + 1) * DSA_HD].astype(F32)

    qb = qrows[...].astype(BF16)
    qg = [qb[g * grows:(g + 1) * grows, :] for g in range(DSA_KV_HEADS)]

    def update(s, pv_fn):
        m_old = m_scr[...]
        m_new = jnp.maximum(m_old, jnp.max(s, axis=-1, keepdims=True))
        alpha = jnp.exp(m_old - m_new)
        p = jnp.exp(s - m_new)
        l_scr[...] = alpha * l_scr[...] + jnp.sum(p, axis=-1, keepdims=True)
        pb = p.astype(BF16)
        pv = jnp.concatenate([pv_fn(pb[g * grows:(g + 1) * grows, :], g) for g in range(DSA_KV_HEADS)], axis=0)
        acc_scr[...] = alpha * acc_scr[...] + pv
        m_scr[...] = m_new

    @pl.when(s_id < n_steps - 1)
    def _():
        kcat = [jnp.concatenate([kp[p][g] for p in range(npg)], axis=1).astype(BF16) for g in range(DSA_KV_HEADS)]
        vcat = [jnp.concatenate([vp[p][g] for p in range(npg)], axis=1).astype(BF16) for g in range(DSA_KV_HEADS)]
        s = jnp.concatenate([_dot(qg[g], kcat[g]) for g in range(DSA_KV_HEADS)], axis=0)
        sel1 = jnp.concatenate([selp_ref[p] for p in range(npg)], axis=1)
        sel = jnp.concatenate([sel1] * DSA_HEADS, axis=0) > 0.5
        update(jnp.where(sel, s, NEG_BIG), lambda pb, g: _dot_nt(pb, vcat[g]))

    @pl.when(s_id == n_steps - 1)
    def _():
        kn = kn_ref[...].astype(BF16)
        vn = vn_ref[...].astype(BF16)
        s = jnp.concatenate([_dot_nt(qg[g], kn[:, g * DSA_HD:(g + 1) * DSA_HD]) for g in range(DSA_KV_HEADS)], axis=0)
        sel = jnp.concatenate([seln_ref[...][:, 0:tq]] * DSA_HEADS, axis=0) > 0.5
        update(jnp.where(sel, s, NEG_BIG), lambda pb, g: _dot(pb, vn[:, g * DSA_HD:(g + 1) * DSA_HD]))
        acc = acc_scr[...] / l_scr[...]
        for hd in range(DSA_HEADS):
            o_ref[:, hd * DSA_HD:(hd + 1) * DSA_HD] = acc[hd * tq:(hd + 1) * tq, :].astype(o_ref.dtype)


def _dsa_sample_call(iq33, ik3, dq3, dk3, h3, pool_k, pool_v, pool_ik, page_table, layer):
    db, t, _ = dq3.shape
    n_pages = page_table.shape[1]
    n_keys = n_pages * PAGE_SIZE + t
    n_sel = min(TOPK_MAX, n_keys // 4)
    npg = _pages_per_step(n_pages)
    n_steps = n_pages // npg + 1
    kw = iq33.shape[-1] // IDX_HEADS
    sw = LANES

    def page_spec(*tail):
        def make(p):
            def imap(i, s, pt):
                return (layer, pt[i, jnp.minimum(s, n_steps - 2) * npg + p]) + (0,) * len(tail)
            return pl.BlockSpec((None, None) + tail, imap)
        return [make(p) for p in range(npg)]

    ns = 4 if db % 4 == 0 else 1

    def select_pages():
        def make(q, p):
            def imap(i, s, pt):
                return (layer, pt[i * ns + q, jnp.minimum(s, n_steps - 2) * npg + p], 0, 0)
            return pl.BlockSpec((None, None, IDX_HD, PAGE_SIZE), imap)
        return [make(q, p) for q in range(ns) for p in range(npg)]

    sel = pl.pallas_call(
        functools.partial(_dsa_select_body, tq=t, npg=npg, ns=ns, n_sel=n_sel, n_pages=n_pages),
        grid_spec=pltpu.PrefetchScalarGridSpec(
            num_scalar_prefetch=1,
            grid=(db // ns, n_steps),
            in_specs=[pl.BlockSpec((ns, t, IDX_HEADS * kw), lambda i, s, pt: (i, 0, 0)),
                      pl.BlockSpec((ns, t, LANES), lambda i, s, pt: (i, 0, C_MISC // LANES)),
                      pl.BlockSpec((ns, t, IDX_HD), lambda i, s, pt: (i, 0, 0))]
                     + select_pages(),
            out_specs=pl.BlockSpec((ns, n_pages + 1, t, sw), lambda i, s, pt: (i, 0, 0, 0)),
            scratch_shapes=[pltpu.VMEM((ns, n_pages + 1, t, LANES), I32), pltpu.VMEM((ns, IDX_HEADS * t, kw), BF16)],
        ),
        out_shape=jax.ShapeDtypeStruct((db, n_pages + 1, t, sw), F32),
        compiler_params=_cparams("parallel", "arbitrary"),
        name="dsa_sample_select",
    )(page_table, iq33, h3, ik3, *([pool_ik] * (ns * npg)))

    nrow = DSA_HEADS * t
    return pl.pallas_call(
        functools.partial(_dsa_attend_body, tq=t, npg=npg),
        grid_spec=pltpu.PrefetchScalarGridSpec(
            num_scalar_prefetch=1,
            grid=(db, n_steps),
            in_specs=[pl.BlockSpec((None, t, MIX_WIDTH), lambda i, s, pt: (i, 0, 0)),
                      pl.BlockSpec((None, t, LANES), lambda i, s, pt: (i, 0, 0)),
                      pl.BlockSpec((None, t, LANES), lambda i, s, pt: (i, 0, C_DSAV // LANES)),
                      pl.BlockSpec((None, npg, t, sw), lambda i, s, pt: (i, jnp.minimum(s, n_steps - 2), 0, 0)),
                      pl.BlockSpec((None, None, t, sw), lambda i, s, pt: (i, n_pages, 0, 0))]
                     + page_spec(DSA_KV_HEADS, DSA_HD, PAGE_SIZE) + page_spec(DSA_KV_HEADS, DSA_HD, PAGE_SIZE),
            out_specs=pl.BlockSpec((None, t, MIX_WIDTH), lambda i, s, pt: (i, 0, 0)),
            scratch_shapes=[pltpu.VMEM((nrow, DSA_HD), F32), pltpu.VMEM((nrow, 1), F32), pltpu.VMEM((nrow, 1), F32),
                            pltpu.VMEM((nrow, DSA_HD), F32)],
        ),
        out_shape=jax.ShapeDtypeStruct((db, t, MIX_WIDTH), BF16),
        compiler_params=_cparams("parallel", "arbitrary"),
        name="dsa_sample_attend",
    )(page_table, dq3, dk3, h3, sel, sel, *([pool_k] * npg), *([pool_v] * npg))


def _merge_body(x_ref, gate_ref, oa_ref, ob_ref, oc_ref, od_ref, wb_ref, wo_ref, o_ref):
    acc = None
    for bi, br in enumerate((oa_ref, ob_ref, oc_ref, od_ref)):
        proj = _dot(br[...], wb_ref[bi])
        term = _sigmoid(gate_ref[:, bi * D_MODEL:(bi + 1) * D_MODEL]) * proj
        acc = term if acc is None else acc + term
    o_ref[...] = x_ref[...] + _dot(acc.astype(BF16), wo_ref[...])


def _merge_call(x, h, branches, wl):
    n, d = x.shape
    tm = _tile(n, 256)
    br = pl.BlockSpec((tm, MIX_WIDTH), lambda i: (i, 0))
    return pl.pallas_call(
        _merge_body,
        grid=(n // tm,),
        in_specs=[pl.BlockSpec((tm, d), lambda i: (i, 0)),
                  pl.BlockSpec((tm, N_BRANCH * d), lambda i: (i, C_GATE // (N_BRANCH * d))),
                  br, br, br, br,
                  pl.BlockSpec((N_BRANCH, MIX_WIDTH, d), lambda i: (0, 0, 0)),
                  pl.BlockSpec((d, d), lambda i: (0, 0))],
        out_specs=pl.BlockSpec((tm, d), lambda i: (i, 0)),
        out_shape=jax.ShapeDtypeStruct((n, d), F32),
        compiler_params=_cparams("parallel"),
        name="merge",
    )(x, h, *branches, wl["w_branch"], wl["w_out"])


def _cross_body(x_ref, g_ref, wq_ref, mk_ref, mv_ref, wo_ref, o_ref, *, merged):
    x = x_ref[...]
    tm = x.shape[0]
    xn = _rms_rows(x, g_ref[...]).astype(BF16)
    q = _dot(xn, wq_ref[...]).astype(BF16)
    if merged:
        qrows = jnp.concatenate([q[:, hh * XA_HD:(hh + 1) * XA_HD] for hh in range(XA_HEADS)], axis=0)
        s = _dot_nt(qrows, mk_ref[...].astype(BF16)) * (XA_HD ** -0.5)
        rr = lax.broadcasted_iota(I32, s.shape, 0) // tm
        cc = lax.broadcasted_iota(I32, s.shape, 1) % XA_HEADS
        s = jnp.where(rr == cc, s, NEG_BIG)
        p = jnp.exp(s - jnp.max(s, axis=-1, keepdims=True))
        p = p / jnp.sum(p, axis=-1, keepdims=True)
        orow = _dot(p.astype(BF16), mv_ref[...].astype(BF16))
        outs = [orow[hh * tm:(hh + 1) * tm, :] for hh in range(XA_HEADS)]
    else:
        outs = []
        for hh in range(XA_HEADS):
            sl = slice(hh * XA_HD, (hh + 1) * XA_HD)
            s = _dot_nt(q[:, sl], mk_ref[:, sl].astype(BF16)) * (XA_HD ** -0.5)
            p = jnp.exp(s - jnp.max(s, axis=-1, keepdims=True))
            p = p / jnp.sum(p, axis=-1, keepdims=True)
            outs.append(_dot(p.astype(BF16), mv_ref[:, sl].astype(BF16)))
    o = jnp.concatenate(outs, axis=-1).astype(BF16)
    o_ref[...] = x + _dot(o, wo_ref[...])


def _cross_call(x3, wl, mk4, mv4, layer):
    b, t, d = x3.shape
    w = XA_HEADS * XA_HD
    tm = _tile(t, 512)
    merged = mk4.shape[-1] == XA_HD
    mem = pl.BlockSpec((None, None) + mk4.shape[2:], lambda i, j: (layer, i, 0, 0))
    return pl.pallas_call(
        functools.partial(_cross_body, merged=merged),
        grid=(b, t // tm),
        in_specs=[pl.BlockSpec((None, tm, d), lambda i, j: (i, j, 0)),
                  pl.BlockSpec((1, d), lambda i, j: (0, 0)),
                  pl.BlockSpec((d, w), lambda i, j: (0, 0)), mem, mem,
                  pl.BlockSpec((w, d), lambda i, j: (0, 0))],
        out_specs=pl.BlockSpec((None, tm, d), lambda i, j: (i, j, 0)),
        out_shape=jax.ShapeDtypeStruct((b, t, d), F32),
        compiler_params=_cparams("parallel", "parallel"),
        name="cross_attn",
    )(x3, wl["norm_xa_g"], wl["xa_wq"], mk4, mv4, wl["xa_wo"])


def _ffn_body(x_ref, g_ref, wg_ref, wu_ref, wd_ref, o_ref, xn_scr, acc_scr):
    f = pl.program_id(1)

    @pl.when(f == 0)
    def _():
        xn_scr[...] = _rms_rows(x_ref[...], g_ref[...]).astype(BF16)
        acc_scr[...] = x_ref[...]

    xn = xn_scr[...]
    hm = _silu(_dot(xn, wg_ref[...])) * _dot(xn, wu_ref[...])
    acc_scr[...] += _dot(hm.astype(BF16), wd_ref[...])

    @pl.when(f == pl.num_programs(1) - 1)
    def _():
        o_ref[...] = acc_scr[...]


def _ffn_call(x, wl):
    n, d = x.shape
    ff = wl["ffn_wg"].shape[1]
    tm = _tile(n, 512)
    tf = _tile(ff, 1408)
    return pl.pallas_call(
        _ffn_body,
        grid=(n // tm, ff // tf),
        in_specs=[pl.BlockSpec((tm, d), lambda i, f: (i, 0)), pl.BlockSpec((1, d), lambda i, f: (0, 0)),
                  pl.BlockSpec((d, tf), lambda i, f: (0, f)), pl.BlockSpec((d, tf), lambda i, f: (0, f)),
                  pl.BlockSpec((tf, d), lambda i, f: (f, 0))],
        out_specs=pl.BlockSpec((tm, d), lambda i, f: (i, 0)),
        out_shape=jax.ShapeDtypeStruct((n, d), F32),
        scratch_shapes=[pltpu.VMEM((tm, d), BF16), pltpu.VMEM((tm, d), F32)],
        compiler_params=_cparams("parallel", "arbitrary"),
        name="ffn_swiglu",
    )(x, wl["norm_ffn_g"], wl["ffn_wg"], wl["ffn_wu"], wl["ffn_wd"])


def _moe_body(x_ref, g_ref, wr_ref, br_ref, wg_ref, wu_ref, wd_ref, o_ref, xn_scr, gate_scr, acc_scr):
    e = pl.program_id(1)
    f = pl.program_id(2)
    first = jnp.logical_and(e == 0, f == 0)

    @pl.when(first)
    def _():
        xn = _rms_rows(x_ref[...], g_ref[...])
        xn_scr[...] = xn.astype(BF16)
        acc_scr[...] = x_ref[...]
        logits = _dot_hi(xn, wr_ref[...]) + br_ref[...]
        lane = lax.broadcasted_iota(I32, logits.shape, 1)
        logits = jnp.where(lane < N_EXPERTS, logits, -jnp.inf)
        top1 = jnp.max(logits, axis=-1, keepdims=True)
        idx1 = jnp.min(jnp.where(logits == top1, lane, LANES), axis=-1, keepdims=True)
        rest = jnp.where(lane == idx1, -jnp.inf, logits)
        top2 = jnp.max(rest, axis=-1, keepdims=True)
        idx2 = jnp.min(jnp.where(rest == top2, lane, LANES), axis=-1, keepdims=True)
        e2 = jnp.exp(top2 - top1)
        p1 = 1.0 / (1.0 + e2)
        p2 = e2 / (1.0 + e2)
        gate_scr[...] = jnp.where(lane == idx1, p1, 0.0) + jnp.where(lane == idx2, p2, 0.0)

    gates = gate_scr[...]
    lane = lax.broadcasted_iota(I32, gates.shape, 1)
    ge = jnp.sum(jnp.where(lane == e, gates, 0.0), axis=-1, keepdims=True)
    xn = xn_scr[...]
    hm = _silu(_dot(xn, wg_ref[...])) * _dot(xn, wu_ref[...])
    acc_scr[...] += ge * _dot(hm.astype(BF16), wd_ref[...])

    @pl.when(jnp.logical_and(e == pl.num_programs(1) - 1, f == pl.num_programs(2) - 1))
    def _():
        o_ref[...] = acc_scr[...]


def _moe_call(x, wl):
    n, d = x.shape
    ne, _, fe = wl["moe_wg"].shape
    tm = _tile(n, 512)
    tf = _tile(fe, 1024)
    return pl.pallas_call(
        _moe_body,
        grid=(n // tm, ne, fe // tf),
        in_specs=[pl.BlockSpec((tm, d), lambda i, e, f: (i, 0)), pl.BlockSpec((1, d), lambda i, e, f: (0, 0)),
                  pl.BlockSpec((d, LANES), lambda i, e, f: (0, 0)), pl.BlockSpec((1, LANES), lambda i, e, f: (0, 0)),
                  pl.BlockSpec((None, d, tf), lambda i, e, f: (e, 0, f)),
                  pl.BlockSpec((None, d, tf), lambda i, e, f: (e, 0, f)),
                  pl.BlockSpec((None, tf, d), lambda i, e, f: (e, f, 0))],
        out_specs=pl.BlockSpec((tm, d), lambda i, e, f: (i, 0)),
        out_shape=jax.ShapeDtypeStruct((n, d), F32),
        scratch_shapes=[pltpu.VMEM((tm, d), BF16), pltpu.VMEM((tm, LANES), F32), pltpu.VMEM((tm, d), F32)],
        compiler_params=_cparams("parallel", "arbitrary", "arbitrary"),
        name="moe_swiglu",
    )(x, wl["norm_ffn_g"], wl["moe_wr"], wl["moe_br"], wl["moe_wg"], wl["moe_wu"], wl["moe_wd"])


def _pad_rows(a, rows):
    return jnp.pad(a, ((0, rows - a.shape[0]), (0, 0)))


def _lane_row(vals, lane0):
    return jnp.zeros((1, LANES), F32).at[0, lane0:lane0 + vals.shape[0]].set(vals.astype(F32))


def _block_diag(w):
    nb, n, _ = w.shape
    eye = jnp.eye(nb, dtype=w.dtype)
    return jnp.einsum("aij,ab->aibj", w, eye).reshape(nb * n, nb * n)


def _layer_weights(l, p):
    w_t = jnp.transpose(p["w_in"], (2, 0, 1))[:, l, :]
    d = w_t.shape[1]
    misc = jnp.concatenate([w_t[O_IDXK:O_IDXK + IDX_HD + IDX_HEADS], w_t[O_GDNB:O_GDNB + 2 * GDN_HEADS],
                            jnp.zeros((LANES - IDX_HD - IDX_HEADS - 2 * GDN_HEADS, d), w_t.dtype)], axis=0)
    w_in_r = jnp.concatenate([
        w_t[O_GATE:], w_t[O_LRUX:O_DSAK], w_t[O_DIFQ:O_GATE], w_t[O_GDNZ:O_GDNB],
        w_t[O_GDNQKV:O_GDNZ], w_t[O_IDXQ:O_IDXK], w_t[O_DSAK:O_IDXQ], misc], axis=0)
    assert w_in_r.shape[0] == H_COLS
    wl = dict(
        norm_mix_g=p["norm_mix_g"][l], w_in=w_in_r.astype(BF16),
        lru_cw=_pad_rows(p["lru_conv_w"][l], SUBLANES), lru_cb=p["lru_conv_b"][l].reshape(1, -1),
        lru_wa=_block_diag(p["lru_wa"][l]).astype(BF16), lru_ba=p["lru_ba"][l].reshape(1, -1),
        lru_wx=_block_diag(p["lru_wx"][l]).astype(BF16), lru_bx=p["lru_bx"][l].reshape(1, -1),
        lru_lam=p["lru_lambda"][l].reshape(1, -1),
        gdn_cw=_pad_rows(p["gdn_conv_w"][l], SUBLANES),
        gdn_alog=_lane_row(p["gdn_a_log"][l], M_GDNA), gdn_dtb=_lane_row(p["gdn_dt_bias"][l], M_GDNA),
        gdn_ng=p["gdn_norm_g"][l].reshape(1, -1),
        diff_lam=p["diff_lambda"][l], diff_sg=p["diff_subln_g"][l].reshape(1, -1),
        w_branch=p["w_branch"][l].astype(BF16), w_out=p["w_out"][l].astype(BF16),
        norm_xa_g=p["norm_xa_g"][l].reshape(1, -1), norm_mem_g=p["norm_mem_g"][l],
        xa_wq=p["xa_wq"][l].astype(BF16), xa_wo=p["xa_wo"][l].astype(BF16),
        xa_wkv=jnp.concatenate([p["xa_wk"][l], p["xa_wv"][l]], axis=1).astype(BF16),
        norm_ffn_g=p["norm_ffn_g"][l].reshape(1, -1),
    )
    j = l // 2
    if l % 2 == 0:
        wl.update(ffn_wg=p["ffn_w_gate"][j].astype(BF16), ffn_wu=p["ffn_w_up"][j].astype(BF16),
                  ffn_wd=p["ffn_w_down"][j].astype(BF16))
    else:
        wl.update(moe_wr=jnp.pad(p["moe_router_w"][j], ((0, 0), (0, LANES - N_EXPERTS))),
                  moe_br=jnp.pad(p["moe_router_b"][j], (0, LANES - N_EXPERTS)).reshape(1, LANES),
                  moe_wg=p["moe_w_gate"][j].astype(BF16), moe_wu=p["moe_w_up"][j].astype(BF16),
                  moe_wd=p["moe_w_down"][j].astype(BF16))
    return wl


def _rope_tables(pos):
    half = DSA_HD // 2
    inv = ROPE_THETA ** (-jnp.arange(half, dtype=F32) / half)
    ang = pos.astype(F32)[:, None] * inv[None, :]
    cos, sin = jnp.cos(ang), jnp.sin(ang)
    return jnp.tile(cos, (1, 4)), jnp.tile(jnp.concatenate([-sin, sin], axis=1), (1, 2))


def _front_pad(buf):
    return jnp.pad(buf, ((0, 0), (SUBLANES - (CONV_W - 1), 0), (0, 0)))


def _group_layer(x3, l, wl, tabs, states, mem_kv, paged):
    b, t, d = x3.shape
    n = b * t
    x = x3.reshape(n, d)
    lam_init = 0.8 - 0.6 * math.exp(-0.3 * l)
    xn = _rmsnorm(x, wl["norm_mix_g"], BF16)
    h = _matmul(xn, wl["w_in"], tn=1408, w_rows_are_outputs=True)
    h3 = h.reshape(b, t, H_COLS)
    dq, fq, fk, fkb, fvb, iq3, dk, dkb, dvb, ik, ik3 = _rope_call(h, tabs[0], tabs[1])
    to3 = lambda a: a.reshape(b, t, a.shape[-1])
    dq3, fq3, fk3, dk3, ik3d = map(to3, (dq, fq, fk, dk, ik))

    o_a, lru_h = _lru_call(h3, wl, _front_pad(states["lru_conv"]), states["lru_h"].reshape(b, 1, -1),
                           reset_first=paged is None)
    o_c, gdn_s = _gdn_call(h3, wl, _front_pad(states["gdn_conv"]), states["gdn_s"])
    if paged is None:
        o_b = _dsa_prompt_call(to3(iq3), to3(ik3), dq3, to3(dkb), to3(dvb), h3)
        o_d = _diff_prompt_call(fq3, to3(fkb), to3(fvb), wl, lam_init)
    else:
        o_b = _dsa_sample_call(to3(iq3), ik3d, dq3, dk3, h3, paged["dsa_k"], paged["dsa_v"], paged["idx_k"],
                               paged["page_table"], l)
        o_d = _diff_sample_call(fq3, fk3, h3, paged["diff_k"], paged["diff_v"], paged["page_table"], l, wl, lam_init)

    flat = lambda a: a.reshape(n, a.shape[-1])
    x1 = _merge_call(x, h, [flat(o_a), flat(o_b), flat(o_c), flat(o_d)], wl)
    x2 = _cross_call(x1.reshape(b, t, d), wl, mem_kv[0], mem_kv[1], mem_kv[2])
    x2 = x2.reshape(n, d)
    x3_new = (_ffn_call(x2, wl) if "ffn_wg" in wl else _moe_call(x2, wl)).reshape(b, t, d)

    new = dict(
        dsa_k=dk3.reshape(b, t, DSA_KV_HEADS, DSA_HD),
        dsa_v=h3[:, :, C_DSAV:C_DSAV + DSA_KV_HEADS * DSA_HD].reshape(b, t, DSA_KV_HEADS, DSA_HD),
        idx_k=ik3d,
        diff_k=fk3.reshape(b, t, DIFF_HEADS, 2 * DIFF_HD),
        diff_v=h3[:, :, C_DIFV:C_DIFV + MIX_WIDTH].reshape(b, t, DIFF_HEADS, 2 * DIFF_HD),
        lru_h=lru_h.reshape(b, -1),
        lru_conv=jnp.concatenate([states["lru_conv"], h3[:, :, C_LRUX:C_LRUX + MIX_WIDTH]], axis=1)[:, -(CONV_W - 1):]
        if t < CONV_W - 1 else h3[:, t - (CONV_W - 1):, C_LRUX:C_LRUX + MIX_WIDTH],
        gdn_s=gdn_s,
        gdn_conv=jnp.concatenate([states["gdn_conv"], h3[:, :, C_GDNQKV:C_GDNQKV + GDN_QKV]], axis=1)[:, -(CONV_W - 1):]
        if t < CONV_W - 1 else h3[:, t - (CONV_W - 1):, C_GDNQKV:C_GDNQKV + GDN_QKV],
    )
    return x3_new, new


def kernel(x_prompt, x_sample, mem_prompt, cache_dsa_k, cache_dsa_v, cache_idx_k, cache_diff_k, cache_diff_v,
           cache_mem_k, cache_mem_v, state_lru_h, state_lru_conv, state_gdn_s, state_gdn_conv, page_table,
           norm_mix_g, w_in, lru_conv_w, lru_conv_b, lru_wa, lru_ba, lru_wx, lru_bx, lru_lambda,
           gdn_conv_w, gdn_a_log, gdn_dt_bias, gdn_norm_g, diff_lambda, diff_subln_g, w_branch, w_out,
           norm_xa_g, norm_mem_g, xa_wq, xa_wk, xa_wv, xa_wo, norm_ffn_g, ffn_w_gate, ffn_w_up, ffn_w_down,
           moe_router_w, moe_router_b, moe_w_gate, moe_w_up, moe_w_down, final_norm_g):
    params = dict(
        norm_mix_g=norm_mix_g, w_in=w_in, lru_conv_w=lru_conv_w, lru_conv_b=lru_conv_b, lru_wa=lru_wa, lru_ba=lru_ba,
        lru_wx=lru_wx, lru_bx=lru_bx, lru_lambda=lru_lambda, gdn_conv_w=gdn_conv_w, gdn_a_log=gdn_a_log,
        gdn_dt_bias=gdn_dt_bias, gdn_norm_g=gdn_norm_g, diff_lambda=diff_lambda, diff_subln_g=diff_subln_g,
        w_branch=w_branch, w_out=w_out, norm_xa_g=norm_xa_g, norm_mem_g=norm_mem_g, xa_wq=xa_wq, xa_wk=xa_wk,
        xa_wv=xa_wv, xa_wo=xa_wo, norm_ffn_g=norm_ffn_g, ffn_w_gate=ffn_w_gate, ffn_w_up=ffn_w_up,
        ffn_w_down=ffn_w_down, moe_router_w=moe_router_w, moe_router_b=moe_router_b, moe_w_gate=moe_w_gate,
        moe_w_up=moe_w_up, moe_w_down=moe_w_down)
    b, s, d = x_prompt.shape
    db, t, _ = x_sample.shape
    m = mem_prompt.shape[1]
    n_pool = cache_dsa_k.shape[1]
    past = page_table.shape[1] * PAGE_SIZE

    tabs_p = _rope_tables(jnp.arange(s))
    cs, ss = _rope_tables(past + jnp.arange(t))
    tabs_s = (jnp.tile(cs, (db, 1)), jnp.tile(ss, (db, 1)))
    paged = dict(
        dsa_k=jnp.transpose(cache_dsa_k, (0, 1, 3, 4, 2)), dsa_v=jnp.transpose(cache_dsa_v, (0, 1, 3, 4, 2)),
        idx_k=jnp.transpose(cache_idx_k, (0, 1, 3, 2)), page_table=page_table,
        diff_k=cache_diff_k.reshape(DEPTH, n_pool, PAGE_SIZE * DIFF_HEADS, 2 * DIFF_HD),
        diff_v=cache_diff_v.reshape(DEPTH, n_pool, PAGE_SIZE * DIFF_HEADS, 2 * DIFF_HD))
    mem_k_s = cache_mem_k.reshape(DEPTH, db, m * XA_HEADS, XA_HD)
    mem_v_s = cache_mem_v.reshape(DEPTH, db, m * XA_HEADS, XA_HD)
    zero_states = dict(lru_conv=jnp.zeros((b, CONV_W - 1, MIX_WIDTH), F32), lru_h=jnp.zeros((b, MIX_WIDTH), F32),
                       gdn_conv=jnp.zeros((b, CONV_W - 1, GDN_QKV), F32),
                       gdn_s=jnp.zeros((b, GDN_HEADS, GDN_HD, GDN_HD), F32))

    hp, hs = x_prompt, x_sample
    new_p, new_s, mem_ks, mem_vs = [], [], [], []
    for l in range(DEPTH):
        wl = _layer_weights(l, params)
        mem_n = _rmsnorm(mem_prompt.reshape(b * m, d), wl["norm_mem_g"], BF16)
        kv = _matmul(mem_n, wl["xa_wkv"], tn=512)
        w = XA_HEADS * XA_HD
        mk = kv[:, :w].reshape(1, b, m, w)
        mv = kv[:, w:].reshape(1, b, m, w)
        hp, st_p = _group_layer(hp, l, wl, tabs_p, zero_states, (mk, mv, 0), None)
        new_p.append(st_p)
        mem_ks.append(mk.reshape(b, m, XA_HEADS, XA_HD))
        mem_vs.append(mv.reshape(b, m, XA_HEADS, XA_HD))
        states_s = dict(lru_conv=state_lru_conv[l], lru_h=state_lru_h[l], gdn_conv=state_gdn_conv[l],
                        gdn_s=state_gdn_s[l])
        hs, st_s = _group_layer(hs, l, wl, tabs_s, states_s, (mem_k_s, mem_v_s, l), paged)
        new_s.append(st_s)

    y_prompt = _rmsnorm(hp.reshape(b * s, d), final_norm_g, F32).reshape(b, s, d)
    y_sample = _rmsnorm(hs.reshape(db * t, d), final_norm_g, F32).reshape(db, t, d)
    stack = lambda states, name: jnp.stack([st[name] for st in states], axis=0)
    outs = [y_prompt, y_sample]
    for name in ("dsa_k", "dsa_v", "idx_k", "diff_k", "diff_v", "lru_h", "lru_conv", "gdn_s", "gdn_conv"):
        outs += [stack(new_p, name), stack(new_s, name)]
    outs += [jnp.stack(mem_ks, axis=0), jnp.stack(mem_vs, axis=0)]
    return tuple(outs)
```

```python
import functools
import math

import jax
import jax.numpy as jnp
from jax import lax
from jax.experimental import pallas as pl
from jax.experimental.pallas import tpu as pltpu

F32 = jnp.float32
BF16 = jnp.bfloat16
I32 = jnp.int32
I16 = jnp.int16

D_MODEL = 1024
DEPTH = 2
PAGE_SIZE = 128
MIX_WIDTH = 512
LRU_BLOCKS = 8
LRU_BLOCK = MIX_WIDTH // LRU_BLOCKS
LRU_C = 8.0
CONV_W = 4
DSA_HEADS = 8
DSA_KV_HEADS = 2
DSA_HD = 64
IDX_HEADS = 4
IDX_HD = 64
TOPK_MAX = 256
GDN_HEADS = 4
GDN_HD = 128
GDN_CHUNK = 128
DIFF_HEADS = 4
DIFF_HD = 64
XA_HEADS = 4
XA_HD = 128
N_EXPERTS = 8
ROPE_THETA = 10000.0
EPS = 1e-6
N_BRANCH = 4
GDN_QKV = 3 * GDN_HEADS * GDN_HD

LANES = 128
SUBLANES = 8
NEG_BIG = -1e30
INT_MIN = -2 ** 31
HALF_BIAS = 2 ** 15
VMEM_LIMIT = 48 * 1024 * 1024
VMEM_LIMIT_DSA_PROMPT = 58 * 1024 * 1024

C_GATE = 0
C_LRUX = 4096
C_LRUG = 4608
C_DSAQ = 5120
C_DIFQ = 5632
C_DIFK = 6144
C_DIFV = 6656
C_GDNZ = 7168
C_GDNQKV = 7680
C_IDXQ = 9216
C_DSAK = 9472
C_DSAV = 9600
C_MISC = 9728
H_COLS = 9856
M_IDXK = 0
M_IDXW = 64
M_GDNB = 68
M_GDNA = 72

O_LRUX, O_LRUG, O_DSAQ, O_DSAK, O_DSAV, O_IDXQ, O_IDXK, O_IDXW = 0, 512, 1024, 1536, 1664, 1792, 2048, 2112
O_GDNQKV, O_GDNZ, O_GDNB, O_GDNA, O_DIFQ, O_DIFK, O_DIFV, O_GATE = 2116, 3652, 4164, 4168, 4172, 4684, 5196, 5708


def _cparams(*sem):
    return pltpu.CompilerParams(dimension_semantics=sem, vmem_limit_bytes=VMEM_LIMIT)


def _tile(n, pref):
    return pref if n % pref == 0 else n


def _pages_per_step(n_pages):
    return next(p for p in (16, 8, 1) if n_pages % p == 0)


def _softplus(x):
    return jnp.maximum(x, 0.0) + jnp.log1p(jnp.exp(-jnp.abs(x)))


def _sigmoid(x):
    return 1.0 / (1.0 + jnp.exp(-x))


def _silu(x):
    return x * _sigmoid(x)


def _dot(a, b):
    return jnp.dot(a, b, preferred_element_type=F32)


def _dot_nt(a, b):
    return lax.dot_general(a, b, (((1,), (1,)), ((), ())), preferred_element_type=F32)


def _dot_hi(a, b):
    return jnp.dot(a, b, preferred_element_type=F32, precision=lax.Precision.HIGHEST)


def _split_hi_lo(x):
    hi = x.astype(BF16)
    return hi, (x - hi.astype(F32)).astype(BF16)


_NN = (((1,), (0,)), ((), ()))
_NT = (((1,), (1,)), ((), ()))
_TN = (((0,), (0,)), ((), ()))


def _dot3(a, b, dims=_NN):
    a_hi, a_lo = _split_hi_lo(a)
    b_hi, b_lo = _split_hi_lo(b)
    dg = functools.partial(lax.dot_general, dimension_numbers=dims, preferred_element_type=F32)
    ca, cb = dims[0][0][0], dims[0][1][0]
    if a.shape[ca] % LANES == 0:
        return dg(jnp.concatenate([a_hi, a_hi, a_lo], axis=ca), jnp.concatenate([b_hi, b_lo, b_hi], axis=cb))
    return dg(a_hi, b_hi) + dg(a_hi, b_lo) + dg(a_lo, b_hi)


def _dot_exact_lhs(a, b):
    ab = a.astype(BF16)
    b1 = b.astype(BF16)
    r1 = b - b1.astype(F32)
    b2 = r1.astype(BF16)
    b3 = (r1 - b2.astype(F32)).astype(BF16)
    if a.shape[1] % LANES == 0:
        return _dot(jnp.concatenate([ab, ab, ab], axis=1), jnp.concatenate([b1, b2, b3], axis=0))
    return _dot(ab, b1) + _dot(ab, b2) + _dot(ab, b3)


def _rms_rows(x, g):
    return x * lax.rsqrt(jnp.mean(x * x, axis=-1, keepdims=True) + EPS) * g


def _rmsnorm_body(x_ref, g_ref, o_ref):
    o_ref[...] = _rms_rows(x_ref[...], g_ref[...]).astype(o_ref.dtype)


def _rmsnorm(x, g, out_dtype):
    n, d = x.shape
    tm = _tile(n, 512)
    return pl.pallas_call(
        _rmsnorm_body,
        grid=(n // tm,),
        in_specs=[pl.BlockSpec((tm, d), lambda i: (i, 0)), pl.BlockSpec((1, d), lambda i: (0, 0))],
        out_specs=pl.BlockSpec((tm, d), lambda i: (i, 0)),
        out_shape=jax.ShapeDtypeStruct((n, d), out_dtype),
        compiler_params=_cparams("parallel"),
        name="rmsnorm",
    )(x, g.reshape(1, d))


def _matmul_body(a_ref, w_ref, o_ref, *, w_rows_are_outputs):
    mm = _dot_nt if w_rows_are_outputs else _dot
    o_ref[...] = mm(a_ref[...], w_ref[...]).astype(o_ref.dtype)


def _matmul(a, w, tn, w_rows_are_outputs=False, out_dtype=F32):
    n, k = a.shape
    c = w.shape[0] if w_rows_are_outputs else w.shape[1]
    tm = _tile(n, 1024)
    w_spec = (pl.BlockSpec((tn, k), lambda j, i: (j, 0)) if w_rows_are_outputs
              else pl.BlockSpec((k, tn), lambda j, i: (0, j)))
    return pl.pallas_call(
        functools.partial(_matmul_body, w_rows_are_outputs=w_rows_are_outputs),
        grid=(c // tn, n // tm),
        in_specs=[pl.BlockSpec((tm, k), lambda j, i: (i, 0)), w_spec],
        out_specs=pl.BlockSpec((tm, tn), lambda j, i: (i, j)),
        out_shape=jax.ShapeDtypeStruct((n, c), out_dtype),
        compiler_params=_cparams("parallel", "parallel"),
        name="matmul",
    )(a, w)


def _rope_block(x, cos, sin):
    lane = lax.broadcasted_iota(I32, x.shape, 1)
    first = (lane % 64) < 32
    partner = jnp.where(first, pltpu.roll(x, LANES - 32, 1), pltpu.roll(x, 32, 1))
    return x * cos + partner * sin


def _rope_wide(x, cos, sin):
    return [_rope_block(x[:, c * LANES:(c + 1) * LANES], cos, sin) for c in range(x.shape[1] // LANES)]


def _rope_body(cos_ref, sin_ref, dq_ref, fq_ref, fk_ref, fv_ref, iq_ref, dk_ref, dv_ref, misc_ref,
               dq_o, fq_o, fk_o, fkb_o, fvb_o, iq3_o, dk_o, dkb_o, dvb_o, ik_o, ik3_o):
    cos = cos_ref[...]
    sin = sin_ref[...]
    lane = lax.broadcasted_iota(I32, cos.shape, 1)
    low = lane < IDX_HD
    for c, blk in enumerate(_rope_wide(dq_ref[...], cos, sin)):
        dq_o[:, c * LANES:(c + 1) * LANES] = (blk * DSA_HD ** -0.5).astype(BF16)
    for c, blk in enumerate(_rope_wide(fq_ref[...], cos, sin)):
        fq_o[:, c * LANES:(c + 1) * LANES] = (blk * DIFF_HD ** -0.5).astype(BF16)
    for c, blk in enumerate(_rope_wide(fk_ref[...], cos, sin)):
        fk_o[:, c * LANES:(c + 1) * LANES] = blk
        fkb_o[:, c * LANES:(c + 1) * LANES] = blk.astype(BF16)
    fvb_o[...] = fv_ref[...].astype(BF16)
    dk = _rope_block(dk_ref[...], cos, sin)
    dk_o[...] = dk
    dkb_o[...] = dk.astype(BF16)
    dv = dv_ref[...]
    dvb_o[:, 0:LANES] = jnp.where(low, dv, 1.0).astype(BF16)
    dvb_o[:, LANES:2 * LANES] = jnp.where(low, pltpu.roll(dv, DSA_HD, 1), 1.0).astype(BF16)
    for c, blk in enumerate(_rope_wide(iq_ref[...], cos, sin)):
        hi = blk.astype(BF16).astype(F32)
        lo = blk - hi
        hi_sw = pltpu.roll(hi, IDX_HD, 1)
        lo_sw = pltpu.roll(lo, IDX_HD, 1)
        base = 2 * c * 2 * LANES
        iq3_o[:, base:base + LANES] = jnp.where(low, hi, hi_sw).astype(BF16)
        iq3_o[:, base + LANES:base + 2 * LANES] = jnp.where(low, lo, 0.0).astype(BF16)
        iq3_o[:, base + 2 * LANES:base + 3 * LANES] = jnp.where(low, hi_sw, hi).astype(BF16)
        iq3_o[:, base + 3 * LANES:base + 4 * LANES] = jnp.where(low, lo_sw, 0.0).astype(BF16)
    ik = _rope_block(misc_ref[...], cos, sin)
    ik_o[...] = ik[:, M_IDXK:M_IDXK + IDX_HD]
    hi = ik.astype(BF16).astype(F32)
    lo = ik - hi
    ik3_o[:, 0:LANES] = jnp.where(low, hi, pltpu.roll(lo, IDX_HD, 1)).astype(BF16)
    ik3_o[:, LANES:2 * LANES] = jnp.where(low, hi, 0.0).astype(BF16)


def _rope_call(h, cos, sin):
    n = h.shape[0]
    tm = _tile(min(n, cos.shape[0]), 512)
    nt = cos.shape[0] // tm

    def col(width, off):
        return pl.BlockSpec((tm, width), lambda i: (i, off // width))

    def out(width):
        return pl.BlockSpec((tm, width), lambda i: (i, 0))

    sds = jax.ShapeDtypeStruct
    tab = pl.BlockSpec((tm, LANES), lambda i: (i % nt, 0))
    return pl.pallas_call(
        _rope_body,
        grid=(n // tm,),
        in_specs=[tab, tab, col(512, C_DSAQ), col(512, C_DIFQ), col(512, C_DIFK), col(512, C_DIFV),
                  col(256, C_IDXQ), col(128, C_DSAK), col(128, C_DSAV), col(128, C_MISC)],
        out_specs=[out(512), out(512), out(512), out(512), out(512), out(1024), out(128), out(128), out(256),
                   out(IDX_HD), out(256)],
        out_shape=[sds((n, 512), BF16), sds((n, 512), BF16), sds((n, 512), F32), sds((n, 512), BF16),
                   sds((n, 512), BF16), sds((n, 1024), BF16), sds((n, 128), F32), sds((n, 128), BF16),
                   sds((n, 256), BF16), sds((n, IDX_HD), F32), sds((n, 256), BF16)],
        compiler_params=_cparams("parallel"),
        name="rope",
    )(cos, sin, h, h, h, h, h, h, h, h)


def _conv_tile(x_ref, buf_ref, cw_ref, xcat, t, tc):
    @pl.when(t == 0)
    def _():
        xcat[0:SUBLANES, :] = buf_ref[...]

    @pl.when(t > 0)
    def _():
        xcat[0:SUBLANES, :] = xcat[tc:tc + SUBLANES, :]

    xcat[SUBLANES:SUBLANES + tc, :] = x_ref[...]
    cw = cw_ref[...]
    y = x_ref[...] * cw[CONV_W - 1:CONV_W, :]
    for j in range(CONV_W - 1):
        off = SUBLANES - (CONV_W - 1) + j
        y = y + xcat[off:off + tc, :] * cw[j:j + 1, :]
    return y


def _lru_body(x_ref, gate_ref, cw_ref, cb_ref, wa_ref, ba_ref, wx_ref, bx_ref, lam_ref, buf_ref, h0_ref,
              o_ref, hl_ref, xcat, hc, *, tc, reset_first):
    t = pl.program_id(1)

    @pl.when(t == 0)
    def _():
        hc[...] = h0_ref[...]

    xa = _conv_tile(x_ref, buf_ref, cw_ref, xcat, t, tc) + cb_ref[...]
    xb = xa.astype(BF16)
    r = _sigmoid(_dot(xb, wa_ref[...]) + ba_ref[...])
    gi = _sigmoid(_dot(xb, wx_ref[...]) + bx_ref[...])
    log_a = -LRU_C * r * _softplus(-lam_ref[...])
    a = jnp.exp(log_a)
    mult = jnp.sqrt(-jnp.tanh(log_a) * (a * a + 1.0))
    row = lax.broadcasted_iota(I32, a.shape, 0)
    if reset_first:
        is0 = jnp.logical_and(row == 0, t == 0)
        a = jnp.where(is0, 0.0, a)
        mult = jnp.where(is0, 1.0, mult)
    u = mult * gi * xa
    d = 1
    while d < tc:
        keep = row >= d
        a_sh = jnp.where(keep, pltpu.roll(a, d, 0), 1.0)
        u_sh = jnp.where(keep, pltpu.roll(u, d, 0), 0.0)
        u = a * u_sh + u
        a = a * a_sh
        d *= 2
    h = a * hc[...] + u
    h_last = h[tc - 1:tc, :]
    hc[...] = h_last
    hl_ref[...] = h_last
    o_ref[...] = (h * jax.nn.gelu(gate_ref[...])).astype(o_ref.dtype)


def _lru_call(h3, wl, buf, h0, reset_first):
    b, t, _ = h3.shape
    tc = _tile(t, 256)
    w = MIX_WIDTH
    row = pl.BlockSpec((1, w), lambda i, j: (0, 0))
    sq = pl.BlockSpec((w, w), lambda i, j: (0, 0))
    return pl.pallas_call(
        functools.partial(_lru_body, tc=tc, reset_first=reset_first),
        grid=(b, t // tc),
        in_specs=[pl.BlockSpec((None, tc, w), lambda i, j: (i, j, C_LRUX // w)),
                  pl.BlockSpec((None, tc, w), lambda i, j: (i, j, C_LRUG // w)),
                  pl.BlockSpec((SUBLANES, w), lambda i, j: (0, 0)), row, sq, row, sq, row, row,
                  pl.BlockSpec((None, SUBLANES, w), lambda i, j: (i, 0, 0)),
                  pl.BlockSpec((None, 1, w), lambda i, j: (i, 0, 0))],
        out_specs=[pl.BlockSpec((None, tc, w), lambda i, j: (i, j, 0)),
                   pl.BlockSpec((None, 1, w), lambda i, j: (i, 0, 0))],
        out_shape=[jax.ShapeDtypeStruct((b, t, w), BF16), jax.ShapeDtypeStruct((b, 1, w), F32)],
        scratch_shapes=[pltpu.VMEM((tc + SUBLANES, w), F32), pltpu.VMEM((1, w), F32)],
        compiler_params=_cparams("parallel", "arbitrary"),
        name="rg_lru",
    )(h3, h3, wl["lru_cw"], wl["lru_cb"], wl["lru_wa"], wl["lru_ba"], wl["lru_wx"], wl["lru_bx"], wl["lru_lam"],
      buf, h0)


def _gdn_body(qkv_ref, z_ref, misc_ref, cw_ref, alog_ref, dtb_ref, ng_ref, buf_ref, s0_ref,
              o_ref, so_ref, xcat, y_scr, state, *, tc, ck):
    t = pl.program_id(1)
    hd = GDN_HD

    @pl.when(t == 0)
    def _():
        state[...] = s0_ref[...]

    y = _conv_tile(qkv_ref, buf_ref, cw_ref, xcat, t, tc)
    y_scr[...] = _silu(y)

    ri = lax.broadcasted_iota(I32, (ck, ck), 0)
    ci = lax.broadcasted_iota(I32, (ck, ck), 1)
    tril = ri >= ci
    strict = ri > ci
    ltri = jnp.where(tril, 1.0, 0.0).astype(F32)
    ones = jnp.ones((ck, ck), F32)
    n_sq = max(int(math.log2(ck)) - 1, 0)

    def chunk(c, carry):
        r0 = pl.multiple_of(c * ck, ck)
        heads = range(GDN_HEADS)
        misc = misc_ref[pl.ds(r0, ck), :]
        beta_all = _sigmoid(misc)
        g_all = -jnp.exp(alog_ref[...]) * _softplus(misc + dtb_ref[...])
        cols = lambda blk: slice(blk * hd, (blk + 1) * hd)
        q = [y_scr[pl.ds(r0, ck), cols(hh)] for hh in heads]
        k = [y_scr[pl.ds(r0, ck), cols(GDN_HEADS + hh)] for hh in heads]
        v = [y_scr[pl.ds(r0, ck), cols(2 * GDN_HEADS + hh)] for hh in heads]
        q = [x * lax.rsqrt(jnp.sum(x * x, axis=-1, keepdims=True) + EPS) * (hd ** -0.5) for x in q]
        k = [x * lax.rsqrt(jnp.sum(x * x, axis=-1, keepdims=True) + EPS) for x in k]
        beta = [jnp.broadcast_to(beta_all[:, M_GDNB + hh:M_GDNB + hh + 1], (ck, hd)) for hh in heads]
        g128 = [jnp.broadcast_to(g_all[:, M_GDNA + hh:M_GDNA + hh + 1], (ck, hd)) for hh in heads]
        gc128 = [_dot_exact_lhs(ltri, g) for g in g128]
        gc_row = [_dot_exact_lhs(ones, jnp.where(ri <= ci, g[:, :ck], 0.0)) for g in g128]
        decay = [jnp.where(tril, jnp.exp(jnp.where(tril, gc128[hh][:, :ck] - gc_row[hh], 0.0)), 0.0) for hh in heads]
        kb = [k[hh] * beta[hh] for hh in heads]
        egc = [jnp.exp(g) for g in gc128]
        low = [jnp.where(strict, _dot3(kb[hh], k[hh], _NT) * decay[hh], 0.0) for hh in heads]
        intra = [jnp.where(tril, _dot3(q[hh], k[hh], _NT) * decay[hh], 0.0) for hh in heads]
        mp = [-x for x in low]
        x = [jnp.concatenate([v[hh] * beta[hh], kb[hh] * egc[hh]], axis=-1) for hh in heads]
        x = [x[hh] + _dot3(mp[hh], x[hh]) for hh in heads]
        for _ in range(n_sq):
            mp = [_dot3(m, m) for m in mp]
            x = [x[hh] + _dot3(mp[hh], x[hh]) for hh in heads]
        s = [state[hh] for hh in heads]
        v_new = [x[hh][:, :hd] - _dot3(x[hh][:, hd:], s[hh]) for hh in heads]
        o = [_dot3(q[hh] * egc[hh], s[hh]) + _dot3(intra[hh], v_new[hh]) for hh in heads]
        g_last = [g[ck - 1:ck, :] for g in gc128]
        kd = [k[hh] * jnp.exp(g_last[hh] - gc128[hh]) for hh in heads]
        s_new = [s[hh] * jnp.exp(g_last[hh]) + _dot3(kd[hh], v_new[hh], _TN) for hh in heads]
        for hh in heads:
            zz = z_ref[pl.ds(r0, ck), cols(hh)]
            state[hh] = s_new[hh]
            o_ref[pl.ds(r0, ck), cols(hh)] = (_rms_rows(o[hh], ng_ref[...]) * _silu(zz)).astype(o_ref.dtype)
        return carry

    lax.fori_loop(0, tc // ck, chunk, 0)
    so_ref[...] = state[...]


def _gdn_call(h3, wl, buf, s0):
    b, t, _ = h3.shape
    ck = min(GDN_CHUNK, t)
    assert t % ck == 0
    tc = _tile(t, 256)
    w = GDN_QKV
    row = pl.BlockSpec((1, LANES), lambda i, j: (0, 0))
    st = pl.BlockSpec((None, GDN_HEADS, GDN_HD, GDN_HD), lambda i, j: (i, 0, 0, 0))
    return pl.pallas_call(
        functools.partial(_gdn_body, tc=tc, ck=ck),
        grid=(b, t // tc),
        in_specs=[pl.BlockSpec((None, tc, w), lambda i, j: (i, j, C_GDNQKV // w)),
                  pl.BlockSpec((None, tc, MIX_WIDTH), lambda i, j: (i, j, C_GDNZ // MIX_WIDTH)),
                  pl.BlockSpec((None, tc, LANES), lambda i, j: (i, j, C_MISC // LANES)),
                  pl.BlockSpec((SUBLANES, w), lambda i, j: (0, 0)), row, row, row,
                  pl.BlockSpec((None, SUBLANES, w), lambda i, j: (i, 0, 0)), st],
        out_specs=[pl.BlockSpec((None, tc, MIX_WIDTH), lambda i, j: (i, j, 0)), st],
        out_shape=[jax.ShapeDtypeStruct((b, t, MIX_WIDTH), BF16),
                   jax.ShapeDtypeStruct((b, GDN_HEADS, GDN_HD, GDN_HD), F32)],
        scratch_shapes=[pltpu.VMEM((tc + SUBLANES, w), F32), pltpu.VMEM((tc, w), F32),
                        pltpu.VMEM((GDN_HEADS, GDN_HD, GDN_HD), F32)],
        compiler_params=_cparams("parallel", "arbitrary"),
        name="gated_deltanet",
    )(h3, h3, h3, wl["gdn_cw"], wl["gdn_alog"], wl["gdn_dtb"], wl["gdn_ng"], buf, s0)


def _lane_tile(x, n):
    return x if n == LANES else jnp.concatenate([x] * (n // LANES), axis=1)


def _online_update(s, v_ext, m_ref, acc_ref, idx):
    m_old = m_ref[idx]
    m_new = jnp.maximum(m_old, jnp.max(s, axis=-1, keepdims=True))
    alpha = jnp.exp(m_old - m_new)
    p = jnp.exp(s - _lane_tile(m_new, s.shape[1]))
    acc = acc_ref[idx]
    acc_ref[idx] = _lane_tile(alpha, acc.shape[1]) * acc + _dot(p.astype(BF16), v_ext)
    m_ref[idx] = m_new


def _online_update_blocks(s, vs, m_ref, l_ref, acc_ref, idx):
    m_old = m_ref[idx]
    m_new = jnp.maximum(m_old, jnp.max(s, axis=-1, keepdims=True))
    alpha = jnp.exp(m_old - m_new)
    p = jnp.exp(s - m_new)
    l_ref[idx] = alpha * l_ref[idx] + jnp.sum(p, axis=-1, keepdims=True)
    pb = p.astype(BF16)
    pv, off = None, 0
    for vblk in vs:
        part = _dot(pb[:, off:off + vblk.shape[0]], vblk)
        pv = part if pv is None else pv + part
        off += vblk.shape[0]
    acc_ref[idx] = alpha * acc_ref[idx] + pv
    m_ref[idx] = m_new


def _reset_softmax(m_ref, l_ref, acc_ref):
    m_ref[...] = jnp.full(m_ref.shape, NEG_BIG, F32)
    l_ref[...] = jnp.zeros(l_ref.shape, F32)
    acc_ref[...] = jnp.zeros(acc_ref.shape, F32)


def _diff_lambda(lam_ref, lam_init):
    lm = lam_ref[...]
    e1 = jnp.exp(jnp.sum(lm[0:1, :] * lm[1:2, :], axis=-1, keepdims=True))
    e2 = jnp.exp(jnp.sum(lm[2:3, :] * lm[3:4, :], axis=-1, keepdims=True))
    return e1 - e2 + lam_init


def _diff_prompt_body(qi_ref, kj_ref, lam_ref, sg_ref, q_ref, k_ref, v_ref, o_ref, m_scr, acc_scr, *, tq, lam_init):
    step = pl.program_id(2)
    qi = qi_ref[step]
    kj = kj_ref[step]
    hv = 2 * DIFF_HD

    @pl.when(kj == 0)
    def _():
        m_scr[...] = jnp.full(m_scr.shape, NEG_BIG, F32)
        acc_scr[...] = jnp.zeros(acc_scr.shape, F32)

    def tile(masked):
        v = v_ref[...]
        v_ext = jnp.concatenate([v, jnp.ones_like(v)], axis=1)
        s = [_dot_nt(q_ref[:, mm * DIFF_HD:(mm + 1) * DIFF_HD], k_ref[:, mm * DIFF_HD:(mm + 1) * DIFF_HD])
             for mm in range(2)]
        for mm in range(2):
            sm = s[mm]
            if masked:
                row = lax.broadcasted_iota(I32, (tq, tq), 0)
                col = lax.broadcasted_iota(I32, (tq, tq), 1)
                sm = jnp.where(col <= row, sm, NEG_BIG)
            _online_update(sm, v_ext, m_scr, acc_scr, mm)

    @pl.when(kj < qi)
    def _():
        tile(False)

    @pl.when(kj == qi)
    def _():
        tile(True)
        lam = _diff_lambda(lam_ref, lam_init)
        o = acc_scr[0, :, 0:hv] / acc_scr[0, :, hv:2 * hv] - lam * (acc_scr[1, :, 0:hv] / acc_scr[1, :, hv:2 * hv])
        o_ref[...] = (_rms_rows(o, sg_ref[...]) * (1.0 - lam_init)).astype(o_ref.dtype)


def _diff_prompt_call(fq3, fkb3, fvb3, wl, lam_init):
    b, t, _ = fq3.shape
    tq = _tile(t, 512)
    nq = t // tq
    hv = 2 * DIFF_HD
    pairs = [(q, k) for q in range(nq) for k in range(q + 1)]
    qi_of = jnp.asarray([p[0] for p in pairs], I32)
    kj_of = jnp.asarray([p[1] for p in pairs], I32)
    grid_spec = pltpu.PrefetchScalarGridSpec(
        num_scalar_prefetch=2,
        grid=(b, DIFF_HEADS, len(pairs)),
        in_specs=[pl.BlockSpec((4, DIFF_HD), lambda i, h, s, qi, kj: (0, 0)),
                  pl.BlockSpec((1, hv), lambda i, h, s, qi, kj: (0, 0)),
                  pl.BlockSpec((None, tq, hv), lambda i, h, s, qi, kj: (i, qi[s], h)),
                  pl.BlockSpec((None, tq, hv), lambda i, h, s, qi, kj: (i, kj[s], h)),
                  pl.BlockSpec((None, tq, hv), lambda i, h, s, qi, kj: (i, kj[s], h))],
        out_specs=pl.BlockSpec((None, tq, hv), lambda i, h, s, qi, kj: (i, qi[s], h)),
        scratch_shapes=[pltpu.VMEM((2, tq, LANES), F32), pltpu.VMEM((2, tq, 2 * hv), F32)],
    )
    return pl.pallas_call(
        functools.partial(_diff_prompt_body, tq=tq, lam_init=lam_init),
        grid_spec=grid_spec,
        out_shape=jax.ShapeDtypeStruct((b, t, MIX_WIDTH), BF16),
        compiler_params=_cparams("parallel", "parallel", "arbitrary"),
        name="diff_attn_prompt",
    )(qi_of, kj_of, wl["diff_lam"], wl["diff_sg"], fq3, fkb3, fvb3)


def _diff_sample_body(pt_ref, lam_ref, sg_ref, q_ref, kn_ref, vn_ref, *rest, tq, npg, lam_init):
    kp = rest[:npg]
    vp = rest[npg:2 * npg]
    o_ref, qrows, m_scr, l_scr, acc_scr = rest[2 * npg:]
    s_id = pl.program_id(1)
    n_steps = pl.num_programs(1)
    nrow = 2 * DIFF_HEADS * tq
    hv = 2 * DIFF_HD
    prow = PAGE_SIZE * DIFF_HEADS

    @pl.when(s_id == 0)
    def _():
        _reset_softmax(m_scr, l_scr, acc_scr)
        qrows[...] = jnp.zeros(qrows.shape, F32)
        q = q_ref[...].astype(F32)
        for j in range(2 * DIFF_HEADS):
            mm = j % 2
            qrows[j * tq:(j + 1) * tq, mm * DIFF_HD:(mm + 1) * DIFF_HD] = q[:, j * DIFF_HD:(j + 1) * DIFF_HD]

    qb = qrows[...].astype(BF16)
    row_head = lax.broadcasted_iota(I32, (nrow, prow), 0) // (2 * tq)

    @pl.when(s_id < n_steps - 1)
    def _():
        key_head = lax.broadcasted_iota(I32, (nrow, prow), 1) % DIFF_HEADS
        own = key_head == row_head
        scores = [jnp.where(own, _dot_nt(qb, kp[p][...].astype(BF16)), NEG_BIG) for p in range(npg)]
        _online_update_blocks(jnp.concatenate(scores, axis=-1), [vp[p][...].astype(BF16) for p in range(npg)],
                       m_scr, l_scr, acc_scr, Ellipsis)

    @pl.when(s_id == n_steps - 1)
    def _():
        kn = jnp.concatenate([kn_ref[:, hh * hv:(hh + 1) * hv] for hh in range(DIFF_HEADS)], axis=0).astype(BF16)
        vn = jnp.concatenate([vn_ref[:, hh * hv:(hh + 1) * hv] for hh in range(DIFF_HEADS)], axis=0).astype(BF16)
        nk = DIFF_HEADS * tq
        rr = lax.broadcasted_iota(I32, (nrow, nk), 0)
        cc = lax.broadcasted_iota(I32, (nrow, nk), 1)
        ok = jnp.logical_and(cc // tq == rr // (2 * tq), cc % tq <= rr % tq)
        s = jnp.where(ok, _dot_nt(qb, kn), NEG_BIG)
        _online_update_blocks(s, [vn], m_scr, l_scr, acc_scr, Ellipsis)
        lam = _diff_lambda(lam_ref, lam_init)
        acc = acc_scr[...] / l_scr[...]
        for hh in range(DIFF_HEADS):
            r1 = (2 * hh) * tq
            r2 = (2 * hh + 1) * tq
            o = acc[r1:r1 + tq, :] - lam * acc[r2:r2 + tq, :]
            o_ref[:, hh * hv:(hh + 1) * hv] = (_rms_rows(o, sg_ref[...]) * (1.0 - lam_init)).astype(o_ref.dtype)


def _diff_sample_call(fq3, fk3, h3, pool_k, pool_v, page_table, layer, wl, lam_init):
    db, t, _ = fq3.shape
    n_pages = page_table.shape[1]
    npg = _pages_per_step(n_pages)
    n_steps = n_pages // npg + 1
    w = MIX_WIDTH
    hv = 2 * DIFF_HD
    nrow = 2 * DIFF_HEADS * t
    prow = PAGE_SIZE * DIFF_HEADS

    def page_spec(p):
        def imap(i, s, pt):
            return (layer, pt[i, jnp.minimum(s, n_steps - 2) * npg + p], 0, 0)
        return pl.BlockSpec((None, None, prow, hv), imap)

    grid_spec = pltpu.PrefetchScalarGridSpec(
        num_scalar_prefetch=1,
        grid=(db, n_steps),
        in_specs=[pl.BlockSpec((4, DIFF_HD), lambda i, s, pt: (0, 0)),
                  pl.BlockSpec((1, hv), lambda i, s, pt: (0, 0)),
                  pl.BlockSpec((None, t, w), lambda i, s, pt: (i, 0, 0)),
                  pl.BlockSpec((None, t, w), lambda i, s, pt: (i, 0, 0)),
                  pl.BlockSpec((None, t, w), lambda i, s, pt: (i, 0, C_DIFV // w))]
                 + [page_spec(p) for p in range(npg)] + [page_spec(p) for p in range(npg)],
        out_specs=pl.BlockSpec((None, t, w), lambda i, s, pt: (i, 0, 0)),
        scratch_shapes=[pltpu.VMEM((nrow, hv), F32), pltpu.VMEM((nrow, 1), F32), pltpu.VMEM((nrow, 1), F32),
                        pltpu.VMEM((nrow, hv), F32)],
    )
    return pl.pallas_call(
        functools.partial(_diff_sample_body, tq=t, npg=npg, lam_init=lam_init),
        grid_spec=grid_spec,
        out_shape=jax.ShapeDtypeStruct((db, t, w), BF16),
        compiler_params=_cparams("parallel", "arbitrary"),
        name="diff_attn_sample",
    )(page_table, wl["diff_lam"], wl["diff_sg"], fq3, fk3, h3, *([pool_k] * npg), *([pool_v] * npg))


def _sortable(score):
    bits = pltpu.bitcast(jnp.where(score == 0.0, 0.0, score), I32)
    return bits ^ ((bits >> 31) & 0x7FFFFFFF)


def _search_threshold(count_ge, shape, n_sel):
    def bit_body(i, t_u):
        cand_u = t_u | (jnp.int32(1) << (31 - i))
        cnt = count_ge(cand_u ^ INT_MIN)
        return jnp.where(cnt >= n_sel, cand_u, t_u)

    t_u = lax.fori_loop(0, 32, bit_body, jnp.zeros(shape, I32))
    return t_u ^ INT_MIN


def _lane_fold(x):
    acc = x[:, 0:LANES]
    for c in range(1, x.shape[1] // LANES):
        acc = acc + x[:, c * LANES:(c + 1) * LANES]
    return acc


def _strictly_before(n):
    ui = lax.broadcasted_iota(I32, (n, n), 0)
    uj = lax.broadcasted_iota(I32, (n, n), 1)
    return jnp.where(ui < uj, 1.0, 0.0).astype(BF16)


def _dsa_prompt_body(iq3_ref, misc_ref, ik3_ref, q_ref, k_ref, v_ref, o_ref, iqs, qs, keys_scr, half_scr, bias_scr,
                     m_scr, acc_scr, w_scr, spread_scr, c16_scr, cnt_scr, cnt16_scr, *, tq, ck, n_sel):
    qi = pl.program_id(1)
    nck = (qi * tq + tq + ck - 1) // ck
    kw = iqs.shape[1]
    for hh in range(IDX_HEADS):
        iqs[hh * tq:(hh + 1) * tq, :] = iq3_ref[:, hh * kw:(hh + 1) * kw]
    wts = misc_ref[...][:, M_IDXW:M_IDXW + IDX_HEADS] * (IDX_HD ** -0.5 * IDX_HEADS ** -0.5)
    for hh in range(IDX_HEADS):
        w_scr[hh] = jnp.broadcast_to(wts[:, hh:hh + 1], (tq, LANES))
    row_pos = qi * tq + lax.broadcasted_iota(I32, (tq, ck), 0)
    lane_pos = lax.broadcasted_iota(I32, (tq, ck), 1)

    def score_chunk(c, carry):
        k0 = pl.multiple_of(c * ck, ck)
        s = jnp.maximum(_dot_nt(iqs[...], ik3_ref[pl.ds(k0, ck), :]), 0.0)
        sc = _lane_tile(w_scr[0], ck) * s[0:tq, :]
        for hh in range(1, IDX_HEADS):
            sc = sc + _lane_tile(w_scr[hh], ck) * s[hh * tq:(hh + 1) * tq, :]
        keys_scr[c] = jnp.where(c * ck + lane_pos <= row_pos, _sortable(sc), INT_MIN)
        return carry

    lax.fori_loop(0, nck, score_chunk, 0)

    def parked(slot):
        return _lane_tile(spread_scr[slot], ck)

    def count(pred):
        cnt_scr[...] = jnp.zeros((tq, LANES), I32)

        def body(c, carry):
            cnt_scr[...] += _lane_fold(jnp.where(pred(keys_scr[c]), 1, 0).astype(I32))
            return carry
        lax.fori_loop(0, nck, body, 0)
        return jnp.sum(cnt_scr[...], axis=-1, keepdims=True)

    ones_sq = jnp.ones((LANES, LANES), BF16)

    def search_half(half_scr):
        def bit_body(i, t_u):
            cand_u = t_u | (jnp.int32(1) << (15 - i))
            c16_scr[...] = (cand_u - HALF_BIAS).astype(I16)
            cnt16_scr[...] = jnp.zeros((tq, LANES), I16)

            def body(c, carry):
                hit = half_scr[c] >= _lane_tile(c16_scr[...], ck)
                cnt16_scr[...] += _lane_fold(jnp.where(hit, jnp.int16(1), jnp.int16(0)))
                return carry
            lax.fori_loop(0, nck, body, 0)
            tot = _dot(cnt16_scr[...].astype(F32).astype(BF16), ones_sq)
            return jnp.where(tot >= n_sel, cand_u, t_u)
        return lax.fori_loop(0, 16, bit_body, jnp.zeros((tq, LANES), I32)) - HALF_BIAS

    def split_hi(c, carry):
        half_scr[c] = (keys_scr[c] >> 16).astype(I16)
        return carry
    lax.fori_loop(0, nck, split_hi, 0)
    t_hi = search_half(half_scr)

    spread_scr[0] = t_hi

    def split_lo(c, carry):
        kk = keys_scr[c]
        hi = kk >> 16
        lo = (kk & 0xFFFF) - HALF_BIAS
        t_hi_b = parked(0)
        half_scr[c] = jnp.where(hi == t_hi_b, lo, jnp.where(hi > t_hi_b, HALF_BIAS - 1, -HALF_BIAS)).astype(I16)
        return carry
    lax.fori_loop(0, nck, split_lo, 0)
    t_lo = search_half(half_scr)
    thr_rep = (t_hi << 16) | ((t_lo + HALF_BIAS) & 0xFFFF)
    spread_scr[1] = thr_rep
    thr = thr_rep[:, 0:1]
    n_ge = count(lambda kk: kk >= parked(1))
    plain = jnp.max(jnp.where(jnp.logical_and(n_ge == n_sel, thr != INT_MIN), 0, 1)) == 0

    @pl.when(plain)
    def _():
        def body(c, carry):
            bias_scr[c] = jnp.where(keys_scr[c] >= parked(1), 0.0, NEG_BIG)
            return carry
        lax.fori_loop(0, nck, body, 0)

    @pl.when(jnp.logical_not(plain))
    def _():
        need = (n_sel - count(lambda kk: kk > parked(1))).astype(F32)
        before = _strictly_before(ck)

        def body(c, eq_seen):
            kk = keys_scr[c]
            thr_b = parked(1)
            eq = jnp.logical_and(kk == thr_b, c * ck + lane_pos <= row_pos)
            eqf = jnp.where(eq, 1.0, 0.0)
            rank = _dot(eqf.astype(BF16), before) + eq_seen
            sel = jnp.logical_or(kk > thr_b, jnp.logical_and(eq, rank < need))
            bias_scr[c] = jnp.where(sel, 0.0, NEG_BIG)
            return eq_seen + jnp.sum(eqf, axis=-1, keepdims=True)

        lax.fori_loop(0, nck, body, jnp.zeros((tq, 1), F32))

    m_scr[...] = jnp.full(m_scr.shape, NEG_BIG, F32)
    acc_scr[...] = jnp.zeros(acc_scr.shape, F32)
    hpg = DSA_HEADS // DSA_KV_HEADS
    for hd in range(DSA_HEADS):
        qs[hd * tq:(hd + 1) * tq, :] = q_ref[:, hd * DSA_HD:(hd + 1) * DSA_HD]

    def attend_chunk(c, carry):
        k0 = pl.multiple_of(c * ck, ck)
        bias = bias_scr[c]
        groups = range(DSA_KV_HEADS)
        rows = [pl.ds(g * hpg * tq, hpg * tq) for g in groups]
        s = [_dot_nt(qs[rows[g], :], k_ref[pl.ds(k0, ck), g * DSA_HD:(g + 1) * DSA_HD]) for g in groups]
        for g in groups:
            vg = v_ref[pl.ds(k0, ck), g * LANES:(g + 1) * LANES]
            sg = s[g].reshape(hpg, tq, ck) + bias[None]
            _online_update(sg.reshape(hpg * tq, ck), vg, m_scr, acc_scr, rows[g])
        return carry

    lax.fori_loop(0, nck, attend_chunk, 0)
    acc = acc_scr[...]
    out = acc / pltpu.roll(acc, DSA_HD, 1)
    for hd in range(DSA_HEADS):
        o_ref[:, hd * DSA_HD:(hd + 1) * DSA_HD] = out[hd * tq:(hd + 1) * tq, 0:DSA_HD].astype(o_ref.dtype)


def _dsa_prompt_call(iq33, ik33, dq3, dkb3, dvb3, h3):
    b, t, _ = dq3.shape
    n_sel = min(TOPK_MAX, t // 4)
    tq = _tile(t, 512)
    ck = _tile(t, 512)
    kw = iq33.shape[-1] // IDX_HEADS
    full = lambda width: pl.BlockSpec((None, t, width), lambda i, q: (i, 0, 0))
    return pl.pallas_call(
        functools.partial(_dsa_prompt_body, tq=tq, ck=ck, n_sel=n_sel),
        grid=(b, t // tq),
        in_specs=[pl.BlockSpec((None, tq, IDX_HEADS * kw), lambda i, q: (i, q, 0)),
                  pl.BlockSpec((None, tq, LANES), lambda i, q: (i, q, C_MISC // LANES)),
                  full(kw),
                  pl.BlockSpec((None, tq, MIX_WIDTH), lambda i, q: (i, q, 0)),
                  full(LANES), full(DSA_KV_HEADS * LANES)],
        out_specs=pl.BlockSpec((None, tq, MIX_WIDTH), lambda i, q: (i, q, 0)),
        out_shape=jax.ShapeDtypeStruct((b, t, MIX_WIDTH), BF16),
        scratch_shapes=[pltpu.VMEM((IDX_HEADS * tq, kw), BF16), pltpu.VMEM((DSA_HEADS * tq, DSA_HD), BF16),
                        pltpu.VMEM((t // ck, tq, ck), I32), pltpu.VMEM((t // ck, tq, ck), I16),
                        pltpu.VMEM((t // ck, tq, ck), F32),
                        pltpu.VMEM((DSA_HEADS * tq, LANES), F32), pltpu.VMEM((DSA_HEADS * tq, LANES), F32),
                        pltpu.VMEM((IDX_HEADS, tq, LANES), F32), pltpu.VMEM((2, tq, LANES), I32),
                        pltpu.VMEM((tq, LANES), I16), pltpu.VMEM((tq, LANES), I32), pltpu.VMEM((tq, LANES), I16)],
        compiler_params=pltpu.CompilerParams(dimension_semantics=("parallel", "arbitrary"),
                                             vmem_limit_bytes=VMEM_LIMIT_DSA_PROMPT),
        name="dsa_prompt",
    )(iq33, h3, ik33, dq3, dkb3, dvb3)


def _dsa_select_body(pt_ref, iq3_ref, misc_ref, ikn_ref, *rest, tq, npg, ns, n_sel, n_pages):
    pages = rest[:ns * npg]
    sel_ref, keys_scr, iqs = rest[ns * npg:]
    s_id = pl.program_id(1)
    n_steps = pl.num_programs(1)
    kw = iqs.shape[-1]
    seqs = range(ns)
    wts = [misc_ref[q][:, M_IDXW:M_IDXW + IDX_HEADS] * (IDX_HD ** -0.5 * IDX_HEADS ** -0.5) for q in seqs]

    @pl.when(s_id == 0)
    def _():
        for q in seqs:
            for hh in range(IDX_HEADS):
                iqs[q, hh * tq:(hh + 1) * tq, :] = iq3_ref[q, :, hh * kw:(hh + 1) * kw]

    def scores(q, ik, feature_major):
        hi, lo = _split_hi_lo(ik)
        a_hi = iqs[q, :, 0:IDX_HD]
        a_lo = iqs[q, :, 2 * IDX_HD:3 * IDX_HD]
        mm = _dot if feature_major else _dot_nt
        s = jnp.maximum(mm(a_hi, hi) + mm(a_hi, lo) + mm(a_lo, hi), 0.0)
        sc = wts[q][:, 0:1] * s[0:tq, :]
        for hh in range(1, IDX_HEADS):
            sc = sc + wts[q][:, hh:hh + 1] * s[hh * tq:(hh + 1) * tq, :]
        return sc

    @pl.when(s_id < n_steps - 1)
    def _():
        for q in seqs:
            ik = jnp.concatenate([pages[q * npg + p][...] for p in range(npg)], axis=1)
            keys = _sortable(scores(q, ik, True))
            for p in range(npg):
                keys_scr[q, s_id * npg + p] = keys[:, p * PAGE_SIZE:(p + 1) * PAGE_SIZE]

    @pl.when(s_id == n_steps - 1)
    def _():
        qrow = lax.broadcasted_iota(I32, (tq, tq), 0)
        kcol = lax.broadcasted_iota(I32, (tq, tq), 1)
        for q in seqs:
            new_keys = jnp.where(kcol <= qrow, _sortable(scores(q, ikn_ref[q], False)), INT_MIN)
            keys_scr[q, n_pages] = jnp.concatenate([new_keys, jnp.full((tq, LANES - tq), INT_MIN, I32)], axis=-1)

        def count(pred):
            acc = jnp.sum(jnp.where(pred(keys_scr[...]), 1, 0).astype(I32), axis=1)
            return jnp.sum(acc, axis=-1, keepdims=True)

        thr = _search_threshold(lambda cand: count(lambda kk: kk >= cand[:, None]), (ns, tq, 1), n_sel)
        n_ge = count(lambda kk: kk >= thr[:, None])
        plain = jnp.max(jnp.where(jnp.logical_and(n_ge == n_sel, thr != INT_MIN), 0, 1)) == 0

        @pl.when(plain)
        def _():
            sel_ref[...] = jnp.where(keys_scr[...] >= thr[:, None], 1.0, 0.0)

        @pl.when(jnp.logical_not(plain))
        def _():
            need = (n_sel - count(lambda kk: kk > thr[:, None])).astype(F32)
            before = _strictly_before(LANES)
            lane = lax.broadcasted_iota(I32, (tq, LANES), 1)
            qr = lax.broadcasted_iota(I32, (tq, LANES), 0)
            for q in seqs:
                def emit(c, eq_seen, q=q):
                    kk = keys_scr[q, c]
                    allowed = jnp.logical_or(c < n_pages, lane <= qr)
                    eq = jnp.logical_and(kk == thr[q], allowed)
                    eqf = jnp.where(eq, 1.0, 0.0)
                    rank = _dot(eqf.astype(BF16), before) + eq_seen
                    sel = jnp.logical_or(kk > thr[q], jnp.logical_and(eq, rank < need[q]))
                    sel_ref[q, c] = jnp.where(sel, 1.0, 0.0)
                    return eq_seen + jnp.sum(eqf, axis=-1, keepdims=True)

                lax.fori_loop(0, n_pages + 1, emit, jnp.zeros((tq, 1), F32))


def _dsa_attend_body(pt_ref, q_ref, kn_ref, vn_ref, selp_ref, seln_ref, *rest, tq, npg):
    kp = rest[:npg]
    vp = rest[npg:2 * npg]
    o_ref, qrows, m_scr, l_scr, acc_scr = rest[2 * npg:]
    s_id = pl.program_id(1)
    n_steps = pl.num_programs(1)
    hpg = DSA_HEADS // DSA_KV_HEADS
    grows = hpg * tq

    @pl.when(s_id == 0)
    def _():
        _reset_softmax(m_scr, l_scr, acc_scr)
        for hd in range(DSA_HEADS):
            qrows[hd * tq:(hd + 1) * tq, :] = q_ref[:, hd * DSA_HD:(hd + 1) * DSA_HD].astype(F32)

    qb = qrows[...].astype(BF16)
    qg = [qb[g * grows:(g + 1) * grows, :] for g in range(DSA_KV_HEADS)]

    def update(s, pv_fn):
        m_old = m_scr[...]
        m_new = jnp.maximum(m_old, jnp.max(s, axis=-1, keepdims=True))
        alpha = jnp.exp(m_old - m_new)
        p = jnp.exp(s - m_new)
        l_scr[...] = alpha * l_scr[...] + jnp.sum(p, axis=-1, keepdims=True)
        pb = p.astype(BF16)
        pv = jnp.concatenate([pv_fn(pb[g * grows:(g + 1) * grows, :], g) for g in range(DSA_KV_HEADS)], axis=0)
        acc_scr[...] = alpha * acc_scr[...] + pv
        m_scr[...] = m_new

    @pl.when(s_id < n_steps - 1)
    def _():
        kcat = [jnp.concatenate([kp[p][g] for p in range(npg)], axis=1).astype(BF16) for g in range(DSA_KV_HEADS)]
        vcat = [jnp.concatenate([vp[p][g] for p in range(npg)], axis=1).astype(BF16) for g in range(DSA_KV_HEADS)]
        s = jnp.concatenate([_dot(qg[g], kcat[g]) for g in range(DSA_KV_HEADS)], axis=0)
        sel1 = jnp.concatenate([selp_ref[p] for p in range(npg)], axis=1)
        sel = jnp.concatenate([sel1] * DSA_HEADS, axis=0) > 0.5
        update(jnp.where(sel, s, NEG_BIG), lambda pb, g: _dot_nt(pb, vcat[g]))

    @pl.when(s_id == n_steps - 1)
    def _():
        kn = kn_ref[...].astype(BF16)
        vn = vn_ref[...].astype(BF16)
        s = jnp.concatenate([_dot_nt(qg[g], kn[:, g * DSA_HD:(g + 1) * DSA_HD]) for g in range(DSA_KV_HEADS)], axis=0)
        sel = jnp.concatenate([seln_ref[...][:, 0:tq]] * DSA_HEADS, axis=0) > 0.5
        update(jnp.where(sel, s, NEG_BIG), lambda pb, g: _dot(pb, vn[:, g * DSA_HD:(g + 1) * DSA_HD]))
        acc = acc_scr[...] / l_scr[...]
        for hd in range(DSA_HEADS):
            o_ref[:, hd * DSA_HD:(hd + 1) * DSA_HD] = acc[hd * tq:(hd + 1) * tq, :].astype(o_ref.dtype)


def _dsa_sample_call(iq33, ik3, dq3, dk3, h3, pool_k, pool_v, pool_ik, page_table, layer):
    db, t, _ = dq3.shape
    n_pages = page_table.shape[1]
    n_keys = n_pages * PAGE_SIZE + t
    n_sel = min(TOPK_MAX, n_keys // 4)
    npg = _pages_per_step(n_pages)
    n_steps = n_pages // npg + 1
    kw = iq33.shape[-1] // IDX_HEADS
    sw = LANES

    def page_spec(*tail):
        def make(p):
            def imap(i, s, pt):
                return (layer, pt[i, jnp.minimum(s, n_steps - 2) * npg + p]) + (0,) * len(tail)
            return pl.BlockSpec((None, None) + tail, imap)
        return [make(p) for p in range(npg)]

    ns = 4 if db % 4 == 0 else 1

    def select_pages():
        def make(q, p):
            def imap(i, s, pt):
                return (layer, pt[i * ns + q, jnp.minimum(s, n_steps - 2) * npg + p], 0, 0)
            return pl.BlockSpec((None, None, IDX_HD, PAGE_SIZE), imap)
        return [make(q, p) for q in range(ns) for p in range(npg)]

    sel = pl.pallas_call(
        functools.partial(_dsa_select_body, tq=t, npg=npg, ns=ns, n_sel=n_sel, n_pages=n_pages),
        grid_spec=pltpu.PrefetchScalarGridSpec(
            num_scalar_prefetch=1,
            grid=(db // ns, n_steps),
            in_specs=[pl.BlockSpec((ns, t, IDX_HEADS * kw), lambda i, s, pt: (i, 0, 0)),
                      pl.BlockSpec((ns, t, LANES), lambda i, s, pt: (i, 0, C_MISC // LANES)),
                      pl.BlockSpec((ns, t, IDX_HD), lambda i, s, pt: (i, 0, 0))]
                     + select_pages(),
            out_specs=pl.BlockSpec((ns, n_pages + 1, t, sw), lambda i, s, pt: (i, 0, 0, 0)),
            scratch_shapes=[pltpu.VMEM((ns, n_pages + 1, t, LANES), I32), pltpu.VMEM((ns, IDX_HEADS * t, kw), BF16)],
        ),
        out_shape=jax.ShapeDtypeStruct((db, n_pages + 1, t, sw), F32),
        compiler_params=_cparams("parallel", "arbitrary"),
        name="dsa_sample_select",
    )(page_table, iq33, h3, ik3, *([pool_ik] * (ns * npg)))

    nrow = DSA_HEADS * t
    return pl.pallas_call(
        functools.partial(_dsa_attend_body, tq=t, npg=npg),
        grid_spec=pltpu.PrefetchScalarGridSpec(
            num_scalar_prefetch=1,
            grid=(db, n_steps),
            in_specs=[pl.BlockSpec((None, t, MIX_WIDTH), lambda i, s, pt: (i, 0, 0)),
                      pl.BlockSpec((None, t, LANES), lambda i, s, pt: (i, 0, 0)),
                      pl.BlockSpec((None, t, LANES), lambda i, s, pt: (i, 0, C_DSAV // LANES)),
                      pl.BlockSpec((None, npg, t, sw), lambda i, s, pt: (i, jnp.minimum(s, n_steps - 2), 0, 0)),
                      pl.BlockSpec((None, None, t, sw), lambda i, s, pt: (i, n_pages, 0, 0))]
                     + page_spec(DSA_KV_HEADS, DSA_HD, PAGE_SIZE) + page_spec(DSA_KV_HEADS, DSA_HD, PAGE_SIZE),
            out_specs=pl.BlockSpec((None, t, MIX_WIDTH), lambda i, s, pt: (i, 0, 0)),
            scratch_shapes=[pltpu.VMEM((nrow, DSA_HD), F32), pltpu.VMEM((nrow, 1), F32), pltpu.VMEM((nrow, 1), F32),
                            pltpu.VMEM((nrow, DSA_HD), F32)],
        ),
        out_shape=jax.ShapeDtypeStruct((db, t, MIX_WIDTH), BF16),
        compiler_params=_cparams("parallel", "arbitrary"),
        name="dsa_sample_attend",
    )(page_table, dq3, dk3, h3, sel, sel, *([pool_k] * npg), *([pool_v] * npg))


def _merge_body(x_ref, gate_ref, oa_ref, ob_ref, oc_ref, od_ref, wb_ref, wo_ref, o_ref):
    acc = None
    for bi, br in enumerate((oa_ref, ob_ref, oc_ref, od_ref)):
        proj = _dot(br[...], wb_ref[bi])
        term = _sigmoid(gate_ref[:, bi * D_MODEL:(bi + 1) * D_MODEL]) * proj
        acc = term if acc is None else acc + term
    o_ref[...] = x_ref[...] + _dot(acc.astype(BF16), wo_ref[...])


def _merge_call(x, h, branches, wl):
    n, d = x.shape
    tm = _tile(n, 256)
    br = pl.BlockSpec((tm, MIX_WIDTH), lambda i: (i, 0))
    return pl.pallas_call(
        _merge_body,
        grid=(n // tm,),
        in_specs=[pl.BlockSpec((tm, d), lambda i: (i, 0)),
                  pl.BlockSpec((tm, N_BRANCH * d), lambda i: (i, C_GATE // (N_BRANCH * d))),
                  br, br, br, br,
                  pl.BlockSpec((N_BRANCH, MIX_WIDTH, d), lambda i: (0, 0, 0)),
                  pl.BlockSpec((d, d), lambda i: (0, 0))],
        out_specs=pl.BlockSpec((tm, d), lambda i: (i, 0)),
        out_shape=jax.ShapeDtypeStruct((n, d), F32),
        compiler_params=_cparams("parallel"),
        name="merge",
    )(x, h, *branches, wl["w_branch"], wl["w_out"])


def _cross_body(x_ref, g_ref, wq_ref, mk_ref, mv_ref, wo_ref, o_ref, *, merged):
    x = x_ref[...]
    tm = x.shape[0]
    xn = _rms_rows(x, g_ref[...]).astype(BF16)
    q = _dot(xn, wq_ref[...]).astype(BF16)
    if merged:
        qrows = jnp.concatenate([q[:, hh * XA_HD:(hh + 1) * XA_HD] for hh in range(XA_HEADS)], axis=0)
        s = _dot_nt(qrows, mk_ref[...].astype(BF16)) * (XA_HD ** -0.5)
        rr = lax.broadcasted_iota(I32, s.shape, 0) // tm
        cc = lax.broadcasted_iota(I32, s.shape, 1) % XA_HEADS
        s = jnp.where(rr == cc, s, NEG_BIG)
        p = jnp.exp(s - jnp.max(s, axis=-1, keepdims=True))
        p = p / jnp.sum(p, axis=-1, keepdims=True)
        orow = _dot(p.astype(BF16), mv_ref[...].astype(BF16))
        outs = [orow[hh * tm:(hh + 1) * tm, :] for hh in range(XA_HEADS)]
    else:
        outs = []
        for hh in range(XA_HEADS):
            sl = slice(hh * XA_HD, (hh + 1) * XA_HD)
            s = _dot_nt(q[:, sl], mk_ref[:, sl].astype(BF16)) * (XA_HD ** -0.5)
            p = jnp.exp(s - jnp.max(s, axis=-1, keepdims=True))
            p = p / jnp.sum(p, axis=-1, keepdims=True)
            outs.append(_dot(p.astype(BF16), mv_ref[:, sl].astype(BF16)))
    o = jnp.concatenate(outs, axis=-1).astype(BF16)
    o_ref[...] = x + _dot(o, wo_ref[...])


def _cross_call(x3, wl, mk4, mv4, layer):
    b, t, d = x3.shape
    w = XA_HEADS * XA_HD
    tm = _tile(t, 512)
    merged = mk4.shape[-1] == XA_HD
    mem = pl.BlockSpec((None, None) + mk4.shape[2:], lambda i, j: (layer, i, 0, 0))
    return pl.pallas_call(
        functools.partial(_cross_body, merged=merged),
        grid=(b, t // tm),
        in_specs=[pl.BlockSpec((None, tm, d), lambda i, j: (i, j, 0)),
                  pl.BlockSpec((1, d), lambda i, j: (0, 0)),
                  pl.BlockSpec((d, w), lambda i, j: (0, 0)), mem, mem,
                  pl.BlockSpec((w, d), lambda i, j: (0, 0))],
        out_specs=pl.BlockSpec((None, tm, d), lambda i, j: (i, j, 0)),
        out_shape=jax.ShapeDtypeStruct((b, t, d), F32),
        compiler_params=_cparams("parallel", "parallel"),
        name="cross_attn",
    )(x3, wl["norm_xa_g"], wl["xa_wq"], mk4, mv4, wl["xa_wo"])


def _ffn_body(x_ref, g_ref, wg_ref, wu_ref, wd_ref, o_ref, xn_scr, acc_scr):
    f = pl.program_id(1)

    @pl.when(f == 0)
    def _():
        xn_scr[...] = _rms_rows(x_ref[...], g_ref[...]).astype(BF16)
        acc_scr[...] = x_ref[...]

    xn = xn_scr[...]
    hm = _silu(_dot(xn, wg_ref[...])) * _dot(xn, wu_ref[...])
    acc_scr[...] += _dot(hm.astype(BF16), wd_ref[...])

    @pl.when(f == pl.num_programs(1) - 1)
    def _():
        o_ref[...] = acc_scr[...]


def _ffn_call(x, wl):
    n, d = x.shape
    ff = wl["ffn_wg"].shape[1]
    tm = _tile(n, 1024)
    tf = _tile(ff, 1408)
    return pl.pallas_call(
        _ffn_body,
        grid=(n // tm, ff // tf),
        in_specs=[pl.BlockSpec((tm, d), lambda i, f: (i, 0)), pl.BlockSpec((1, d), lambda i, f: (0, 0)),
                  pl.BlockSpec((d, tf), lambda i, f: (0, f)), pl.BlockSpec((d, tf), lambda i, f: (0, f)),
                  pl.BlockSpec((tf, d), lambda i, f: (f, 0))],
        out_specs=pl.BlockSpec((tm, d), lambda i, f: (i, 0)),
        out_shape=jax.ShapeDtypeStruct((n, d), F32),
        scratch_shapes=[pltpu.VMEM((tm, d), BF16), pltpu.VMEM((tm, d), F32)],
        compiler_params=_cparams("parallel", "arbitrary"),
        name="ffn_swiglu",
    )(x, wl["norm_ffn_g"], wl["ffn_wg"], wl["ffn_wu"], wl["ffn_wd"])


def _moe_body(x_ref, g_ref, wr_ref, br_ref, wg_ref, wu_ref, wd_ref, o_ref, xn_scr, gate_scr, acc_scr):
    e = pl.program_id(1)
    f = pl.program_id(2)
    first = jnp.logical_and(e == 0, f == 0)

    @pl.when(first)
    def _():
        xn = _rms_rows(x_ref[...], g_ref[...])
        xn_scr[...] = xn.astype(BF16)
        acc_scr[...] = x_ref[...]
        logits = _dot_hi(xn, wr_ref[...]) + br_ref[...]
        lane = lax.broadcasted_iota(I32, logits.shape, 1)
        logits = jnp.where(lane < N_EXPERTS, logits, -jnp.inf)
        top1 = jnp.max(logits, axis=-1, keepdims=True)
        idx1 = jnp.min(jnp.where(logits == top1, lane, LANES), axis=-1, keepdims=True)
        rest = jnp.where(lane == idx1, -jnp.inf, logits)
        top2 = jnp.max(rest, axis=-1, keepdims=True)
        idx2 = jnp.min(jnp.where(rest == top2, lane, LANES), axis=-1, keepdims=True)
        e2 = jnp.exp(top2 - top1)
        p1 = 1.0 / (1.0 + e2)
        p2 = e2 / (1.0 + e2)
        gate_scr[...] = jnp.where(lane == idx1, p1, 0.0) + jnp.where(lane == idx2, p2, 0.0)

    gates = gate_scr[...]
    lane = lax.broadcasted_iota(I32, gates.shape, 1)
    ge = jnp.sum(jnp.where(lane == e, gates, 0.0), axis=-1, keepdims=True)
    xn = xn_scr[...]
    hm = _silu(_dot(xn, wg_ref[...])) * _dot(xn, wu_ref[...])
    acc_scr[...] += ge * _dot(hm.astype(BF16), wd_ref[...])

    @pl.when(jnp.logical_and(e == pl.num_programs(1) - 1, f == pl.num_programs(2) - 1))
    def _():
        o_ref[...] = acc_scr[...]


def _moe_call(x, wl):
    n, d = x.shape
    ne, _, fe = wl["moe_wg"].shape
    tm = _tile(n, 1024)
    tf = _tile(fe, 1024)
    return pl.pallas_call(
        _moe_body,
        grid=(n // tm, ne, fe // tf),
        in_specs=[pl.BlockSpec((tm, d), lambda i, e, f: (i, 0)), pl.BlockSpec((1, d), lambda i, e, f: (0, 0)),
                  pl.BlockSpec((d, LANES), lambda i, e, f: (0, 0)), pl.BlockSpec((1, LANES), lambda i, e, f: (0, 0)),
                  pl.BlockSpec((None, d, tf), lambda i, e, f: (e, 0, f)),
                  pl.BlockSpec((None, d, tf), lambda i, e, f: (e, 0, f)),
                  pl.BlockSpec((None, tf, d), lambda i, e, f: (e, f, 0))],
        out_specs=pl.BlockSpec((tm, d), lambda i, e, f: (i, 0)),
        out_shape=jax.ShapeDtypeStruct((n, d), F32),
        scratch_shapes=[pltpu.VMEM((tm, d), BF16), pltpu.VMEM((tm, LANES), F32), pltpu.VMEM((tm, d), F32)],
        compiler_params=_cparams("parallel", "arbitrary", "arbitrary"),
        name="moe_swiglu",
    )(x, wl["norm_ffn_g"], wl["moe_wr"], wl["moe_br"], wl["moe_wg"], wl["moe_wu"], wl["moe_wd"])


def _pad_rows(a, rows):
    return jnp.pad(a, ((0, rows - a.shape[0]), (0, 0)))


def _lane_row(vals, lane0):
    return jnp.zeros((1, LANES), F32).at[0, lane0:lane0 + vals.shape[0]].set(vals.astype(F32))


def _block_diag(w):
    nb, n, _ = w.shape
    eye = jnp.eye(nb, dtype=w.dtype)
    return jnp.einsum("aij,ab->aibj", w, eye).reshape(nb * n, nb * n)


def _layer_weights(l, p):
    w_t = jnp.transpose(p["w_in"], (2, 0, 1))[:, l, :]
    d = w_t.shape[1]
    misc = jnp.concatenate([w_t[O_IDXK:O_IDXK + IDX_HD + IDX_HEADS], w_t[O_GDNB:O_GDNB + 2 * GDN_HEADS],
                            jnp.zeros((LANES - IDX_HD - IDX_HEADS - 2 * GDN_HEADS, d), w_t.dtype)], axis=0)
    w_in_r = jnp.concatenate([
        w_t[O_GATE:], w_t[O_LRUX:O_DSAK], w_t[O_DIFQ:O_GATE], w_t[O_GDNZ:O_GDNB],
        w_t[O_GDNQKV:O_GDNZ], w_t[O_IDXQ:O_IDXK], w_t[O_DSAK:O_IDXQ], misc], axis=0)
    assert w_in_r.shape[0] == H_COLS
    wl = dict(
        norm_mix_g=p["norm_mix_g"][l], w_in=w_in_r.astype(BF16),
        lru_cw=_pad_rows(p["lru_conv_w"][l], SUBLANES), lru_cb=p["lru_conv_b"][l].reshape(1, -1),
        lru_wa=_block_diag(p["lru_wa"][l]).astype(BF16), lru_ba=p["lru_ba"][l].reshape(1, -1),
        lru_wx=_block_diag(p["lru_wx"][l]).astype(BF16), lru_bx=p["lru_bx"][l].reshape(1, -1),
        lru_lam=p["lru_lambda"][l].reshape(1, -1),
        gdn_cw=_pad_rows(p["gdn_conv_w"][l], SUBLANES),
        gdn_alog=_lane_row(p["gdn_a_log"][l], M_GDNA), gdn_dtb=_lane_row(p["gdn_dt_bias"][l], M_GDNA),
        gdn_ng=p["gdn_norm_g"][l].reshape(1, -1),
        diff_lam=p["diff_lambda"][l], diff_sg=p["diff_subln_g"][l].reshape(1, -1),
        w_branch=p["w_branch"][l].astype(BF16), w_out=p["w_out"][l].astype(BF16),
        norm_xa_g=p["norm_xa_g"][l].reshape(1, -1), norm_mem_g=p["norm_mem_g"][l],
        xa_wq=p["xa_wq"][l].astype(BF16), xa_wo=p["xa_wo"][l].astype(BF16),
        xa_wkv=jnp.concatenate([p["xa_wk"][l], p["xa_wv"][l]], axis=1).astype(BF16),
        norm_ffn_g=p["norm_ffn_g"][l].reshape(1, -1),
    )
    j = l // 2
    if l % 2 == 0:
        wl.update(ffn_wg=p["ffn_w_gate"][j].astype(BF16), ffn_wu=p["ffn_w_up"][j].astype(BF16),
                  ffn_wd=p["ffn_w_down"][j].astype(BF16))
    else:
        wl.update(moe_wr=jnp.pad(p["moe_router_w"][j], ((0, 0), (0, LANES - N_EXPERTS))),
                  moe_br=jnp.pad(p["moe_router_b"][j], (0, LANES - N_EXPERTS)).reshape(1, LANES),
                  moe_wg=p["moe_w_gate"][j].astype(BF16), moe_wu=p["moe_w_up"][j].astype(BF16),
                  moe_wd=p["moe_w_down"][j].astype(BF16))
    return wl


def _rope_tables(pos):
    half = DSA_HD // 2
    inv = ROPE_THETA ** (-jnp.arange(half, dtype=F32) / half)
    ang = pos.astype(F32)[:, None] * inv[None, :]
    cos, sin = jnp.cos(ang), jnp.sin(ang)
    return jnp.tile(cos, (1, 4)), jnp.tile(jnp.concatenate([-sin, sin], axis=1), (1, 2))


def _front_pad(buf):
    return jnp.pad(buf, ((0, 0), (SUBLANES - (CONV_W - 1), 0), (0, 0)))


def _group_layer(x3, l, wl, tabs, states, mem_kv, paged):
    b, t, d = x3.shape
    n = b * t
    x = x3.reshape(n, d)
    lam_init = 0.8 - 0.6 * math.exp(-0.3 * l)
    xn = _rmsnorm(x, wl["norm_mix_g"], BF16)
    h = _matmul(xn, wl["w_in"], tn=1408, w_rows_are_outputs=True)
    h3 = h.reshape(b, t, H_COLS)
    dq, fq, fk, fkb, fvb, iq3, dk, dkb, dvb, ik, ik3 = _rope_call(h, tabs[0], tabs[1])
    to3 = lambda a: a.reshape(b, t, a.shape[-1])
    dq3, fq3, fk3, dk3, ik3d = map(to3, (dq, fq, fk, dk, ik))

    o_a, lru_h = _lru_call(h3, wl, _front_pad(states["lru_conv"]), states["lru_h"].reshape(b, 1, -1),
                           reset_first=paged is None)
    o_c, gdn_s = _gdn_call(h3, wl, _front_pad(states["gdn_conv"]), states["gdn_s"])
    if paged is None:
        o_b = _dsa_prompt_call(to3(iq3), to3(ik3), dq3, to3(dkb), to3(dvb), h3)
        o_d = _diff_prompt_call(fq3, to3(fkb), to3(fvb), wl, lam_init)
    else:
        o_b = _dsa_sample_call(to3(iq3), ik3d, dq3, dk3, h3, paged["dsa_k"], paged["dsa_v"], paged["idx_k"],
                               paged["page_table"], l)
        o_d = _diff_sample_call(fq3, fk3, h3, paged["diff_k"], paged["diff_v"], paged["page_table"], l, wl, lam_init)

    flat = lambda a: a.reshape(n, a.shape[-1])
    x1 = _merge_call(x, h, [flat(o_a), flat(o_b), flat(o_c), flat(o_d)], wl)
    x2 = _cross_call(x1.reshape(b, t, d), wl, mem_kv[0], mem_kv[1], mem_kv[2])
    x2 = x2.reshape(n, d)
    x3_new = (_ffn_call(x2, wl) if "ffn_wg" in wl else _moe_call(x2, wl)).reshape(b, t, d)

    new = dict(
        dsa_k=dk3.reshape(b, t, DSA_KV_HEADS, DSA_HD),
        dsa_v=h3[:, :, C_DSAV:C_DSAV + DSA_KV_HEADS * DSA_HD].reshape(b, t, DSA_KV_HEADS, DSA_HD),
        idx_k=ik3d,
        diff_k=fk3.reshape(b, t, DIFF_HEADS, 2 * DIFF_HD),
        diff_v=h3[:, :, C_DIFV:C_DIFV + MIX_WIDTH].reshape(b, t, DIFF_HEADS, 2 * DIFF_HD),
        lru_h=lru_h.reshape(b, -1),
        lru_conv=jnp.concatenate([states["lru_conv"], h3[:, :, C_LRUX:C_LRUX + MIX_WIDTH]], axis=1)[:, -(CONV_W - 1):]
        if t < CONV_W - 1 else h3[:, t - (CONV_W - 1):, C_LRUX:C_LRUX + MIX_WIDTH],
        gdn_s=gdn_s,
        gdn_conv=jnp.concatenate([states["gdn_conv"], h3[:, :, C_GDNQKV:C_GDNQKV + GDN_QKV]], axis=1)[:, -(CONV_W - 1):]
        if t < CONV_W - 1 else h3[:, t - (CONV_W - 1):, C_GDNQKV:C_GDNQKV + GDN_QKV],
    )
    return x3_new, new


def kernel(x_prompt, x_sample, mem_prompt, cache_dsa_k, cache_dsa_v, cache_idx_k, cache_diff_k, cache_diff_v,
           cache_mem_k, cache_mem_v, state_lru_h, state_lru_conv, state_gdn_s, state_gdn_conv, page_table,
           norm_mix_g, w_in, lru_conv_w, lru_conv_b, lru_wa, lru_ba, lru_wx, lru_bx, lru_lambda,
           gdn_conv_w, gdn_a_log, gdn_dt_bias, gdn_norm_g, diff_lambda, diff_subln_g, w_branch, w_out,
           norm_xa_g, norm_mem_g, xa_wq, xa_wk, xa_wv, xa_wo, norm_ffn_g, ffn_w_gate, ffn_w_up, ffn_w_down,
           moe_router_w, moe_router_b, moe_w_gate, moe_w_up, moe_w_down, final_norm_g):
    params = dict(
        norm_mix_g=norm_mix_g, w_in=w_in, lru_conv_w=lru_conv_w, lru_conv_b=lru_conv_b, lru_wa=lru_wa, lru_ba=lru_ba,
        lru_wx=lru_wx, lru_bx=lru_bx, lru_lambda=lru_lambda, gdn_conv_w=gdn_conv_w, gdn_a_log=gdn_a_log,
        gdn_dt_bias=gdn_dt_bias, gdn_norm_g=gdn_norm_g, diff_lambda=diff_lambda, diff_subln_g=diff_subln_g,
        w_branch=w_branch, w_out=w_out, norm_xa_g=norm_xa_g, norm_mem_g=norm_mem_g, xa_wq=xa_wq, xa_wk=xa_wk,
        xa_wv=xa_wv, xa_wo=xa_wo, norm_ffn_g=norm_ffn_g, ffn_w_gate=ffn_w_gate, ffn_w_up=ffn_w_up,
        ffn_w_down=ffn_w_down, moe_router_w=moe_router_w, moe_router_b=moe_router_b, moe_w_gate=moe_w_gate,
        moe_w_up=moe_w_up, moe_w_down=moe_w_down)
    b, s, d = x_prompt.shape
    db, t, _ = x_sample.shape
    m = mem_prompt.shape[1]
    n_pool = cache_dsa_k.shape[1]
    past = page_table.shape[1] * PAGE_SIZE

    tabs_p = _rope_tables(jnp.arange(s))
    cs, ss = _rope_tables(past + jnp.arange(t))
    tabs_s = (jnp.tile(cs, (db, 1)), jnp.tile(ss, (db, 1)))
    paged = dict(
        dsa_k=jnp.transpose(cache_dsa_k, (0, 1, 3, 4, 2)), dsa_v=jnp.transpose(cache_dsa_v, (0, 1, 3, 4, 2)),
        idx_k=jnp.transpose(cache_idx_k, (0, 1, 3, 2)), page_table=page_table,
        diff_k=cache_diff_k.reshape(DEPTH, n_pool, PAGE_SIZE * DIFF_HEADS, 2 * DIFF_HD),
        diff_v=cache_diff_v.reshape(DEPTH, n_pool, PAGE_SIZE * DIFF_HEADS, 2 * DIFF_HD))
    mem_k_s = cache_mem_k.reshape(DEPTH, db, m * XA_HEADS, XA_HD)
    mem_v_s = cache_mem_v.reshape(DEPTH, db, m * XA_HEADS, XA_HD)
    zero_states = dict(lru_conv=jnp.zeros((b, CONV_W - 1, MIX_WIDTH), F32), lru_h=jnp.zeros((b, MIX_WIDTH), F32),
                       gdn_conv=jnp.zeros((b, CONV_W - 1, GDN_QKV), F32),
                       gdn_s=jnp.zeros((b, GDN_HEADS, GDN_HD, GDN_HD), F32))

    hp, hs = x_prompt, x_sample
    new_p, new_s, mem_ks, mem_vs = [], [], [], []
    for l in range(DEPTH):
        wl = _layer_weights(l, params)
        mem_n = _rmsnorm(mem_prompt.reshape(b * m, d), wl["norm_mem_g"], BF16)
        kv = _matmul(mem_n, wl["xa_wkv"], tn=512)
        w = XA_HEADS * XA_HD
        mk = kv[:, :w].reshape(1, b, m, w)
        mv = kv[:, w:].reshape(1, b, m, w)
        hp, st_p = _group_layer(hp, l, wl, tabs_p, zero_states, (mk, mv, 0), None)
        new_p.append(st_p)
        mem_ks.append(mk.reshape(b, m, XA_HEADS, XA_HD))
        mem_vs.append(mv.reshape(b, m, XA_HEADS, XA_HD))
        states_s = dict(lru_conv=state_lru_conv[l], lru_h=state_lru_h[l], gdn_conv=state_gdn_conv[l],
                        gdn_s=state_gdn_s[l])
        hs, st_s = _group_layer(hs, l, wl, tabs_s, states_s, (mem_k_s, mem_v_s, l), paged)
        new_s.append(st_s)

    y_prompt = _rmsnorm(hp.reshape(b * s, d), final_norm_g, F32).reshape(b, s, d)
    y_sample = _rmsnorm(hs.reshape(db * t, d), final_norm_g, F32).reshape(db, t, d)
    stack = lambda states, name: jnp.stack([st[name] for st in states], axis=0)
    outs = [y_prompt, y_sample]
    for name in ("dsa_k", "dsa_v", "idx_k", "diff_k", "diff_v", "lru_h", "lru_conv", "gdn_s", "gdn_conv"):
        outs += [stack(new_p, name), stack(new_s, name)]
    outs += [jnp.stack(mem_ks, axis=0), jnp.stack(mem_vs, axis=0)]
    return tuple(outs)
```

```python
import functools
import math

import jax
import jax.numpy as jnp
from jax import lax
from jax.experimental import pallas as pl
from jax.experimental.pallas import tpu as pltpu

F32 = jnp.float32
BF16 = jnp.bfloat16
I32 = jnp.int32
I16 = jnp.int16

D_MODEL = 1024
DEPTH = 2
PAGE_SIZE = 128
MIX_WIDTH = 512
LRU_BLOCKS = 8
LRU_BLOCK = MIX_WIDTH // LRU_BLOCKS
LRU_C = 8.0
CONV_W = 4
DSA_HEADS = 8
DSA_KV_HEADS = 2
DSA_HD = 64
IDX_HEADS = 4
IDX_HD = 64
TOPK_MAX = 256
GDN_HEADS = 4
GDN_HD = 128
GDN_CHUNK = 128
DIFF_HEADS = 4
DIFF_HD = 64
XA_HEADS = 4
XA_HD = 128
N_EXPERTS = 8
ROPE_THETA = 10000.0
EPS = 1e-6
N_BRANCH = 4
GDN_QKV = 3 * GDN_HEADS * GDN_HD

LANES = 128
SUBLANES = 8
NEG_BIG = -1e30
INT_MIN = -2 ** 31
HALF_BIAS = 2 ** 15
VMEM_LIMIT = 48 * 1024 * 1024
VMEM_LIMIT_DSA_PROMPT = 58 * 1024 * 1024

C_GATE = 0
C_LRUX = 4096
C_LRUG = 4608
C_DSAQ = 5120
C_DIFQ = 5632
C_DIFK = 6144
C_DIFV = 6656
C_GDNZ = 7168
C_GDNQKV = 7680
C_IDXQ = 9216
C_DSAK = 9472
C_DSAV = 9600
C_MISC = 9728
H_COLS = 9856
M_IDXK = 0
M_IDXW = 64
M_GDNB = 68
M_GDNA = 72

O_LRUX, O_LRUG, O_DSAQ, O_DSAK, O_DSAV, O_IDXQ, O_IDXK, O_IDXW = 0, 512, 1024, 1536, 1664, 1792, 2048, 2112
O_GDNQKV, O_GDNZ, O_GDNB, O_GDNA, O_DIFQ, O_DIFK, O_DIFV, O_GATE = 2116, 3652, 4164, 4168, 4172, 4684, 5196, 5708


def _cparams(*sem):
    return pltpu.CompilerParams(dimension_semantics=sem, vmem_limit_bytes=VMEM_LIMIT)


def _tile(n, pref):
    return pref if n % pref == 0 else n


def _pages_per_step(n_pages):
    return next(p for p in (16, 8, 1) if n_pages % p == 0)


def _softplus(x):
    return jnp.maximum(x, 0.0) + jnp.log1p(jnp.exp(-jnp.abs(x)))


def _sigmoid(x):
    return 1.0 / (1.0 + jnp.exp(-x))


def _silu(x):
    return x * _sigmoid(x)


def _dot(a, b):
    return jnp.dot(a, b, preferred_element_type=F32)


def _dot_nt(a, b):
    return lax.dot_general(a, b, (((1,), (1,)), ((), ())), preferred_element_type=F32)


def _dot_hi(a, b):
    return jnp.dot(a, b, preferred_element_type=F32, precision=lax.Precision.HIGHEST)


def _split_hi_lo(x):
    hi = x.astype(BF16)
    return hi, (x - hi.astype(F32)).astype(BF16)


_NN = (((1,), (0,)), ((), ()))
_NT = (((1,), (1,)), ((), ()))
_TN = (((0,), (0,)), ((), ()))


def _dot3(a, b, dims=_NN):
    a_hi, a_lo = _split_hi_lo(a)
    b_hi, b_lo = _split_hi_lo(b)
    dg = functools.partial(lax.dot_general, dimension_numbers=dims, preferred_element_type=F32)
    ca, cb = dims[0][0][0], dims[0][1][0]
    if a.shape[ca] % LANES == 0:
        return dg(jnp.concatenate([a_hi, a_hi, a_lo], axis=ca), jnp.concatenate([b_hi, b_lo, b_hi], axis=cb))
    return dg(a_hi, b_hi) + dg(a_hi, b_lo) + dg(a_lo, b_hi)


def _dot_exact_lhs(a, b):
    ab = a.astype(BF16)
    b1 = b.astype(BF16)
    r1 = b - b1.astype(F32)
    b2 = r1.astype(BF16)
    b3 = (r1 - b2.astype(F32)).astype(BF16)
    if a.shape[1] % LANES == 0:
        return _dot(jnp.concatenate([ab, ab, ab], axis=1), jnp.concatenate([b1, b2, b3], axis=0))
    return _dot(ab, b1) + _dot(ab, b2) + _dot(ab, b3)


def _rms_rows(x, g):
    return x * lax.rsqrt(jnp.mean(x * x, axis=-1, keepdims=True) + EPS) * g


def _rmsnorm_body(x_ref, g_ref, o_ref):
    o_ref[...] = _rms_rows(x_ref[...], g_ref[...]).astype(o_ref.dtype)


def _rmsnorm(x, g, out_dtype):
    n, d = x.shape
    tm = _tile(n, 512)
    return pl.pallas_call(
        _rmsnorm_body,
        grid=(n // tm,),
        in_specs=[pl.BlockSpec((tm, d), lambda i: (i, 0)), pl.BlockSpec((1, d), lambda i: (0, 0))],
        out_specs=pl.BlockSpec((tm, d), lambda i: (i, 0)),
        out_shape=jax.ShapeDtypeStruct((n, d), out_dtype),
        compiler_params=_cparams("parallel"),
        name="rmsnorm",
    )(x, g.reshape(1, d))


def _matmul_body(a_ref, w_ref, o_ref, *, w_rows_are_outputs):
    mm = _dot_nt if w_rows_are_outputs else _dot
    o_ref[...] = mm(a_ref[...], w_ref[...]).astype(o_ref.dtype)


def _matmul(a, w, tn, w_rows_are_outputs=False, out_dtype=F32):
    n, k = a.shape
    c = w.shape[0] if w_rows_are_outputs else w.shape[1]
    tm = _tile(n, 1024)
    w_spec = (pl.BlockSpec((tn, k), lambda j, i: (j, 0)) if w_rows_are_outputs
              else pl.BlockSpec((k, tn), lambda j, i: (0, j)))
    return pl.pallas_call(
        functools.partial(_matmul_body, w_rows_are_outputs=w_rows_are_outputs),
        grid=(c // tn, n // tm),
        in_specs=[pl.BlockSpec((tm, k), lambda j, i: (i, 0)), w_spec],
        out_specs=pl.BlockSpec((tm, tn), lambda j, i: (i, j)),
        out_shape=jax.ShapeDtypeStruct((n, c), out_dtype),
        compiler_params=_cparams("parallel", "parallel"),
        name="matmul",
    )(a, w)


def _rope_block(x, cos, sin):
    lane = lax.broadcasted_iota(I32, x.shape, 1)
    first = (lane % 64) < 32
    partner = jnp.where(first, pltpu.roll(x, LANES - 32, 1), pltpu.roll(x, 32, 1))
    return x * cos + partner * sin


def _rope_wide(x, cos, sin):
    return [_rope_block(x[:, c * LANES:(c + 1) * LANES], cos, sin) for c in range(x.shape[1] // LANES)]


def _rope_body(cos_ref, sin_ref, dq_ref, fq_ref, fk_ref, fv_ref, iq_ref, dk_ref, dv_ref, misc_ref,
               dq_o, fq_o, fk_o, fkb_o, fvb_o, iq3_o, dk_o, dkb_o, dvb_o, ik_o, ik3_o):
    cos = cos_ref[...]
    sin = sin_ref[...]
    lane = lax.broadcasted_iota(I32, cos.shape, 1)
    low = lane < IDX_HD
    for c, blk in enumerate(_rope_wide(dq_ref[...], cos, sin)):
        dq_o[:, c * LANES:(c + 1) * LANES] = (blk * DSA_HD ** -0.5).astype(BF16)
    for c, blk in enumerate(_rope_wide(fq_ref[...], cos, sin)):
        fq_o[:, c * LANES:(c + 1) * LANES] = (blk * DIFF_HD ** -0.5).astype(BF16)
    for c, blk in enumerate(_rope_wide(fk_ref[...], cos, sin)):
        fk_o[:, c * LANES:(c + 1) * LANES] = blk
        fkb_o[:, c * LANES:(c + 1) * LANES] = blk.astype(BF16)
    fvb_o[...] = fv_ref[...].astype(BF16)
    dk = _rope_block(dk_ref[...], cos, sin)
    dk_o[...] = dk
    dkb_o[...] = dk.astype(BF16)
    dv = dv_ref[...]
    dvb_o[:, 0:LANES] = jnp.where(low, dv, 1.0).astype(BF16)
    dvb_o[:, LANES:2 * LANES] = jnp.where(low, pltpu.roll(dv, DSA_HD, 1), 1.0).astype(BF16)
    for c, blk in enumerate(_rope_wide(iq_ref[...], cos, sin)):
        hi = blk.astype(BF16).astype(F32)
        lo = blk - hi
        hi_sw = pltpu.roll(hi, IDX_HD, 1)
        lo_sw = pltpu.roll(lo, IDX_HD, 1)
        base = 2 * c * 2 * LANES
        iq3_o[:, base:base + LANES] = jnp.where(low, hi, hi_sw).astype(BF16)
        iq3_o[:, base + LANES:base + 2 * LANES] = jnp.where(low, lo, 0.0).astype(BF16)
        iq3_o[:, base + 2 * LANES:base + 3 * LANES] = jnp.where(low, hi_sw, hi).astype(BF16)
        iq3_o[:, base + 3 * LANES:base + 4 * LANES] = jnp.where(low, lo_sw, 0.0).astype(BF16)
    ik = _rope_block(misc_ref[...], cos, sin)
    ik_o[...] = ik[:, M_IDXK:M_IDXK + IDX_HD]
    hi = ik.astype(BF16).astype(F32)
    lo = ik - hi
    ik3_o[:, 0:LANES] = jnp.where(low, hi, pltpu.roll(lo, IDX_HD, 1)).astype(BF16)
    ik3_o[:, LANES:2 * LANES] = jnp.where(low, hi, 0.0).astype(BF16)


def _rope_call(h, cos, sin):
    n = h.shape[0]
    tm = _tile(min(n, cos.shape[0]), 512)
    nt = cos.shape[0] // tm

    def col(width, off):
        return pl.BlockSpec((tm, width), lambda i: (i, off // width))

    def out(width):
        return pl.BlockSpec((tm, width), lambda i: (i, 0))

    sds = jax.ShapeDtypeStruct
    tab = pl.BlockSpec((tm, LANES), lambda i: (i % nt, 0))
    return pl.pallas_call(
        _rope_body,
        grid=(n // tm,),
        in_specs=[tab, tab, col(512, C_DSAQ), col(512, C_DIFQ), col(512, C_DIFK), col(512, C_DIFV),
                  col(256, C_IDXQ), col(128, C_DSAK), col(128, C_DSAV), col(128, C_MISC)],
        out_specs=[out(512), out(512), out(512), out(512), out(512), out(1024), out(128), out(128), out(256),
                   out(IDX_HD), out(256)],
        out_shape=[sds((n, 512), BF16), sds((n, 512), BF16), sds((n, 512), F32), sds((n, 512), BF16),
                   sds((n, 512), BF16), sds((n, 1024), BF16), sds((n, 128), F32), sds((n, 128), BF16),
                   sds((n, 256), BF16), sds((n, IDX_HD), F32), sds((n, 256), BF16)],
        compiler_params=_cparams("parallel"),
        name="rope",
    )(cos, sin, h, h, h, h, h, h, h, h)


def _conv_tile(x_ref, buf_ref, cw_ref, xcat, t, tc):
    @pl.when(t == 0)
    def _():
        xcat[0:SUBLANES, :] = buf_ref[...]

    @pl.when(t > 0)
    def _():
        xcat[0:SUBLANES, :] = xcat[tc:tc + SUBLANES, :]

    xcat[SUBLANES:SUBLANES + tc, :] = x_ref[...]
    cw = cw_ref[...]
    y = x_ref[...] * cw[CONV_W - 1:CONV_W, :]
    for j in range(CONV_W - 1):
        off = SUBLANES - (CONV_W - 1) + j
        y = y + xcat[off:off + tc, :] * cw[j:j + 1, :]
    return y


def _lru_body(x_ref, gate_ref, cw_ref, cb_ref, wa_ref, ba_ref, wx_ref, bx_ref, lam_ref, buf_ref, h0_ref,
              o_ref, hl_ref, xcat, hc, *, tc, reset_first):
    t = pl.program_id(1)

    @pl.when(t == 0)
    def _():
        hc[...] = h0_ref[...]

    xa = _conv_tile(x_ref, buf_ref, cw_ref, xcat, t, tc) + cb_ref[...]
    xb = xa.astype(BF16)
    r = _sigmoid(_dot(xb, wa_ref[...]) + ba_ref[...])
    gi = _sigmoid(_dot(xb, wx_ref[...]) + bx_ref[...])
    log_a = -LRU_C * r * _softplus(-lam_ref[...])
    a = jnp.exp(log_a)
    mult = jnp.sqrt(-jnp.tanh(log_a) * (a * a + 1.0))
    row = lax.broadcasted_iota(I32, a.shape, 0)
    if reset_first:
        is0 = jnp.logical_and(row == 0, t == 0)
        a = jnp.where(is0, 0.0, a)
        mult = jnp.where(is0, 1.0, mult)
    u = mult * gi * xa
    d = 1
    while d < tc:
        keep = row >= d
        a_sh = jnp.where(keep, pltpu.roll(a, d, 0), 1.0)
        u_sh = jnp.where(keep, pltpu.roll(u, d, 0), 0.0)
        u = a * u_sh + u
        a = a * a_sh
        d *= 2
    h = a * hc[...] + u
    h_last = h[tc - 1:tc, :]
    hc[...] = h_last
    hl_ref[...] = h_last
    o_ref[...] = (h * jax.nn.gelu(gate_ref[...])).astype(o_ref.dtype)


def _lru_call(h3, wl, buf, h0, reset_first):
    b, t, _ = h3.shape
    tc = _tile(t, 256)
    w = MIX_WIDTH
    row = pl.BlockSpec((1, w), lambda i, j: (0, 0))
    sq = pl.BlockSpec((w, w), lambda i, j: (0, 0))
    return pl.pallas_call(
        functools.partial(_lru_body, tc=tc, reset_first=reset_first),
        grid=(b, t // tc),
        in_specs=[pl.BlockSpec((None, tc, w), lambda i, j: (i, j, C_LRUX // w)),
                  pl.BlockSpec((None, tc, w), lambda i, j: (i, j, C_LRUG // w)),
                  pl.BlockSpec((SUBLANES, w), lambda i, j: (0, 0)), row, sq, row, sq, row, row,
                  pl.BlockSpec((None, SUBLANES, w), lambda i, j: (i, 0, 0)),
                  pl.BlockSpec((None, 1, w), lambda i, j: (i, 0, 0))],
        out_specs=[pl.BlockSpec((None, tc, w), lambda i, j: (i, j, 0)),
                   pl.BlockSpec((None, 1, w), lambda i, j: (i, 0, 0))],
        out_shape=[jax.ShapeDtypeStruct((b, t, w), BF16), jax.ShapeDtypeStruct((b, 1, w), F32)],
        scratch_shapes=[pltpu.VMEM((tc + SUBLANES, w), F32), pltpu.VMEM((1, w), F32)],
        compiler_params=_cparams("parallel", "arbitrary"),
        name="rg_lru",
    )(h3, h3, wl["lru_cw"], wl["lru_cb"], wl["lru_wa"], wl["lru_ba"], wl["lru_wx"], wl["lru_bx"], wl["lru_lam"],
      buf, h0)


def _gdn_body(qkv_ref, z_ref, misc_ref, cw_ref, alog_ref, dtb_ref, ng_ref, buf_ref, s0_ref,
              o_ref, so_ref, xcat, y_scr, state, *, tc, ck):
    t = pl.program_id(1)
    hd = GDN_HD

    @pl.when(t == 0)
    def _():
        state[...] = s0_ref[...]

    y = _conv_tile(qkv_ref, buf_ref, cw_ref, xcat, t, tc)
    y_scr[...] = _silu(y)

    ri = lax.broadcasted_iota(I32, (ck, ck), 0)
    ci = lax.broadcasted_iota(I32, (ck, ck), 1)
    tril = ri >= ci
    strict = ri > ci
    ltri = jnp.where(tril, 1.0, 0.0).astype(F32)
    ones = jnp.ones((ck, ck), F32)
    n_sq = max(int(math.log2(ck)) - 1, 0)

    def chunk(c, carry):
        r0 = pl.multiple_of(c * ck, ck)
        heads = range(GDN_HEADS)
        misc = misc_ref[pl.ds(r0, ck), :]
        beta_all = _sigmoid(misc)
        g_all = -jnp.exp(alog_ref[...]) * _softplus(misc + dtb_ref[...])
        cols = lambda blk: slice(blk * hd, (blk + 1) * hd)
        q = [y_scr[pl.ds(r0, ck), cols(hh)] for hh in heads]
        k = [y_scr[pl.ds(r0, ck), cols(GDN_HEADS + hh)] for hh in heads]
        v = [y_scr[pl.ds(r0, ck), cols(2 * GDN_HEADS + hh)] for hh in heads]
        q = [x * lax.rsqrt(jnp.sum(x * x, axis=-1, keepdims=True) + EPS) * (hd ** -0.5) for x in q]
        k = [x * lax.rsqrt(jnp.sum(x * x, axis=-1, keepdims=True) + EPS) for x in k]
        beta = [jnp.broadcast_to(beta_all[:, M_GDNB + hh:M_GDNB + hh + 1], (ck, hd)) for hh in heads]
        g128 = [jnp.broadcast_to(g_all[:, M_GDNA + hh:M_GDNA + hh + 1], (ck, hd)) for hh in heads]
        gc128 = [_dot_exact_lhs(ltri, g) for g in g128]
        gc_row = [_dot_exact_lhs(ones, jnp.where(ri <= ci, g[:, :ck], 0.0)) for g in g128]
        decay = [jnp.where(tril, jnp.exp(jnp.where(tril, gc128[hh][:, :ck] - gc_row[hh], 0.0)), 0.0) for hh in heads]
        kb = [k[hh] * beta[hh] for hh in heads]
        egc = [jnp.exp(g) for g in gc128]
        low = [jnp.where(strict, _dot3(kb[hh], k[hh], _NT) * decay[hh], 0.0) for hh in heads]
        intra = [jnp.where(tril, _dot3(q[hh], k[hh], _NT) * decay[hh], 0.0) for hh in heads]
        mp = [-x for x in low]
        x = [jnp.concatenate([v[hh] * beta[hh], kb[hh] * egc[hh]], axis=-1) for hh in heads]
        x = [x[hh] + _dot3(mp[hh], x[hh]) for hh in heads]
        for _ in range(n_sq):
            mp = [_dot3(m, m) for m in mp]
            x = [x[hh] + _dot3(mp[hh], x[hh]) for hh in heads]
        s = [state[hh] for hh in heads]
        v_new = [x[hh][:, :hd] - _dot3(x[hh][:, hd:], s[hh]) for hh in heads]
        o = [_dot3(q[hh] * egc[hh], s[hh]) + _dot3(intra[hh], v_new[hh]) for hh in heads]
        g_last = [g[ck - 1:ck, :] for g in gc128]
        kd = [k[hh] * jnp.exp(g_last[hh] - gc128[hh]) for hh in heads]
        s_new = [s[hh] * jnp.exp(g_last[hh]) + _dot3(kd[hh], v_new[hh], _TN) for hh in heads]
        for hh in heads:
            zz = z_ref[pl.ds(r0, ck), cols(hh)]
            state[hh] = s_new[hh]
            o_ref[pl.ds(r0, ck), cols(hh)] = (_rms_rows(o[hh], ng_ref[...]) * _silu(zz)).astype(o_ref.dtype)
        return carry

    lax.fori_loop(0, tc // ck, chunk, 0)
    so_ref[...] = state[...]


def _gdn_call(h3, wl, buf, s0):
    b, t, _ = h3.shape
    ck = min(GDN_CHUNK, t)
    assert t % ck == 0
    tc = _tile(t, 256)
    w = GDN_QKV
    row = pl.BlockSpec((1, LANES), lambda i, j: (0, 0))
    st = pl.BlockSpec((None, GDN_HEADS, GDN_HD, GDN_HD), lambda i, j: (i, 0, 0, 0))
    return pl.pallas_call(
        functools.partial(_gdn_body, tc=tc, ck=ck),
        grid=(b, t // tc),
        in_specs=[pl.BlockSpec((None, tc, w), lambda i, j: (i, j, C_GDNQKV // w)),
                  pl.BlockSpec((None, tc, MIX_WIDTH), lambda i, j: (i, j, C_GDNZ // MIX_WIDTH)),
                  pl.BlockSpec((None, tc, LANES), lambda i, j: (i, j, C_MISC // LANES)),
                  pl.BlockSpec((SUBLANES, w), lambda i, j: (0, 0)), row, row, row,
                  pl.BlockSpec((None, SUBLANES, w), lambda i, j: (i, 0, 0)), st],
        out_specs=[pl.BlockSpec((None, tc, MIX_WIDTH), lambda i, j: (i, j, 0)), st],
        out_shape=[jax.ShapeDtypeStruct((b, t, MIX_WIDTH), BF16),
                   jax.ShapeDtypeStruct((b, GDN_HEADS, GDN_HD, GDN_HD), F32)],
        scratch_shapes=[pltpu.VMEM((tc + SUBLANES, w), F32), pltpu.VMEM((tc, w), F32),
                        pltpu.VMEM((GDN_HEADS, GDN_HD, GDN_HD), F32)],
        compiler_params=_cparams("parallel", "arbitrary"),
        name="gated_deltanet",
    )(h3, h3, h3, wl["gdn_cw"], wl["gdn_alog"], wl["gdn_dtb"], wl["gdn_ng"], buf, s0)


def _lane_tile(x, n):
    return x if n == LANES else jnp.concatenate([x] * (n // LANES), axis=1)


def _online_update(s, v_ext, m_ref, acc_ref, idx):
    m_old = m_ref[idx]
    m_new = jnp.maximum(m_old, jnp.max(s, axis=-1, keepdims=True))
    alpha = jnp.exp(m_old - m_new)
    p = jnp.exp(s - _lane_tile(m_new, s.shape[1]))
    acc = acc_ref[idx]
    acc_ref[idx] = _lane_tile(alpha, acc.shape[1]) * acc + _dot(p.astype(BF16), v_ext)
    m_ref[idx] = m_new


def _online_update_blocks(s, vs, m_ref, l_ref, acc_ref, idx):
    m_old = m_ref[idx]
    m_new = jnp.maximum(m_old, jnp.max(s, axis=-1, keepdims=True))
    alpha = jnp.exp(m_old - m_new)
    p = jnp.exp(s - m_new)
    l_ref[idx] = alpha * l_ref[idx] + jnp.sum(p, axis=-1, keepdims=True)
    pb = p.astype(BF16)
    pv, off = None, 0
    for vblk in vs:
        part = _dot(pb[:, off:off + vblk.shape[0]], vblk)
        pv = part if pv is None else pv + part
        off += vblk.shape[0]
    acc_ref[idx] = alpha * acc_ref[idx] + pv
    m_ref[idx] = m_new


def _reset_softmax(m_ref, l_ref, acc_ref):
    m_ref[...] = jnp.full(m_ref.shape, NEG_BIG, F32)
    l_ref[...] = jnp.zeros(l_ref.shape, F32)
    acc_ref[...] = jnp.zeros(acc_ref.shape, F32)


def _diff_lambda(lam_ref, lam_init):
    lm = lam_ref[...]
    e1 = jnp.exp(jnp.sum(lm[0:1, :] * lm[1:2, :], axis=-1, keepdims=True))
    e2 = jnp.exp(jnp.sum(lm[2:3, :] * lm[3:4, :], axis=-1, keepdims=True))
    return e1 - e2 + lam_init


def _diff_prompt_body(qi_ref, kj_ref, lam_ref, sg_ref, q_ref, k_ref, v_ref, o_ref, m_scr, acc_scr, *, tq, lam_init):
    step = pl.program_id(2)
    qi = qi_ref[step]
    kj = kj_ref[step]
    hv = 2 * DIFF_HD

    @pl.when(kj == 0)
    def _():
        m_scr[...] = jnp.full(m_scr.shape, NEG_BIG, F32)
        acc_scr[...] = jnp.zeros(acc_scr.shape, F32)

    def tile(masked):
        v = v_ref[...]
        v_ext = jnp.concatenate([v, jnp.ones_like(v)], axis=1)
        s = [_dot_nt(q_ref[:, mm * DIFF_HD:(mm + 1) * DIFF_HD], k_ref[:, mm * DIFF_HD:(mm + 1) * DIFF_HD])
             for mm in range(2)]
        for mm in range(2):
            sm = s[mm]
            if masked:
                row = lax.broadcasted_iota(I32, (tq, tq), 0)
                col = lax.broadcasted_iota(I32, (tq, tq), 1)
                sm = jnp.where(col <= row, sm, NEG_BIG)
            _online_update(sm, v_ext, m_scr, acc_scr, mm)

    @pl.when(kj < qi)
    def _():
        tile(False)

    @pl.when(kj == qi)
    def _():
        tile(True)
        lam = _diff_lambda(lam_ref, lam_init)
        o = acc_scr[0, :, 0:hv] / acc_scr[0, :, hv:2 * hv] - lam * (acc_scr[1, :, 0:hv] / acc_scr[1, :, hv:2 * hv])
        o_ref[...] = (_rms_rows(o, sg_ref[...]) * (1.0 - lam_init)).astype(o_ref.dtype)


def _diff_prompt_call(fq3, fkb3, fvb3, wl, lam_init):
    b, t, _ = fq3.shape
    tq = _tile(t, 512)
    nq = t // tq
    hv = 2 * DIFF_HD
    pairs = [(q, k) for q in range(nq) for k in range(q + 1)]
    qi_of = jnp.asarray([p[0] for p in pairs], I32)
    kj_of = jnp.asarray([p[1] for p in pairs], I32)
    grid_spec = pltpu.PrefetchScalarGridSpec(
        num_scalar_prefetch=2,
        grid=(b, DIFF_HEADS, len(pairs)),
        in_specs=[pl.BlockSpec((4, DIFF_HD), lambda i, h, s, qi, kj: (0, 0)),
                  pl.BlockSpec((1, hv), lambda i, h, s, qi, kj: (0, 0)),
                  pl.BlockSpec((None, tq, hv), lambda i, h, s, qi, kj: (i, qi[s], h)),
                  pl.BlockSpec((None, tq, hv), lambda i, h, s, qi, kj: (i, kj[s], h)),
                  pl.BlockSpec((None, tq, hv), lambda i, h, s, qi, kj: (i, kj[s], h))],
        out_specs=pl.BlockSpec((None, tq, hv), lambda i, h, s, qi, kj: (i, qi[s], h)),
        scratch_shapes=[pltpu.VMEM((2, tq, LANES), F32), pltpu.VMEM((2, tq, 2 * hv), F32)],
    )
    return pl.pallas_call(
        functools.partial(_diff_prompt_body, tq=tq, lam_init=lam_init),
        grid_spec=grid_spec,
        out_shape=jax.ShapeDtypeStruct((b, t, MIX_WIDTH), BF16),
        compiler_params=_cparams("parallel", "parallel", "arbitrary"),
        name="diff_attn_prompt",
    )(qi_of, kj_of, wl["diff_lam"], wl["diff_sg"], fq3, fkb3, fvb3)


def _diff_sample_body(pt_ref, lam_ref, sg_ref, q_ref, kn_ref, vn_ref, *rest, tq, npg, lam_init):
    kp = rest[:npg]
    vp = rest[npg:2 * npg]
    o_ref, qrows, m_scr, l_scr, acc_scr = rest[2 * npg:]
    s_id = pl.program_id(1)
    n_steps = pl.num_programs(1)
    nrow = 2 * DIFF_HEADS * tq
    hv = 2 * DIFF_HD
    prow = PAGE_SIZE * DIFF_HEADS

    @pl.when(s_id == 0)
    def _():
        _reset_softmax(m_scr, l_scr, acc_scr)
        qrows[...] = jnp.zeros(qrows.shape, F32)
        q = q_ref[...].astype(F32)
        for j in range(2 * DIFF_HEADS):
            mm = j % 2
            qrows[j * tq:(j + 1) * tq, mm * DIFF_HD:(mm + 1) * DIFF_HD] = q[:, j * DIFF_HD:(j + 1) * DIFF_HD]

    qb = qrows[...].astype(BF16)
    row_head = lax.broadcasted_iota(I32, (nrow, prow), 0) // (2 * tq)

    @pl.when(s_id < n_steps - 1)
    def _():
        key_head = lax.broadcasted_iota(I32, (nrow, prow), 1) % DIFF_HEADS
        own = key_head == row_head
        scores = [jnp.where(own, _dot_nt(qb, kp[p][...].astype(BF16)), NEG_BIG) for p in range(npg)]
        _online_update_blocks(jnp.concatenate(scores, axis=-1), [vp[p][...].astype(BF16) for p in range(npg)],
                       m_scr, l_scr, acc_scr, Ellipsis)

    @pl.when(s_id == n_steps - 1)
    def _():
        kn = jnp.concatenate([kn_ref[:, hh * hv:(hh + 1) * hv] for hh in range(DIFF_HEADS)], axis=0).astype(BF16)
        vn = jnp.concatenate([vn_ref[:, hh * hv:(hh + 1) * hv] for hh in range(DIFF_HEADS)], axis=0).astype(BF16)
        nk = DIFF_HEADS * tq
        rr = lax.broadcasted_iota(I32, (nrow, nk), 0)
        cc = lax.broadcasted_iota(I32, (nrow, nk), 1)
        ok = jnp.logical_and(cc // tq == rr // (2 * tq), cc % tq <= rr % tq)
        s = jnp.where(ok, _dot_nt(qb, kn), NEG_BIG)
        _online_update_blocks(s, [vn], m_scr, l_scr, acc_scr, Ellipsis)
        lam = _diff_lambda(lam_ref, lam_init)
        acc = acc_scr[...] / l_scr[...]
        for hh in range(DIFF_HEADS):
            r1 = (2 * hh) * tq
            r2 = (2 * hh + 1) * tq
            o = acc[r1:r1 + tq, :] - lam * acc[r2:r2 + tq, :]
            o_ref[:, hh * hv:(hh + 1) * hv] = (_rms_rows(o, sg_ref[...]) * (1.0 - lam_init)).astype(o_ref.dtype)


def _diff_sample_call(fq3, fk3, h3, pool_k, pool_v, page_table, layer, wl, lam_init):
    db, t, _ = fq3.shape
    n_pages = page_table.shape[1]
    npg = _pages_per_step(n_pages)
    n_steps = n_pages // npg + 1
    w = MIX_WIDTH
    hv = 2 * DIFF_HD
    nrow = 2 * DIFF_HEADS * t
    prow = PAGE_SIZE * DIFF_HEADS

    def page_spec(p):
        def imap(i, s, pt):
            return (layer, pt[i, jnp.minimum(s, n_steps - 2) * npg + p], 0, 0)
        return pl.BlockSpec((None, None, prow, hv), imap)

    grid_spec = pltpu.PrefetchScalarGridSpec(
        num_scalar_prefetch=1,
        grid=(db, n_steps),
        in_specs=[pl.BlockSpec((4, DIFF_HD), lambda i, s, pt: (0, 0)),
                  pl.BlockSpec((1, hv), lambda i, s, pt: (0, 0)),
                  pl.BlockSpec((None, t, w), lambda i, s, pt: (i, 0, 0)),
                  pl.BlockSpec((None, t, w), lambda i, s, pt: (i, 0, 0)),
                  pl.BlockSpec((None, t, w), lambda i, s, pt: (i, 0, C_DIFV // w))]
                 + [page_spec(p) for p in range(npg)] + [page_spec(p) for p in range(npg)],
        out_specs=pl.BlockSpec((None, t, w), lambda i, s, pt: (i, 0, 0)),
        scratch_shapes=[pltpu.VMEM((nrow, hv), F32), pltpu.VMEM((nrow, 1), F32), pltpu.VMEM((nrow, 1), F32),
                        pltpu.VMEM((nrow, hv), F32)],
    )
    return pl.pallas_call(
        functools.partial(_diff_sample_body, tq=t, npg=npg, lam_init=lam_init),
        grid_spec=grid_spec,
        out_shape=jax.ShapeDtypeStruct((db, t, w), BF16),
        compiler_params=_cparams("parallel", "arbitrary"),
        name="diff_attn_sample",
    )(page_table, wl["diff_lam"], wl["diff_sg"], fq3, fk3, h3, *([pool_k] * npg), *([pool_v] * npg))


def _sortable(score):
    bits = pltpu.bitcast(jnp.where(score == 0.0, 0.0, score), I32)
    return bits ^ ((bits >> 31) & 0x7FFFFFFF)


def _search_threshold(count_ge, shape, n_sel):
    def bit_body(i, t_u):
        cand_u = t_u | (jnp.int32(1) << (31 - i))
        cnt = count_ge(cand_u ^ INT_MIN)
        return jnp.where(cnt >= n_sel, cand_u, t_u)

    t_u = lax.fori_loop(0, 32, bit_body, jnp.zeros(shape, I32))
    return t_u ^ INT_MIN


def _lane_fold(x):
    acc = x[:, 0:LANES]
    for c in range(1, x.shape[1] // LANES):
        acc = acc + x[:, c * LANES:(c + 1) * LANES]
    return acc


def _strictly_before(n):
    ui = lax.broadcasted_iota(I32, (n, n), 0)
    uj = lax.broadcasted_iota(I32, (n, n), 1)
    return jnp.where(ui < uj, 1.0, 0.0).astype(BF16)


def _dsa_prompt_body(iq3_ref, misc_ref, ik3_ref, q_ref, k_ref, v_ref, o_ref, iqs, qs, keys_scr, half_scr, bias_scr,
                     m_scr, acc_scr, w_scr, spread_scr, c16_scr, cnt_scr, cnt16_scr, *, tq, ck, n_sel):
    qi = pl.program_id(1)
    nck = (qi * tq + tq + ck - 1) // ck
    kw = iqs.shape[1]
    for hh in range(IDX_HEADS):
        iqs[hh * tq:(hh + 1) * tq, :] = iq3_ref[:, hh * kw:(hh + 1) * kw]
    wts = misc_ref[...][:, M_IDXW:M_IDXW + IDX_HEADS] * (IDX_HD ** -0.5 * IDX_HEADS ** -0.5)
    for hh in range(IDX_HEADS):
        w_scr[hh] = jnp.broadcast_to(wts[:, hh:hh + 1], (tq, LANES))
    row_pos = qi * tq + lax.broadcasted_iota(I32, (tq, ck), 0)
    lane_pos = lax.broadcasted_iota(I32, (tq, ck), 1)

    def score_chunk(c, carry):
        k0 = pl.multiple_of(c * ck, ck)
        s = jnp.maximum(_dot_nt(iqs[...], ik3_ref[pl.ds(k0, ck), :]), 0.0)
        sc = _lane_tile(w_scr[0], ck) * s[0:tq, :]
        for hh in range(1, IDX_HEADS):
            sc = sc + _lane_tile(w_scr[hh], ck) * s[hh * tq:(hh + 1) * tq, :]
        keys_scr[c] = jnp.where(c * ck + lane_pos <= row_pos, _sortable(sc), INT_MIN)
        return carry

    lax.fori_loop(0, nck, score_chunk, 0)

    def parked(slot):
        return _lane_tile(spread_scr[slot], ck)

    def count(pred):
        cnt_scr[...] = jnp.zeros((tq, LANES), I32)

        def body(c, carry):
            cnt_scr[...] += _lane_fold(jnp.where(pred(keys_scr[c]), 1, 0).astype(I32))
            return carry
        lax.fori_loop(0, nck, body, 0)
        return jnp.sum(cnt_scr[...], axis=-1, keepdims=True)

    ones_sq = jnp.ones((LANES, LANES), BF16)

    def search_half(half_scr):
        def bit_body(i, t_u):
            cand_u = t_u | (jnp.int32(1) << (15 - i))
            c16_scr[...] = (cand_u - HALF_BIAS).astype(I16)
            cnt16_scr[...] = jnp.zeros((tq, LANES), I16)

            def body(c, carry):
                hit = half_scr[c] >= _lane_tile(c16_scr[...], ck)
                cnt16_scr[...] += _lane_fold(jnp.where(hit, jnp.int16(1), jnp.int16(0)))
                return carry
            lax.fori_loop(0, nck, body, 0)
            tot = _dot(cnt16_scr[...].astype(F32).astype(BF16), ones_sq)
            return jnp.where(tot >= n_sel, cand_u, t_u)
        return lax.fori_loop(0, 16, bit_body, jnp.zeros((tq, LANES), I32)) - HALF_BIAS

    def split_hi(c, carry):
        half_scr[c] = (keys_scr[c] >> 16).astype(I16)
        return carry
    lax.fori_loop(0, nck, split_hi, 0)
    t_hi = search_half(half_scr)

    spread_scr[0] = t_hi

    def split_lo(c, carry):
        kk = keys_scr[c]
        hi = kk >> 16
        lo = (kk & 0xFFFF) - HALF_BIAS
        t_hi_b = parked(0)
        half_scr[c] = jnp.where(hi == t_hi_b, lo, jnp.where(hi > t_hi_b, HALF_BIAS - 1, -HALF_BIAS)).astype(I16)
        return carry
    lax.fori_loop(0, nck, split_lo, 0)
    t_lo = search_half(half_scr)
    thr_rep = (t_hi << 16) | ((t_lo + HALF_BIAS) & 0xFFFF)
    spread_scr[1] = thr_rep
    thr = thr_rep[:, 0:1]
    n_ge = count(lambda kk: kk >= parked(1))
    plain = jnp.max(jnp.where(jnp.logical_and(n_ge == n_sel, thr != INT_MIN), 0, 1)) == 0

    @pl.when(plain)
    def _():
        def body(c, carry):
            bias_scr[c] = jnp.where(keys_scr[c] >= parked(1), 0.0, NEG_BIG)
            return carry
        lax.fori_loop(0, nck, body, 0)

    @pl.when(jnp.logical_not(plain))
    def _():
        need = (n_sel - count(lambda kk: kk > parked(1))).astype(F32)
        before = _strictly_before(ck)

        def body(c, eq_seen):
            kk = keys_scr[c]
            thr_b = parked(1)
            eq = jnp.logical_and(kk == thr_b, c * ck + lane_pos <= row_pos)
            eqf = jnp.where(eq, 1.0, 0.0)
            rank = _dot(eqf.astype(BF16), before) + eq_seen
            sel = jnp.logical_or(kk > thr_b, jnp.logical_and(eq, rank < need))
            bias_scr[c] = jnp.where(sel, 0.0, NEG_BIG)
            return eq_seen + jnp.sum(eqf, axis=-1, keepdims=True)

        lax.fori_loop(0, nck, body, jnp.zeros((tq, 1), F32))

    m_scr[...] = jnp.full(m_scr.shape, NEG_BIG, F32)
    acc_scr[...] = jnp.zeros(acc_scr.shape, F32)
    hpg = DSA_HEADS // DSA_KV_HEADS
    for hd in range(DSA_HEADS):
        qs[hd * tq:(hd + 1) * tq, :] = q_ref[:, hd * DSA_HD:(hd + 1) * DSA_HD]

    def attend_chunk(c, carry):
        k0 = pl.multiple_of(c * ck, ck)
        bias = bias_scr[c]
        groups = range(DSA_KV_HEADS)
        rows = [pl.ds(g * hpg * tq, hpg * tq) for g in groups]
        s = [_dot_nt(qs[rows[g], :], k_ref[pl.ds(k0, ck), g * DSA_HD:(g + 1) * DSA_HD]) for g in groups]
        for g in groups:
            vg = v_ref[pl.ds(k0, ck), g * LANES:(g + 1) * LANES]
            sg = s[g].reshape(hpg, tq, ck) + bias[None]
            _online_update(sg.reshape(hpg * tq, ck), vg, m_scr, acc_scr, rows[g])
        return carry

    lax.fori_loop(0, nck, attend_chunk, 0)
    acc = acc_scr[...]
    out = acc / pltpu.roll(acc, DSA_HD, 1)
    for hd in range(DSA_HEADS):
        o_ref[:, hd * DSA_HD:(hd + 1) * DSA_HD] = out[hd * tq:(hd + 1) * tq, 0:DSA_HD].astype(o_ref.dtype)


def _dsa_prompt_call(iq33, ik33, dq3, dkb3, dvb3, h3):
    b, t, _ = dq3.shape
    n_sel = min(TOPK_MAX, t // 4)
    tq = _tile(t, 512)
    ck = _tile(t, 512)
    kw = iq33.shape[-1] // IDX_HEADS
    full = lambda width: pl.BlockSpec((None, t, width), lambda i, q: (i, 0, 0))
    return pl.pallas_call(
        functools.partial(_dsa_prompt_body, tq=tq, ck=ck, n_sel=n_sel),
        grid=(b, t // tq),
        in_specs=[pl.BlockSpec((None, tq, IDX_HEADS * kw), lambda i, q: (i, q, 0)),
                  pl.BlockSpec((None, tq, LANES), lambda i, q: (i, q, C_MISC // LANES)),
                  full(kw),
                  pl.BlockSpec((None, tq, MIX_WIDTH), lambda i, q: (i, q, 0)),
                  full(LANES), full(DSA_KV_HEADS * LANES)],
        out_specs=pl.BlockSpec((None, tq, MIX_WIDTH), lambda i, q: (i, q, 0)),
        out_shape=jax.ShapeDtypeStruct((b, t, MIX_WIDTH), BF16),
        scratch_shapes=[pltpu.VMEM((IDX_HEADS * tq, kw), BF16), pltpu.VMEM((DSA_HEADS * tq, DSA_HD), BF16),
                        pltpu.VMEM((t // ck, tq, ck), I32), pltpu.VMEM((t // ck, tq, ck), I16),
                        pltpu.VMEM((t // ck, tq, ck), F32),
                        pltpu.VMEM((DSA_HEADS * tq, LANES), F32), pltpu.VMEM((DSA_HEADS * tq, LANES), F32),
                        pltpu.VMEM((IDX_HEADS, tq, LANES), F32), pltpu.VMEM((2, tq, LANES), I32),
                        pltpu.VMEM((tq, LANES), I16), pltpu.VMEM((tq, LANES), I32), pltpu.VMEM((tq, LANES), I16)],
        compiler_params=pltpu.CompilerParams(dimension_semantics=("parallel", "arbitrary"),
                                             vmem_limit_bytes=VMEM_LIMIT_DSA_PROMPT),
        name="dsa_prompt",
    )(iq33, h3, ik33, dq3, dkb3, dvb3)


def _dsa_select_body(pt_ref, iq3_ref, misc_ref, ikn_ref, *rest, tq, npg, ns, n_sel, n_pages):
    pages = rest[:ns * npg]
    sel_ref, keys_scr, iqs = rest[ns * npg:]
    s_id = pl.program_id(1)
    n_steps = pl.num_programs(1)
    kw = iqs.shape[-1]
    seqs = range(ns)
    wts = [misc_ref[q][:, M_IDXW:M_IDXW + IDX_HEADS] * (IDX_HD ** -0.5 * IDX_HEADS ** -0.5) for q in seqs]

    @pl.when(s_id == 0)
    def _():
        for q in seqs:
            for hh in range(IDX_HEADS):
                iqs[q, hh * tq:(hh + 1) * tq, :] = iq3_ref[q, :, hh * kw:(hh + 1) * kw]

    def scores(q, ik, feature_major):
        hi, lo = _split_hi_lo(ik)
        a_hi = iqs[q, :, 0:IDX_HD]
        a_lo = iqs[q, :, 2 * IDX_HD:3 * IDX_HD]
        mm = _dot if feature_major else _dot_nt
        s = jnp.maximum(mm(a_hi, hi) + mm(a_hi, lo) + mm(a_lo, hi), 0.0)
        sc = wts[q][:, 0:1] * s[0:tq, :]
        for hh in range(1, IDX_HEADS):
            sc = sc + wts[q][:, hh:hh + 1] * s[hh * tq:(hh + 1) * tq, :]
        return sc

    @pl.when(s_id < n_steps - 1)
    def _():
        for q in seqs:
            ik = jnp.concatenate([pages[q * npg + p][...] for p in range(npg)], axis=1)
            keys = _sortable(scores(q, ik, True))
            for p in range(npg):
                keys_scr[q, s_id * npg + p] = keys[:, p * PAGE_SIZE:(p + 1) * PAGE_SIZE]

    @pl.when(s_id == n_steps - 1)
    def _():
        qrow = lax.broadcasted_iota(I32, (tq, tq), 0)
        kcol = lax.broadcasted_iota(I32, (tq, tq), 1)
        for q in seqs:
            new_keys = jnp.where(kcol <= qrow, _sortable(scores(q, ikn_ref[q], False)), INT_MIN)
            keys_scr[q, n_pages] = jnp.concatenate([new_keys, jnp.full((tq, LANES - tq), INT_MIN, I32)], axis=-1)

        def count(pred):
            acc = jnp.sum(jnp.where(pred(keys_scr[...]), 1, 0).astype(I32), axis=1)
            return jnp.sum(acc, axis=-1, keepdims=True)

        thr = _search_threshold(lambda cand: count(lambda kk: kk >= cand[:, None]), (ns, tq, 1), n_sel)
        n_ge = count(lambda kk: kk >= thr[:, None])
        plain = jnp.max(jnp.where(jnp.logical_and(n_ge == n_sel, thr != INT_MIN), 0, 1)) == 0

        @pl.when(plain)
        def _():
            sel_ref[...] = jnp.where(keys_scr[...] >= thr[:, None], 1.0, 0.0)

        @pl.when(jnp.logical_not(plain))
        def _():
            need = (n_sel - count(lambda kk: kk > thr[:, None])).astype(F32)
            before = _strictly_before(LANES)
            lane = lax.broadcasted_iota(I32, (tq, LANES), 1)
            qr = lax.broadcasted_iota(I32, (tq, LANES), 0)
            for q in seqs:
                def emit(c, eq_seen, q=q):
                    kk = keys_scr[q, c]
                    allowed = jnp.logical_or(c < n_pages, lane <= qr)
                    eq = jnp.logical_and(kk == thr[q], allowed)
                    eqf = jnp.where(eq, 1.0, 0.0)
                    rank = _dot(eqf.astype(BF16), before) + eq_seen
                    sel = jnp.logical_or(kk > thr[q], jnp.logical_and(eq, rank < need[q]))
                    sel_ref[q, c] = jnp.where(sel, 1.0, 0.0)
                    return eq_seen + jnp.sum(eqf, axis=-1, keepdims=True)

                lax.fori_loop(0, n_pages + 1, emit, jnp.zeros((tq, 1), F32))


def _dsa_attend_body(pt_ref, q_ref, kn_ref, vn_ref, selp_ref, seln_ref, *rest, tq, npg):
    kp = rest[:npg]
    vp = rest[npg:2 * npg]
    o_ref, qrows, m_scr, l_scr, acc_scr = rest[2 * npg:]
    s_id = pl.program_id(1)
    n_steps = pl.num_programs(1)
    hpg = DSA_HEADS // DSA_KV_HEADS
    grows = hpg * tq

    @pl.when(s_id == 0)
    def _():
        _reset_softmax(m_scr, l_scr, acc_scr)
        for hd in range(DSA_HEADS):
            qrows[hd * tq:(hd + 1) * tq, :] = q_ref[:, hd * DSA_HD:(hd + 1) * DSA_HD].astype(F32)

    qb = qrows[...].astype(BF16)
    qg = [qb[g * grows:(g + 1) * grows, :] for g in range(DSA_KV_HEADS)]

    def update(s, pv_fn):
        m_old = m_scr[...]
        m_new = jnp.maximum(m_old, jnp.max(s, axis=-1, keepdims=True))
        alpha = jnp.exp(m_old - m_new)
        p = jnp.exp(s - m_new)
        l_scr[...] = alpha * l_scr[...] + jnp.sum(p, axis=-1, keepdims=True)
        pb = p.astype(BF16)
        pv = jnp.concatenate([pv_fn(pb[g * grows:(g + 1) * grows, :], g) for g in range(DSA_KV_HEADS)], axis=0)
        acc_scr[...] = alpha * acc_scr[...] + pv
        m_scr[...] = m_new

    @pl.when(s_id < n_steps - 1)
    def _():
        kcat = [jnp.concatenate([kp[p][g] for p in range(npg)], axis=1).astype(BF16) for g in range(DSA_KV_HEADS)]
        vcat = [jnp.concatenate([vp[p][g] for p in range(npg)], axis=1).astype(BF16) for g in range(DSA_KV_HEADS)]
        s = jnp.concatenate([_dot(qg[g], kcat[g]) for g in range(DSA_KV_HEADS)], axis=0)
        sel1 = jnp.concatenate([selp_ref[p] for p in range(npg)], axis=1)
        sel = jnp.concatenate([sel1] * DSA_HEADS, axis=0) > 0.5
        update(jnp.where(sel, s, NEG_BIG), lambda pb, g: _dot_nt(pb, vcat[g]))

    @pl.when(s_id == n_steps - 1)
    def _():
        kn = kn_ref[...].astype(BF16)
        vn = vn_ref[...].astype(BF16)
        s = jnp.concatenate([_dot_nt(qg[g], kn[:, g * DSA_HD:(g + 1) * DSA_HD]) for g in range(DSA_KV_HEADS)], axis=0)
        sel = jnp.concatenate([seln_ref[...][:, 0:tq]] * DSA_HEADS, axis=0) > 0.5
        update(jnp.where(sel, s, NEG_BIG), lambda pb, g: _dot(pb, vn[:, g * DSA_HD:(g + 1) * DSA_HD]))
        acc = acc_scr[...] / l_scr[...]
        for hd in range(DSA_HEADS):
            o_ref[:, hd * DSA_HD:(hd + 1) * DSA_HD] = acc[hd * tq:(hd + 1) * tq, :].astype(o_ref.dtype)


def _dsa_sample_call(iq33, ik3, dq3, dk3, h3, pool_k, pool_v, pool_ik, page_table, layer):
    db, t, _ = dq3.shape
    n_pages = page_table.shape[1]
    n_keys = n_pages * PAGE_SIZE + t
    n_sel = min(TOPK_MAX, n_keys // 4)
    npg = _pages_per_step(n_pages)
    n_steps = n_pages // npg + 1
    kw = iq33.shape[-1] // IDX_HEADS
    sw = LANES

    def page_spec(*tail):
        def make(p):
            def imap(i, s, pt):
                return (layer, pt[i, jnp.minimum(s, n_steps - 2) * npg + p]) + (0,) * len(tail)
            return pl.BlockSpec((None, None) + tail, imap)
        return [make(p) for p in range(npg)]

    ns = 1

    def select_pages():
        def make(q, p):
            def imap(i, s, pt):
                return (layer, pt[i * ns + q, jnp.minimum(s, n_steps - 2) * npg + p], 0, 0)
            return pl.BlockSpec((None, None, IDX_HD, PAGE_SIZE), imap)
        return [make(q, p) for q in range(ns) for p in range(npg)]

    sel = pl.pallas_call(
        functools.partial(_dsa_select_body, tq=t, npg=npg, ns=ns, n_sel=n_sel, n_pages=n_pages),
        grid_spec=pltpu.PrefetchScalarGridSpec(
            num_scalar_prefetch=1,
            grid=(db // ns, n_steps),
            in_specs=[pl.BlockSpec((ns, t, IDX_HEADS * kw), lambda i, s, pt: (i, 0, 0)),
                      pl.BlockSpec((ns, t, LANES), lambda i, s, pt: (i, 0, C_MISC // LANES)),
                      pl.BlockSpec((ns, t, IDX_HD), lambda i, s, pt: (i, 0, 0))]
                     + select_pages(),
            out_specs=pl.BlockSpec((ns, n_pages + 1, t, sw), lambda i, s, pt: (i, 0, 0, 0)),
            scratch_shapes=[pltpu.VMEM((ns, n_pages + 1, t, LANES), I32), pltpu.VMEM((ns, IDX_HEADS * t, kw), BF16)],
        ),
        out_shape=jax.ShapeDtypeStruct((db, n_pages + 1, t, sw), F32),
        compiler_params=_cparams("parallel", "arbitrary"),
        name="dsa_sample_select",
    )(page_table, iq33, h3, ik3, *([pool_ik] * (ns * npg)))

    nrow = DSA_HEADS * t
    return pl.pallas_call(
        functools.partial(_dsa_attend_body, tq=t, npg=npg),
        grid_spec=pltpu.PrefetchScalarGridSpec(
            num_scalar_prefetch=1,
            grid=(db, n_steps),
            in_specs=[pl.BlockSpec((None, t, MIX_WIDTH), lambda i, s, pt: (i, 0, 0)),
                      pl.BlockSpec((None, t, LANES), lambda i, s, pt: (i, 0, 0)),
                      pl.BlockSpec((None, t, LANES), lambda i, s, pt: (i, 0, C_DSAV // LANES)),
                      pl.BlockSpec((None, npg, t, sw), lambda i, s, pt: (i, jnp.minimum(s, n_steps - 2), 0, 0)),
                      pl.BlockSpec((None, None, t, sw), lambda i, s, pt: (i, n_pages, 0, 0))]
                     + page_spec(DSA_KV_HEADS, DSA_HD, PAGE_SIZE) + page_spec(DSA_KV_HEADS, DSA_HD, PAGE_SIZE),
            out_specs=pl.BlockSpec((None, t, MIX_WIDTH), lambda i, s, pt: (i, 0, 0)),
            scratch_shapes=[pltpu.VMEM((nrow, DSA_HD), F32), pltpu.VMEM((nrow, 1), F32), pltpu.VMEM((nrow, 1), F32),
                            pltpu.VMEM((nrow, DSA_HD), F32)],
        ),
        out_shape=jax.ShapeDtypeStruct((db, t, MIX_WIDTH), BF16),
        compiler_params=_cparams("parallel", "arbitrary"),
        name="dsa_sample_attend",
    )(page_table, dq3, dk3, h3, sel, sel, *([pool_k] * npg), *([pool_v] * npg))


def _merge_body(x_ref, gate_ref, oa_ref, ob_ref, oc_ref, od_ref, wb_ref, wo_ref, o_ref):
    acc = None
    for bi, br in enumerate((oa_ref, ob_ref, oc_ref, od_ref)):
        proj = _dot(br[...], wb_ref[bi])
        term = _sigmoid(gate_ref[:, bi * D_MODEL:(bi + 1) * D_MODEL]) * proj
        acc = term if acc is None else acc + term
    o_ref[...] = x_ref[...] + _dot(acc.astype(BF16), wo_ref[...])


def _merge_call(x, h, branches, wl):
    n, d = x.shape
    tm = _tile(n, 512)
    br = pl.BlockSpec((tm, MIX_WIDTH), lambda i: (i, 0))
    return pl.pallas_call(
        _merge_body,
        grid=(n // tm,),
        in_specs=[pl.BlockSpec((tm, d), lambda i: (i, 0)),
                  pl.BlockSpec((tm, N_BRANCH * d), lambda i: (i, C_GATE // (N_BRANCH * d))),
                  br, br, br, br,
                  pl.BlockSpec((N_BRANCH, MIX_WIDTH, d), lambda i: (0, 0, 0)),
                  pl.BlockSpec((d, d), lambda i: (0, 0))],
        out_specs=pl.BlockSpec((tm, d), lambda i: (i, 0)),
        out_shape=jax.ShapeDtypeStruct((n, d), F32),
        compiler_params=_cparams("parallel"),
        name="merge",
    )(x, h, *branches, wl["w_branch"], wl["w_out"])


def _cross_body(x_ref, g_ref, wq_ref, mk_ref, mv_ref, wo_ref, o_ref, *, merged):
    x = x_ref[...]
    tm = x.shape[0]
    xn = _rms_rows(x, g_ref[...]).astype(BF16)
    q = _dot(xn, wq_ref[...]).astype(BF16)
    if merged:
        qrows = jnp.concatenate([q[:, hh * XA_HD:(hh + 1) * XA_HD] for hh in range(XA_HEADS)], axis=0)
        s = _dot_nt(qrows, mk_ref[...].astype(BF16)) * (XA_HD ** -0.5)
        rr = lax.broadcasted_iota(I32, s.shape, 0) // tm
        cc = lax.broadcasted_iota(I32, s.shape, 1) % XA_HEADS
        s = jnp.where(rr == cc, s, NEG_BIG)
        p = jnp.exp(s - jnp.max(s, axis=-1, keepdims=True))
        p = p / jnp.sum(p, axis=-1, keepdims=True)
        orow = _dot(p.astype(BF16), mv_ref[...].astype(BF16))
        outs = [orow[hh * tm:(hh + 1) * tm, :] for hh in range(XA_HEADS)]
    else:
        outs = []
        for hh in range(XA_HEADS):
            sl = slice(hh * XA_HD, (hh + 1) * XA_HD)
            s = _dot_nt(q[:, sl], mk_ref[:, sl].astype(BF16)) * (XA_HD ** -0.5)
            p = jnp.exp(s - jnp.max(s, axis=-1, keepdims=True))
            p = p / jnp.sum(p, axis=-1, keepdims=True)
            outs.append(_dot(p.astype(BF16), mv_ref[:, sl].astype(BF16)))
    o = jnp.concatenate(outs, axis=-1).astype(BF16)
    o_ref[...] = x + _dot(o, wo_ref[...])


def _cross_call(x3, wl, mk4, mv4, layer):
    b, t, d = x3.shape
    w = XA_HEADS * XA_HD
    tm = _tile(t, 512)
    merged = mk4.shape[-1] == XA_HD
    mem = pl.BlockSpec((None, None) + mk4.shape[2:], lambda i, j: (layer, i, 0, 0))
    return pl.pallas_call(
        functools.partial(_cross_body, merged=merged),
        grid=(b, t // tm),
        in_specs=[pl.BlockSpec((None, tm, d), lambda i, j: (i, j, 0)),
                  pl.BlockSpec((1, d), lambda i, j: (0, 0)),
                  pl.BlockSpec((d, w), lambda i, j: (0, 0)), mem, mem,
                  pl.BlockSpec((w, d), lambda i, j: (0, 0))],
        out_specs=pl.BlockSpec((None, tm, d), lambda i, j: (i, j, 0)),
        out_shape=jax.ShapeDtypeStruct((b, t, d), F32),
        compiler_params=_cparams("parallel", "parallel"),
        name="cross_attn",
    )(x3, wl["norm_xa_g"], wl["xa_wq"], mk4, mv4, wl["xa_wo"])


def _ffn_body(x_ref, g_ref, wg_ref, wu_ref, wd_ref, o_ref, xn_scr, acc_scr):
    f = pl.program_id(1)

    @pl.when(f == 0)
    def _():
        xn_scr[...] = _rms_rows(x_ref[...], g_ref[...]).astype(BF16)
        acc_scr[...] = x_ref[...]

    xn = xn_scr[...]
    hm = _silu(_dot(xn, wg_ref[...])) * _dot(xn, wu_ref[...])
    acc_scr[...] += _dot(hm.astype(BF16), wd_ref[...])

    @pl.when(f == pl.num_programs(1) - 1)
    def _():
        o_ref[...] = acc_scr[...]


def _ffn_call(x, wl):
    n, d = x.shape
    ff = wl["ffn_wg"].shape[1]
    tm = _tile(n, 1024)
    tf = _tile(ff, 1408)
    return pl.pallas_call(
        _ffn_body,
        grid=(n // tm, ff // tf),
        in_specs=[pl.BlockSpec((tm, d), lambda i, f: (i, 0)), pl.BlockSpec((1, d), lambda i, f: (0, 0)),
                  pl.BlockSpec((d, tf), lambda i, f: (0, f)), pl.BlockSpec((d, tf), lambda i, f: (0, f)),
                  pl.BlockSpec((tf, d), lambda i, f: (f, 0))],
        out_specs=pl.BlockSpec((tm, d), lambda i, f: (i, 0)),
        out_shape=jax.ShapeDtypeStruct((n, d), F32),
        scratch_shapes=[pltpu.VMEM((tm, d), BF16), pltpu.VMEM((tm, d), F32)],
        compiler_params=_cparams("parallel", "arbitrary"),
        name="ffn_swiglu",
    )(x, wl["norm_ffn_g"], wl["ffn_wg"], wl["ffn_wu"], wl["ffn_wd"])


def _moe_body(x_ref, g_ref, wr_ref, br_ref, wg_ref, wu_ref, wd_ref, o_ref, xn_scr, gate_scr, acc_scr):
    e = pl.program_id(1)
    f = pl.program_id(2)
    first = jnp.logical_and(e == 0, f == 0)

    @pl.when(first)
    def _():
        xn = _rms_rows(x_ref[...], g_ref[...])
        xn_scr[...] = xn.astype(BF16)
        acc_scr[...] = x_ref[...]
        logits = _dot_hi(xn, wr_ref[...]) + br_ref[...]
        lane = lax.broadcasted_iota(I32, logits.shape, 1)
        logits = jnp.where(lane < N_EXPERTS, logits, -jnp.inf)
        top1 = jnp.max(logits, axis=-1, keepdims=True)
        idx1 = jnp.min(jnp.where(logits == top1, lane, LANES), axis=-1, keepdims=True)
        rest = jnp.where(lane == idx1, -jnp.inf, logits)
        top2 = jnp.max(rest, axis=-1, keepdims=True)
        idx2 = jnp.min(jnp.where(rest == top2, lane, LANES), axis=-1, keepdims=True)
        e2 = jnp.exp(top2 - top1)
        p1 = 1.0 / (1.0 + e2)
        p2 = e2 / (1.0 + e2)
        gate_scr[...] = jnp.where(lane == idx1, p1, 0.0) + jnp.where(lane == idx2, p2, 0.0)

    gates = gate_scr[...]
    lane = lax.broadcasted_iota(I32, gates.shape, 1)
    ge = jnp.sum(jnp.where(lane == e, gates, 0.0), axis=-1, keepdims=True)
    xn = xn_scr[...]
    hm = _silu(_dot(xn, wg_ref[...])) * _dot(xn, wu_ref[...])
    acc_scr[...] += ge * _dot(hm.astype(BF16), wd_ref[...])

    @pl.when(jnp.logical_and(e == pl.num_programs(1) - 1, f == pl.num_programs(2) - 1))
    def _():
        o_ref[...] = acc_scr[...]


def _moe_call(x, wl):
    n, d = x.shape
    ne, _, fe = wl["moe_wg"].shape
    tm = _tile(n, 1024)
    tf = _tile(fe, 1024)
    return pl.pallas_call(
        _moe_body,
        grid=(n // tm, ne, fe // tf),
        in_specs=[pl.BlockSpec((tm, d), lambda i, e, f: (i, 0)), pl.BlockSpec((1, d), lambda i, e, f: (0, 0)),
                  pl.BlockSpec((d, LANES), lambda i, e, f: (0, 0)), pl.BlockSpec((1, LANES), lambda i, e, f: (0, 0)),
                  pl.BlockSpec((None, d, tf), lambda i, e, f: (e, 0, f)),
                  pl.BlockSpec((None, d, tf), lambda i, e, f: (e, 0, f)),
                  pl.BlockSpec((None, tf, d), lambda i, e, f: (e, f, 0))],
        out_specs=pl.BlockSpec((tm, d), lambda i, e, f: (i, 0)),
        out_shape=jax.ShapeDtypeStruct((n, d), F32),
        scratch_shapes=[pltpu.VMEM((tm, d), BF16), pltpu.VMEM((tm, LANES), F32), pltpu.VMEM((tm, d), F32)],
        compiler_params=_cparams("parallel", "arbitrary", "arbitrary"),
        name="moe_swiglu",
    )(x, wl["norm_ffn_g"], wl["moe_wr"], wl["moe_br"], wl["moe_wg"], wl["moe_wu"], wl["moe_wd"])


def _pad_rows(a, rows):
    return jnp.pad(a, ((0, rows - a.shape[0]), (0, 0)))


def _lane_row(vals, lane0):
    return jnp.zeros((1, LANES), F32).at[0, lane0:lane0 + vals.shape[0]].set(vals.astype(F32))


def _block_diag(w):
    nb, n, _ = w.shape
    eye = jnp.eye(nb, dtype=w.dtype)
    return jnp.einsum("aij,ab->aibj", w, eye).reshape(nb * n, nb * n)


def _layer_weights(l, p):
    w_t = jnp.transpose(p["w_in"], (2, 0, 1))[:, l, :]
    d = w_t.shape[1]
    misc = jnp.concatenate([w_t[O_IDXK:O_IDXK + IDX_HD + IDX_HEADS], w_t[O_GDNB:O_GDNB + 2 * GDN_HEADS],
                            jnp.zeros((LANES - IDX_HD - IDX_HEADS - 2 * GDN_HEADS, d), w_t.dtype)], axis=0)
    w_in_r = jnp.concatenate([
        w_t[O_GATE:], w_t[O_LRUX:O_DSAK], w_t[O_DIFQ:O_GATE], w_t[O_GDNZ:O_GDNB],
        w_t[O_GDNQKV:O_GDNZ], w_t[O_IDXQ:O_IDXK], w_t[O_DSAK:O_IDXQ], misc], axis=0)
    assert w_in_r.shape[0] == H_COLS
    wl = dict(
        norm_mix_g=p["norm_mix_g"][l], w_in=w_in_r.astype(BF16),
        lru_cw=_pad_rows(p["lru_conv_w"][l], SUBLANES), lru_cb=p["lru_conv_b"][l].reshape(1, -1),
        lru_wa=_block_diag(p["lru_wa"][l]).astype(BF16), lru_ba=p["lru_ba"][l].reshape(1, -1),
        lru_wx=_block_diag(p["lru_wx"][l]).astype(BF16), lru_bx=p["lru_bx"][l].reshape(1, -1),
        lru_lam=p["lru_lambda"][l].reshape(1, -1),
        gdn_cw=_pad_rows(p["gdn_conv_w"][l], SUBLANES),
        gdn_alog=_lane_row(p["gdn_a_log"][l], M_GDNA), gdn_dtb=_lane_row(p["gdn_dt_bias"][l], M_GDNA),
        gdn_ng=p["gdn_norm_g"][l].reshape(1, -1),
        diff_lam=p["diff_lambda"][l], diff_sg=p["diff_subln_g"][l].reshape(1, -1),
        w_branch=p["w_branch"][l].astype(BF16), w_out=p["w_out"][l].astype(BF16),
        norm_xa_g=p["norm_xa_g"][l].reshape(1, -1), norm_mem_g=p["norm_mem_g"][l],
        xa_wq=p["xa_wq"][l].astype(BF16), xa_wo=p["xa_wo"][l].astype(BF16),
        xa_wkv=jnp.concatenate([p["xa_wk"][l], p["xa_wv"][l]], axis=1).astype(BF16),
        norm_ffn_g=p["norm_ffn_g"][l].reshape(1, -1),
    )
    j = l // 2
    if l % 2 == 0:
        wl.update(ffn_wg=p["ffn_w_gate"][j].astype(BF16), ffn_wu=p["ffn_w_up"][j].astype(BF16),
                  ffn_wd=p["ffn_w_down"][j].astype(BF16))
    else:
        wl.update(moe_wr=jnp.pad(p["moe_router_w"][j], ((0, 0), (0, LANES - N_EXPERTS))),
                  moe_br=jnp.pad(p["moe_router_b"][j], (0, LANES - N_EXPERTS)).reshape(1, LANES),
                  moe_wg=p["moe_w_gate"][j].astype(BF16), moe_wu=p["moe_w_up"][j].astype(BF16),
                  moe_wd=p["moe_w_down"][j].astype(BF16))
    return wl


def _rope_tables(pos):
    half = DSA_HD // 2
    inv = ROPE_THETA ** (-jnp.arange(half, dtype=F32) / half)
    ang = pos.astype(F32)[:, None] * inv[None, :]
    cos, sin = jnp.cos(ang), jnp.sin(ang)
    return jnp.tile(cos, (1, 4)), jnp.tile(jnp.concatenate([-sin, sin], axis=1), (1, 2))


def _front_pad(buf):
    return jnp.pad(buf, ((0, 0), (SUBLANES - (CONV_W - 1), 0), (0, 0)))


def _group_layer(x3, l, wl, tabs, states, mem_kv, paged):
    b, t, d = x3.shape
    n = b * t
    x = x3.reshape(n, d)
    lam_init = 0.8 - 0.6 * math.exp(-0.3 * l)
    xn = _rmsnorm(x, wl["norm_mix_g"], BF16)
    h = _matmul(xn, wl["w_in"], tn=1408, w_rows_are_outputs=True)
    h3 = h.reshape(b, t, H_COLS)
    dq, fq, fk, fkb, fvb, iq3, dk, dkb, dvb, ik, ik3 = _rope_call(h, tabs[0], tabs[1])
    to3 = lambda a: a.reshape(b, t, a.shape[-1])
    dq3, fq3, fk3, dk3, ik3d = map(to3, (dq, fq, fk, dk, ik))

    o_a, lru_h = _lru_call(h3, wl, _front_pad(states["lru_conv"]), states["lru_h"].reshape(b, 1, -1),
                           reset_first=paged is None)
    o_c, gdn_s = _gdn_call(h3, wl, _front_pad(states["gdn_conv"]), states["gdn_s"])
    if paged is None:
        o_b = _dsa_prompt_call(to3(iq3), to3(ik3), dq3, to3(dkb), to3(dvb), h3)
        o_d = _diff_prompt_call(fq3, to3(fkb), to3(fvb), wl, lam_init)
    else:
        o_b = _dsa_sample_call(to3(iq3), ik3d, dq3, dk3, h3, paged["dsa_k"], paged["dsa_v"], paged["idx_k"],
                               paged["page_table"], l)
        o_d = _diff_sample_call(fq3, fk3, h3, paged["diff_k"], paged["diff_v"], paged["page_table"], l, wl, lam_init)

    flat = lambda a: a.reshape(n, a.shape[-1])
    x1 = _merge_call(x, h, [flat(o_a), flat(o_b), flat(o_c), flat(o_d)], wl)
    x2 = _cross_call(x1.reshape(b, t, d), wl, mem_kv[0], mem_kv[1], mem_kv[2])
    x2 = x2.reshape(n, d)
    x3_new = (_ffn_call(x2, wl) if "ffn_wg" in wl else _moe_call(x2, wl)).reshape(b, t, d)

    new = dict(
        dsa_k=dk3.reshape(b, t, DSA_KV_HEADS, DSA_HD),
        dsa_v=h3[:, :, C_DSAV:C_DSAV + DSA_KV_HEADS * DSA_HD].reshape(b, t, DSA_KV_HEADS, DSA_HD),
        idx_k=ik3d,
        diff_k=fk3.reshape(b, t, DIFF_HEADS, 2 * DIFF_HD),
        diff_v=h3[:, :, C_DIFV:C_DIFV + MIX_WIDTH].reshape(b, t, DIFF_HEADS, 2 * DIFF_HD),
        lru_h=lru_h.reshape(b, -1),
        lru_conv=jnp.concatenate([states["lru_conv"], h3[:, :, C_LRUX:C_LRUX + MIX_WIDTH]], axis=1)[:, -(CONV_W - 1):]
        if t < CONV_W - 1 else h3[:, t - (CONV_W - 1):, C_LRUX:C_LRUX + MIX_WIDTH],
        gdn_s=gdn_s,
        gdn_conv=jnp.concatenate([states["gdn_conv"], h3[:, :, C_GDNQKV:C_GDNQKV + GDN_QKV]], axis=1)[:, -(CONV_W - 1):]
        if t < CONV_W - 1 else h3[:, t - (CONV_W - 1):, C_GDNQKV:C_GDNQKV + GDN_QKV],
    )
    return x3_new, new


def kernel(x_prompt, x_sample, mem_prompt, cache_dsa_k, cache_dsa_v, cache_idx_k, cache_diff_k, cache_diff_v,
           cache_mem_k, cache_mem_v, state_lru_h, state_lru_conv, state_gdn_s, state_gdn_conv, page_table,
           norm_mix_g, w_in, lru_conv_w, lru_conv_b, lru_wa, lru_ba, lru_wx, lru_bx, lru_lambda,
           gdn_conv_w, gdn_a_log, gdn_dt_bias, gdn_norm_g, diff_lambda, diff_subln_g, w_branch, w_out,
           norm_xa_g, norm_mem_g, xa_wq, xa_wk, xa_wv, xa_wo, norm_ffn_g, ffn_w_gate, ffn_w_up, ffn_w_down,
           moe_router_w, moe_router_b, moe_w_gate, moe_w_up, moe_w_down, final_norm_g):
    params = dict(
        norm_mix_g=norm_mix_g, w_in=w_in, lru_conv_w=lru_conv_w, lru_conv_b=lru_conv_b, lru_wa=lru_wa, lru_ba=lru_ba,
        lru_wx=lru_wx, lru_bx=lru_bx, lru_lambda=lru_lambda, gdn_conv_w=gdn_conv_w, gdn_a_log=gdn_a_log,
        gdn_dt_bias=gdn_dt_bias, gdn_norm_g=gdn_norm_g, diff_lambda=diff_lambda, diff_subln_g=diff_subln_g,
        w_branch=w_branch, w_out=w_out, norm_xa_g=norm_xa_g, norm_mem_g=norm_mem_g, xa_wq=xa_wq, xa_wk=xa_wk,
        xa_wv=xa_wv, xa_wo=xa_wo, norm_ffn_g=norm_ffn_g, ffn_w_gate=ffn_w_gate, ffn_w_up=ffn_w_up,
        ffn_w_down=ffn_w_down, moe_router_w=moe_router_w, moe_router_b=moe_router_b, moe_w_gate=moe_w_gate,
        moe_w_up=moe_w_up, moe_w_down=moe_w_down)
    b, s, d = x_prompt.shape
    db, t, _ = x_sample.shape
    m = mem_prompt.shape[1]
    n_pool = cache_dsa_k.shape[1]
    past = page_table.shape[1] * PAGE_SIZE

    tabs_p = _rope_tables(jnp.arange(s))
    cs, ss = _rope_tables(past + jnp.arange(t))
    tabs_s = (jnp.tile(cs, (db, 1)), jnp.tile(ss, (db, 1)))
    paged = dict(
        dsa_k=jnp.transpose(cache_dsa_k, (0, 1, 3, 4, 2)), dsa_v=jnp.transpose(cache_dsa_v, (0, 1, 3, 4, 2)),
        idx_k=jnp.transpose(cache_idx_k, (0, 1, 3, 2)), page_table=page_table,
        diff_k=cache_diff_k.reshape(DEPTH, n_pool, PAGE_SIZE * DIFF_HEADS, 2 * DIFF_HD),
        diff_v=cache_diff_v.reshape(DEPTH, n_pool, PAGE_SIZE * DIFF_HEADS, 2 * DIFF_HD))
    mem_k_s = cache_mem_k.reshape(DEPTH, db, m * XA_HEADS, XA_HD)
    mem_v_s = cache_mem_v.reshape(DEPTH, db, m * XA_HEADS, XA_HD)
    zero_states = dict(lru_conv=jnp.zeros((b, CONV_W - 1, MIX_WIDTH), F32), lru_h=jnp.zeros((b, MIX_WIDTH), F32),
                       gdn_conv=jnp.zeros((b, CONV_W - 1, GDN_QKV), F32),
                       gdn_s=jnp.zeros((b, GDN_HEADS, GDN_HD, GDN_HD), F32))

    hp, hs = x_prompt, x_sample
    new_p, new_s, mem_ks, mem_vs = [], [], [], []
    for l in range(DEPTH):
        wl = _layer_weights(l, params)
        mem_n = _rmsnorm(mem_prompt.reshape(b * m, d), wl["norm_mem_g"], BF16)
        kv = _matmul(mem_n, wl["xa_wkv"], tn=512)
        w = XA_HEADS * XA_HD
        mk = kv[:, :w].reshape(1, b, m, w)
        mv = kv[:, w:].reshape(1, b, m, w)
        hp, st_p = _group_layer(hp, l, wl, tabs_p, zero_states, (mk, mv, 0), None)
        new_p.append(st_p)
        mem_ks.append(mk.reshape(b, m, XA_HEADS, XA_HD))
        mem_vs.append(mv.reshape(b, m, XA_HEADS, XA_HD))
        states_s = dict(lru_conv=state_lru_conv[l], lru_h=state_lru_h[l], gdn_conv=state_gdn_conv[l],
                        gdn_s=state_gdn_s[l])
        hs, st_s = _group_layer(hs, l, wl, tabs_s, states_s, (mem_k_s, mem_v_s, l), paged)
        new_s.append(st_s)

    y_prompt = _rmsnorm(hp.reshape(b * s, d), final_norm_g, F32).reshape(b, s, d)
    y_sample = _rmsnorm(hs.reshape(db * t, d), final_norm_g, F32).reshape(db, t, d)
    stack = lambda states, name: jnp.stack([st[name] for st in states], axis=0)
    outs = [y_prompt, y_sample]
    for name in ("dsa_k", "dsa_v", "idx_k", "diff_k", "diff_v", "lru_h", "lru_conv", "gdn_s", "gdn_conv"):
        outs += [stack(new_p, name), stack(new_s, name)]
    outs += [jnp.stack(mem_ks, axis=0), jnp.stack(mem_vs, axis=0)]
    return tuple(outs)
```

```python
import functools
import math

import jax
import jax.numpy as jnp
from jax import lax
from jax.experimental import pallas as pl
from jax.experimental.pallas import tpu as pltpu

F32 = jnp.float32
BF16 = jnp.bfloat16
I32 = jnp.int32
I16 = jnp.int16

D_MODEL = 1024
DEPTH = 2
PAGE_SIZE = 128
MIX_WIDTH = 512
LRU_BLOCKS = 8
LRU_BLOCK = MIX_WIDTH // LRU_BLOCKS
LRU_C = 8.0
CONV_W = 4
DSA_HEADS = 8
DSA_KV_HEADS = 2
DSA_HD = 64
IDX_HEADS = 4
IDX_HD = 64
TOPK_MAX = 256
GDN_HEADS = 4
GDN_HD = 128
GDN_CHUNK = 128
DIFF_HEADS = 4
DIFF_HD = 64
XA_HEADS = 4
XA_HD = 128
N_EXPERTS = 8
ROPE_THETA = 10000.0
EPS = 1e-6
N_BRANCH = 4
GDN_QKV = 3 * GDN_HEADS * GDN_HD

LANES = 128
SUBLANES = 8
NEG_BIG = -1e30
INT_MIN = -2 ** 31
HALF_BIAS = 2 ** 15
VMEM_LIMIT = 48 * 1024 * 1024
VMEM_LIMIT_DSA_PROMPT = 58 * 1024 * 1024

C_GATE = 0
C_LRUX = 4096
C_LRUG = 4608
C_DSAQ = 5120
C_DIFQ = 5632
C_DIFK = 6144
C_DIFV = 6656
C_GDNZ = 7168
C_GDNQKV = 7680
C_IDXQ = 9216
C_DSAK = 9472
C_DSAV = 9600
C_MISC = 9728
H_COLS = 9856
M_IDXK = 0
M_IDXW = 64
M_GDNB = 68
M_GDNA = 72

O_LRUX, O_LRUG, O_DSAQ, O_DSAK, O_DSAV, O_IDXQ, O_IDXK, O_IDXW = 0, 512, 1024, 1536, 1664, 1792, 2048, 2112
O_GDNQKV, O_GDNZ, O_GDNB, O_GDNA, O_DIFQ, O_DIFK, O_DIFV, O_GATE = 2116, 3652, 4164, 4168, 4172, 4684, 5196, 5708


def _cparams(*sem):
    return pltpu.CompilerParams(dimension_semantics=sem, vmem_limit_bytes=VMEM_LIMIT)


def _tile(n, pref):
    return pref if n % pref == 0 else n


def _pages_per_step(n_pages):
    return next(p for p in (16, 8, 1) if n_pages % p == 0)


def _softplus(x):
    return jnp.maximum(x, 0.0) + jnp.log1p(jnp.exp(-jnp.abs(x)))


def _sigmoid(x):
    return 1.0 / (1.0 + jnp.exp(-x))


def _silu(x):
    return x * _sigmoid(x)


def _dot(a, b):
    return jnp.dot(a, b, preferred_element_type=F32)


def _dot_nt(a, b):
    return lax.dot_general(a, b, (((1,), (1,)), ((), ())), preferred_element_type=F32)


def _dot_hi(a, b):
    return jnp.dot(a, b, preferred_element_type=F32, precision=lax.Precision.HIGHEST)


def _split_hi_lo(x):
    hi = x.astype(BF16)
    return hi, (x - hi.astype(F32)).astype(BF16)


_NN = (((1,), (0,)), ((), ()))
_NT = (((1,), (1,)), ((), ()))
_TN = (((0,), (0,)), ((), ()))


def _dot3(a, b, dims=_NN):
    a_hi, a_lo = _split_hi_lo(a)
    b_hi, b_lo = _split_hi_lo(b)
    dg = functools.partial(lax.dot_general, dimension_numbers=dims, preferred_element_type=F32)
    ca, cb = dims[0][0][0], dims[0][1][0]
    if a.shape[ca] % LANES == 0:
        return dg(jnp.concatenate([a_hi, a_hi, a_lo], axis=ca), jnp.concatenate([b_hi, b_lo, b_hi], axis=cb))
    return dg(a_hi, b_hi) + dg(a_hi, b_lo) + dg(a_lo, b_hi)


def _dot_exact_lhs(a, b):
    ab = a.astype(BF16)
    b1 = b.astype(BF16)
    r1 = b - b1.astype(F32)
    b2 = r1.astype(BF16)
    b3 = (r1 - b2.astype(F32)).astype(BF16)
    if a.shape[1] % LANES == 0:
        return _dot(jnp.concatenate([ab, ab, ab], axis=1), jnp.concatenate([b1, b2, b3], axis=0))
    return _dot(ab, b1) + _dot(ab, b2) + _dot(ab, b3)


def _rms_rows(x, g):
    return x * lax.rsqrt(jnp.mean(x * x, axis=-1, keepdims=True) + EPS) * g


def _rmsnorm_body(x_ref, g_ref, o_ref):
    o_ref[...] = _rms_rows(x_ref[...], g_ref[...]).astype(o_ref.dtype)


def _rmsnorm(x, g, out_dtype):
    n, d = x.shape
    tm = _tile(n, 512)
    return pl.pallas_call(
        _rmsnorm_body,
        grid=(n // tm,),
        in_specs=[pl.BlockSpec((tm, d), lambda i: (i, 0)), pl.BlockSpec((1, d), lambda i: (0, 0))],
        out_specs=pl.BlockSpec((tm, d), lambda i: (i, 0)),
        out_shape=jax.ShapeDtypeStruct((n, d), out_dtype),
        compiler_params=_cparams("parallel"),
        name="rmsnorm",
    )(x, g.reshape(1, d))


def _matmul_body(a_ref, w_ref, o_ref, *, w_rows_are_outputs):
    mm = _dot_nt if w_rows_are_outputs else _dot
    o_ref[...] = mm(a_ref[...], w_ref[...]).astype(o_ref.dtype)


def _matmul(a, w, tn, w_rows_are_outputs=False, out_dtype=F32):
    n, k = a.shape
    c = w.shape[0] if w_rows_are_outputs else w.shape[1]
    tm = _tile(n, 1024)
    w_spec = (pl.BlockSpec((tn, k), lambda j, i: (j, 0)) if w_rows_are_outputs
              else pl.BlockSpec((k, tn), lambda j, i: (0, j)))
    return pl.pallas_call(
        functools.partial(_matmul_body, w_rows_are_outputs=w_rows_are_outputs),
        grid=(c // tn, n // tm),
        in_specs=[pl.BlockSpec((tm, k), lambda j, i: (i, 0)), w_spec],
        out_specs=pl.BlockSpec((tm, tn), lambda j, i: (i, j)),
        out_shape=jax.ShapeDtypeStruct((n, c), out_dtype),
        compiler_params=_cparams("parallel", "parallel"),
        name="matmul",
    )(a, w)


def _rope_block(x, cos, sin):
    lane = lax.broadcasted_iota(I32, x.shape, 1)
    first = (lane % 64) < 32
    partner = jnp.where(first, pltpu.roll(x, LANES - 32, 1), pltpu.roll(x, 32, 1))
    return x * cos + partner * sin


def _rope_wide(x, cos, sin):
    return [_rope_block(x[:, c * LANES:(c + 1) * LANES], cos, sin) for c in range(x.shape[1] // LANES)]


def _rope_body(cos_ref, sin_ref, dq_ref, fq_ref, fk_ref, fv_ref, iq_ref, dk_ref, dv_ref, misc_ref,
               dq_o, fq_o, fk_o, fkb_o, fvb_o, iq3_o, dk_o, dkb_o, dvb_o, ik_o, ik3_o):
    cos = cos_ref[...]
    sin = sin_ref[...]
    lane = lax.broadcasted_iota(I32, cos.shape, 1)
    low = lane < IDX_HD
    for c, blk in enumerate(_rope_wide(dq_ref[...], cos, sin)):
        dq_o[:, c * LANES:(c + 1) * LANES] = (blk * DSA_HD ** -0.5).astype(BF16)
    for c, blk in enumerate(_rope_wide(fq_ref[...], cos, sin)):
        fq_o[:, c * LANES:(c + 1) * LANES] = (blk * DIFF_HD ** -0.5).astype(BF16)
    for c, blk in enumerate(_rope_wide(fk_ref[...], cos, sin)):
        fk_o[:, c * LANES:(c + 1) * LANES] = blk
        fkb_o[:, c * LANES:(c + 1) * LANES] = blk.astype(BF16)
    fvb_o[...] = fv_ref[...].astype(BF16)
    dk = _rope_block(dk_ref[...], cos, sin)
    dk_o[...] = dk
    dkb_o[...] = dk.astype(BF16)
    dv = dv_ref[...]
    dvb_o[:, 0:LANES] = jnp.where(low, dv, 1.0).astype(BF16)
    dvb_o[:, LANES:2 * LANES] = jnp.where(low, pltpu.roll(dv, DSA_HD, 1), 1.0).astype(BF16)
    for c, blk in enumerate(_rope_wide(iq_ref[...], cos, sin)):
        hi = blk.astype(BF16).astype(F32)
        lo = blk - hi
        hi_sw = pltpu.roll(hi, IDX_HD, 1)
        lo_sw = pltpu.roll(lo, IDX_HD, 1)
        base = 2 * c * 2 * LANES
        iq3_o[:, base:base + LANES] = jnp.where(low, hi, hi_sw).astype(BF16)
        iq3_o[:, base + LANES:base + 2 * LANES] = jnp.where(low, lo, 0.0).astype(BF16)
        iq3_o[:, base + 2 * LANES:base + 3 * LANES] = jnp.where(low, hi_sw, hi).astype(BF16)
        iq3_o[:, base + 3 * LANES:base + 4 * LANES] = jnp.where(low, lo_sw, 0.0).astype(BF16)
    ik = _rope_block(misc_ref[...], cos, sin)
    ik_o[...] = ik[:, M_IDXK:M_IDXK + IDX_HD]
    hi = ik.astype(BF16).astype(F32)
    lo = ik - hi
    ik3_o[:, 0:LANES] = jnp.where(low, hi, pltpu.roll(lo, IDX_HD, 1)).astype(BF16)
    ik3_o[:, LANES:2 * LANES] = jnp.where(low, hi, 0.0).astype(BF16)


def _rope_call(h, cos, sin):
    n = h.shape[0]
    tm = _tile(min(n, cos.shape[0]), 512)
    nt = cos.shape[0] // tm

    def col(width, off):
        return pl.BlockSpec((tm, width), lambda i: (i, off // width))

    def out(width):
        return pl.BlockSpec((tm, width), lambda i: (i, 0))

    sds = jax.ShapeDtypeStruct
    tab = pl.BlockSpec((tm, LANES), lambda i: (i % nt, 0))
    return pl.pallas_call(
        _rope_body,
        grid=(n // tm,),
        in_specs=[tab, tab, col(512, C_DSAQ), col(512, C_DIFQ), col(512, C_DIFK), col(512, C_DIFV),
                  col(256, C_IDXQ), col(128, C_DSAK), col(128, C_DSAV), col(128, C_MISC)],
        out_specs=[out(512), out(512), out(512), out(512), out(512), out(1024), out(128), out(128), out(256),
                   out(IDX_HD), out(256)],
        out_shape=[sds((n, 512), BF16), sds((n, 512), BF16), sds((n, 512), F32), sds((n, 512), BF16),
                   sds((n, 512), BF16), sds((n, 1024), BF16), sds((n, 128), F32), sds((n, 128), BF16),
                   sds((n, 256), BF16), sds((n, IDX_HD), F32), sds((n, 256), BF16)],
        compiler_params=_cparams("parallel"),
        name="rope",
    )(cos, sin, h, h, h, h, h, h, h, h)


def _conv_tile(x_ref, buf_ref, cw_ref, xcat, t, tc):
    @pl.when(t == 0)
    def _():
        xcat[0:SUBLANES, :] = buf_ref[...]

    @pl.when(t > 0)
    def _():
        xcat[0:SUBLANES, :] = xcat[tc:tc + SUBLANES, :]

    xcat[SUBLANES:SUBLANES + tc, :] = x_ref[...]
    cw = cw_ref[...]
    y = x_ref[...] * cw[CONV_W - 1:CONV_W, :]
    for j in range(CONV_W - 1):
        off = SUBLANES - (CONV_W - 1) + j
        y = y + xcat[off:off + tc, :] * cw[j:j + 1, :]
    return y


def _lru_body(x_ref, gate_ref, cw_ref, cb_ref, wa_ref, ba_ref, wx_ref, bx_ref, lam_ref, buf_ref, h0_ref,
              o_ref, hl_ref, xcat, hc, *, tc, reset_first):
    t = pl.program_id(1)

    @pl.when(t == 0)
    def _():
        hc[...] = h0_ref[...]

    xa = _conv_tile(x_ref, buf_ref, cw_ref, xcat, t, tc) + cb_ref[...]
    xb = xa.astype(BF16)
    r = _sigmoid(_dot(xb, wa_ref[...]) + ba_ref[...])
    gi = _sigmoid(_dot(xb, wx_ref[...]) + bx_ref[...])
    log_a = -LRU_C * r * _softplus(-lam_ref[...])
    a = jnp.exp(log_a)
    mult = jnp.sqrt(-jnp.tanh(log_a) * (a * a + 1.0))
    row = lax.broadcasted_iota(I32, a.shape, 0)
    if reset_first:
        is0 = jnp.logical_and(row == 0, t == 0)
        a = jnp.where(is0, 0.0, a)
        mult = jnp.where(is0, 1.0, mult)
    u = mult * gi * xa
    d = 1
    while d < tc:
        keep = row >= d
        a_sh = jnp.where(keep, pltpu.roll(a, d, 0), 1.0)
        u_sh = jnp.where(keep, pltpu.roll(u, d, 0), 0.0)
        u = a * u_sh + u
        a = a * a_sh
        d *= 2
    h = a * hc[...] + u
    h_last = h[tc - 1:tc, :]
    hc[...] = h_last
    hl_ref[...] = h_last
    o_ref[...] = (h * jax.nn.gelu(gate_ref[...])).astype(o_ref.dtype)


def _lru_call(h3, wl, buf, h0, reset_first):
    b, t, _ = h3.shape
    tc = _tile(t, 256)
    w = MIX_WIDTH
    row = pl.BlockSpec((1, w), lambda i, j: (0, 0))
    sq = pl.BlockSpec((w, w), lambda i, j: (0, 0))
    return pl.pallas_call(
        functools.partial(_lru_body, tc=tc, reset_first=reset_first),
        grid=(b, t // tc),
        in_specs=[pl.BlockSpec((None, tc, w), lambda i, j: (i, j, C_LRUX // w)),
                  pl.BlockSpec((None, tc, w), lambda i, j: (i, j, C_LRUG // w)),
                  pl.BlockSpec((SUBLANES, w), lambda i, j: (0, 0)), row, sq, row, sq, row, row,
                  pl.BlockSpec((None, SUBLANES, w), lambda i, j: (i, 0, 0)),
                  pl.BlockSpec((None, 1, w), lambda i, j: (i, 0, 0))],
        out_specs=[pl.BlockSpec((None, tc, w), lambda i, j: (i, j, 0)),
                   pl.BlockSpec((None, 1, w), lambda i, j: (i, 0, 0))],
        out_shape=[jax.ShapeDtypeStruct((b, t, w), BF16), jax.ShapeDtypeStruct((b, 1, w), F32)],
        scratch_shapes=[pltpu.VMEM((tc + SUBLANES, w), F32), pltpu.VMEM((1, w), F32)],
        compiler_params=_cparams("parallel", "arbitrary"),
        name="rg_lru",
    )(h3, h3, wl["lru_cw"], wl["lru_cb"], wl["lru_wa"], wl["lru_ba"], wl["lru_wx"], wl["lru_bx"], wl["lru_lam"],
      buf, h0)


def _gdn_body(qkv_ref, z_ref, misc_ref, cw_ref, alog_ref, dtb_ref, ng_ref, buf_ref, s0_ref,
              o_ref, so_ref, xcat, y_scr, state, *, tc, ck):
    t = pl.program_id(1)
    hd = GDN_HD

    @pl.when(t == 0)
    def _():
        state[...] = s0_ref[...]

    y = _conv_tile(qkv_ref, buf_ref, cw_ref, xcat, t, tc)
    y_scr[...] = _silu(y)

    ri = lax.broadcasted_iota(I32, (ck, ck), 0)
    ci = lax.broadcasted_iota(I32, (ck, ck), 1)
    tril = ri >= ci
    strict = ri > ci
    ltri = jnp.where(tril, 1.0, 0.0).astype(F32)
    ones = jnp.ones((ck, ck), F32)
    n_sq = max(int(math.log2(ck)) - 1, 0)

    def chunk(c, carry):
        r0 = pl.multiple_of(c * ck, ck)
        heads = range(GDN_HEADS)
        misc = misc_ref[pl.ds(r0, ck), :]
        beta_all = _sigmoid(misc)
        g_all = -jnp.exp(alog_ref[...]) * _softplus(misc + dtb_ref[...])
        cols = lambda blk: slice(blk * hd, (blk + 1) * hd)
        q = [y_scr[pl.ds(r0, ck), cols(hh)] for hh in heads]
        k = [y_scr[pl.ds(r0, ck), cols(GDN_HEADS + hh)] for hh in heads]
        v = [y_scr[pl.ds(r0, ck), cols(2 * GDN_HEADS + hh)] for hh in heads]
        q = [x * lax.rsqrt(jnp.sum(x * x, axis=-1, keepdims=True) + EPS) * (hd ** -0.5) for x in q]
        k = [x * lax.rsqrt(jnp.sum(x * x, axis=-1, keepdims=True) + EPS) for x in k]
        beta = [jnp.broadcast_to(beta_all[:, M_GDNB + hh:M_GDNB + hh + 1], (ck, hd)) for hh in heads]
        g128 = [jnp.broadcast_to(g_all[:, M_GDNA + hh:M_GDNA + hh + 1], (ck, hd)) for hh in heads]
        gc128 = [_dot_exact_lhs(ltri, g) for g in g128]
        gc_row = [_dot_exact_lhs(ones, jnp.where(ri <= ci, g[:, :ck], 0.0)) for g in g128]
        decay = [jnp.where(tril, jnp.exp(jnp.where(tril, gc128[hh][:, :ck] - gc_row[hh], 0.0)), 0.0) for hh in heads]
        kb = [k[hh] * beta[hh] for hh in heads]
        egc = [jnp.exp(g) for g in gc128]
        low = [jnp.where(strict, _dot3(kb[hh], k[hh], _NT) * decay[hh], 0.0) for hh in heads]
        intra = [jnp.where(tril, _dot3(q[hh], k[hh], _NT) * decay[hh], 0.0) for hh in heads]
        mp = [-x for x in low]
        x = [jnp.concatenate([v[hh] * beta[hh], kb[hh] * egc[hh]], axis=-1) for hh in heads]
        x = [x[hh] + _dot3(mp[hh], x[hh]) for hh in heads]
        for _ in range(n_sq):
            mp = [_dot3(m, m) for m in mp]
            x = [x[hh] + _dot3(mp[hh], x[hh]) for hh in heads]
        s = [state[hh] for hh in heads]
        v_new = [x[hh][:, :hd] - _dot3(x[hh][:, hd:], s[hh]) for hh in heads]
        o = [_dot3(q[hh] * egc[hh], s[hh]) + _dot3(intra[hh], v_new[hh]) for hh in heads]
        g_last = [g[ck - 1:ck, :] for g in gc128]
        kd = [k[hh] * jnp.exp(g_last[hh] - gc128[hh]) for hh in heads]
        s_new = [s[hh] * jnp.exp(g_last[hh]) + _dot3(kd[hh], v_new[hh], _TN) for hh in heads]
        for hh in heads:
            zz = z_ref[pl.ds(r0, ck), cols(hh)]
            state[hh] = s_new[hh]
            o_ref[pl.ds(r0, ck), cols(hh)] = (_rms_rows(o[hh], ng_ref[...]) * _silu(zz)).astype(o_ref.dtype)
        return carry

    lax.fori_loop(0, tc // ck, chunk, 0)
    so_ref[...] = state[...]


def _gdn_call(h3, wl, buf, s0):
    b, t, _ = h3.shape
    ck = min(GDN_CHUNK, t)
    assert t % ck == 0
    tc = _tile(t, 256)
    w = GDN_QKV
    row = pl.BlockSpec((1, LANES), lambda i, j: (0, 0))
    st = pl.BlockSpec((None, GDN_HEADS, GDN_HD, GDN_HD), lambda i, j: (i, 0, 0, 0))
    return pl.pallas_call(
        functools.partial(_gdn_body, tc=tc, ck=ck),
        grid=(b, t // tc),
        in_specs=[pl.BlockSpec((None, tc, w), lambda i, j: (i, j, C_GDNQKV // w)),
                  pl.BlockSpec((None, tc, MIX_WIDTH), lambda i, j: (i, j, C_GDNZ // MIX_WIDTH)),
                  pl.BlockSpec((None, tc, LANES), lambda i, j: (i, j, C_MISC // LANES)),
                  pl.BlockSpec((SUBLANES, w), lambda i, j: (0, 0)), row, row, row,
                  pl.BlockSpec((None, SUBLANES, w), lambda i, j: (i, 0, 0)), st],
        out_specs=[pl.BlockSpec((None, tc, MIX_WIDTH), lambda i, j: (i, j, 0)), st],
        out_shape=[jax.ShapeDtypeStruct((b, t, MIX_WIDTH), BF16),
                   jax.ShapeDtypeStruct((b, GDN_HEADS, GDN_HD, GDN_HD), F32)],
        scratch_shapes=[pltpu.VMEM((tc + SUBLANES, w), F32), pltpu.VMEM((tc, w), F32),
                        pltpu.VMEM((GDN_HEADS, GDN_HD, GDN_HD), F32)],
        compiler_params=_cparams("parallel", "arbitrary"),
        name="gated_deltanet",
    )(h3, h3, h3, wl["gdn_cw"], wl["gdn_alog"], wl["gdn_dtb"], wl["gdn_ng"], buf, s0)


def _lane_tile(x, n):
    return x if n == LANES else jnp.concatenate([x] * (n // LANES), axis=1)


def _online_update(s, v_ext, m_ref, acc_ref, idx):
    m_old = m_ref[idx]
    m_new = jnp.maximum(m_old, jnp.max(s, axis=-1, keepdims=True))
    alpha = jnp.exp(m_old - m_new)
    p = jnp.exp(s - _lane_tile(m_new, s.shape[1]))
    acc = acc_ref[idx]
    acc_ref[idx] = _lane_tile(alpha, acc.shape[1]) * acc + _dot(p.astype(BF16), v_ext)
    m_ref[idx] = m_new


def _online_update_blocks(s, vs, m_ref, l_ref, acc_ref, idx):
    m_old = m_ref[idx]
    m_new = jnp.maximum(m_old, jnp.max(s, axis=-1, keepdims=True))
    alpha = jnp.exp(m_old - m_new)
    p = jnp.exp(s - m_new)
    l_ref[idx] = alpha * l_ref[idx] + jnp.sum(p, axis=-1, keepdims=True)
    pb = p.astype(BF16)
    pv, off = None, 0
    for vblk in vs:
        part = _dot(pb[:, off:off + vblk.shape[0]], vblk)
        pv = part if pv is None else pv + part
        off += vblk.shape[0]
    acc_ref[idx] = alpha * acc_ref[idx] + pv
    m_ref[idx] = m_new


def _reset_softmax(m_ref, l_ref, acc_ref):
    m_ref[...] = jnp.full(m_ref.shape, NEG_BIG, F32)
    l_ref[...] = jnp.zeros(l_ref.shape, F32)
    acc_ref[...] = jnp.zeros(acc_ref.shape, F32)


def _diff_lambda(lam_ref, lam_init):
    lm = lam_ref[...]
    e1 = jnp.exp(jnp.sum(lm[0:1, :] * lm[1:2, :], axis=-1, keepdims=True))
    e2 = jnp.exp(jnp.sum(lm[2:3, :] * lm[3:4, :], axis=-1, keepdims=True))
    return e1 - e2 + lam_init


def _diff_prompt_body(qi_ref, kj_ref, lam_ref, sg_ref, q_ref, k_ref, v_ref, o_ref, m_scr, acc_scr, *, tq, lam_init):
    step = pl.program_id(2)
    qi = qi_ref[step]
    kj = kj_ref[step]
    hv = 2 * DIFF_HD

    @pl.when(kj == 0)
    def _():
        m_scr[...] = jnp.full(m_scr.shape, NEG_BIG, F32)
        acc_scr[...] = jnp.zeros(acc_scr.shape, F32)

    def tile(masked):
        v = v_ref[...]
        v_ext = jnp.concatenate([v, jnp.ones_like(v)], axis=1)
        s = [_dot_nt(q_ref[:, mm * DIFF_HD:(mm + 1) * DIFF_HD], k_ref[:, mm * DIFF_HD:(mm + 1) * DIFF_HD])
             for mm in range(2)]
        for mm in range(2):
            sm = s[mm]
            if masked:
                row = lax.broadcasted_iota(I32, (tq, tq), 0)
                col = lax.broadcasted_iota(I32, (tq, tq), 1)
                sm = jnp.where(col <= row, sm, NEG_BIG)
            _online_update(sm, v_ext, m_scr, acc_scr, mm)

    @pl.when(kj < qi)
    def _():
        tile(False)

    @pl.when(kj == qi)
    def _():
        tile(True)
        lam = _diff_lambda(lam_ref, lam_init)
        o = acc_scr[0, :, 0:hv] / acc_scr[0, :, hv:2 * hv] - lam * (acc_scr[1, :, 0:hv] / acc_scr[1, :, hv:2 * hv])
        o_ref[...] = (_rms_rows(o, sg_ref[...]) * (1.0 - lam_init)).astype(o_ref.dtype)


def _diff_prompt_call(fq3, fkb3, fvb3, wl, lam_init):
    b, t, _ = fq3.shape
    tq = _tile(t, 1024)
    nq = t // tq
    hv = 2 * DIFF_HD
    pairs = [(q, k) for q in range(nq) for k in range(q + 1)]
    qi_of = jnp.asarray([p[0] for p in pairs], I32)
    kj_of = jnp.asarray([p[1] for p in pairs], I32)
    grid_spec = pltpu.PrefetchScalarGridSpec(
        num_scalar_prefetch=2,
        grid=(b, DIFF_HEADS, len(pairs)),
        in_specs=[pl.BlockSpec((4, DIFF_HD), lambda i, h, s, qi, kj: (0, 0)),
                  pl.BlockSpec((1, hv), lambda i, h, s, qi, kj: (0, 0)),
                  pl.BlockSpec((None, tq, hv), lambda i, h, s, qi, kj: (i, qi[s], h)),
                  pl.BlockSpec((None, tq, hv), lambda i, h, s, qi, kj: (i, kj[s], h)),
                  pl.BlockSpec((None, tq, hv), lambda i, h, s, qi, kj: (i, kj[s], h))],
        out_specs=pl.BlockSpec((None, tq, hv), lambda i, h, s, qi, kj: (i, qi[s], h)),
        scratch_shapes=[pltpu.VMEM((2, tq, LANES), F32), pltpu.VMEM((2, tq, 2 * hv), F32)],
    )
    return pl.pallas_call(
        functools.partial(_diff_prompt_body, tq=tq, lam_init=lam_init),
        grid_spec=grid_spec,
        out_shape=jax.ShapeDtypeStruct((b, t, MIX_WIDTH), BF16),
        compiler_params=_cparams("parallel", "parallel", "arbitrary"),
        name="diff_attn_prompt",
    )(qi_of, kj_of, wl["diff_lam"], wl["diff_sg"], fq3, fkb3, fvb3)


def _diff_sample_body(pt_ref, lam_ref, sg_ref, q_ref, kn_ref, vn_ref, *rest, tq, npg, lam_init):
    kp = rest[:npg]
    vp = rest[npg:2 * npg]
    o_ref, qrows, m_scr, l_scr, acc_scr = rest[2 * npg:]
    s_id = pl.program_id(1)
    n_steps = pl.num_programs(1)
    nrow = 2 * DIFF_HEADS * tq
    hv = 2 * DIFF_HD
    prow = PAGE_SIZE * DIFF_HEADS

    @pl.when(s_id == 0)
    def _():
        _reset_softmax(m_scr, l_scr, acc_scr)
        qrows[...] = jnp.zeros(qrows.shape, F32)
        q = q_ref[...].astype(F32)
        for j in range(2 * DIFF_HEADS):
            mm = j % 2
            qrows[j * tq:(j + 1) * tq, mm * DIFF_HD:(mm + 1) * DIFF_HD] = q[:, j * DIFF_HD:(j + 1) * DIFF_HD]

    qb = qrows[...].astype(BF16)
    row_head = lax.broadcasted_iota(I32, (nrow, prow), 0) // (2 * tq)

    @pl.when(s_id < n_steps - 1)
    def _():
        key_head = lax.broadcasted_iota(I32, (nrow, prow), 1) % DIFF_HEADS
        own = key_head == row_head
        scores = [jnp.where(own, _dot_nt(qb, kp[p][...].astype(BF16)), NEG_BIG) for p in range(npg)]
        _online_update_blocks(jnp.concatenate(scores, axis=-1), [vp[p][...].astype(BF16) for p in range(npg)],
                       m_scr, l_scr, acc_scr, Ellipsis)

    @pl.when(s_id == n_steps - 1)
    def _():
        kn = jnp.concatenate([kn_ref[:, hh * hv:(hh + 1) * hv] for hh in range(DIFF_HEADS)], axis=0).astype(BF16)
        vn = jnp.concatenate([vn_ref[:, hh * hv:(hh + 1) * hv] for hh in range(DIFF_HEADS)], axis=0).astype(BF16)
        nk = DIFF_HEADS * tq
        rr = lax.broadcasted_iota(I32, (nrow, nk), 0)
        cc = lax.broadcasted_iota(I32, (nrow, nk), 1)
        ok = jnp.logical_and(cc // tq == rr // (2 * tq), cc % tq <= rr % tq)
        s = jnp.where(ok, _dot_nt(qb, kn), NEG_BIG)
        _online_update_blocks(s, [vn], m_scr, l_scr, acc_scr, Ellipsis)
        lam = _diff_lambda(lam_ref, lam_init)
        acc = acc_scr[...] / l_scr[...]
        for hh in range(DIFF_HEADS):
            r1 = (2 * hh) * tq
            r2 = (2 * hh + 1) * tq
            o = acc[r1:r1 + tq, :] - lam * acc[r2:r2 + tq, :]
            o_ref[:, hh * hv:(hh + 1) * hv] = (_rms_rows(o, sg_ref[...]) * (1.0 - lam_init)).astype(o_ref.dtype)


def _diff_sample_call(fq3, fk3, h3, pool_k, pool_v, page_table, layer, wl, lam_init):
    db, t, _ = fq3.shape
    n_pages = page_table.shape[1]
    npg = _pages_per_step(n_pages)
    n_steps = n_pages // npg + 1
    w = MIX_WIDTH
    hv = 2 * DIFF_HD
    nrow = 2 * DIFF_HEADS * t
    prow = PAGE_SIZE * DIFF_HEADS

    def page_spec(p):
        def imap(i, s, pt):
            return (layer, pt[i, jnp.minimum(s, n_steps - 2) * npg + p], 0, 0)
        return pl.BlockSpec((None, None, prow, hv), imap)

    grid_spec = pltpu.PrefetchScalarGridSpec(
        num_scalar_prefetch=1,
        grid=(db, n_steps),
        in_specs=[pl.BlockSpec((4, DIFF_HD), lambda i, s, pt: (0, 0)),
                  pl.BlockSpec((1, hv), lambda i, s, pt: (0, 0)),
                  pl.BlockSpec((None, t, w), lambda i, s, pt: (i, 0, 0)),
                  pl.BlockSpec((None, t, w), lambda i, s, pt: (i, 0, 0)),
                  pl.BlockSpec((None, t, w), lambda i, s, pt: (i, 0, C_DIFV // w))]
                 + [page_spec(p) for p in range(npg)] + [page_spec(p) for p in range(npg)],
        out_specs=pl.BlockSpec((None, t, w), lambda i, s, pt: (i, 0, 0)),
        scratch_shapes=[pltpu.VMEM((nrow, hv), F32), pltpu.VMEM((nrow, 1), F32), pltpu.VMEM((nrow, 1), F32),
                        pltpu.VMEM((nrow, hv), F32)],
    )
    return pl.pallas_call(
        functools.partial(_diff_sample_body, tq=t, npg=npg, lam_init=lam_init),
        grid_spec=grid_spec,
        out_shape=jax.ShapeDtypeStruct((db, t, w), BF16),
        compiler_params=_cparams("parallel", "arbitrary"),
        name="diff_attn_sample",
    )(page_table, wl["diff_lam"], wl["diff_sg"], fq3, fk3, h3, *([pool_k] * npg), *([pool_v] * npg))


def _sortable(score):
    bits = pltpu.bitcast(jnp.where(score == 0.0, 0.0, score), I32)
    return bits ^ ((bits >> 31) & 0x7FFFFFFF)


def _search_threshold(count_ge, shape, n_sel):
    def bit_body(i, t_u):
        cand_u = t_u | (jnp.int32(1) << (31 - i))
        cnt = count_ge(cand_u ^ INT_MIN)
        return jnp.where(cnt >= n_sel, cand_u, t_u)

    t_u = lax.fori_loop(0, 32, bit_body, jnp.zeros(shape, I32))
    return t_u ^ INT_MIN


def _lane_fold(x):
    acc = x[:, 0:LANES]
    for c in range(1, x.shape[1] // LANES):
        acc = acc + x[:, c * LANES:(c + 1) * LANES]
    return acc


def _strictly_before(n):
    ui = lax.broadcasted_iota(I32, (n, n), 0)
    uj = lax.broadcasted_iota(I32, (n, n), 1)
    return jnp.where(ui < uj, 1.0, 0.0).astype(BF16)


def _dsa_prompt_body(iq3_ref, misc_ref, ik3_ref, q_ref, k_ref, v_ref, o_ref, iqs, qs, keys_scr, half_scr, bias_scr,
                     m_scr, acc_scr, w_scr, spread_scr, c16_scr, cnt_scr, cnt16_scr, *, tq, ck, n_sel):
    qi = pl.program_id(1)
    nck = (qi * tq + tq + ck - 1) // ck
    kw = iqs.shape[1]
    for hh in range(IDX_HEADS):
        iqs[hh * tq:(hh + 1) * tq, :] = iq3_ref[:, hh * kw:(hh + 1) * kw]
    wts = misc_ref[...][:, M_IDXW:M_IDXW + IDX_HEADS] * (IDX_HD ** -0.5 * IDX_HEADS ** -0.5)
    for hh in range(IDX_HEADS):
        w_scr[hh] = jnp.broadcast_to(wts[:, hh:hh + 1], (tq, LANES))
    row_pos = qi * tq + lax.broadcasted_iota(I32, (tq, ck), 0)
    lane_pos = lax.broadcasted_iota(I32, (tq, ck), 1)

    def score_chunk(c, carry):
        k0 = pl.multiple_of(c * ck, ck)
        s = jnp.maximum(_dot_nt(iqs[...], ik3_ref[pl.ds(k0, ck), :]), 0.0)
        sc = _lane_tile(w_scr[0], ck) * s[0:tq, :]
        for hh in range(1, IDX_HEADS):
            sc = sc + _lane_tile(w_scr[hh], ck) * s[hh * tq:(hh + 1) * tq, :]
        keys_scr[c] = jnp.where(c * ck + lane_pos <= row_pos, _sortable(sc), INT_MIN)
        return carry

    lax.fori_loop(0, nck, score_chunk, 0)

    def parked(slot):
        return _lane_tile(spread_scr[slot], ck)

    def count(pred):
        cnt_scr[...] = jnp.zeros((tq, LANES), I32)

        def body(c, carry):
            cnt_scr[...] += _lane_fold(jnp.where(pred(keys_scr[c]), 1, 0).astype(I32))
            return carry
        lax.fori_loop(0, nck, body, 0)
        return jnp.sum(cnt_scr[...], axis=-1, keepdims=True)

    ones_sq = jnp.ones((LANES, LANES), BF16)

    def search_half(half_scr):
        def bit_body(i, t_u):
            cand_u = t_u | (jnp.int32(1) << (15 - i))
            c16_scr[...] = (cand_u - HALF_BIAS).astype(I16)
            cnt16_scr[...] = jnp.zeros((tq, LANES), I16)

            def body(c, carry):
                hit = half_scr[c] >= _lane_tile(c16_scr[...], ck)
                cnt16_scr[...] += _lane_fold(jnp.where(hit, jnp.int16(1), jnp.int16(0)))
                return carry
            lax.fori_loop(0, nck, body, 0)
            tot = _dot(cnt16_scr[...].astype(F32).astype(BF16), ones_sq)
            return jnp.where(tot >= n_sel, cand_u, t_u)
        return lax.fori_loop(0, 16, bit_body, jnp.zeros((tq, LANES), I32)) - HALF_BIAS

    def split_hi(c, carry):
        half_scr[c] = (keys_scr[c] >> 16).astype(I16)
        return carry
    lax.fori_loop(0, nck, split_hi, 0)
    t_hi = search_half(half_scr)

    spread_scr[0] = t_hi

    def split_lo(c, carry):
        kk = keys_scr[c]
        hi = kk >> 16
        lo = (kk & 0xFFFF) - HALF_BIAS
        t_hi_b = parked(0)
        half_scr[c] = jnp.where(hi == t_hi_b, lo, jnp.where(hi > t_hi_b, HALF_BIAS - 1, -HALF_BIAS)).astype(I16)
        return carry
    lax.fori_loop(0, nck, split_lo, 0)
    t_lo = search_half(half_scr)
    thr_rep = (t_hi << 16) | ((t_lo + HALF_BIAS) & 0xFFFF)
    spread_scr[1] = thr_rep
    thr = thr_rep[:, 0:1]
    n_ge = count(lambda kk: kk >= parked(1))
    plain = jnp.max(jnp.where(jnp.logical_and(n_ge == n_sel, thr != INT_MIN), 0, 1)) == 0

    @pl.when(plain)
    def _():
        def body(c, carry):
            bias_scr[c] = jnp.where(keys_scr[c] >= parked(1), 0.0, NEG_BIG)
            return carry
        lax.fori_loop(0, nck, body, 0)

    @pl.when(jnp.logical_not(plain))
    def _():
        need = (n_sel - count(lambda kk: kk > parked(1))).astype(F32)
        before = _strictly_before(ck)

        def body(c, eq_seen):
            kk = keys_scr[c]
            thr_b = parked(1)
            eq = jnp.logical_and(kk == thr_b, c * ck + lane_pos <= row_pos)
            eqf = jnp.where(eq, 1.0, 0.0)
            rank = _dot(eqf.astype(BF16), before) + eq_seen
            sel = jnp.logical_or(kk > thr_b, jnp.logical_and(eq, rank < need))
            bias_scr[c] = jnp.where(sel, 0.0, NEG_BIG)
            return eq_seen + jnp.sum(eqf, axis=-1, keepdims=True)

        lax.fori_loop(0, nck, body, jnp.zeros((tq, 1), F32))

    m_scr[...] = jnp.full(m_scr.shape, NEG_BIG, F32)
    acc_scr[...] = jnp.zeros(acc_scr.shape, F32)
    hpg = DSA_HEADS // DSA_KV_HEADS
    for hd in range(DSA_HEADS):
        qs[hd * tq:(hd + 1) * tq, :] = q_ref[:, hd * DSA_HD:(hd + 1) * DSA_HD]

    def attend_chunk(c, carry):
        k0 = pl.multiple_of(c * ck, ck)
        bias = bias_scr[c]
        groups = range(DSA_KV_HEADS)
        rows = [pl.ds(g * hpg * tq, hpg * tq) for g in groups]
        s = [_dot_nt(qs[rows[g], :], k_ref[pl.ds(k0, ck), g * DSA_HD:(g + 1) * DSA_HD]) for g in groups]
        for g in groups:
            vg = v_ref[pl.ds(k0, ck), g * LANES:(g + 1) * LANES]
            sg = s[g].reshape(hpg, tq, ck) + bias[None]
            _online_update(sg.reshape(hpg * tq, ck), vg, m_scr, acc_scr, rows[g])
        return carry

    lax.fori_loop(0, nck, attend_chunk, 0)
    acc = acc_scr[...]
    out = acc / pltpu.roll(acc, DSA_HD, 1)
    for hd in range(DSA_HEADS):
        o_ref[:, hd * DSA_HD:(hd + 1) * DSA_HD] = out[hd * tq:(hd + 1) * tq, 0:DSA_HD].astype(o_ref.dtype)


def _dsa_prompt_call(iq33, ik33, dq3, dkb3, dvb3, h3):
    b, t, _ = dq3.shape
    n_sel = min(TOPK_MAX, t // 4)
    tq = _tile(t, 512)
    ck = _tile(t, 512)
    kw = iq33.shape[-1] // IDX_HEADS
    full = lambda width: pl.BlockSpec((None, t, width), lambda i, q: (i, 0, 0))
    return pl.pallas_call(
        functools.partial(_dsa_prompt_body, tq=tq, ck=ck, n_sel=n_sel),
        grid=(b, t // tq),
        in_specs=[pl.BlockSpec((None, tq, IDX_HEADS * kw), lambda i, q: (i, q, 0)),
                  pl.BlockSpec((None, tq, LANES), lambda i, q: (i, q, C_MISC // LANES)),
                  full(kw),
                  pl.BlockSpec((None, tq, MIX_WIDTH), lambda i, q: (i, q, 0)),
                  full(LANES), full(DSA_KV_HEADS * LANES)],
        out_specs=pl.BlockSpec((None, tq, MIX_WIDTH), lambda i, q: (i, q, 0)),
        out_shape=jax.ShapeDtypeStruct((b, t, MIX_WIDTH), BF16),
        scratch_shapes=[pltpu.VMEM((IDX_HEADS * tq, kw), BF16), pltpu.VMEM((DSA_HEADS * tq, DSA_HD), BF16),
                        pltpu.VMEM((t // ck, tq, ck), I32), pltpu.VMEM((t // ck, tq, ck), I16),
                        pltpu.VMEM((t // ck, tq, ck), F32),
                        pltpu.VMEM((DSA_HEADS * tq, LANES), F32), pltpu.VMEM((DSA_HEADS * tq, LANES), F32),
                        pltpu.VMEM((IDX_HEADS, tq, LANES), F32), pltpu.VMEM((2, tq, LANES), I32),
                        pltpu.VMEM((tq, LANES), I16), pltpu.VMEM((tq, LANES), I32), pltpu.VMEM((tq, LANES), I16)],
        compiler_params=pltpu.CompilerParams(dimension_semantics=("parallel", "arbitrary"),
                                             vmem_limit_bytes=VMEM_LIMIT_DSA_PROMPT),
        name="dsa_prompt",
    )(iq33, h3, ik33, dq3, dkb3, dvb3)


def _dsa_select_body(pt_ref, iq3_ref, misc_ref, ikn_ref, *rest, tq, npg, ns, n_sel, n_pages):
    pages = rest[:ns * npg]
    sel_ref, keys_scr, iqs = rest[ns * npg:]
    s_id = pl.program_id(1)
    n_steps = pl.num_programs(1)
    kw = iqs.shape[-1]
    seqs = range(ns)
    wts = [misc_ref[q][:, M_IDXW:M_IDXW + IDX_HEADS] * (IDX_HD ** -0.5 * IDX_HEADS ** -0.5) for q in seqs]

    @pl.when(s_id == 0)
    def _():
        for q in seqs:
            for hh in range(IDX_HEADS):
                iqs[q, hh * tq:(hh + 1) * tq, :] = iq3_ref[q, :, hh * kw:(hh + 1) * kw]

    def scores(q, ik, feature_major):
        hi, lo = _split_hi_lo(ik)
        a_hi = iqs[q, :, 0:IDX_HD]
        a_lo = iqs[q, :, 2 * IDX_HD:3 * IDX_HD]
        mm = _dot if feature_major else _dot_nt
        s = jnp.maximum(mm(a_hi, hi) + mm(a_hi, lo) + mm(a_lo, hi), 0.0)
        sc = wts[q][:, 0:1] * s[0:tq, :]
        for hh in range(1, IDX_HEADS):
            sc = sc + wts[q][:, hh:hh + 1] * s[hh * tq:(hh + 1) * tq, :]
        return sc

    @pl.when(s_id < n_steps - 1)
    def _():
        for q in seqs:
            ik = jnp.concatenate([pages[q * npg + p][...] for p in range(npg)], axis=1)
            keys = _sortable(scores(q, ik, True))
            for p in range(npg):
                keys_scr[q, s_id * npg + p] = keys[:, p * PAGE_SIZE:(p + 1) * PAGE_SIZE]

    @pl.when(s_id == n_steps - 1)
    def _():
        qrow = lax.broadcasted_iota(I32, (tq, tq), 0)
        kcol = lax.broadcasted_iota(I32, (tq, tq), 1)
        for q in seqs:
            new_keys = jnp.where(kcol <= qrow, _sortable(scores(q, ikn_ref[q], False)), INT_MIN)
            keys_scr[q, n_pages] = jnp.concatenate([new_keys, jnp.full((tq, LANES - tq), INT_MIN, I32)], axis=-1)

        def count(pred):
            acc = jnp.sum(jnp.where(pred(keys_scr[...]), 1, 0).astype(I32), axis=1)
            return jnp.sum(acc, axis=-1, keepdims=True)

        thr = _search_threshold(lambda cand: count(lambda kk: kk >= cand[:, None]), (ns, tq, 1), n_sel)
        n_ge = count(lambda kk: kk >= thr[:, None])
        plain = jnp.max(jnp.where(jnp.logical_and(n_ge == n_sel, thr != INT_MIN), 0, 1)) == 0

        @pl.when(plain)
        def _():
            sel_ref[...] = jnp.where(keys_scr[...] >= thr[:, None], 1.0, 0.0)

        @pl.when(jnp.logical_not(plain))
        def _():
            need = (n_sel - count(lambda kk: kk > thr[:, None])).astype(F32)
            before = _strictly_before(LANES)
            lane = lax.broadcasted_iota(I32, (tq, LANES), 1)
            qr = lax.broadcasted_iota(I32, (tq, LANES), 0)
            for q in seqs:
                def emit(c, eq_seen, q=q):
                    kk = keys_scr[q, c]
                    allowed = jnp.logical_or(c < n_pages, lane <= qr)
                    eq = jnp.logical_and(kk == thr[q], allowed)
                    eqf = jnp.where(eq, 1.0, 0.0)
                    rank = _dot(eqf.astype(BF16), before) + eq_seen
                    sel = jnp.logical_or(kk > thr[q], jnp.logical_and(eq, rank < need[q]))
                    sel_ref[q, c] = jnp.where(sel, 1.0, 0.0)
                    return eq_seen + jnp.sum(eqf, axis=-1, keepdims=True)

                lax.fori_loop(0, n_pages + 1, emit, jnp.zeros((tq, 1), F32))


def _dsa_attend_body(pt_ref, q_ref, kn_ref, vn_ref, selp_ref, seln_ref, *rest, tq, npg):
    kp = rest[:npg]
    vp = rest[npg:2 * npg]
    o_ref, qrows, m_scr, l_scr, acc_scr = rest[2 * npg:]
    s_id = pl.program_id(1)
    n_steps = pl.num_programs(1)
    hpg = DSA_HEADS // DSA_KV_HEADS
    grows = hpg * tq

    @pl.when(s_id == 0)
    def _():
        _reset_softmax(m_scr, l_scr, acc_scr)
        for hd in range(DSA_HEADS):
            qrows[hd * tq:(hd + 1) * tq, :] = q_ref[:, hd * DSA_HD:(hd + 1) * DSA_HD].astype(F32)

    qb = qrows[...].astype(BF16)
    qg = [qb[g * grows:(g + 1) * grows, :] for g in range(DSA_KV_HEADS)]

    def update(s, pv_fn):
        m_old = m_scr[...]
        m_new = jnp.maximum(m_old, jnp.max(s, axis=-1, keepdims=True))
        alpha = jnp.exp(m_old - m_new)
        p = jnp.exp(s - m_new)
        l_scr[...] = alpha * l_scr[...] + jnp.sum(p, axis=-1, keepdims=True)
        pb = p.astype(BF16)
        pv = jnp.concatenate([pv_fn(pb[g * grows:(g + 1) * grows, :], g) for g in range(DSA_KV_HEADS)], axis=0)
        acc_scr[...] = alpha * acc_scr[...] + pv
        m_scr[...] = m_new

    @pl.when(s_id < n_steps - 1)
    def _():
        kcat = [jnp.concatenate([kp[p][g] for p in range(npg)], axis=1).astype(BF16) for g in range(DSA_KV_HEADS)]
        vcat = [jnp.concatenate([vp[p][g] for p in range(npg)], axis=1).astype(BF16) for g in range(DSA_KV_HEADS)]
        s = jnp.concatenate([_dot(qg[g], kcat[g]) for g in range(DSA_KV_HEADS)], axis=0)
        sel1 = jnp.concatenate([selp_ref[p] for p in range(npg)], axis=1)
        sel = jnp.concatenate([sel1] * DSA_HEADS, axis=0) > 0.5
        update(jnp.where(sel, s, NEG_BIG), lambda pb, g: _dot_nt(pb, vcat[g]))

    @pl.when(s_id == n_steps - 1)
    def _():
        kn = kn_ref[...].astype(BF16)
        vn = vn_ref[...].astype(BF16)
        s = jnp.concatenate([_dot_nt(qg[g], kn[:, g * DSA_HD:(g + 1) * DSA_HD]) for g in range(DSA_KV_HEADS)], axis=0)
        sel = jnp.concatenate([seln_ref[...][:, 0:tq]] * DSA_HEADS, axis=0) > 0.5
        update(jnp.where(sel, s, NEG_BIG), lambda pb, g: _dot(pb, vn[:, g * DSA_HD:(g + 1) * DSA_HD]))
        acc = acc_scr[...] / l_scr[...]
        for hd in range(DSA_HEADS):
            o_ref[:, hd * DSA_HD:(hd + 1) * DSA_HD] = acc[hd * tq:(hd + 1) * tq, :].astype(o_ref.dtype)


def _dsa_sample_call(iq33, ik3, dq3, dk3, h3, pool_k, pool_v, pool_ik, page_table, layer):
    db, t, _ = dq3.shape
    n_pages = page_table.shape[1]
    n_keys = n_pages * PAGE_SIZE + t
    n_sel = min(TOPK_MAX, n_keys // 4)
    npg = _pages_per_step(n_pages)
    n_steps = n_pages // npg + 1
    kw = iq33.shape[-1] // IDX_HEADS
    sw = LANES

    def page_spec(*tail):
        def make(p):
            def imap(i, s, pt):
                return (layer, pt[i, jnp.minimum(s, n_steps - 2) * npg + p]) + (0,) * len(tail)
            return pl.BlockSpec((None, None) + tail, imap)
        return [make(p) for p in range(npg)]

    ns = 1

    def select_pages():
        def make(q, p):
            def imap(i, s, pt):
                return (layer, pt[i * ns + q, jnp.minimum(s, n_steps - 2) * npg + p], 0, 0)
            return pl.BlockSpec((None, None, IDX_HD, PAGE_SIZE), imap)
        return [make(q, p) for q in range(ns) for p in range(npg)]

    sel = pl.pallas_call(
        functools.partial(_dsa_select_body, tq=t, npg=npg, ns=ns, n_sel=n_sel, n_pages=n_pages),
        grid_spec=pltpu.PrefetchScalarGridSpec(
            num_scalar_prefetch=1,
            grid=(db // ns, n_steps),
            in_specs=[pl.BlockSpec((ns, t, IDX_HEADS * kw), lambda i, s, pt: (i, 0, 0)),
                      pl.BlockSpec((ns, t, LANES), lambda i, s, pt: (i, 0, C_MISC // LANES)),
                      pl.BlockSpec((ns, t, IDX_HD), lambda i, s, pt: (i, 0, 0))]
                     + select_pages(),
            out_specs=pl.BlockSpec((ns, n_pages + 1, t, sw), lambda i, s, pt: (i, 0, 0, 0)),
            scratch_shapes=[pltpu.VMEM((ns, n_pages + 1, t, LANES), I32), pltpu.VMEM((ns, IDX_HEADS * t, kw), BF16)],
        ),
        out_shape=jax.ShapeDtypeStruct((db, n_pages + 1, t, sw), F32),
        compiler_params=_cparams("parallel", "arbitrary"),
        name="dsa_sample_select",
    )(page_table, iq33, h3, ik3, *([pool_ik] * (ns * npg)))

    nrow = DSA_HEADS * t
    return pl.pallas_call(
        functools.partial(_dsa_attend_body, tq=t, npg=npg),
        grid_spec=pltpu.PrefetchScalarGridSpec(
            num_scalar_prefetch=1,
            grid=(db, n_steps),
            in_specs=[pl.BlockSpec((None, t, MIX_WIDTH), lambda i, s, pt: (i, 0, 0)),
                      pl.BlockSpec((None, t, LANES), lambda i, s, pt: (i, 0, 0)),
                      pl.BlockSpec((None, t, LANES), lambda i, s, pt: (i, 0, C_DSAV // LANES)),
                      pl.BlockSpec((None, npg, t, sw), lambda i, s, pt: (i, jnp.minimum(s, n_steps - 2), 0, 0)),
                      pl.BlockSpec((None, None, t, sw), lambda i, s, pt: (i, n_pages, 0, 0))]
                     + page_spec(DSA_KV_HEADS, DSA_HD, PAGE_SIZE) + page_spec(DSA_KV_HEADS, DSA_HD, PAGE_SIZE),
            out_specs=pl.BlockSpec((None, t, MIX_WIDTH), lambda i, s, pt: (i, 0, 0)),
            scratch_shapes=[pltpu.VMEM((nrow, DSA_HD), F32), pltpu.VMEM((nrow, 1), F32), pltpu.VMEM((nrow, 1), F32),
                            pltpu.VMEM((nrow, DSA_HD), F32)],
        ),
        out_shape=jax.ShapeDtypeStruct((db, t, MIX_WIDTH), BF16),
        compiler_params=_cparams("parallel", "arbitrary"),
        name="dsa_sample_attend",
    )(page_table, dq3, dk3, h3, sel, sel, *([pool_k] * npg), *([pool_v] * npg))


def _merge_body(x_ref, gate_ref, oa_ref, ob_ref, oc_ref, od_ref, wb_ref, wo_ref, o_ref):
    acc = None
    for bi, br in enumerate((oa_ref, ob_ref, oc_ref, od_ref)):
        proj = _dot(br[...], wb_ref[bi])
        term = _sigmoid(gate_ref[:, bi * D_MODEL:(bi + 1) * D_MODEL]) * proj
        acc = term if acc is None else acc + term
    o_ref[...] = x_ref[...] + _dot(acc.astype(BF16), wo_ref[...])


def _merge_call(x, h, branches, wl):
    n, d = x.shape
    tm = _tile(n, 512)
    br = pl.BlockSpec((tm, MIX_WIDTH), lambda i: (i, 0))
    return pl.pallas_call(
        _merge_body,
        grid=(n // tm,),
        in_specs=[pl.BlockSpec((tm, d), lambda i: (i, 0)),
                  pl.BlockSpec((tm, N_BRANCH * d), lambda i: (i, C_GATE // (N_BRANCH * d))),
                  br, br, br, br,
                  pl.BlockSpec((N_BRANCH, MIX_WIDTH, d), lambda i: (0, 0, 0)),
                  pl.BlockSpec((d, d), lambda i: (0, 0))],
        out_specs=pl.BlockSpec((tm, d), lambda i: (i, 0)),
        out_shape=jax.ShapeDtypeStruct((n, d), F32),
        compiler_params=_cparams("parallel"),
        name="merge",
    )(x, h, *branches, wl["w_branch"], wl["w_out"])


def _cross_body(x_ref, g_ref, wq_ref, mk_ref, mv_ref, wo_ref, o_ref, *, merged):
    x = x_ref[...]
    tm = x.shape[0]
    xn = _rms_rows(x, g_ref[...]).astype(BF16)
    q = _dot(xn, wq_ref[...]).astype(BF16)
    if merged:
        qrows = jnp.concatenate([q[:, hh * XA_HD:(hh + 1) * XA_HD] for hh in range(XA_HEADS)], axis=0)
        s = _dot_nt(qrows, mk_ref[...].astype(BF16)) * (XA_HD ** -0.5)
        rr = lax.broadcasted_iota(I32, s.shape, 0) // tm
        cc = lax.broadcasted_iota(I32, s.shape, 1) % XA_HEADS
        s = jnp.where(rr == cc, s, NEG_BIG)
        p = jnp.exp(s - jnp.max(s, axis=-1, keepdims=True))
        p = p / jnp.sum(p, axis=-1, keepdims=True)
        orow = _dot(p.astype(BF16), mv_ref[...].astype(BF16))
        outs = [orow[hh * tm:(hh + 1) * tm, :] for hh in range(XA_HEADS)]
    else:
        outs = []
        for hh in range(XA_HEADS):
            sl = slice(hh * XA_HD, (hh + 1) * XA_HD)
            s = _dot_nt(q[:, sl], mk_ref[:, sl].astype(BF16)) * (XA_HD ** -0.5)
            p = jnp.exp(s - jnp.max(s, axis=-1, keepdims=True))
            p = p / jnp.sum(p, axis=-1, keepdims=True)
            outs.append(_dot(p.astype(BF16), mv_ref[:, sl].astype(BF16)))
    o = jnp.concatenate(outs, axis=-1).astype(BF16)
    o_ref[...] = x + _dot(o, wo_ref[...])


def _cross_call(x3, wl, mk4, mv4, layer):
    b, t, d = x3.shape
    w = XA_HEADS * XA_HD
    tm = _tile(t, 512)
    merged = mk4.shape[-1] == XA_HD
    mem = pl.BlockSpec((None, None) + mk4.shape[2:], lambda i, j: (layer, i, 0, 0))
    return pl.pallas_call(
        functools.partial(_cross_body, merged=merged),
        grid=(b, t // tm),
        in_specs=[pl.BlockSpec((None, tm, d), lambda i, j: (i, j, 0)),
                  pl.BlockSpec((1, d), lambda i, j: (0, 0)),
                  pl.BlockSpec((d, w), lambda i, j: (0, 0)), mem, mem,
                  pl.BlockSpec((w, d), lambda i, j: (0, 0))],
        out_specs=pl.BlockSpec((None, tm, d), lambda i, j: (i, j, 0)),
        out_shape=jax.ShapeDtypeStruct((b, t, d), F32),
        compiler_params=_cparams("parallel", "parallel"),
        name="cross_attn",
    )(x3, wl["norm_xa_g"], wl["xa_wq"], mk4, mv4, wl["xa_wo"])


def _ffn_body(x_ref, g_ref, wg_ref, wu_ref, wd_ref, o_ref, xn_scr, acc_scr):
    f = pl.program_id(1)

    @pl.when(f == 0)
    def _():
        xn_scr[...] = _rms_rows(x_ref[...], g_ref[...]).astype(BF16)
        acc_scr[...] = x_ref[...]

    xn = xn_scr[...]
    hm = _silu(_dot(xn, wg_ref[...])) * _dot(xn, wu_ref[...])
    acc_scr[...] += _dot(hm.astype(BF16), wd_ref[...])

    @pl.when(f == pl.num_programs(1) - 1)
    def _():
        o_ref[...] = acc_scr[...]


def _ffn_call(x, wl):
    n, d = x.shape
    ff = wl["ffn_wg"].shape[1]
    tm = _tile(n, 1024)
    tf = _tile(ff, 1408)
    return pl.pallas_call(
        _ffn_body,
        grid=(n // tm, ff // tf),
        in_specs=[pl.BlockSpec((tm, d), lambda i, f: (i, 0)), pl.BlockSpec((1, d), lambda i, f: (0, 0)),
                  pl.BlockSpec((d, tf), lambda i, f: (0, f)), pl.BlockSpec((d, tf), lambda i, f: (0, f)),
                  pl.BlockSpec((tf, d), lambda i, f: (f, 0))],
        out_specs=pl.BlockSpec((tm, d), lambda i, f: (i, 0)),
        out_shape=jax.ShapeDtypeStruct((n, d), F32),
        scratch_shapes=[pltpu.VMEM((tm, d), BF16), pltpu.VMEM((tm, d), F32)],
        compiler_params=_cparams("parallel", "arbitrary"),
        name="ffn_swiglu",
    )(x, wl["norm_ffn_g"], wl["ffn_wg"], wl["ffn_wu"], wl["ffn_wd"])


def _moe_body(x_ref, g_ref, wr_ref, br_ref, wg_ref, wu_ref, wd_ref, o_ref, xn_scr, gate_scr, acc_scr):
    e = pl.program_id(1)
    f = pl.program_id(2)
    first = jnp.logical_and(e == 0, f == 0)

    @pl.when(first)
    def _():
        xn = _rms_rows(x_ref[...], g_ref[...])
        xn_scr[...] = xn.astype(BF16)
        acc_scr[...] = x_ref[...]
        logits = _dot_hi(xn, wr_ref[...]) + br_ref[...]
        lane = lax.broadcasted_iota(I32, logits.shape, 1)
        logits = jnp.where(lane < N_EXPERTS, logits, -jnp.inf)
        top1 = jnp.max(logits, axis=-1, keepdims=True)
        idx1 = jnp.min(jnp.where(logits == top1, lane, LANES), axis=-1, keepdims=True)
        rest = jnp.where(lane == idx1, -jnp.inf, logits)
        top2 = jnp.max(rest, axis=-1, keepdims=True)
        idx2 = jnp.min(jnp.where(rest == top2, lane, LANES), axis=-1, keepdims=True)
        e2 = jnp.exp(top2 - top1)
        p1 = 1.0 / (1.0 + e2)
        p2 = e2 / (1.0 + e2)
        gate_scr[...] = jnp.where(lane == idx1, p1, 0.0) + jnp.where(lane == idx2, p2, 0.0)

    gates = gate_scr[...]
    lane = lax.broadcasted_iota(I32, gates.shape, 1)
    ge = jnp.sum(jnp.where(lane == e, gates, 0.0), axis=-1, keepdims=True)
    xn = xn_scr[...]
    hm = _silu(_dot(xn, wg_ref[...])) * _dot(xn, wu_ref[...])
    acc_scr[...] += ge * _dot(hm.astype(BF16), wd_ref[...])

    @pl.when(jnp.logical_and(e == pl.num_programs(1) - 1, f == pl.num_programs(2) - 1))
    def _():
        o_ref[...] = acc_scr[...]


def _moe_call(x, wl):
    n, d = x.shape
    ne, _, fe = wl["moe_wg"].shape
    tm = _tile(n, 1024)
    tf = _tile(fe, 1024)
    return pl.pallas_call(
        _moe_body,
        grid=(n // tm, ne, fe // tf),
        in_specs=[pl.BlockSpec((tm, d), lambda i, e, f: (i, 0)), pl.BlockSpec((1, d), lambda i, e, f: (0, 0)),
                  pl.BlockSpec((d, LANES), lambda i, e, f: (0, 0)), pl.BlockSpec((1, LANES), lambda i, e, f: (0, 0)),
                  pl.BlockSpec((None, d, tf), lambda i, e, f: (e, 0, f)),
                  pl.BlockSpec((None, d, tf), lambda i, e, f: (e, 0, f)),
                  pl.BlockSpec((None, tf, d), lambda i, e, f: (e, f, 0))],
        out_specs=pl.BlockSpec((tm, d), lambda i, e, f: (i, 0)),
        out_shape=jax.ShapeDtypeStruct((n, d), F32),
        scratch_shapes=[pltpu.VMEM((tm, d), BF16), pltpu.VMEM((tm, LANES), F32), pltpu.VMEM((tm, d), F32)],
        compiler_params=_cparams("parallel", "arbitrary", "arbitrary"),
        name="moe_swiglu",
    )(x, wl["norm_ffn_g"], wl["moe_wr"], wl["moe_br"], wl["moe_wg"], wl["moe_wu"], wl["moe_wd"])


def _pad_rows(a, rows):
    return jnp.pad(a, ((0, rows - a.shape[0]), (0, 0)))


def _lane_row(vals, lane0):
    return jnp.zeros((1, LANES), F32).at[0, lane0:lane0 + vals.shape[0]].set(vals.astype(F32))


def _block_diag(w):
    nb, n, _ = w.shape
    eye = jnp.eye(nb, dtype=w.dtype)
    return jnp.einsum("aij,ab->aibj", w, eye).reshape(nb * n, nb * n)


def _layer_weights(l, p):
    w_t = jnp.transpose(p["w_in"], (2, 0, 1))[:, l, :]
    d = w_t.shape[1]
    misc = jnp.concatenate([w_t[O_IDXK:O_IDXK + IDX_HD + IDX_HEADS], w_t[O_GDNB:O_GDNB + 2 * GDN_HEADS],
                            jnp.zeros((LANES - IDX_HD - IDX_HEADS - 2 * GDN_HEADS, d), w_t.dtype)], axis=0)
    w_in_r = jnp.concatenate([
        w_t[O_GATE:], w_t[O_LRUX:O_DSAK], w_t[O_DIFQ:O_GATE], w_t[O_GDNZ:O_GDNB],
        w_t[O_GDNQKV:O_GDNZ], w_t[O_IDXQ:O_IDXK], w_t[O_DSAK:O_IDXQ], misc], axis=0)
    assert w_in_r.shape[0] == H_COLS
    wl = dict(
        norm_mix_g=p["norm_mix_g"][l], w_in=w_in_r.astype(BF16),
        lru_cw=_pad_rows(p["lru_conv_w"][l], SUBLANES), lru_cb=p["lru_conv_b"][l].reshape(1, -1),
        lru_wa=_block_diag(p["lru_wa"][l]).astype(BF16), lru_ba=p["lru_ba"][l].reshape(1, -1),
        lru_wx=_block_diag(p["lru_wx"][l]).astype(BF16), lru_bx=p["lru_bx"][l].reshape(1, -1),
        lru_lam=p["lru_lambda"][l].reshape(1, -1),
        gdn_cw=_pad_rows(p["gdn_conv_w"][l], SUBLANES),
        gdn_alog=_lane_row(p["gdn_a_log"][l], M_GDNA), gdn_dtb=_lane_row(p["gdn_dt_bias"][l], M_GDNA),
        gdn_ng=p["gdn_norm_g"][l].reshape(1, -1),
        diff_lam=p["diff_lambda"][l], diff_sg=p["diff_subln_g"][l].reshape(1, -1),
        w_branch=p["w_branch"][l].astype(BF16), w_out=p["w_out"][l].astype(BF16),
        norm_xa_g=p["norm_xa_g"][l].reshape(1, -1), norm_mem_g=p["norm_mem_g"][l],
        xa_wq=p["xa_wq"][l].astype(BF16), xa_wo=p["xa_wo"][l].astype(BF16),
        xa_wkv=jnp.concatenate([p["xa_wk"][l], p["xa_wv"][l]], axis=1).astype(BF16),
        norm_ffn_g=p["norm_ffn_g"][l].reshape(1, -1),
    )
    j = l // 2
    if l % 2 == 0:
        wl.update(ffn_wg=p["ffn_w_gate"][j].astype(BF16), ffn_wu=p["ffn_w_up"][j].astype(BF16),
                  ffn_wd=p["ffn_w_down"][j].astype(BF16))
    else:
        wl.update(moe_wr=jnp.pad(p["moe_router_w"][j], ((0, 0), (0, LANES - N_EXPERTS))),
                  moe_br=jnp.pad(p["moe_router_b"][j], (0, LANES - N_EXPERTS)).reshape(1, LANES),
                  moe_wg=p["moe_w_gate"][j].astype(BF16), moe_wu=p["moe_w_up"][j].astype(BF16),
                  moe_wd=p["moe_w_down"][j].astype(BF16))
    return wl


def _rope_tables(pos):
    half = DSA_HD // 2
    inv = ROPE_THETA ** (-jnp.arange(half, dtype=F32) / half)
    ang = pos.astype(F32)[:, None] * inv[None, :]
    cos, sin = jnp.cos(ang), jnp.sin(ang)
    return jnp.tile(cos, (1, 4)), jnp.tile(jnp.concatenate([-sin, sin], axis=1), (1, 2))


def _front_pad(buf):
    return jnp.pad(buf, ((0, 0), (SUBLANES - (CONV_W - 1), 0), (0, 0)))


def _group_layer(x3, l, wl, tabs, states, mem_kv, paged):
    b, t, d = x3.shape
    n = b * t
    x = x3.reshape(n, d)
    lam_init = 0.8 - 0.6 * math.exp(-0.3 * l)
    xn = _rmsnorm(x, wl["norm_mix_g"], BF16)
    h = _matmul(xn, wl["w_in"], tn=1408, w_rows_are_outputs=True)
    h3 = h.reshape(b, t, H_COLS)
    dq, fq, fk, fkb, fvb, iq3, dk, dkb, dvb, ik, ik3 = _rope_call(h, tabs[0], tabs[1])
    to3 = lambda a: a.reshape(b, t, a.shape[-1])
    dq3, fq3, fk3, dk3, ik3d = map(to3, (dq, fq, fk, dk, ik))

    o_a, lru_h = _lru_call(h3, wl, _front_pad(states["lru_conv"]), states["lru_h"].reshape(b, 1, -1),
                           reset_first=paged is None)
    o_c, gdn_s = _gdn_call(h3, wl, _front_pad(states["gdn_conv"]), states["gdn_s"])
    if paged is None:
        o_b = _dsa_prompt_call(to3(iq3), to3(ik3), dq3, to3(dkb), to3(dvb), h3)
        o_d = _diff_prompt_call(fq3, to3(fkb), to3(fvb), wl, lam_init)
    else:
        o_b = _dsa_sample_call(to3(iq3), ik3d, dq3, dk3, h3, paged["dsa_k"], paged["dsa_v"], paged["idx_k"],
                               paged["page_table"], l)
        o_d = _diff_sample_call(fq3, fk3, h3, paged["diff_k"], paged["diff_v"], paged["page_table"], l, wl, lam_init)

    flat = lambda a: a.reshape(n, a.shape[-1])
    x1 = _merge_call(x, h, [flat(o_a), flat(o_b), flat(o_c), flat(o_d)], wl)
    x2 = _cross_call(x1.reshape(b, t, d), wl, mem_kv[0], mem_kv[1], mem_kv[2])
    x2 = x2.reshape(n, d)
    x3_new = (_ffn_call(x2, wl) if "ffn_wg" in wl else _moe_call(x2, wl)).reshape(b, t, d)

    new = dict(
        dsa_k=dk3.reshape(b, t, DSA_KV_HEADS, DSA_HD),
        dsa_v=h3[:, :, C_DSAV:C_DSAV + DSA_KV_HEADS * DSA_HD].reshape(b, t, DSA_KV_HEADS, DSA_HD),
        idx_k=ik3d,
        diff_k=fk3.reshape(b, t, DIFF_HEADS, 2 * DIFF_HD),
        diff_v=h3[:, :, C_DIFV:C_DIFV + MIX_WIDTH].reshape(b, t, DIFF_HEADS, 2 * DIFF_HD),
        lru_h=lru_h.reshape(b, -1),
        lru_conv=jnp.concatenate([states["lru_conv"], h3[:, :, C_LRUX:C_LRUX + MIX_WIDTH]], axis=1)[:, -(CONV_W - 1):]
        if t < CONV_W - 1 else h3[:, t - (CONV_W - 1):, C_LRUX:C_LRUX + MIX_WIDTH],
        gdn_s=gdn_s,
        gdn_conv=jnp.concatenate([states["gdn_conv"], h3[:, :, C_GDNQKV:C_GDNQKV + GDN_QKV]], axis=1)[:, -(CONV_W - 1):]
        if t < CONV_W - 1 else h3[:, t - (CONV_W - 1):, C_GDNQKV:C_GDNQKV + GDN_QKV],
    )
    return x3_new, new


def kernel(x_prompt, x_sample, mem_prompt, cache_dsa_k, cache_dsa_v, cache_idx_k, cache_diff_k, cache_diff_v,
           cache_mem_k, cache_mem_v, state_lru_h, state_lru_conv, state_gdn_s, state_gdn_conv, page_table,
           norm_mix_g, w_in, lru_conv_w, lru_conv_b, lru_wa, lru_ba, lru_wx, lru_bx, lru_lambda,
           gdn_conv_w, gdn_a_log, gdn_dt_bias, gdn_norm_g, diff_lambda, diff_subln_g, w_branch, w_out,
           norm_xa_g, norm_mem_g, xa_wq, xa_wk, xa_wv, xa_wo, norm_ffn_g, ffn_w_gate, ffn_w_up, ffn_w_down,
           moe_router_w, moe_router_b, moe_w_gate, moe_w_up, moe_w_down, final_norm_g):
    params = dict(
        norm_mix_g=norm_mix_g, w_in=w_in, lru_conv_w=lru_conv_w, lru_conv_b=lru_conv_b, lru_wa=lru_wa, lru_ba=lru_ba,
        lru_wx=lru_wx, lru_bx=lru_bx, lru_lambda=lru_lambda, gdn_conv_w=gdn_conv_w, gdn_a_log=gdn_a_log,
        gdn_dt_bias=gdn_dt_bias, gdn_norm_g=gdn_norm_g, diff_lambda=diff_lambda, diff_subln_g=diff_subln_g,
        w_branch=w_branch, w_out=w_out, norm_xa_g=norm_xa_g, norm_mem_g=norm_mem_g, xa_wq=xa_wq, xa_wk=xa_wk,
        xa_wv=xa_wv, xa_wo=xa_wo, norm_ffn_g=norm_ffn_g, ffn_w_gate=ffn_w_gate, ffn_w_up=ffn_w_up,
        ffn_w_down=ffn_w_down, moe_router_w=moe_router_w, moe_router_b=moe_router_b, moe_w_gate=moe_w_gate,
        moe_w_up=moe_w_up, moe_w_down=moe_w_down)
    b, s, d = x_prompt.shape
    db, t, _ = x_sample.shape
    m = mem_prompt.shape[1]
    n_pool = cache_dsa_k.shape[1]
    past = page_table.shape[1] * PAGE_SIZE

    tabs_p = _rope_tables(jnp.arange(s))
    cs, ss = _rope_tables(past + jnp.arange(t))
    tabs_s = (jnp.tile(cs, (db, 1)), jnp.tile(ss, (db, 1)))
    paged = dict(
        dsa_k=jnp.transpose(cache_dsa_k, (0, 1, 3, 4, 2)), dsa_v=jnp.transpose(cache_dsa_v, (0, 1, 3, 4, 2)),
        idx_k=jnp.transpose(cache_idx_k, (0, 1, 3, 2)), page_table=page_table,
        diff_k=cache_diff_k.reshape(DEPTH, n_pool, PAGE_SIZE * DIFF_HEADS, 2 * DIFF_HD),
        diff_v=cache_diff_v.reshape(DEPTH, n_pool, PAGE_SIZE * DIFF_HEADS, 2 * DIFF_HD))
    mem_k_s = cache_mem_k.reshape(DEPTH, db, m * XA_HEADS, XA_HD)
    mem_v_s = cache_mem_v.reshape(DEPTH, db, m * XA_HEADS, XA_HD)
    zero_states = dict(lru_conv=jnp.zeros((b, CONV_W - 1, MIX_WIDTH), F32), lru_h=jnp.zeros((b, MIX_WIDTH), F32),
                       gdn_conv=jnp.zeros((b, CONV_W - 1, GDN_QKV), F32),
                       gdn_s=jnp.zeros((b, GDN_HEADS, GDN_HD, GDN_HD), F32))

    hp, hs = x_prompt, x_sample
    new_p, new_s, mem_ks, mem_vs = [], [], [], []
    for l in range(DEPTH):
        wl = _layer_weights(l, params)
        mem_n = _rmsnorm(mem_prompt.reshape(b * m, d), wl["norm_mem_g"], BF16)
        kv = _matmul(mem_n, wl["xa_wkv"], tn=512)
        w = XA_HEADS * XA_HD
        mk = kv[:, :w].reshape(1, b, m, w)
        mv = kv[:, w:].reshape(1, b, m, w)
        hp, st_p = _group_layer(hp, l, wl, tabs_p, zero_states, (mk, mv, 0), None)
        new_p.append(st_p)
        mem_ks.append(mk.reshape(b, m, XA_HEADS, XA_HD))
        mem_vs.append(mv.reshape(b, m, XA_HEADS, XA_HD))
        states_s = dict(lru_conv=state_lru_conv[l], lru_h=state_lru_h[l], gdn_conv=state_gdn_conv[l],
                        gdn_s=state_gdn_s[l])
        hs, st_s = _group_layer(hs, l, wl, tabs_s, states_s, (mem_k_s, mem_v_s, l), paged)
        new_s.append(st_s)

    y_prompt = _rmsnorm(hp.reshape(b * s, d), final_norm_g, F32).reshape(b, s, d)
    y_sample = _rmsnorm(hs.reshape(db * t, d), final_norm_g, F32).reshape(db, t, d)
    stack = lambda states, name: jnp.stack([st[name] for st in states], axis=0)
    outs = [y_prompt, y_sample]
    for name in ("dsa_k", "dsa_v", "idx_k", "diff_k", "diff_v", "lru_h", "lru_conv", "gdn_s", "gdn_conv"):
        outs += [stack(new_p, name), stack(new_s, name)]
    outs += [jnp.stack(mem_ks, axis=0), jnp.stack(mem_vs, axis=0)]
    return tuple(outs)
```

```python
import functools
import math

import jax
import jax.numpy as jnp
from jax import lax
from jax.experimental import pallas as pl
from jax.experimental.pallas import tpu as pltpu

F32 = jnp.float32
BF16 = jnp.bfloat16
I32 = jnp.int32
I16 = jnp.int16

D_MODEL = 1024
DEPTH = 2
PAGE_SIZE = 128
MIX_WIDTH = 512
LRU_BLOCKS = 8
LRU_BLOCK = MIX_WIDTH // LRU_BLOCKS
LRU_C = 8.0
CONV_W = 4
DSA_HEADS = 8
DSA_KV_HEADS = 2
DSA_HD = 64
IDX_HEADS = 4
IDX_HD = 64
TOPK_MAX = 256
GDN_HEADS = 4
GDN_HD = 128
GDN_CHUNK = 128
DIFF_HEADS = 4
DIFF_HD = 64
XA_HEADS = 4
XA_HD = 128
N_EXPERTS = 8
ROPE_THETA = 10000.0
EPS = 1e-6
N_BRANCH = 4
GDN_QKV = 3 * GDN_HEADS * GDN_HD

LANES = 128
SUBLANES = 8
NEG_BIG = -1e30
INT_MIN = -2 ** 31
HALF_BIAS = 2 ** 15
VMEM_LIMIT = 48 * 1024 * 1024
VMEM_LIMIT_DSA_PROMPT = 58 * 1024 * 1024

C_GATE = 0
C_LRUX = 4096
C_LRUG = 4608
C_DSAQ = 5120
C_DIFQ = 5632
C_DIFK = 6144
C_DIFV = 6656
C_GDNZ = 7168
C_GDNQKV = 7680
C_IDXQ = 9216
C_DSAK = 9472
C_DSAV = 9600
C_MISC = 9728
H_COLS = 9856
M_IDXK = 0
M_IDXW = 64
M_GDNB = 68
M_GDNA = 72

O_LRUX, O_LRUG, O_DSAQ, O_DSAK, O_DSAV, O_IDXQ, O_IDXK, O_IDXW = 0, 512, 1024, 1536, 1664, 1792, 2048, 2112
O_GDNQKV, O_GDNZ, O_GDNB, O_GDNA, O_DIFQ, O_DIFK, O_DIFV, O_GATE = 2116, 3652, 4164, 4168, 4172, 4684, 5196, 5708


def _cparams(*sem):
    return pltpu.CompilerParams(dimension_semantics=sem, vmem_limit_bytes=VMEM_LIMIT)


def _tile(n, pref):
    return pref if n % pref == 0 else n


def _pages_per_step(n_pages):
    return next(p for p in (16, 8, 1) if n_pages % p == 0)


def _softplus(x):
    return jnp.maximum(x, 0.0) + jnp.log1p(jnp.exp(-jnp.abs(x)))


def _sigmoid(x):
    return 1.0 / (1.0 + jnp.exp(-x))


def _silu(x):
    return x * _sigmoid(x)


def _dot(a, b):
    return jnp.dot(a, b, preferred_element_type=F32)


def _dot_nt(a, b):
    return lax.dot_general(a, b, (((1,), (1,)), ((), ())), preferred_element_type=F32)


def _dot_hi(a, b):
    return jnp.dot(a, b, preferred_element_type=F32, precision=lax.Precision.HIGHEST)


def _split_hi_lo(x):
    hi = x.astype(BF16)
    return hi, (x - hi.astype(F32)).astype(BF16)


_NN = (((1,), (0,)), ((), ()))
_NT = (((1,), (1,)), ((), ()))
_TN = (((0,), (0,)), ((), ()))


def _dot3(a, b, dims=_NN):
    a_hi, a_lo = _split_hi_lo(a)
    b_hi, b_lo = _split_hi_lo(b)
    dg = functools.partial(lax.dot_general, dimension_numbers=dims, preferred_element_type=F32)
    ca, cb = dims[0][0][0], dims[0][1][0]
    if a.shape[ca] % LANES == 0:
        return dg(jnp.concatenate([a_hi, a_hi, a_lo], axis=ca), jnp.concatenate([b_hi, b_lo, b_hi], axis=cb))
    return dg(a_hi, b_hi) + dg(a_hi, b_lo) + dg(a_lo, b_hi)


def _dot_exact_lhs(a, b):
    ab = a.astype(BF16)
    b1 = b.astype(BF16)
    r1 = b - b1.astype(F32)
    b2 = r1.astype(BF16)
    b3 = (r1 - b2.astype(F32)).astype(BF16)
    if a.shape[1] % LANES == 0:
        return _dot(jnp.concatenate([ab, ab, ab], axis=1), jnp.concatenate([b1, b2, b3], axis=0))
    return _dot(ab, b1) + _dot(ab, b2) + _dot(ab, b3)


def _rms_rows(x, g):
    return x * lax.rsqrt(jnp.mean(x * x, axis=-1, keepdims=True) + EPS) * g


def _rmsnorm_body(x_ref, g_ref, o_ref):
    o_ref[...] = _rms_rows(x_ref[...], g_ref[...]).astype(o_ref.dtype)


def _rmsnorm(x, g, out_dtype):
    n, d = x.shape
    tm = _tile(n, 512)
    return pl.pallas_call(
        _rmsnorm_body,
        grid=(n // tm,),
        in_specs=[pl.BlockSpec((tm, d), lambda i: (i, 0)), pl.BlockSpec((1, d), lambda i: (0, 0))],
        out_specs=pl.BlockSpec((tm, d), lambda i: (i, 0)),
        out_shape=jax.ShapeDtypeStruct((n, d), out_dtype),
        compiler_params=_cparams("parallel"),
        name="rmsnorm",
    )(x, g.reshape(1, d))


def _matmul_body(a_ref, w_ref, o_ref, *, w_rows_are_outputs):
    mm = _dot_nt if w_rows_are_outputs else _dot
    o_ref[...] = mm(a_ref[...], w_ref[...]).astype(o_ref.dtype)


def _matmul(a, w, tn, w_rows_are_outputs=False, out_dtype=F32):
    n, k = a.shape
    c = w.shape[0] if w_rows_are_outputs else w.shape[1]
    tm = _tile(n, 1024)
    w_spec = (pl.BlockSpec((tn, k), lambda j, i: (j, 0)) if w_rows_are_outputs
              else pl.BlockSpec((k, tn), lambda j, i: (0, j)))
    return pl.pallas_call(
        functools.partial(_matmul_body, w_rows_are_outputs=w_rows_are_outputs),
        grid=(c // tn, n // tm),
        in_specs=[pl.BlockSpec((tm, k), lambda j, i: (i, 0)), w_spec],
        out_specs=pl.BlockSpec((tm, tn), lambda j, i: (i, j)),
        out_shape=jax.ShapeDtypeStruct((n, c), out_dtype),
        compiler_params=_cparams("parallel", "parallel"),
        name="matmul",
    )(a, w)


def _rope_block(x, cos, sin):
    lane = lax.broadcasted_iota(I32, x.shape, 1)
    first = (lane % 64) < 32
    partner = jnp.where(first, pltpu.roll(x, LANES - 32, 1), pltpu.roll(x, 32, 1))
    return x * cos + partner * sin


def _rope_wide(x, cos, sin):
    return [_rope_block(x[:, c * LANES:(c + 1) * LANES], cos, sin) for c in range(x.shape[1] // LANES)]


def _rope_body(cos_ref, sin_ref, dq_ref, fq_ref, fk_ref, fv_ref, iq_ref, dk_ref, dv_ref, misc_ref,
               dq_o, fq_o, fk_o, fkb_o, fvb_o, iq3_o, dk_o, dkb_o, dvb_o, ik_o, ik3_o):
    cos = cos_ref[...]
    sin = sin_ref[...]
    lane = lax.broadcasted_iota(I32, cos.shape, 1)
    low = lane < IDX_HD
    for c, blk in enumerate(_rope_wide(dq_ref[...], cos, sin)):
        dq_o[:, c * LANES:(c + 1) * LANES] = (blk * DSA_HD ** -0.5).astype(BF16)
    for c, blk in enumerate(_rope_wide(fq_ref[...], cos, sin)):
        fq_o[:, c * LANES:(c + 1) * LANES] = (blk * DIFF_HD ** -0.5).astype(BF16)
    for c, blk in enumerate(_rope_wide(fk_ref[...], cos, sin)):
        fk_o[:, c * LANES:(c + 1) * LANES] = blk
        fkb_o[:, c * LANES:(c + 1) * LANES] = blk.astype(BF16)
    fvb_o[...] = fv_ref[...].astype(BF16)
    dk = _rope_block(dk_ref[...], cos, sin)
    dk_o[...] = dk
    dkb_o[...] = dk.astype(BF16)
    dv = dv_ref[...]
    dvb_o[:, 0:LANES] = jnp.where(low, dv, 1.0).astype(BF16)
    dvb_o[:, LANES:2 * LANES] = jnp.where(low, pltpu.roll(dv, DSA_HD, 1), 1.0).astype(BF16)
    for c, blk in enumerate(_rope_wide(iq_ref[...], cos, sin)):
        hi = blk.astype(BF16).astype(F32)
        lo = blk - hi
        hi_sw = pltpu.roll(hi, IDX_HD, 1)
        lo_sw = pltpu.roll(lo, IDX_HD, 1)
        base = 2 * c * 2 * LANES
        iq3_o[:, base:base + LANES] = jnp.where(low, hi, hi_sw).astype(BF16)
        iq3_o[:, base + LANES:base + 2 * LANES] = jnp.where(low, lo, 0.0).astype(BF16)
        iq3_o[:, base + 2 * LANES:base + 3 * LANES] = jnp.where(low, hi_sw, hi).astype(BF16)
        iq3_o[:, base + 3 * LANES:base + 4 * LANES] = jnp.where(low, lo_sw, 0.0).astype(BF16)
    ik = _rope_block(misc_ref[...], cos, sin)
    ik_o[...] = ik[:, M_IDXK:M_IDXK + IDX_HD]
    hi = ik.astype(BF16).astype(F32)
    lo = ik - hi
    ik3_o[:, 0:LANES] = jnp.where(low, hi, pltpu.roll(lo, IDX_HD, 1)).astype(BF16)
    ik3_o[:, LANES:2 * LANES] = jnp.where(low, hi, 0.0).astype(BF16)


def _rope_call(h, cos, sin):
    n = h.shape[0]
    tm = _tile(min(n, cos.shape[0]), 512)
    nt = cos.shape[0] // tm

    def col(width, off):
        return pl.BlockSpec((tm, width), lambda i: (i, off // width))

    def out(width):
        return pl.BlockSpec((tm, width), lambda i: (i, 0))

    sds = jax.ShapeDtypeStruct
    tab = pl.BlockSpec((tm, LANES), lambda i: (i % nt, 0))
    return pl.pallas_call(
        _rope_body,
        grid=(n // tm,),
        in_specs=[tab, tab, col(512, C_DSAQ), col(512, C_DIFQ), col(512, C_DIFK), col(512, C_DIFV),
                  col(256, C_IDXQ), col(128, C_DSAK), col(128, C_DSAV), col(128, C_MISC)],
        out_specs=[out(512), out(512), out(512), out(512), out(512), out(1024), out(128), out(128), out(256),
                   out(IDX_HD), out(256)],
        out_shape=[sds((n, 512), BF16), sds((n, 512), BF16), sds((n, 512), F32), sds((n, 512), BF16),
                   sds((n, 512), BF16), sds((n, 1024), BF16), sds((n, 128), F32), sds((n, 128), BF16),
                   sds((n, 256), BF16), sds((n, IDX_HD), F32), sds((n, 256), BF16)],
        compiler_params=_cparams("parallel"),
        name="rope",
    )(cos, sin, h, h, h, h, h, h, h, h)


def _conv_tile(x_ref, buf_ref, cw_ref, xcat, t, tc):
    @pl.when(t == 0)
    def _():
        xcat[0:SUBLANES, :] = buf_ref[...]

    @pl.when(t > 0)
    def _():
        xcat[0:SUBLANES, :] = xcat[tc:tc + SUBLANES, :]

    xcat[SUBLANES:SUBLANES + tc, :] = x_ref[...]
    cw = cw_ref[...]
    y = x_ref[...] * cw[CONV_W - 1:CONV_W, :]
    for j in range(CONV_W - 1):
        off = SUBLANES - (CONV_W - 1) + j
        y = y + xcat[off:off + tc, :] * cw[j:j + 1, :]
    return y


def _lru_body(x_ref, gate_ref, cw_ref, cb_ref, wa_ref, ba_ref, wx_ref, bx_ref, lam_ref, buf_ref, h0_ref,
              o_ref, hl_ref, xcat, hc, *, tc, reset_first):
    t = pl.program_id(1)

    @pl.when(t == 0)
    def _():
        hc[...] = h0_ref[...]

    xa = _conv_tile(x_ref, buf_ref, cw_ref, xcat, t, tc) + cb_ref[...]
    xb = xa.astype(BF16)
    r = _sigmoid(_dot(xb, wa_ref[...]) + ba_ref[...])
    gi = _sigmoid(_dot(xb, wx_ref[...]) + bx_ref[...])
    log_a = -LRU_C * r * _softplus(-lam_ref[...])
    a = jnp.exp(log_a)
    mult = jnp.sqrt(-jnp.tanh(log_a) * (a * a + 1.0))
    row = lax.broadcasted_iota(I32, a.shape, 0)
    if reset_first:
        is0 = jnp.logical_and(row == 0, t == 0)
        a = jnp.where(is0, 0.0, a)
        mult = jnp.where(is0, 1.0, mult)
    u = mult * gi * xa
    d = 1
    while d < tc:
        keep = row >= d
        a_sh = jnp.where(keep, pltpu.roll(a, d, 0), 1.0)
        u_sh = jnp.where(keep, pltpu.roll(u, d, 0), 0.0)
        u = a * u_sh + u
        a = a * a_sh
        d *= 2
    h = a * hc[...] + u
    h_last = h[tc - 1:tc, :]
    hc[...] = h_last
    hl_ref[...] = h_last
    o_ref[...] = (h * jax.nn.gelu(gate_ref[...])).astype(o_ref.dtype)


def _lru_call(h3, wl, buf, h0, reset_first):
    b, t, _ = h3.shape
    tc = _tile(t, 256)
    w = MIX_WIDTH
    row = pl.BlockSpec((1, w), lambda i, j: (0, 0))
    sq = pl.BlockSpec((w, w), lambda i, j: (0, 0))
    return pl.pallas_call(
        functools.partial(_lru_body, tc=tc, reset_first=reset_first),
        grid=(b, t // tc),
        in_specs=[pl.BlockSpec((None, tc, w), lambda i, j: (i, j, C_LRUX // w)),
                  pl.BlockSpec((None, tc, w), lambda i, j: (i, j, C_LRUG // w)),
                  pl.BlockSpec((SUBLANES, w), lambda i, j: (0, 0)), row, sq, row, sq, row, row,
                  pl.BlockSpec((None, SUBLANES, w), lambda i, j: (i, 0, 0)),
                  pl.BlockSpec((None, 1, w), lambda i, j: (i, 0, 0))],
        out_specs=[pl.BlockSpec((None, tc, w), lambda i, j: (i, j, 0)),
                   pl.BlockSpec((None, 1, w), lambda i, j: (i, 0, 0))],
        out_shape=[jax.ShapeDtypeStruct((b, t, w), BF16), jax.ShapeDtypeStruct((b, 1, w), F32)],
        scratch_shapes=[pltpu.VMEM((tc + SUBLANES, w), F32), pltpu.VMEM((1, w), F32)],
        compiler_params=_cparams("parallel", "arbitrary"),
        name="rg_lru",
    )(h3, h3, wl["lru_cw"], wl["lru_cb"], wl["lru_wa"], wl["lru_ba"], wl["lru_wx"], wl["lru_bx"], wl["lru_lam"],
      buf, h0)


def _gdn_body(qkv_ref, z_ref, misc_ref, cw_ref, alog_ref, dtb_ref, ng_ref, buf_ref, s0_ref,
              o_ref, so_ref, xcat, y_scr, state, *, tc, ck):
    t = pl.program_id(1)
    hd = GDN_HD

    @pl.when(t == 0)
    def _():
        state[...] = s0_ref[...]

    y = _conv_tile(qkv_ref, buf_ref, cw_ref, xcat, t, tc)
    y_scr[...] = _silu(y)

    ri = lax.broadcasted_iota(I32, (ck, ck), 0)
    ci = lax.broadcasted_iota(I32, (ck, ck), 1)
    tril = ri >= ci
    strict = ri > ci
    ltri = jnp.where(tril, 1.0, 0.0).astype(F32)
    ones = jnp.ones((ck, ck), F32)
    n_sq = max(int(math.log2(ck)) - 1, 0)

    def chunk(c, carry):
        r0 = pl.multiple_of(c * ck, ck)
        heads = range(GDN_HEADS)
        misc = misc_ref[pl.ds(r0, ck), :]
        beta_all = _sigmoid(misc)
        g_all = -jnp.exp(alog_ref[...]) * _softplus(misc + dtb_ref[...])
        cols = lambda blk: slice(blk * hd, (blk + 1) * hd)
        q = [y_scr[pl.ds(r0, ck), cols(hh)] for hh in heads]
        k = [y_scr[pl.ds(r0, ck), cols(GDN_HEADS + hh)] for hh in heads]
        v = [y_scr[pl.ds(r0, ck), cols(2 * GDN_HEADS + hh)] for hh in heads]
        q = [x * lax.rsqrt(jnp.sum(x * x, axis=-1, keepdims=True) + EPS) * (hd ** -0.5) for x in q]
        k = [x * lax.rsqrt(jnp.sum(x * x, axis=-1, keepdims=True) + EPS) for x in k]
        beta = [jnp.broadcast_to(beta_all[:, M_GDNB + hh:M_GDNB + hh + 1], (ck, hd)) for hh in heads]
        g128 = [jnp.broadcast_to(g_all[:, M_GDNA + hh:M_GDNA + hh + 1], (ck, hd)) for hh in heads]
        gc128 = [_dot_exact_lhs(ltri, g) for g in g128]
        gc_row = [_dot_exact_lhs(ones, jnp.where(ri <= ci, g[:, :ck], 0.0)) for g in g128]
        decay = [jnp.where(tril, jnp.exp(jnp.where(tril, gc128[hh][:, :ck] - gc_row[hh], 0.0)), 0.0) for hh in heads]
        kb = [k[hh] * beta[hh] for hh in heads]
        egc = [jnp.exp(g) for g in gc128]
        low = [jnp.where(strict, _dot3(kb[hh], k[hh], _NT) * decay[hh], 0.0) for hh in heads]
        intra = [jnp.where(tril, _dot3(q[hh], k[hh], _NT) * decay[hh], 0.0) for hh in heads]
        mp = [-x for x in low]
        x = [jnp.concatenate([v[hh] * beta[hh], kb[hh] * egc[hh]], axis=-1) for hh in heads]
        x = [x[hh] + _dot3(mp[hh], x[hh]) for hh in heads]
        for _ in range(n_sq):
            mp = [_dot3(m, m) for m in mp]
            x = [x[hh] + _dot3(mp[hh], x[hh]) for hh in heads]
        s = [state[hh] for hh in heads]
        v_new = [x[hh][:, :hd] - _dot3(x[hh][:, hd:], s[hh]) for hh in heads]
        o = [_dot3(q[hh] * egc[hh], s[hh]) + _dot3(intra[hh], v_new[hh]) for hh in heads]
        g_last = [g[ck - 1:ck, :] for g in gc128]
        kd = [k[hh] * jnp.exp(g_last[hh] - gc128[hh]) for hh in heads]
        s_new = [s[hh] * jnp.exp(g_last[hh]) + _dot3(kd[hh], v_new[hh], _TN) for hh in heads]
        for hh in heads:
            zz = z_ref[pl.ds(r0, ck), cols(hh)]
            state[hh] = s_new[hh]
            o_ref[pl.ds(r0, ck), cols(hh)] = (_rms_rows(o[hh], ng_ref[...]) * _silu(zz)).astype(o_ref.dtype)
        return carry

    lax.fori_loop(0, tc // ck, chunk, 0)
    so_ref[...] = state[...]


def _gdn_call(h3, wl, buf, s0):
    b, t, _ = h3.shape
    ck = min(GDN_CHUNK, t)
    assert t % ck == 0
    tc = _tile(t, 256)
    w = GDN_QKV
    row = pl.BlockSpec((1, LANES), lambda i, j: (0, 0))
    st = pl.BlockSpec((None, GDN_HEADS, GDN_HD, GDN_HD), lambda i, j: (i, 0, 0, 0))
    return pl.pallas_call(
        functools.partial(_gdn_body, tc=tc, ck=ck),
        grid=(b, t // tc),
        in_specs=[pl.BlockSpec((None, tc, w), lambda i, j: (i, j, C_GDNQKV // w)),
                  pl.BlockSpec((None, tc, MIX_WIDTH), lambda i, j: (i, j, C_GDNZ // MIX_WIDTH)),
                  pl.BlockSpec((None, tc, LANES), lambda i, j: (i, j, C_MISC // LANES)),
                  pl.BlockSpec((SUBLANES, w), lambda i, j: (0, 0)), row, row, row,
                  pl.BlockSpec((None, SUBLANES, w), lambda i, j: (i, 0, 0)), st],
        out_specs=[pl.BlockSpec((None, tc, MIX_WIDTH), lambda i, j: (i, j, 0)), st],
        out_shape=[jax.ShapeDtypeStruct((b, t, MIX_WIDTH), BF16),
                   jax.ShapeDtypeStruct((b, GDN_HEADS, GDN_HD, GDN_HD), F32)],
        scratch_shapes=[pltpu.VMEM((tc + SUBLANES, w), F32), pltpu.VMEM((tc, w), F32),
                        pltpu.VMEM((GDN_HEADS, GDN_HD, GDN_HD), F32)],
        compiler_params=_cparams("parallel", "arbitrary"),
        name="gated_deltanet",
    )(h3, h3, h3, wl["gdn_cw"], wl["gdn_alog"], wl["gdn_dtb"], wl["gdn_ng"], buf, s0)


def _lane_tile(x, n):
    return x if n == LANES else jnp.concatenate([x] * (n // LANES), axis=1)


def _online_update(s, v_ext, m_ref, acc_ref, idx):
    m_old = m_ref[idx]
    m_new = jnp.maximum(m_old, jnp.max(s, axis=-1, keepdims=True))
    alpha = jnp.exp(m_old - m_new)
    p = jnp.exp(s - _lane_tile(m_new, s.shape[1]))
    acc = acc_ref[idx]
    acc_ref[idx] = _lane_tile(alpha, acc.shape[1]) * acc + _dot(p.astype(BF16), v_ext)
    m_ref[idx] = m_new


def _online_update_blocks(s, vs, m_ref, l_ref, acc_ref, idx):
    m_old = m_ref[idx]
    m_new = jnp.maximum(m_old, jnp.max(s, axis=-1, keepdims=True))
    alpha = jnp.exp(m_old - m_new)
    p = jnp.exp(s - m_new)
    l_ref[idx] = alpha * l_ref[idx] + jnp.sum(p, axis=-1, keepdims=True)
    pb = p.astype(BF16)
    pv, off = None, 0
    for vblk in vs:
        part = _dot(pb[:, off:off + vblk.shape[0]], vblk)
        pv = part if pv is None else pv + part
        off += vblk.shape[0]
    acc_ref[idx] = alpha * acc_ref[idx] + pv
    m_ref[idx] = m_new


def _reset_softmax(m_ref, l_ref, acc_ref):
    m_ref[...] = jnp.full(m_ref.shape, NEG_BIG, F32)
    l_ref[...] = jnp.zeros(l_ref.shape, F32)
    acc_ref[...] = jnp.zeros(acc_ref.shape, F32)


def _diff_lambda(lam_ref, lam_init):
    lm = lam_ref[...]
    e1 = jnp.exp(jnp.sum(lm[0:1, :] * lm[1:2, :], axis=-1, keepdims=True))
    e2 = jnp.exp(jnp.sum(lm[2:3, :] * lm[3:4, :], axis=-1, keepdims=True))
    return e1 - e2 + lam_init


def _diff_prompt_body(qi_ref, kj_ref, lam_ref, sg_ref, q_ref, k_ref, v_ref, o_ref, m_scr, acc_scr, *, tq, lam_init):
    step = pl.program_id(2)
    qi = qi_ref[step]
    kj = kj_ref[step]
    hv = 2 * DIFF_HD

    @pl.when(kj == 0)
    def _():
        m_scr[...] = jnp.full(m_scr.shape, NEG_BIG, F32)
        acc_scr[...] = jnp.zeros(acc_scr.shape, F32)

    def tile(masked):
        v = v_ref[...]
        v_ext = jnp.concatenate([v, jnp.ones_like(v)], axis=1)
        s = [_dot_nt(q_ref[:, mm * DIFF_HD:(mm + 1) * DIFF_HD], k_ref[:, mm * DIFF_HD:(mm + 1) * DIFF_HD])
             for mm in range(2)]
        for mm in range(2):
            sm = s[mm]
            if masked:
                row = lax.broadcasted_iota(I32, (tq, tq), 0)
                col = lax.broadcasted_iota(I32, (tq, tq), 1)
                sm = jnp.where(col <= row, sm, NEG_BIG)
            _online_update(sm, v_ext, m_scr, acc_scr, mm)

    @pl.when(kj < qi)
    def _():
        tile(False)

    @pl.when(kj == qi)
    def _():
        tile(True)
        lam = _diff_lambda(lam_ref, lam_init)
        o = acc_scr[0, :, 0:hv] / acc_scr[0, :, hv:2 * hv] - lam * (acc_scr[1, :, 0:hv] / acc_scr[1, :, hv:2 * hv])
        o_ref[...] = (_rms_rows(o, sg_ref[...]) * (1.0 - lam_init)).astype(o_ref.dtype)


def _diff_prompt_call(fq3, fkb3, fvb3, wl, lam_init):
    b, t, _ = fq3.shape
    tq = _tile(t, 1024)
    nq = t // tq
    hv = 2 * DIFF_HD
    pairs = [(q, k) for q in range(nq) for k in range(q + 1)]
    qi_of = jnp.asarray([p[0] for p in pairs], I32)
    kj_of = jnp.asarray([p[1] for p in pairs], I32)
    grid_spec = pltpu.PrefetchScalarGridSpec(
        num_scalar_prefetch=2,
        grid=(b, DIFF_HEADS, len(pairs)),
        in_specs=[pl.BlockSpec((4, DIFF_HD), lambda i, h, s, qi, kj: (0, 0)),
                  pl.BlockSpec((1, hv), lambda i, h, s, qi, kj: (0, 0)),
                  pl.BlockSpec((None, tq, hv), lambda i, h, s, qi, kj: (i, qi[s], h)),
                  pl.BlockSpec((None, tq, hv), lambda i, h, s, qi, kj: (i, kj[s], h)),
                  pl.BlockSpec((None, tq, hv), lambda i, h, s, qi, kj: (i, kj[s], h))],
        out_specs=pl.BlockSpec((None, tq, hv), lambda i, h, s, qi, kj: (i, qi[s], h)),
        scratch_shapes=[pltpu.VMEM((2, tq, LANES), F32), pltpu.VMEM((2, tq, 2 * hv), F32)],
    )
    return pl.pallas_call(
        functools.partial(_diff_prompt_body, tq=tq, lam_init=lam_init),
        grid_spec=grid_spec,
        out_shape=jax.ShapeDtypeStruct((b, t, MIX_WIDTH), BF16),
        compiler_params=_cparams("parallel", "parallel", "arbitrary"),
        name="diff_attn_prompt",
    )(qi_of, kj_of, wl["diff_lam"], wl["diff_sg"], fq3, fkb3, fvb3)


def _diff_sample_body(pt_ref, lam_ref, sg_ref, q_ref, kn_ref, vn_ref, *rest, tq, npg, lam_init):
    kp = rest[:npg]
    vp = rest[npg:2 * npg]
    o_ref, qrows, m_scr, l_scr, acc_scr = rest[2 * npg:]
    s_id = pl.program_id(1)
    n_steps = pl.num_programs(1)
    nrow = 2 * DIFF_HEADS * tq
    hv = 2 * DIFF_HD
    prow = PAGE_SIZE * DIFF_HEADS

    @pl.when(s_id == 0)
    def _():
        _reset_softmax(m_scr, l_scr, acc_scr)
        qrows[...] = jnp.zeros(qrows.shape, F32)
        q = q_ref[...].astype(F32)
        for j in range(2 * DIFF_HEADS):
            mm = j % 2
            qrows[j * tq:(j + 1) * tq, mm * DIFF_HD:(mm + 1) * DIFF_HD] = q[:, j * DIFF_HD:(j + 1) * DIFF_HD]

    qb = qrows[...].astype(BF16)
    row_head = lax.broadcasted_iota(I32, (nrow, prow), 0) // (2 * tq)

    @pl.when(s_id < n_steps - 1)
    def _():
        key_head = lax.broadcasted_iota(I32, (nrow, prow), 1) % DIFF_HEADS
        own = key_head == row_head
        scores = [jnp.where(own, _dot_nt(qb, kp[p][...].astype(BF16)), NEG_BIG) for p in range(npg)]
        _online_update_blocks(jnp.concatenate(scores, axis=-1), [vp[p][...].astype(BF16) for p in range(npg)],
                       m_scr, l_scr, acc_scr, Ellipsis)

    @pl.when(s_id == n_steps - 1)
    def _():
        kn = jnp.concatenate([kn_ref[:, hh * hv:(hh + 1) * hv] for hh in range(DIFF_HEADS)], axis=0).astype(BF16)
        vn = jnp.concatenate([vn_ref[:, hh * hv:(hh + 1) * hv] for hh in range(DIFF_HEADS)], axis=0).astype(BF16)
        nk = DIFF_HEADS * tq
        rr = lax.broadcasted_iota(I32, (nrow, nk), 0)
        cc = lax.broadcasted_iota(I32, (nrow, nk), 1)
        ok = jnp.logical_and(cc // tq == rr // (2 * tq), cc % tq <= rr % tq)
        s = jnp.where(ok, _dot_nt(qb, kn), NEG_BIG)
        _online_update_blocks(s, [vn], m_scr, l_scr, acc_scr, Ellipsis)
        lam = _diff_lambda(lam_ref, lam_init)
        acc = acc_scr[...] / l_scr[...]
        for hh in range(DIFF_HEADS):
            r1 = (2 * hh) * tq
            r2 = (2 * hh + 1) * tq
            o = acc[r1:r1 + tq, :] - lam * acc[r2:r2 + tq, :]
            o_ref[:, hh * hv:(hh + 1) * hv] = (_rms_rows(o, sg_ref[...]) * (1.0 - lam_init)).astype(o_ref.dtype)


def _diff_sample_call(fq3, fk3, h3, pool_k, pool_v, page_table, layer, wl, lam_init):
    db, t, _ = fq3.shape
    n_pages = page_table.shape[1]
    npg = _pages_per_step(n_pages)
    n_steps = n_pages // npg + 1
    w = MIX_WIDTH
    hv = 2 * DIFF_HD
    nrow = 2 * DIFF_HEADS * t
    prow = PAGE_SIZE * DIFF_HEADS

    def page_spec(p):
        def imap(i, s, pt):
            return (layer, pt[i, jnp.minimum(s, n_steps - 2) * npg + p], 0, 0)
        return pl.BlockSpec((None, None, prow, hv), imap)

    grid_spec = pltpu.PrefetchScalarGridSpec(
        num_scalar_prefetch=1,
        grid=(db, n_steps),
        in_specs=[pl.BlockSpec((4, DIFF_HD), lambda i, s, pt: (0, 0)),
                  pl.BlockSpec((1, hv), lambda i, s, pt: (0, 0)),
                  pl.BlockSpec((None, t, w), lambda i, s, pt: (i, 0, 0)),
                  pl.BlockSpec((None, t, w), lambda i, s, pt: (i, 0, 0)),
                  pl.BlockSpec((None, t, w), lambda i, s, pt: (i, 0, C_DIFV // w))]
                 + [page_spec(p) for p in range(npg)] + [page_spec(p) for p in range(npg)],
        out_specs=pl.BlockSpec((None, t, w), lambda i, s, pt: (i, 0, 0)),
        scratch_shapes=[pltpu.VMEM((nrow, hv), F32), pltpu.VMEM((nrow, 1), F32), pltpu.VMEM((nrow, 1), F32),
                        pltpu.VMEM((nrow, hv), F32)],
    )
    return pl.pallas_call(
        functools.partial(_diff_sample_body, tq=t, npg=npg, lam_init=lam_init),
        grid_spec=grid_spec,
        out_shape=jax.ShapeDtypeStruct((db, t, w), BF16),
        compiler_params=_cparams("parallel", "arbitrary"),
        name="diff_attn_sample",
    )(page_table, wl["diff_lam"], wl["diff_sg"], fq3, fk3, h3, *([pool_k] * npg), *([pool_v] * npg))


def _sortable(score):
    bits = pltpu.bitcast(jnp.where(score == 0.0, 0.0, score), I32)
    return bits ^ ((bits >> 31) & 0x7FFFFFFF)


def _search_threshold(count_ge, shape, n_sel):
    def bit_body(i, t_u):
        cand_u = t_u | (jnp.int32(1) << (31 - i))
        cnt = count_ge(cand_u ^ INT_MIN)
        return jnp.where(cnt >= n_sel, cand_u, t_u)

    t_u = lax.fori_loop(0, 32, bit_body, jnp.zeros(shape, I32))
    return t_u ^ INT_MIN


def _lane_fold(x):
    acc = x[:, 0:LANES]
    for c in range(1, x.shape[1] // LANES):
        acc = acc + x[:, c * LANES:(c + 1) * LANES]
    return acc


def _strictly_before(n):
    ui = lax.broadcasted_iota(I32, (n, n), 0)
    uj = lax.broadcasted_iota(I32, (n, n), 1)
    return jnp.where(ui < uj, 1.0, 0.0).astype(BF16)


def _dsa_prompt_body(iq3_ref, misc_ref, ik3_ref, q_ref, k_ref, v_ref, o_ref, iqs, qs, keys_scr, half_scr, bias_scr,
                     m_scr, acc_scr, w_scr, spread_scr, c16_scr, cnt_scr, cnt16_scr, *, tq, ck, n_sel):
    qi = pl.program_id(1)
    nck = (qi * tq + tq + ck - 1) // ck
    kw = iqs.shape[1]
    for hh in range(IDX_HEADS):
        iqs[hh * tq:(hh + 1) * tq, :] = iq3_ref[:, hh * kw:(hh + 1) * kw]
    wts = misc_ref[...][:, M_IDXW:M_IDXW + IDX_HEADS] * (IDX_HD ** -0.5 * IDX_HEADS ** -0.5)
    for hh in range(IDX_HEADS):
        w_scr[hh] = jnp.broadcast_to(wts[:, hh:hh + 1], (tq, LANES))
    row_pos = qi * tq + lax.broadcasted_iota(I32, (tq, ck), 0)
    lane_pos = lax.broadcasted_iota(I32, (tq, ck), 1)

    def score_chunk(c, carry):
        k0 = pl.multiple_of(c * ck, ck)
        s = jnp.maximum(_dot_nt(iqs[...], ik3_ref[pl.ds(k0, ck), :]), 0.0)
        sc = _lane_tile(w_scr[0], ck) * s[0:tq, :]
        for hh in range(1, IDX_HEADS):
            sc = sc + _lane_tile(w_scr[hh], ck) * s[hh * tq:(hh + 1) * tq, :]
        keys_scr[c] = jnp.where(c * ck + lane_pos <= row_pos, _sortable(sc), INT_MIN)
        return carry

    lax.fori_loop(0, nck, score_chunk, 0)

    def parked(slot):
        return _lane_tile(spread_scr[slot], ck)

    def count(pred):
        cnt_scr[...] = jnp.zeros((tq, LANES), I32)

        def body(c, carry):
            cnt_scr[...] += _lane_fold(jnp.where(pred(keys_scr[c]), 1, 0).astype(I32))
            return carry
        lax.fori_loop(0, nck, body, 0)
        return jnp.sum(cnt_scr[...], axis=-1, keepdims=True)

    ones_sq = jnp.ones((LANES, LANES), BF16)

    def search_half(half_scr):
        def bit_body(i, t_u):
            cand_u = t_u | (jnp.int32(1) << (15 - i))
            c16_scr[...] = (cand_u - HALF_BIAS).astype(I16)
            cnt16_scr[...] = jnp.zeros((tq, LANES), I16)

            def body(c, carry):
                hit = half_scr[c] >= _lane_tile(c16_scr[...], ck)
                cnt16_scr[...] += _lane_fold(jnp.where(hit, jnp.int16(1), jnp.int16(0)))
                return carry
            lax.fori_loop(0, nck, body, 0)
            tot = _dot(cnt16_scr[...].astype(F32).astype(BF16), ones_sq)
            return jnp.where(tot >= n_sel, cand_u, t_u)
        return lax.fori_loop(0, 16, bit_body, jnp.zeros((tq, LANES), I32)) - HALF_BIAS

    def split_hi(c, carry):
        half_scr[c] = (keys_scr[c] >> 16).astype(I16)
        return carry
    lax.fori_loop(0, nck, split_hi, 0)
    t_hi = search_half(half_scr)

    spread_scr[0] = t_hi

    def split_lo(c, carry):
        kk = keys_scr[c]
        hi = kk >> 16
        lo = (kk & 0xFFFF) - HALF_BIAS
        t_hi_b = parked(0)
        half_scr[c] = jnp.where(hi == t_hi_b, lo, jnp.where(hi > t_hi_b, HALF_BIAS - 1, -HALF_BIAS)).astype(I16)
        return carry
    lax.fori_loop(0, nck, split_lo, 0)
    t_lo = search_half(half_scr)
    thr_rep = (t_hi << 16) | ((t_lo + HALF_BIAS) & 0xFFFF)
    spread_scr[1] = thr_rep
    thr = thr_rep[:, 0:1]
    n_ge = count(lambda kk: kk >= parked(1))
    plain = jnp.max(jnp.where(jnp.logical_and(n_ge == n_sel, thr != INT_MIN), 0, 1)) == 0

    @pl.when(plain)
    def _():
        def body(c, carry):
            bias_scr[c] = jnp.where(keys_scr[c] >= parked(1), 0.0, NEG_BIG)
            return carry
        lax.fori_loop(0, nck, body, 0)

    @pl.when(jnp.logical_not(plain))
    def _():
        need = (n_sel - count(lambda kk: kk > parked(1))).astype(F32)
        before = _strictly_before(ck)

        def body(c, eq_seen):
            kk = keys_scr[c]
            thr_b = parked(1)
            eq = jnp.logical_and(kk == thr_b, c * ck + lane_pos <= row_pos)
            eqf = jnp.where(eq, 1.0, 0.0)
            rank = _dot(eqf.astype(BF16), before) + eq_seen
            sel = jnp.logical_or(kk > thr_b, jnp.logical_and(eq, rank < need))
            bias_scr[c] = jnp.where(sel, 0.0, NEG_BIG)
            return eq_seen + jnp.sum(eqf, axis=-1, keepdims=True)

        lax.fori_loop(0, nck, body, jnp.zeros((tq, 1), F32))

    m_scr[...] = jnp.full(m_scr.shape, NEG_BIG, F32)
    acc_scr[...] = jnp.zeros(acc_scr.shape, F32)
    hpg = DSA_HEADS // DSA_KV_HEADS
    for hd in range(DSA_HEADS):
        qs[hd * tq:(hd + 1) * tq, :] = q_ref[:, hd * DSA_HD:(hd + 1) * DSA_HD]

    def attend_chunk(c, carry):
        k0 = pl.multiple_of(c * ck, ck)
        bias = bias_scr[c]
        groups = range(DSA_KV_HEADS)
        rows = [pl.ds(g * hpg * tq, hpg * tq) for g in groups]
        s = [_dot_nt(qs[rows[g], :], k_ref[pl.ds(k0, ck), g * DSA_HD:(g + 1) * DSA_HD]) for g in groups]
        for g in groups:
            vg = v_ref[pl.ds(k0, ck), g * LANES:(g + 1) * LANES]
            sg = s[g].reshape(hpg, tq, ck) + bias[None]
            _online_update(sg.reshape(hpg * tq, ck), vg, m_scr, acc_scr, rows[g])
        return carry

    lax.fori_loop(0, nck, attend_chunk, 0)
    acc = acc_scr[...]
    out = acc / pltpu.roll(acc, DSA_HD, 1)
    for hd in range(DSA_HEADS):
        o_ref[:, hd * DSA_HD:(hd + 1) * DSA_HD] = out[hd * tq:(hd + 1) * tq, 0:DSA_HD].astype(o_ref.dtype)


def _dsa_prompt_call(iq33, ik33, dq3, dkb3, dvb3, h3):
    b, t, _ = dq3.shape
    n_sel = min(TOPK_MAX, t // 4)
    tq = _tile(t, 512)
    ck = _tile(t, 512)
    kw = iq33.shape[-1] // IDX_HEADS
    full = lambda width: pl.BlockSpec((None, t, width), lambda i, q: (i, 0, 0))
    return pl.pallas_call(
        functools.partial(_dsa_prompt_body, tq=tq, ck=ck, n_sel=n_sel),
        grid=(b, t // tq),
        in_specs=[pl.BlockSpec((None, tq, IDX_HEADS * kw), lambda i, q: (i, q, 0)),
                  pl.BlockSpec((None, tq, LANES), lambda i, q: (i, q, C_MISC // LANES)),
                  full(kw),
                  pl.BlockSpec((None, tq, MIX_WIDTH), lambda i, q: (i, q, 0)),
                  full(LANES), full(DSA_KV_HEADS * LANES)],
        out_specs=pl.BlockSpec((None, tq, MIX_WIDTH), lambda i, q: (i, q, 0)),
        out_shape=jax.ShapeDtypeStruct((b, t, MIX_WIDTH), BF16),
        scratch_shapes=[pltpu.VMEM((IDX_HEADS * tq, kw), BF16), pltpu.VMEM((DSA_HEADS * tq, DSA_HD), BF16),
                        pltpu.VMEM((t // ck, tq, ck), I32), pltpu.VMEM((t // ck, tq, ck), I16),
                        pltpu.VMEM((t // ck, tq, ck), F32),
                        pltpu.VMEM((DSA_HEADS * tq, LANES), F32), pltpu.VMEM((DSA_HEADS * tq, LANES), F32),
                        pltpu.VMEM((IDX_HEADS, tq, LANES), F32), pltpu.VMEM((2, tq, LANES), I32),
                        pltpu.VMEM((tq, LANES), I16), pltpu.VMEM((tq, LANES), I32), pltpu.VMEM((tq, LANES), I16)],
        compiler_params=pltpu.CompilerParams(dimension_semantics=("parallel", "arbitrary"),
                                             vmem_limit_bytes=VMEM_LIMIT_DSA_PROMPT),
        name="dsa_prompt",
    )(iq33, h3, ik33, dq3, dkb3, dvb3)


def _dsa_select_body(pt_ref, iq3_ref, misc_ref, ikn_ref, *rest, tq, npg, ns, n_sel, n_pages):
    pages = rest[:ns * npg]
    sel_ref, keys_scr, iqs = rest[ns * npg:]
    s_id = pl.program_id(1)
    n_steps = pl.num_programs(1)
    kw = iqs.shape[-1]
    seqs = range(ns)
    wts = [misc_ref[q][:, M_IDXW:M_IDXW + IDX_HEADS] * (IDX_HD ** -0.5 * IDX_HEADS ** -0.5) for q in seqs]

    @pl.when(s_id == 0)
    def _():
        for q in seqs:
            for hh in range(IDX_HEADS):
                iqs[q, hh * tq:(hh + 1) * tq, :] = iq3_ref[q, :, hh * kw:(hh + 1) * kw]

    def scores(q, ik, feature_major):
        hi, lo = _split_hi_lo(ik)
        a_hi = iqs[q, :, 0:IDX_HD]
        a_lo = iqs[q, :, 2 * IDX_HD:3 * IDX_HD]
        mm = _dot if feature_major else _dot_nt
        s = jnp.maximum(mm(a_hi, hi) + mm(a_hi, lo) + mm(a_lo, hi), 0.0)
        sc = wts[q][:, 0:1] * s[0:tq, :]
        for hh in range(1, IDX_HEADS):
            sc = sc + wts[q][:, hh:hh + 1] * s[hh * tq:(hh + 1) * tq, :]
        return sc

    @pl.when(s_id < n_steps - 1)
    def _():
        for q in seqs:
            ik = jnp.concatenate([pages[q * npg + p][...] for p in range(npg)], axis=1)
            keys = _sortable(scores(q, ik, True))
            for p in range(npg):
                keys_scr[q, s_id * npg + p] = keys[:, p * PAGE_SIZE:(p + 1) * PAGE_SIZE]

    @pl.when(s_id == n_steps - 1)
    def _():
        qrow = lax.broadcasted_iota(I32, (tq, tq), 0)
        kcol = lax.broadcasted_iota(I32, (tq, tq), 1)
        for q in seqs:
            new_keys = jnp.where(kcol <= qrow, _sortable(scores(q, ikn_ref[q], False)), INT_MIN)
            keys_scr[q, n_pages] = jnp.concatenate([new_keys, jnp.full((tq, LANES - tq), INT_MIN, I32)], axis=-1)

        def count(pred):
            acc = jnp.sum(jnp.where(pred(keys_scr[...]), 1, 0).astype(I32), axis=1)
            return jnp.sum(acc, axis=-1, keepdims=True)

        thr = _search_threshold(lambda cand: count(lambda kk: kk >= cand[:, None]), (ns, tq, 1), n_sel)
        n_ge = count(lambda kk: kk >= thr[:, None])
        plain = jnp.max(jnp.where(jnp.logical_and(n_ge == n_sel, thr != INT_MIN), 0, 1)) == 0

        @pl.when(plain)
        def _():
            sel_ref[...] = jnp.where(keys_scr[...] >= thr[:, None], 1.0, 0.0)

        @pl.when(jnp.logical_not(plain))
        def _():
            need = (n_sel - count(lambda kk: kk > thr[:, None])).astype(F32)
            before = _strictly_before(LANES)
            lane = lax.broadcasted_iota(I32, (tq, LANES), 1)
            qr = lax.broadcasted_iota(I32, (tq, LANES), 0)
            for q in seqs:
                def emit(c, eq_seen, q=q):
                    kk = keys_scr[q, c]
                    allowed = jnp.logical_or(c < n_pages, lane <= qr)
                    eq = jnp.logical_and(kk == thr[q], allowed)
                    eqf = jnp.where(eq, 1.0, 0.0)
                    rank = _dot(eqf.astype(BF16), before) + eq_seen
                    sel = jnp.logical_or(kk > thr[q], jnp.logical_and(eq, rank < need[q]))
                    sel_ref[q, c] = jnp.where(sel, 1.0, 0.0)
                    return eq_seen + jnp.sum(eqf, axis=-1, keepdims=True)

                lax.fori_loop(0, n_pages + 1, emit, jnp.zeros((tq, 1), F32))


def _dsa_attend_body(pt_ref, q_ref, kn_ref, vn_ref, selp_ref, seln_ref, *rest, tq, npg):
    kp = rest[:npg]
    vp = rest[npg:2 * npg]
    o_ref, qrows, m_scr, l_scr, acc_scr = rest[2 * npg:]
    s_id = pl.program_id(1)
    n_steps = pl.num_programs(1)
    hpg = DSA_HEADS // DSA_KV_HEADS
    grows = hpg * tq

    @pl.when(s_id == 0)
    def _():
        _reset_softmax(m_scr, l_scr, acc_scr)
        for hd in range(DSA_HEADS):
            qrows[hd * tq:(hd + 1) * tq, :] = q_ref[:, hd * DSA_HD:(hd + 1) * DSA_HD].astype(F32)

    qb = qrows[...].astype(BF16)
    qg = [qb[g * grows:(g + 1) * grows, :] for g in range(DSA_KV_HEADS)]

    def update(s, pv_fn):
        m_old = m_scr[...]
        m_new = jnp.maximum(m_old, jnp.max(s, axis=-1, keepdims=True))
        alpha = jnp.exp(m_old - m_new)
        p = jnp.exp(s - m_new)
        l_scr[...] = alpha * l_scr[...] + jnp.sum(p, axis=-1, keepdims=True)
        pb = p.astype(BF16)
        pv = jnp.concatenate([pv_fn(pb[g * grows:(g + 1) * grows, :], g) for g in range(DSA_KV_HEADS)], axis=0)
        acc_scr[...] = alpha * acc_scr[...] + pv
        m_scr[...] = m_new

    @pl.when(s_id < n_steps - 1)
    def _():
        kcat = [jnp.concatenate([kp[p][g] for p in range(npg)], axis=1).astype(BF16) for g in range(DSA_KV_HEADS)]
        vcat = [jnp.concatenate([vp[p][g] for p in range(npg)], axis=1).astype(BF16) for g in range(DSA_KV_HEADS)]
        s = jnp.concatenate([_dot(qg[g], kcat[g]) for g in range(DSA_KV_HEADS)], axis=0)
        sel1 = jnp.concatenate([selp_ref[p] for p in range(npg)], axis=1)
        sel = jnp.concatenate([sel1] * DSA_HEADS, axis=0) > 0.5
        update(jnp.where(sel, s, NEG_BIG), lambda pb, g: _dot_nt(pb, vcat[g]))

    @pl.when(s_id == n_steps - 1)
    def _():
        kn = kn_ref[...].astype(BF16)
        vn = vn_ref[...].astype(BF16)
        s = jnp.concatenate([_dot_nt(qg[g], kn[:, g * DSA_HD:(g + 1) * DSA_HD]) for g in range(DSA_KV_HEADS)], axis=0)
        sel = jnp.concatenate([seln_ref[...][:, 0:tq]] * DSA_HEADS, axis=0) > 0.5
        update(jnp.where(sel, s, NEG_BIG), lambda pb, g: _dot(pb, vn[:, g * DSA_HD:(g + 1) * DSA_HD]))
        acc = acc_scr[...] / l_scr[...]
        for hd in range(DSA_HEADS):
            o_ref[:, hd * DSA_HD:(hd + 1) * DSA_HD] = acc[hd * tq:(hd + 1) * tq, :].astype(o_ref.dtype)


def _dsa_sample_call(iq33, ik3, dq3, dk3, h3, pool_k, pool_v, pool_ik, page_table, layer):
    db, t, _ = dq3.shape
    n_pages = page_table.shape[1]
    n_keys = n_pages * PAGE_SIZE + t
    n_sel = min(TOPK_MAX, n_keys // 4)
    npg = _pages_per_step(n_pages)
    n_steps = n_pages // npg + 1
    kw = iq33.shape[-1] // IDX_HEADS
    sw = LANES

    def page_spec(*tail):
        def make(p):
            def imap(i, s, pt):
                return (layer, pt[i, jnp.minimum(s, n_steps - 2) * npg + p]) + (0,) * len(tail)
            return pl.BlockSpec((None, None) + tail, imap)
        return [make(p) for p in range(npg)]

    ns = 1

    def select_pages():
        def make(q, p):
            def imap(i, s, pt):
                return (layer, pt[i * ns + q, jnp.minimum(s, n_steps - 2) * npg + p], 0, 0)
            return pl.BlockSpec((None, None, IDX_HD, PAGE_SIZE), imap)
        return [make(q, p) for q in range(ns) for p in range(npg)]

    sel = pl.pallas_call(
        functools.partial(_dsa_select_body, tq=t, npg=npg, ns=ns, n_sel=n_sel, n_pages=n_pages),
        grid_spec=pltpu.PrefetchScalarGridSpec(
            num_scalar_prefetch=1,
            grid=(db // ns, n_steps),
            in_specs=[pl.BlockSpec((ns, t, IDX_HEADS * kw), lambda i, s, pt: (i, 0, 0)),
                      pl.BlockSpec((ns, t, LANES), lambda i, s, pt: (i, 0, C_MISC // LANES)),
                      pl.BlockSpec((ns, t, IDX_HD), lambda i, s, pt: (i, 0, 0))]
                     + select_pages(),
            out_specs=pl.BlockSpec((ns, n_pages + 1, t, sw), lambda i, s, pt: (i, 0, 0, 0)),
            scratch_shapes=[pltpu.VMEM((ns, n_pages + 1, t, LANES), I32), pltpu.VMEM((ns, IDX_HEADS * t, kw), BF16)],
        ),
        out_shape=jax.ShapeDtypeStruct((db, n_pages + 1, t, sw), F32),
        compiler_params=_cparams("parallel", "arbitrary"),
        name="dsa_sample_select",
    )(page_table, iq33, h3, ik3, *([pool_ik] * (ns * npg)))

    nrow = DSA_HEADS * t
    return pl.pallas_call(
        functools.partial(_dsa_attend_body, tq=t, npg=npg),
        grid_spec=pltpu.PrefetchScalarGridSpec(
            num_scalar_prefetch=1,
            grid=(db, n_steps),
            in_specs=[pl.BlockSpec((None, t, MIX_WIDTH), lambda i, s, pt: (i, 0, 0)),
                      pl.BlockSpec((None, t, LANES), lambda i, s, pt: (i, 0, 0)),
                      pl.BlockSpec((None, t, LANES), lambda i, s, pt: (i, 0, C_DSAV // LANES)),
                      pl.BlockSpec((None, npg, t, sw), lambda i, s, pt: (i, jnp.minimum(s, n_steps - 2), 0, 0)),
                      pl.BlockSpec((None, None, t, sw), lambda i, s, pt: (i, n_pages, 0, 0))]
                     + page_spec(DSA_KV_HEADS, DSA_HD, PAGE_SIZE) + page_spec(DSA_KV_HEADS, DSA_HD, PAGE_SIZE),
            out_specs=pl.BlockSpec((None, t, MIX_WIDTH), lambda i, s, pt: (i, 0, 0)),
            scratch_shapes=[pltpu.VMEM((nrow, DSA_HD), F32), pltpu.VMEM((nrow, 1), F32), pltpu.VMEM((nrow, 1), F32),
                            pltpu.VMEM((nrow, DSA_HD), F32)],
        ),
        out_shape=jax.ShapeDtypeStruct((db, t, MIX_WIDTH), BF16),
        compiler_params=_cparams("parallel", "arbitrary"),
        name="dsa_sample_attend",
    )(page_table, dq3, dk3, h3, sel, sel, *([pool_k] * npg), *([pool_v] * npg))


def _merge_body(x_ref, gate_ref, oa_ref, ob_ref, oc_ref, od_ref, wb_ref, wo_ref, o_ref):
    acc = None
    for bi, br in enumerate((oa_ref, ob_ref, oc_ref, od_ref)):
        proj = _dot(br[...], wb_ref[bi])
        term = _sigmoid(gate_ref[:, bi * D_MODEL:(bi + 1) * D_MODEL]) * proj
        acc = term if acc is None else acc + term
    o_ref[...] = x_ref[...] + _dot(acc.astype(BF16), wo_ref[...])


def _merge_call(x, h, branches, wl):
    n, d = x.shape
    tm = _tile(n, 512)
    br = pl.BlockSpec((tm, MIX_WIDTH), lambda i: (i, 0))
    return pl.pallas_call(
        _merge_body,
        grid=(n // tm,),
        in_specs=[pl.BlockSpec((tm, d), lambda i: (i, 0)),
                  pl.BlockSpec((tm, N_BRANCH * d), lambda i: (i, C_GATE // (N_BRANCH * d))),
                  br, br, br, br,
                  pl.BlockSpec((N_BRANCH, MIX_WIDTH, d), lambda i: (0, 0, 0)),
                  pl.BlockSpec((d, d), lambda i: (0, 0))],
        out_specs=pl.BlockSpec((tm, d), lambda i: (i, 0)),
        out_shape=jax.ShapeDtypeStruct((n, d), F32),
        compiler_params=_cparams("parallel"),
        name="merge",
    )(x, h, *branches, wl["w_branch"], wl["w_out"])


def _cross_body(x_ref, g_ref, wq_ref, mk_ref, mv_ref, wo_ref, o_ref, *, merged):
    x = x_ref[...]
    tm = x.shape[0]
    xn = _rms_rows(x, g_ref[...]).astype(BF16)
    q = _dot(xn, wq_ref[...]).astype(BF16)
    if merged:
        qrows = jnp.concatenate([q[:, hh * XA_HD:(hh + 1) * XA_HD] for hh in range(XA_HEADS)], axis=0)
        s = _dot_nt(qrows, mk_ref[...].astype(BF16)) * (XA_HD ** -0.5)
        rr = lax.broadcasted_iota(I32, s.shape, 0) // tm
        cc = lax.broadcasted_iota(I32, s.shape, 1) % XA_HEADS
        s = jnp.where(rr == cc, s, NEG_BIG)
        p = jnp.exp(s - jnp.max(s, axis=-1, keepdims=True))
        p = p / jnp.sum(p, axis=-1, keepdims=True)
        orow = _dot(p.astype(BF16), mv_ref[...].astype(BF16))
        outs = [orow[hh * tm:(hh + 1) * tm, :] for hh in range(XA_HEADS)]
    else:
        outs = []
        for hh in range(XA_HEADS):
            sl = slice(hh * XA_HD, (hh + 1) * XA_HD)
            s = _dot_nt(q[:, sl], mk_ref[:, sl].astype(BF16)) * (XA_HD ** -0.5)
            p = jnp.exp(s - jnp.max(s, axis=-1, keepdims=True))
            p = p / jnp.sum(p, axis=-1, keepdims=True)
            outs.append(_dot(p.astype(BF16), mv_ref[:, sl].astype(BF16)))
    o = jnp.concatenate(outs, axis=-1).astype(BF16)
    o_ref[...] = x + _dot(o, wo_ref[...])


def _cross_call(x3, wl, mk4, mv4, layer):
    b, t, d = x3.shape
    w = XA_HEADS * XA_HD
    tm = _tile(t, 512)
    merged = mk4.shape[-1] == XA_HD
    mem = pl.BlockSpec((None, None) + mk4.shape[2:], lambda i, j: (layer, i, 0, 0))
    return pl.pallas_call(
        functools.partial(_cross_body, merged=merged),
        grid=(b, t // tm),
        in_specs=[pl.BlockSpec((None, tm, d), lambda i, j: (i, j, 0)),
                  pl.BlockSpec((1, d), lambda i, j: (0, 0)),
                  pl.BlockSpec((d, w), lambda i, j: (0, 0)), mem, mem,
                  pl.BlockSpec((w, d), lambda i, j: (0, 0))],
        out_specs=pl.BlockSpec((None, tm, d), lambda i, j: (i, j, 0)),
        out_shape=jax.ShapeDtypeStruct((b, t, d), F32),
        compiler_params=_cparams("parallel", "parallel"),
        name="cross_attn",
    )(x3, wl["norm_xa_g"], wl["xa_wq"], mk4, mv4, wl["xa_wo"])


def _ffn_out(acc, fg_ref, final_norm):
    return _rms_rows(acc, fg_ref[...]) if final_norm else acc


def _ffn_body(x_ref, g_ref, wg_ref, wu_ref, wd_ref, fg_ref, o_ref, xn_scr, acc_scr, *, final_norm):
    f = pl.program_id(1)

    @pl.when(f == 0)
    def _():
        xn_scr[...] = _rms_rows(x_ref[...], g_ref[...]).astype(BF16)
        acc_scr[...] = x_ref[...]

    xn = xn_scr[...]
    hm = _silu(_dot(xn, wg_ref[...])) * _dot(xn, wu_ref[...])
    acc_scr[...] += _dot(hm.astype(BF16), wd_ref[...])

    @pl.when(f == pl.num_programs(1) - 1)
    def _():
        o_ref[...] = _ffn_out(acc_scr[...], fg_ref, final_norm)


def _ffn_call(x, wl, final_g, final_norm):
    n, d = x.shape
    ff = wl["ffn_wg"].shape[1]
    tm = _tile(n, 1024)
    tf = _tile(ff, 1408)
    return pl.pallas_call(
        functools.partial(_ffn_body, final_norm=final_norm),
        grid=(n // tm, ff // tf),
        in_specs=[pl.BlockSpec((tm, d), lambda i, f: (i, 0)), pl.BlockSpec((1, d), lambda i, f: (0, 0)),
                  pl.BlockSpec((d, tf), lambda i, f: (0, f)), pl.BlockSpec((d, tf), lambda i, f: (0, f)),
                  pl.BlockSpec((tf, d), lambda i, f: (f, 0)), pl.BlockSpec((1, d), lambda i, f: (0, 0))],
        out_specs=pl.BlockSpec((tm, d), lambda i, f: (i, 0)),
        out_shape=jax.ShapeDtypeStruct((n, d), F32),
        scratch_shapes=[pltpu.VMEM((tm, d), BF16), pltpu.VMEM((tm, d), F32)],
        compiler_params=_cparams("parallel", "arbitrary"),
        name="ffn_swiglu",
    )(x, wl["norm_ffn_g"], wl["ffn_wg"], wl["ffn_wu"], wl["ffn_wd"], final_g)


def _moe_body(x_ref, g_ref, wr_ref, br_ref, wg_ref, wu_ref, wd_ref, fg_ref, o_ref, xn_scr, gate_scr, acc_scr,
              *, final_norm):
    e = pl.program_id(1)
    f = pl.program_id(2)
    first = jnp.logical_and(e == 0, f == 0)

    @pl.when(first)
    def _():
        xn = _rms_rows(x_ref[...], g_ref[...])
        xn_scr[...] = xn.astype(BF16)
        acc_scr[...] = x_ref[...]
        logits = _dot_hi(xn, wr_ref[...]) + br_ref[...]
        lane = lax.broadcasted_iota(I32, logits.shape, 1)
        logits = jnp.where(lane < N_EXPERTS, logits, -jnp.inf)
        top1 = jnp.max(logits, axis=-1, keepdims=True)
        idx1 = jnp.min(jnp.where(logits == top1, lane, LANES), axis=-1, keepdims=True)
        rest = jnp.where(lane == idx1, -jnp.inf, logits)
        top2 = jnp.max(rest, axis=-1, keepdims=True)
        idx2 = jnp.min(jnp.where(rest == top2, lane, LANES), axis=-1, keepdims=True)
        e2 = jnp.exp(top2 - top1)
        p1 = 1.0 / (1.0 + e2)
        p2 = e2 / (1.0 + e2)
        gate_scr[...] = jnp.where(lane == idx1, p1, 0.0) + jnp.where(lane == idx2, p2, 0.0)

    gates = gate_scr[...]
    lane = lax.broadcasted_iota(I32, gates.shape, 1)
    ge = jnp.sum(jnp.where(lane == e, gates, 0.0), axis=-1, keepdims=True)
    xn = xn_scr[...]
    hm = _silu(_dot(xn, wg_ref[...])) * _dot(xn, wu_ref[...])
    acc_scr[...] += ge * _dot(hm.astype(BF16), wd_ref[...])

    @pl.when(jnp.logical_and(e == pl.num_programs(1) - 1, f == pl.num_programs(2) - 1))
    def _():
        o_ref[...] = _ffn_out(acc_scr[...], fg_ref, final_norm)


def _moe_call(x, wl, final_g, final_norm):
    n, d = x.shape
    ne, _, fe = wl["moe_wg"].shape
    tm = _tile(n, 1024)
    tf = _tile(fe, 1024)
    return pl.pallas_call(
        functools.partial(_moe_body, final_norm=final_norm),
        grid=(n // tm, ne, fe // tf),
        in_specs=[pl.BlockSpec((tm, d), lambda i, e, f: (i, 0)), pl.BlockSpec((1, d), lambda i, e, f: (0, 0)),
                  pl.BlockSpec((d, LANES), lambda i, e, f: (0, 0)), pl.BlockSpec((1, LANES), lambda i, e, f: (0, 0)),
                  pl.BlockSpec((None, d, tf), lambda i, e, f: (e, 0, f)),
                  pl.BlockSpec((None, d, tf), lambda i, e, f: (e, 0, f)),
                  pl.BlockSpec((None, tf, d), lambda i, e, f: (e, f, 0)),
                  pl.BlockSpec((1, d), lambda i, e, f: (0, 0))],
        out_specs=pl.BlockSpec((tm, d), lambda i, e, f: (i, 0)),
        out_shape=jax.ShapeDtypeStruct((n, d), F32),
        scratch_shapes=[pltpu.VMEM((tm, d), BF16), pltpu.VMEM((tm, LANES), F32), pltpu.VMEM((tm, d), F32)],
        compiler_params=_cparams("parallel", "arbitrary", "arbitrary"),
        name="moe_swiglu",
    )(x, wl["norm_ffn_g"], wl["moe_wr"], wl["moe_br"], wl["moe_wg"], wl["moe_wu"], wl["moe_wd"], final_g)


def _pad_rows(a, rows):
    return jnp.pad(a, ((0, rows - a.shape[0]), (0, 0)))


def _lane_row(vals, lane0):
    return jnp.zeros((1, LANES), F32).at[0, lane0:lane0 + vals.shape[0]].set(vals.astype(F32))


def _block_diag(w):
    nb, n, _ = w.shape
    eye = jnp.eye(nb, dtype=w.dtype)
    return jnp.einsum("aij,ab->aibj", w, eye).reshape(nb * n, nb * n)


def _layer_weights(l, p):
    w_t = jnp.transpose(p["w_in"], (2, 0, 1))[:, l, :]
    d = w_t.shape[1]
    misc = jnp.concatenate([w_t[O_IDXK:O_IDXK + IDX_HD + IDX_HEADS], w_t[O_GDNB:O_GDNB + 2 * GDN_HEADS],
                            jnp.zeros((LANES - IDX_HD - IDX_HEADS - 2 * GDN_HEADS, d), w_t.dtype)], axis=0)
    w_in_r = jnp.concatenate([
        w_t[O_GATE:], w_t[O_LRUX:O_DSAK], w_t[O_DIFQ:O_GATE], w_t[O_GDNZ:O_GDNB],
        w_t[O_GDNQKV:O_GDNZ], w_t[O_IDXQ:O_IDXK], w_t[O_DSAK:O_IDXQ], misc], axis=0)
    assert w_in_r.shape[0] == H_COLS
    wl = dict(
        norm_mix_g=p["norm_mix_g"][l], w_in=w_in_r.astype(BF16),
        lru_cw=_pad_rows(p["lru_conv_w"][l], SUBLANES), lru_cb=p["lru_conv_b"][l].reshape(1, -1),
        lru_wa=_block_diag(p["lru_wa"][l]).astype(BF16), lru_ba=p["lru_ba"][l].reshape(1, -1),
        lru_wx=_block_diag(p["lru_wx"][l]).astype(BF16), lru_bx=p["lru_bx"][l].reshape(1, -1),
        lru_lam=p["lru_lambda"][l].reshape(1, -1),
        gdn_cw=_pad_rows(p["gdn_conv_w"][l], SUBLANES),
        gdn_alog=_lane_row(p["gdn_a_log"][l], M_GDNA), gdn_dtb=_lane_row(p["gdn_dt_bias"][l], M_GDNA),
        gdn_ng=p["gdn_norm_g"][l].reshape(1, -1),
        diff_lam=p["diff_lambda"][l], diff_sg=p["diff_subln_g"][l].reshape(1, -1),
        w_branch=p["w_branch"][l].astype(BF16), w_out=p["w_out"][l].astype(BF16),
        norm_xa_g=p["norm_xa_g"][l].reshape(1, -1), norm_mem_g=p["norm_mem_g"][l],
        xa_wq=p["xa_wq"][l].astype(BF16), xa_wo=p["xa_wo"][l].astype(BF16),
        xa_wkv=jnp.concatenate([p["xa_wk"][l], p["xa_wv"][l]], axis=1).astype(BF16),
        norm_ffn_g=p["norm_ffn_g"][l].reshape(1, -1),
    )
    j = l // 2
    if l % 2 == 0:
        wl.update(ffn_wg=p["ffn_w_gate"][j].astype(BF16), ffn_wu=p["ffn_w_up"][j].astype(BF16),
                  ffn_wd=p["ffn_w_down"][j].astype(BF16))
    else:
        wl.update(moe_wr=jnp.pad(p["moe_router_w"][j], ((0, 0), (0, LANES - N_EXPERTS))),
                  moe_br=jnp.pad(p["moe_router_b"][j], (0, LANES - N_EXPERTS)).reshape(1, LANES),
                  moe_wg=p["moe_w_gate"][j].astype(BF16), moe_wu=p["moe_w_up"][j].astype(BF16),
                  moe_wd=p["moe_w_down"][j].astype(BF16))
    return wl


def _rope_tables(pos):
    half = DSA_HD // 2
    inv = ROPE_THETA ** (-jnp.arange(half, dtype=F32) / half)
    ang = pos.astype(F32)[:, None] * inv[None, :]
    cos, sin = jnp.cos(ang), jnp.sin(ang)
    return jnp.tile(cos, (1, 4)), jnp.tile(jnp.concatenate([-sin, sin], axis=1), (1, 2))


def _front_pad(buf):
    return jnp.pad(buf, ((0, 0), (SUBLANES - (CONV_W - 1), 0), (0, 0)))


def _group_layer(x3, l, wl, tabs, states, mem_kv, paged):
    b, t, d = x3.shape
    n = b * t
    x = x3.reshape(n, d)
    lam_init = 0.8 - 0.6 * math.exp(-0.3 * l)
    xn = _rmsnorm(x, wl["norm_mix_g"], BF16)
    h = _matmul(xn, wl["w_in"], tn=1408, w_rows_are_outputs=True)
    h3 = h.reshape(b, t, H_COLS)
    dq, fq, fk, fkb, fvb, iq3, dk, dkb, dvb, ik, ik3 = _rope_call(h, tabs[0], tabs[1])
    to3 = lambda a: a.reshape(b, t, a.shape[-1])
    dq3, fq3, fk3, dk3, ik3d = map(to3, (dq, fq, fk, dk, ik))

    o_a, lru_h = _lru_call(h3, wl, _front_pad(states["lru_conv"]), states["lru_h"].reshape(b, 1, -1),
                           reset_first=paged is None)
    o_c, gdn_s = _gdn_call(h3, wl, _front_pad(states["gdn_conv"]), states["gdn_s"])
    if paged is None:
        o_b = _dsa_prompt_call(to3(iq3), to3(ik3), dq3, to3(dkb), to3(dvb), h3)
        o_d = _diff_prompt_call(fq3, to3(fkb), to3(fvb), wl, lam_init)
    else:
        o_b = _dsa_sample_call(to3(iq3), ik3d, dq3, dk3, h3, paged["dsa_k"], paged["dsa_v"], paged["idx_k"],
                               paged["page_table"], l)
        o_d = _diff_sample_call(fq3, fk3, h3, paged["diff_k"], paged["diff_v"], paged["page_table"], l, wl, lam_init)

    flat = lambda a: a.reshape(n, a.shape[-1])
    x1 = _merge_call(x, h, [flat(o_a), flat(o_b), flat(o_c), flat(o_d)], wl)
    x2 = _cross_call(x1.reshape(b, t, d), wl, mem_kv[0], mem_kv[1], mem_kv[2])
    x2 = x2.reshape(n, d)
    last = l == DEPTH - 1
    ffn = _ffn_call if "ffn_wg" in wl else _moe_call
    x3_new = ffn(x2, wl, wl["final_norm_g"], last).reshape(b, t, d)

    new = dict(
        dsa_k=dk3.reshape(b, t, DSA_KV_HEADS, DSA_HD),
        dsa_v=h3[:, :, C_DSAV:C_DSAV + DSA_KV_HEADS * DSA_HD].reshape(b, t, DSA_KV_HEADS, DSA_HD),
        idx_k=ik3d,
        diff_k=fk3.reshape(b, t, DIFF_HEADS, 2 * DIFF_HD),
        diff_v=h3[:, :, C_DIFV:C_DIFV + MIX_WIDTH].reshape(b, t, DIFF_HEADS, 2 * DIFF_HD),
        lru_h=lru_h.reshape(b, -1),
        lru_conv=jnp.concatenate([states["lru_conv"], h3[:, :, C_LRUX:C_LRUX + MIX_WIDTH]], axis=1)[:, -(CONV_W - 1):]
        if t < CONV_W - 1 else h3[:, t - (CONV_W - 1):, C_LRUX:C_LRUX + MIX_WIDTH],
        gdn_s=gdn_s,
        gdn_conv=jnp.concatenate([states["gdn_conv"], h3[:, :, C_GDNQKV:C_GDNQKV + GDN_QKV]], axis=1)[:, -(CONV_W - 1):]
        if t < CONV_W - 1 else h3[:, t - (CONV_W - 1):, C_GDNQKV:C_GDNQKV + GDN_QKV],
    )
    return x3_new, new


def kernel(x_prompt, x_sample, mem_prompt, cache_dsa_k, cache_dsa_v, cache_idx_k, cache_diff_k, cache_diff_v,
           cache_mem_k, cache_mem_v, state_lru_h, state_lru_conv, state_gdn_s, state_gdn_conv, page_table,
           norm_mix_g, w_in, lru_conv_w, lru_conv_b, lru_wa, lru_ba, lru_wx, lru_bx, lru_lambda,
           gdn_conv_w, gdn_a_log, gdn_dt_bias, gdn_norm_g, diff_lambda, diff_subln_g, w_branch, w_out,
           norm_xa_g, norm_mem_g, xa_wq, xa_wk, xa_wv, xa_wo, norm_ffn_g, ffn_w_gate, ffn_w_up, ffn_w_down,
           moe_router_w, moe_router_b, moe_w_gate, moe_w_up, moe_w_down, final_norm_g):
    params = dict(
        norm_mix_g=norm_mix_g, w_in=w_in, lru_conv_w=lru_conv_w, lru_conv_b=lru_conv_b, lru_wa=lru_wa, lru_ba=lru_ba,
        lru_wx=lru_wx, lru_bx=lru_bx, lru_lambda=lru_lambda, gdn_conv_w=gdn_conv_w, gdn_a_log=gdn_a_log,
        gdn_dt_bias=gdn_dt_bias, gdn_norm_g=gdn_norm_g, diff_lambda=diff_lambda, diff_subln_g=diff_subln_g,
        w_branch=w_branch, w_out=w_out, norm_xa_g=norm_xa_g, norm_mem_g=norm_mem_g, xa_wq=xa_wq, xa_wk=xa_wk,
        xa_wv=xa_wv, xa_wo=xa_wo, norm_ffn_g=norm_ffn_g, ffn_w_gate=ffn_w_gate, ffn_w_up=ffn_w_up,
        ffn_w_down=ffn_w_down, moe_router_w=moe_router_w, moe_router_b=moe_router_b, moe_w_gate=moe_w_gate,
        moe_w_up=moe_w_up, moe_w_down=moe_w_down)
    b, s, d = x_prompt.shape
    db, t, _ = x_sample.shape
    m = mem_prompt.shape[1]
    n_pool = cache_dsa_k.shape[1]
    past = page_table.shape[1] * PAGE_SIZE

    tabs_p = _rope_tables(jnp.arange(s))
    cs, ss = _rope_tables(past + jnp.arange(t))
    tabs_s = (jnp.tile(cs, (db, 1)), jnp.tile(ss, (db, 1)))
    paged = dict(
        dsa_k=jnp.transpose(cache_dsa_k, (0, 1, 3, 4, 2)), dsa_v=jnp.transpose(cache_dsa_v, (0, 1, 3, 4, 2)),
        idx_k=jnp.transpose(cache_idx_k, (0, 1, 3, 2)), page_table=page_table,
        diff_k=cache_diff_k.reshape(DEPTH, n_pool, PAGE_SIZE * DIFF_HEADS, 2 * DIFF_HD),
        diff_v=cache_diff_v.reshape(DEPTH, n_pool, PAGE_SIZE * DIFF_HEADS, 2 * DIFF_HD))
    mem_k_s = cache_mem_k.reshape(DEPTH, db, m * XA_HEADS, XA_HD)
    mem_v_s = cache_mem_v.reshape(DEPTH, db, m * XA_HEADS, XA_HD)
    zero_states = dict(lru_conv=jnp.zeros((b, CONV_W - 1, MIX_WIDTH), F32), lru_h=jnp.zeros((b, MIX_WIDTH), F32),
                       gdn_conv=jnp.zeros((b, CONV_W - 1, GDN_QKV), F32),
                       gdn_s=jnp.zeros((b, GDN_HEADS, GDN_HD, GDN_HD), F32))

    hp, hs = x_prompt, x_sample
    new_p, new_s, mem_ks, mem_vs = [], [], [], []
    for l in range(DEPTH):
        wl = _layer_weights(l, params)
        wl["final_norm_g"] = final_norm_g.reshape(1, d)
        mem_n = _rmsnorm(mem_prompt.reshape(b * m, d), wl["norm_mem_g"], BF16)
        kv = _matmul(mem_n, wl["xa_wkv"], tn=512)
        w = XA_HEADS * XA_HD
        mk = kv[:, :w].reshape(1, b, m, w)
        mv = kv[:, w:].reshape(1, b, m, w)
        hp, st_p = _group_layer(hp, l, wl, tabs_p, zero_states, (mk, mv, 0), None)
        new_p.append(st_p)
        mem_ks.append(mk.reshape(b, m, XA_HEADS, XA_HD))
        mem_vs.append(mv.reshape(b, m, XA_HEADS, XA_HD))
        states_s = dict(lru_conv=state_lru_conv[l], lru_h=state_lru_h[l], gdn_conv=state_gdn_conv[l],
                        gdn_s=state_gdn_s[l])
        hs, st_s = _group_layer(hs, l, wl, tabs_s, states_s, (mem_k_s, mem_v_s, l), paged)
        new_s.append(st_s)

    y_prompt, y_sample = hp, hs
    stack = lambda states, name: jnp.stack([st[name] for st in states], axis=0)
    outs = [y_prompt, y_sample]
    for name in ("dsa_k", "dsa_v", "idx_k", "diff_k", "diff_v", "lru_h", "lru_conv", "gdn_s", "gdn_conv"):
        outs += [stack(new_p, name), stack(new_s, name)]
    outs += [jnp.stack(mem_ks, axis=0), jnp.stack(mem_vs, axis=0)]
    return tuple(outs)
```
